```python
import math
import jax, jax.numpy as jnp
from jax import lax
import numpy as np

D_MODEL = 1024
BATCH = 16
SEQ = 256
DEPTH = 4
DEC_BATCH = 2
DEC_SEQ = 1024
PAST_LEN = 512

GRID_W = 64
N_EVEN = (DEPTH + 1) // 2
N_ODD = DEPTH // 2
Q_BLOCK = 128
ROPE_THETA = 10000.0
NORM_EPS = 1e-6
DH = 64
A_W = D_MODEL // 2
CONV_K = 31
H_B = D_MODEL // (4 * DH)
B_W = H_B * 2 * DH
H_C = D_MODEL // (2 * DH)
KV_C = H_C // 4
G_C = H_C // KV_C
C_W = H_C * DH
H_D = D_MODEL // (2 * DH)
KV_D = H_D // 4
G_D = H_D // KV_D
D_W = H_D * DH
WINDOW = 128

IN_E = 3 * A_W + 4 * B_W
OUT_E = A_W + B_W
SPLIT_E = [A_W, 2 * A_W, 3 * A_W, 3 * A_W + B_W, 3 * A_W + 2 * B_W, 3 * A_W + 3 * B_W]
IN_O = 2 * C_W + 2 * KV_C * DH + 2 * D_W + 2 * KV_D * DH
OUT_O = C_W + D_W
_o = np.cumsum([C_W, KV_C * DH, KV_C * DH, C_W, D_W, KV_D * DH, KV_D * DH])
SPLIT_O = [int(v) for v in _o]

kernel_name = "hybrid_diffusion_prefix_step"

F32 = jnp.float32


def rms_norm(x, g):
    xf = x.astype(F32)
    y = xf * lax.rsqrt(jnp.mean(xf * xf, axis=-1, keepdims=True) + NORM_EPS)
    return (y * g.astype(F32)).astype(x.dtype)


def layer_norm(x, g, b):
    xf = x.astype(F32)
    mu = jnp.mean(xf, axis=-1, keepdims=True)
    var = jnp.mean(jnp.square(xf - mu), axis=-1, keepdims=True)
    y = (xf - mu) * lax.rsqrt(var + NORM_EPS) * g.astype(F32) + b.astype(F32)
    return y.astype(x.dtype)


def rope_2d(x):
    T, d = x.shape[1], x.shape[-1]
    rows = T // GRID_W
    nf = d // 4
    row = jnp.repeat(jnp.arange(rows), GRID_W).astype(F32)
    col = jnp.tile(jnp.arange(GRID_W), rows).astype(F32)
    inv = ROPE_THETA ** (-jnp.arange(nf, dtype=F32) / nf)
    ang = jnp.stack([row[:, None] * inv, col[:, None] * inv], axis=1)
    ang = ang.reshape((T,) + (1,) * (x.ndim - 3) + (2, nf))
    cos, sin = jnp.cos(ang), jnp.sin(ang)
    xr = x.astype(F32).reshape(x.shape[:-1] + (2, 2, nf))
    x1, x2 = xr[..., 0, :], xr[..., 1, :]
    out = jnp.stack([x1 * cos - x2 * sin, x2 * cos + x1 * sin], axis=-2)
    return out.reshape(x.shape).astype(x.dtype)


def over_query_blocks(fn, q):
    b, T = q.shape[:2]
    nb = T // Q_BLOCK
    qb = jnp.moveaxis(q.reshape((b, nb, Q_BLOCK) + q.shape[2:]), 1, 0)
    ob = lax.map(fn, qb)
    return jnp.moveaxis(ob, 0, 1).reshape((b, T) + ob.shape[3:])


def diff_attend(q, k, v, lam):
    s = jnp.einsum('bqhcd,bkhcd->bhcqk', q, k).astype(F32) * (q.shape[-1] ** -0.5)
    p = jax.nn.softmax(s, axis=-1)
    w = p[:, :, 0] - lam * p[:, :, 1]
    return jnp.einsum('bhqk,bkhv->bqhv', w.astype(v.dtype), v)


def gqa_attend(q, k, v, sink=None):
    s = jnp.einsum('bqngd,bknd->bngqk', q, k).astype(F32) * (q.shape[-1] ** -0.5)
    if sink is None:
        p = jax.nn.softmax(s, axis=-1)
    else:
        sk = jnp.broadcast_to(sink.astype(F32)[None, :, :, None, None], s.shape[:-1] + (1,))
        p = jax.nn.softmax(jnp.concatenate([s, sk], axis=-1), axis=-1)[..., :-1]
    return jnp.einsum('bngqk,bknd->bqngd', p.astype(v.dtype), v)


def banded_sink_attend(q, k, v, ck, cv, sink):
    b, T, N, G, d = q.shape
    nb = T // Q_BLOCK
    L = ck.shape[1]
    scale = d ** -0.5

    def band(a):
        ab = a.reshape(b, nb, Q_BLOCK, N, d)
        ap = jnp.pad(ab, ((0, 0), (1, 1), (0, 0), (0, 0), (0, 0)))
        return jnp.concatenate([ap[:, :-2], ap[:, 1:-1], ap[:, 2:]], axis=2)

    qb = jnp.moveaxis(q.reshape(b, nb, Q_BLOCK, N, G, d), 1, 0)
    kb = jnp.moveaxis(band(k), 1, 0)
    vb = jnp.moveaxis(band(v), 1, 0)
    blk = jnp.arange(nb)[:, None, None]
    qpos = blk * Q_BLOCK + jnp.arange(Q_BLOCK)[None, :, None]
    kpos = (blk - 1) * Q_BLOCK + jnp.arange(3 * Q_BLOCK)[None, None, :]
    valid = (jnp.abs(qpos - kpos) <= WINDOW) & (kpos >= 0) & (kpos < T)

    def one(args):
        qi, ki, vi, mi = args
        s_loc = jnp.einsum('bqngd,bknd->bngqk', qi, ki).astype(F32) * scale
        s_loc = jnp.where(mi[None, None, None], s_loc, -jnp.inf)
        s_ctx = jnp.einsum('bqngd,bknd->bngqk', qi, ck).astype(F32) * scale
        sk = jnp.broadcast_to(sink.astype(F32)[None, :, :, None, None], s_ctx.shape[:-1] + (1,))
        p = jax.nn.softmax(jnp.concatenate([s_ctx, s_loc, sk], axis=-1), axis=-1)
        p_ctx = p[..., :L].astype(v.dtype)
        p_loc = p[..., L:L + 3 * Q_BLOCK].astype(v.dtype)
        return (jnp.einsum('bngqk,bknd->bqngd', p_ctx, cv)
                + jnp.einsum('bngqk,bknd->bqngd', p_loc, vi))

    ob = lax.map(one, (qb, kb, vb, valid))
    return jnp.moveaxis(ob, 0, 1).reshape(b, T, N, G, d)


def even_mix(h, w_in, conv_w, conv_b, ln_g, ln_b, lam_vec, subln_g, w_out, lam_init, ctx_kv):
    b, T, _ = h.shape
    a_u, a_g, a_z, b_q, b_k, b_v, b_z = jnp.split(h @ w_in, SPLIT_E, axis=-1)
    a = a_u * jax.nn.sigmoid(a_g)
    a = lax.conv_general_dilated(a, conv_w[:, None, :].astype(a.dtype), window_strides=(1,),
                                 padding=[(CONV_K // 2, CONV_K // 2)],
                                 dimension_numbers=('NWC', 'WIO', 'NWC'),
                                 feature_group_count=A_W) + conv_b
    a = jax.nn.silu(layer_norm(a, ln_g, ln_b)) * jax.nn.silu(a_z)
    q = b_q.reshape(b, T, H_B, 2, DH)
    k = b_k.reshape(b, T, H_B, 2, DH)
    v = b_v.reshape(b, T, H_B, 2 * DH)
    lv = lam_vec.astype(F32)
    lam = jnp.exp(jnp.sum(lv[0] * lv[1])) - jnp.exp(jnp.sum(lv[2] * lv[3])) + lam_init
    if ctx_kv is None:
        keys, vals, new_kv = k, v, (k, v)
    else:
        q, k = rope_2d(q), rope_2d(k)
        keys = jnp.concatenate([ctx_kv[0], k], axis=1)
        vals = jnp.concatenate([ctx_kv[1], v], axis=1)
        new_kv = None
    o = over_query_blocks(lambda qb: diff_attend(qb, keys, vals, lam), q)
    o = rms_norm(o, subln_g) * (1.0 - lam_init)
    o = o.reshape(b, T, B_W) * jax.nn.silu(b_z)
    return jnp.concatenate([a, o], axis=-1) @ w_out, new_kv


def odd_mix(h, w_in, q_norm, k_norm, sink, w_out, ctx_kv):
    b, T, _ = h.shape
    c_q, c_k, c_v, c_z, d_q, d_k, d_v, d_z = jnp.split(h @ w_in, SPLIT_O, axis=-1)
    cq = rms_norm(c_q.reshape(b, T, KV_C, G_C, DH), q_norm)
    ck = rms_norm(c_k.reshape(b, T, KV_C, DH), k_norm)
    cv = c_v.reshape(b, T, KV_C, DH)
    dq = d_q.reshape(b, T, KV_D, G_D, DH)
    dk = d_k.reshape(b, T, KV_D, DH)
    dv = d_v.reshape(b, T, KV_D, DH)
    sk = sink.reshape(KV_D, G_D)
    if ctx_kv is None:
        oc = over_query_blocks(lambda qb: gqa_attend(qb, ck, cv), cq)
        od = over_query_blocks(lambda qb: gqa_attend(qb, dk, dv, sk), dq)
        new_kv = (ck, cv, dk, dv)
    else:
        cck, ccv, cdk, cdv = ctx_kv
        cq, ck, dq, dk = rope_2d(cq), rope_2d(ck), rope_2d(dq), rope_2d(dk)
        keys = jnp.concatenate([cck, ck], axis=1)
        vals = jnp.concatenate([ccv, cv], axis=1)
        oc = over_query_blocks(lambda qb: gqa_attend(qb, keys, vals), cq)
        od = banded_sink_attend(dq, dk, dv, cdk, cdv, sk)
        new_kv = None
    oc = oc.reshape(b, T, C_W) * jax.nn.silu(c_z)
    od = od.reshape(b, T, D_W) * jax.nn.silu(d_z)
    return jnp.concatenate([oc, od], axis=-1) @ w_out, new_kv


def modulation(cond, w_mod, b_mod):
    m = (jax.nn.silu(cond) @ w_mod + b_mod).reshape(-1, 1, 3 * D_MODEL)
    return jnp.split(m, 3, axis=-1)


def setup_inputs(seed: int = 0) -> dict:
    key = jax.random.key(seed)
    ks = iter(jax.random.split(key, 32))
    nrm = lambda shape, s=1.0: jax.random.normal(next(ks), shape, F32) * s
    return {
        'x_prompt': nrm((BATCH, SEQ, D_MODEL)),
        'x_sample': nrm((DEC_BATCH, DEC_SEQ, D_MODEL)),
        'cache_b_k': nrm((DEC_BATCH, N_EVEN, PAST_LEN, H_B, 2, DH)),
        'cache_b_v': nrm((DEC_BATCH, N_EVEN, PAST_LEN, H_B, 2 * DH)),
        'cache_c_k': nrm((DEC_BATCH, N_ODD, PAST_LEN, KV_C, DH)),
        'cache_c_v': nrm((DEC_BATCH, N_ODD, PAST_LEN, KV_C, DH)),
        'cache_d_k': nrm((DEC_BATCH, N_ODD, PAST_LEN, KV_D, DH)),
        'cache_d_v': nrm((DEC_BATCH, N_ODD, PAST_LEN, KV_D, DH)),
        'c': nrm((DEC_BATCH, D_MODEL)),
        'c_ctx': nrm((D_MODEL,)),
        'norm_pre': 1.0 + nrm((DEPTH, D_MODEL), 0.1),
        'norm_post': 1.0 + nrm((DEPTH, D_MODEL), 0.1),
        'w_mod': nrm((DEPTH, D_MODEL, 3 * D_MODEL), 0.5 * D_MODEL ** -0.5),
        'b_mod': nrm((DEPTH, 3 * D_MODEL), 0.02),
        'w_in_even': nrm((N_EVEN, D_MODEL, IN_E), D_MODEL ** -0.5),
        'a_conv_w': nrm((N_EVEN, CONV_K, A_W), CONV_K ** -0.5),
        'a_conv_b': nrm((N_EVEN, A_W), 0.02),
        'a_ln_g': 1.0 + nrm((N_EVEN, A_W), 0.1),
        'a_ln_b': nrm((N_EVEN, A_W), 0.02),
        'b_lambda': nrm((N_EVEN, 4, DH), 0.1),
        'b_subln_g': 1.0 + nrm((N_EVEN, 2 * DH), 0.1),
        'w_out_even': nrm((N_EVEN, OUT_E, D_MODEL), OUT_E ** -0.5),
        'w_in_odd': nrm((N_ODD, D_MODEL, IN_O), D_MODEL ** -0.5),
        'c_q_norm': 1.0 + nrm((N_ODD, DH), 0.1),
        'c_k_norm': 1.0 + nrm((N_ODD, DH), 0.1),
        'd_sink': nrm((N_ODD, H_D), 0.5),
        'w_out_odd': nrm((N_ODD, OUT_O, D_MODEL), OUT_O ** -0.5),
    }


def reference(x_prompt, x_sample, cache_b_k, cache_b_v, cache_c_k, cache_c_v, cache_d_k, cache_d_v,
              c, c_ctx, norm_pre, norm_post, w_mod, b_mod, w_in_even, a_conv_w, a_conv_b, a_ln_g,
              a_ln_b, b_lambda, b_subln_g, w_out_even, w_in_odd, c_q_norm, c_k_norm, d_sink,
              w_out_odd):
    xp, xs = x_prompt, x_sample
    nbk, nbv, nck, ncv, ndk, ndv = [], [], [], [], [], []
    for l in range(DEPTH):
        i = l // 2
        sh_p, sc_p, g_p = modulation(c_ctx, w_mod[l], b_mod[l])
        sh_s, sc_s, g_s = modulation(c, w_mod[l], b_mod[l])
        hp = rms_norm(xp, norm_pre[l]) * (1.0 + sc_p) + sh_p
        hs = rms_norm(xs, norm_pre[l]) * (1.0 + sc_s) + sh_s
        if l % 2 == 0:
            lam_init = 0.8 - 0.6 * math.exp(-0.3 * l)
            args = (w_in_even[i], a_conv_w[i], a_conv_b[i], a_ln_g[i], a_ln_b[i],
                    b_lambda[i], b_subln_g[i], w_out_even[i], lam_init)
            op, (kp, vp) = even_mix(hp, *args, None)
            os_, _ = even_mix(hs, *args, (cache_b_k[:, i], cache_b_v[:, i]))
            nbk.append(kp)
            nbv.append(vp)
        else:
            args = (w_in_odd[i], c_q_norm[i], c_k_norm[i], d_sink[i], w_out_odd[i])
            op, (ck, cv, dk, dv) = odd_mix(hp, *args, None)
            os_, _ = odd_mix(hs, *args, (cache_c_k[:, i], cache_c_v[:, i],
                                         cache_d_k[:, i], cache_d_v[:, i]))
            nck.append(ck)
            ncv.append(cv)
            ndk.append(dk)
            ndv.append(dv)
        xp = xp + g_p * rms_norm(op, norm_post[l])
        xs = xs + g_s * rms_norm(os_, norm_post[l])
    return (xp, xs, jnp.stack(nbk, axis=1), jnp.stack(nbv, axis=1), jnp.stack(nck, axis=1),
            jnp.stack(ncv, axis=1), jnp.stack(ndk, axis=1), jnp.stack(ndv, axis=1))
```

```python
import functools
import math

import jax
import jax.numpy as jnp
from jax import lax
from jax.experimental import pallas as pl
from jax.experimental.pallas import tpu as pltpu

F32 = jnp.float32
BF16 = jnp.bfloat16

D_MODEL = 1024
BATCH = 16
SEQ = 256
DEPTH = 4
DEC_BATCH = 2
DEC_SEQ = 1024
PAST_LEN = 512
GRID_W = 64
ROPE_THETA = 10000.0
NORM_EPS = 1e-6
DH = 64
A_W = 512
CONV_K = 31
H_B = 4
B_W = 512
C_W = 512
KV_W = 128
D_W = 512
WINDOW = 128
IN_E = 3 * A_W + 4 * B_W
IN_O = 2 * C_W + 2 * KV_W + 2 * D_W + 2 * KV_W
QK_SCALE = DH ** -0.5

LANES = 128
SUBLANES = 8
VMEM_LIMIT = 56 * 1024 * 1024

TM = 512
TA = 256
TQ = 256
HALO = 16


def _params(*sem):
    return pltpu.CompilerParams(dimension_semantics=sem, vmem_limit_bytes=VMEM_LIMIT)


def _silu(x):
    return x * jax.nn.sigmoid(x)


def _dot(a, b):
    return jnp.dot(a, b, preferred_element_type=F32)


def _dot_nt(a, b):
    return lax.dot_general(a, b, (((1,), (1,)), ((), ())), preferred_element_type=F32)


def _mod_kernel(cond_ref, w_ref, b_ref, o_ref):
    a = _silu(cond_ref[...]).astype(BF16)
    o_ref[...] = _dot(a, w_ref[...].astype(BF16)) + b_ref[...]


def _modulation(cond8, w_mod, b_mod):
    nblk = 3
    return pl.pallas_call(
        _mod_kernel,
        grid=(DEPTH, nblk),
        in_specs=[
            pl.BlockSpec((SUBLANES, D_MODEL), lambda l, j: (0, 0)),
            pl.BlockSpec((None, D_MODEL, D_MODEL), lambda l, j: (l, 0, j)),
            pl.BlockSpec((None, 1, D_MODEL), lambda l, j: (l, 0, j)),
        ],
        out_specs=pl.BlockSpec((None, SUBLANES, D_MODEL), lambda l, j: (l, 0, j)),
        out_shape=jax.ShapeDtypeStruct((DEPTH, SUBLANES, 3 * D_MODEL), F32),
        compiler_params=_params("arbitrary", "arbitrary"),
        name="modulation",
    )(cond8, w_mod, b_mod.reshape(DEPTH, 1, 3 * D_MODEL))


def _pre_norm(x_ref, g_ref, mod_ref):
    x = x_ref[...]
    ms = jnp.mean(x * x, axis=-1, keepdims=True)
    mod = mod_ref[...]
    sh = mod[:, :D_MODEL]
    sc = mod[:, D_MODEL:2 * D_MODEL]
    h = (x * lax.rsqrt(ms + NORM_EPS) * g_ref[...]) * (1.0 + sc) + sh
    return h.astype(BF16)


def _rope(x, cos, sin_signed):
    w = x.shape[-1]
    lane = lax.broadcasted_iota(jnp.int32, (1, w), 1)
    first = (lane % 32) < 16
    partner = jnp.where(first, pltpu.roll(x, w - 16, 1), pltpu.roll(x, 16, 1))
    return x * cos + partner * sin_signed


def _in_even_kernel(x_ref, g_ref, mod_ref, w_ref, *rest, rope):
    if rope:
        cos_ref, sin_ref, ug_ref, az_ref, q_ref, k_ref, v_ref, bz_ref = rest
    else:
        ug_ref, az_ref, q_ref, k_ref, v_ref, bz_ref = rest
    hb = _pre_norm(x_ref, g_ref, mod_ref)
    ug_ref[...] = _dot(hb, w_ref[:, 0:2 * A_W])
    az_ref[...] = _dot(hb, w_ref[:, 2 * A_W:3 * A_W])
    o = 3 * A_W
    q = _dot(hb, w_ref[:, o:o + B_W])
    k = _dot(hb, w_ref[:, o + B_W:o + 2 * B_W])
    if rope:
        cos = cos_ref[...]
        sin = sin_ref[...]
        q = _rope(q, cos, sin)
        k = _rope(k, cos, sin)
    q_ref[...] = (q * QK_SCALE).astype(BF16)
    k_ref[...] = k
    v_ref[...] = _dot(hb, w_ref[:, o + 2 * B_W:o + 3 * B_W])
    bz_ref[...] = _dot(hb, w_ref[:, o + 3 * B_W:o + 4 * B_W])


def _group_mean_sq(x, width):
    xx = x * x
    hi = xx.astype(BF16)
    lo = (xx - hi.astype(F32)).astype(BF16)
    r = lax.broadcasted_iota(jnp.int32, (width, width), 0) // DH
    c = lax.broadcasted_iota(jnp.int32, (width, width), 1) // DH
    g = jnp.where(r == c, 1.0, 0.0).astype(BF16)
    return (_dot(hi, g) + _dot(lo, g)) * (1.0 / DH)


def _head_rms(x, gain):
    return x * lax.rsqrt(_group_mean_sq(x, x.shape[-1]) + NORM_EPS) * gain


def _in_odd_kernel(x_ref, g_ref, mod_ref, w_ref, qn_ref, kn_ref, *rest, rope):
    if rope:
        cos_ref, sin_ref = rest[:2]
        rest = rest[2:]
    cq_ref, ck_ref, cv_ref, cz_ref, dq_ref, dk_ref, dv_ref, dz_ref = rest
    hb = _pre_norm(x_ref, g_ref, mod_ref)
    o = 0
    cq = _head_rms(_dot(hb, w_ref[:, o:o + C_W]), qn_ref[...])
    o += C_W
    ck = _head_rms(_dot(hb, w_ref[:, o:o + KV_W]), kn_ref[...])
    o += KV_W
    cv_ref[...] = _dot(hb, w_ref[:, o:o + KV_W])
    o += KV_W
    cz_ref[...] = _dot(hb, w_ref[:, o:o + C_W])
    o += C_W
    dq = _dot(hb, w_ref[:, o:o + D_W])
    o += D_W
    dk = _dot(hb, w_ref[:, o:o + KV_W])
    o += KV_W
    dv_ref[...] = _dot(hb, w_ref[:, o:o + KV_W])
    o += KV_W
    dz_ref[...] = _dot(hb, w_ref[:, o:o + D_W])
    if rope:
        cos = cos_ref[...]
        sin = sin_ref[...]
        cq = _rope(cq, cos, sin)
        dq = _rope(dq, cos, sin)
        ck = _rope(ck, cos[:, :KV_W], sin[:, :KV_W])
        dk = _rope(dk, cos[:, :KV_W], sin[:, :KV_W])
    cq_ref[...] = (cq * QK_SCALE).astype(BF16)
    dq_ref[...] = (dq * QK_SCALE).astype(BF16)
    ck_ref[...] = ck
    dk_ref[...] = dk


def _in_proj(x, g_pre, mod4, layer, w_bf, *, sample, extra=(), tables=None):
    n = x.shape[0]
    even = layer % 2 == 0
    tiles_per_seq = DEC_SEQ // TM
    if sample:
        mod_map = lambda i: (layer, 1 + i // tiles_per_seq, 0, 0)
    else:
        mod_map = lambda i: (layer, 0, 0, 0)
    row = lambda i: (i, 0)
    fixed = lambda i: (0, 0)
    in_w = w_bf.shape[1]
    in_specs = [
        pl.BlockSpec((TM, D_MODEL), row),
        pl.BlockSpec((1, D_MODEL), fixed),
        pl.BlockSpec((None, None, 1, 3 * D_MODEL), mod_map),
        pl.BlockSpec((D_MODEL, in_w), fixed),
    ]
    args = [x, g_pre, mod4, w_bf]
    for e in extra:
        in_specs.append(pl.BlockSpec(e.shape, fixed))
        args.append(e)
    if sample:
        tab = lambda i: (i % tiles_per_seq, 0)
        in_specs += [pl.BlockSpec((TM, B_W), tab), pl.BlockSpec((TM, B_W), tab)]
        args += list(tables)
    if even:
        widths = [(2 * A_W, F32), (A_W, F32), (B_W, BF16), (B_W, F32), (B_W, F32), (B_W, F32)]
        body = functools.partial(_in_even_kernel, rope=sample)
    else:
        widths = [(C_W, BF16), (KV_W, F32), (KV_W, F32), (C_W, F32),
                  (D_W, BF16), (KV_W, F32), (KV_W, F32), (D_W, F32)]
        body = functools.partial(_in_odd_kernel, rope=sample)
    return pl.pallas_call(
        body,
        grid=(n // TM,),
        in_specs=in_specs,
        out_specs=[pl.BlockSpec((TM, w), row) for w, _ in widths],
        out_shape=[jax.ShapeDtypeStruct((n, w), dt) for w, dt in widths],
        compiler_params=_params("arbitrary"),
        name=f"in_proj_{'s' if sample else 'p'}{layer}",
    )(*args)


ROW_CHUNK = 64


def _conv_kernel(ug_ref, prev_ref, next_ref, az_ref, cw_ref, cb_ref, lg_ref, lb_ref, o_ref, pad_ref,
                 *, tiles_per_seq):
    i = pl.program_id(0)
    pos = i % tiles_per_seq
    has_prev = pos != 0
    has_next = pos != tiles_per_seq - 1

    def glu(ref):
        v = ref[...]
        return v[:, :A_W] * jax.nn.sigmoid(v[:, A_W:])

    pad_ref[0:HALO, :] = jnp.where(has_prev, glu(prev_ref), 0.0)
    pad_ref[HALO:HALO + TA, :] = glu(ug_ref)
    pad_ref[HALO + TA:HALO + TA + HALO, :] = jnp.where(has_next, glu(next_ref), 0.0)

    base = HALO - CONV_K // 2
    for c0 in range(0, A_W, LANES):
        cs = slice(c0, c0 + LANES)
        for r0 in range(0, TA, ROW_CHUNK):
            acc = jnp.zeros((ROW_CHUNK, LANES), F32) + cb_ref[:, cs]
            for k in range(CONV_K):
                s = r0 + k + base
                acc = acc + pad_ref[s:s + ROW_CHUNK, cs] * cw_ref[k:k + 1, cs]
            o_ref[r0:r0 + ROW_CHUNK, cs] = acc

    a = o_ref[...]
    mu = jnp.mean(a, axis=-1, keepdims=True)
    d = a - mu
    var = jnp.mean(d * d, axis=-1, keepdims=True)
    y = d * lax.rsqrt(var + NORM_EPS) * lg_ref[...] + lb_ref[...]
    o_ref[...] = _silu(y) * _silu(az_ref[...])


def _conv_branch(ug, az, conv_w, conv_b, ln_g, ln_b, *, seq_len):
    n = ug.shape[0]
    tiles_per_seq = seq_len // TA
    hb = TA // HALO
    last = n // HALO - 1
    row = lambda i: (i, 0)
    fixed = lambda i: (0, 0)
    cw = jnp.zeros((4 * SUBLANES, A_W), F32).at[:CONV_K].set(conv_w)
    return pl.pallas_call(
        functools.partial(_conv_kernel, tiles_per_seq=tiles_per_seq),
        grid=(n // TA,),
        in_specs=[
            pl.BlockSpec((TA, 2 * A_W), row),
            pl.BlockSpec((HALO, 2 * A_W), lambda i: (jnp.maximum(i * hb - 1, 0), 0)),
            pl.BlockSpec((HALO, 2 * A_W), lambda i: (jnp.minimum((i + 1) * hb, last), 0)),
            pl.BlockSpec((TA, A_W), row),
            pl.BlockSpec((4 * SUBLANES, A_W), fixed),
            pl.BlockSpec((1, A_W), fixed),
            pl.BlockSpec((1, A_W), fixed),
            pl.BlockSpec((1, A_W), fixed),
        ],
        out_specs=pl.BlockSpec((TA, A_W), row),
        out_shape=jax.ShapeDtypeStruct((n, A_W), F32),
        scratch_shapes=[pltpu.VMEM((TA + 2 * HALO, A_W), F32)],
        compiler_params=_params("arbitrary"),
        name=f"conv_branch_{seq_len}",
    )(ug, ug, ug, az, cw, conv_b.reshape(1, A_W), ln_g.reshape(1, A_W), ln_b.reshape(1, A_W))


def _softmax_parts(segs, extra=None):
    m = None
    for s in segs:
        mi = jnp.max(s, axis=-1, keepdims=True)
        m = mi if m is None else jnp.maximum(m, mi)
    if extra is not None:
        m = jnp.maximum(m, extra)
    es = [jnp.exp(s - m) for s in segs]
    den = None
    for e in es:
        di = jnp.sum(e, axis=-1, keepdims=True)
        den = di if den is None else den + di
    if extra is not None:
        den = den + jnp.exp(extra - m)
    return es, den


def _half_masks(dtype):
    lane = lax.broadcasted_iota(jnp.int32, (1, LANES), 1)
    lo = jnp.where(lane < DH, 1.0, 0.0).astype(dtype)
    hi = jnp.where(lane < DH, 0.0, 1.0).astype(dtype)
    return lo, hi


def _diff_attn(q_ref, kvs, z_ref, lam_ref, g_ref, o_ref, *, lam_init):
    lv = lam_ref[...]
    lam = (jnp.exp(jnp.sum(lv[0:1] * lv[1:2], axis=-1, keepdims=True))
           - jnp.exp(jnp.sum(lv[2:3] * lv[3:4], axis=-1, keepdims=True)) + lam_init)
    masks = _half_masks(BF16)
    for h in range(H_B):
        cs = slice(h * LANES, (h + 1) * LANES)
        qh = q_ref[:, cs]
        ks = [k_ref[:, cs].astype(BF16) for k_ref, _ in kvs]
        vs = [v_ref[:, cs].astype(BF16) for _, v_ref in kvs]
        ws = None
        for c in range(2):
            qc = qh * masks[c]
            es, den = _softmax_parts([_dot_nt(qc, kk) for kk in ks])
            r = 1.0 / den
            if c == 0:
                ws = [e * r for e in es]
            else:
                r = r * lam
                ws = [w - e * r for w, e in zip(ws, es)]
        o = None
        for w, v in zip(ws, vs):
            oi = _dot(w.astype(BF16), v)
            o = oi if o is None else o + oi
        ms = jnp.mean(o * o, axis=-1, keepdims=True)
        o = (o * lax.rsqrt(ms + NORM_EPS) * g_ref[...]) * (1.0 - lam_init)
        o_ref[:, cs] = o * _silu(z_ref[:, cs])


def _diff_prompt_kernel(q_ref, k_ref, v_ref, z_ref, lam_ref, g_ref, o_ref, *, lam_init):
    _diff_attn(q_ref, [(k_ref, v_ref)], z_ref, lam_ref, g_ref, o_ref, lam_init=lam_init)


def _diff_sample_kernel(q_ref, k_ref, v_ref, ck_ref, cv_ref, z_ref, lam_ref, g_ref, o_ref, *, lam_init):
    _diff_attn(q_ref, [(ck_ref, cv_ref), (k_ref, v_ref)], z_ref, lam_ref, g_ref, o_ref,
               lam_init=lam_init)


def _diff_branch_prompt(q, k, v, z, lam_vec, subln_g, lam_init):
    n = q.shape[0]
    row = lambda i: (i, 0)
    fixed = lambda i: (0, 0)
    blk = pl.BlockSpec((SEQ, B_W), row)
    return pl.pallas_call(
        functools.partial(_diff_prompt_kernel, lam_init=lam_init),
        grid=(n // SEQ,),
        in_specs=[blk, blk, blk, blk,
                  pl.BlockSpec((4, DH), fixed), pl.BlockSpec((1, 2 * DH), fixed)],
        out_specs=blk,
        out_shape=jax.ShapeDtypeStruct((n, B_W), F32),
        compiler_params=_params("arbitrary"),
        name="diff_attn_prompt",
    )(q, k, v, z, lam_vec, subln_g.reshape(1, 2 * DH))


def _diff_branch_sample(q, k, v, z, cache_k, cache_v, layer_idx, lam_vec, subln_g, lam_init):
    n = q.shape[0]
    nq = DEC_SEQ // TQ
    qrow = lambda b, t: (b * nq + t, 0)
    seq = lambda b, t: (b, 0)
    cache = lambda b, t: (b, layer_idx, 0, 0)
    fixed = lambda b, t: (0, 0)
    return pl.pallas_call(
        functools.partial(_diff_sample_kernel, lam_init=lam_init),
        grid=(DEC_BATCH, nq),
        in_specs=[
            pl.BlockSpec((TQ, B_W), qrow),
            pl.BlockSpec((DEC_SEQ, B_W), seq),
            pl.BlockSpec((DEC_SEQ, B_W), seq),
            pl.BlockSpec((None, None, PAST_LEN, B_W), cache),
            pl.BlockSpec((None, None, PAST_LEN, B_W), cache),
            pl.BlockSpec((TQ, B_W), qrow),
            pl.BlockSpec((4, DH), fixed),
            pl.BlockSpec((1, 2 * DH), fixed),
        ],
        out_specs=pl.BlockSpec((TQ, B_W), qrow),
        out_shape=jax.ShapeDtypeStruct((n, B_W), F32),
        compiler_params=_params("arbitrary", "arbitrary"),
        name="diff_attn_sample",
    )(q, k, v, cache_k, cache_v, z, lam_vec, subln_g.reshape(1, 2 * DH))


def _dup_halves(x):
    lane = lax.broadcasted_iota(jnp.int32, (1, LANES), 1)
    lo = lane < DH
    rolled = pltpu.roll(x, DH, 1)
    return (jnp.where(lo, x, rolled).astype(BF16), jnp.where(lo, rolled, x).astype(BF16))


def _gqa(q_ref, segs, z_ref, o_ref, sink_ref=None):
    masks = _half_masks(BF16)
    lane = lax.broadcasted_iota(jnp.int32, (1, LANES), 1)
    lo = lane < DH
    k2 = [_dup_halves(k) for k, _, _ in segs]
    v2 = [_dup_halves(v) for _, v, _ in segs]
    for n in range(2):
        for p in range(2):
            cs = slice(n * 2 * LANES + p * LANES, n * 2 * LANES + (p + 1) * LANES)
            qp = q_ref[:, cs]
            outs = []
            for hh in range(2):
                j = n * 4 + p * 2 + hh
                qm = qp * masks[hh]
                ss = []
                for si, (_, _, valid) in enumerate(segs):
                    s = _dot_nt(qm, k2[si][n])
                    if valid is not None:
                        s = jnp.where(valid, s, -jnp.inf)
                    ss.append(s)
                extra = None if sink_ref is None else sink_ref[:, j:j + 1]
                es, den = _softmax_parts(ss, extra)
                o = None
                for si, e in enumerate(es):
                    oi = _dot(e.astype(BF16), v2[si][n])
                    o = oi if o is None else o + oi
                outs.append(o * (1.0 / den))
            o_pair = jnp.where(lo, outs[0], outs[1])
            o_ref[:, cs] = o_pair * _silu(z_ref[:, cs])


def _gqa_prompt_kernel(cq_ref, ck_ref, cv_ref, cz_ref, dq_ref, dk_ref, dv_ref, dz_ref, sink_ref,
                       oc_ref, od_ref):
    _gqa(cq_ref, [(ck_ref[...], cv_ref[...], None)], cz_ref, oc_ref)
    _gqa(dq_ref, [(dk_ref[...], dv_ref[...], None)], dz_ref, od_ref, sink_ref)


def _gqa_sample_kernel(cq_ref, ck_ref, cv_ref, cck_ref, ccv_ref, cz_ref,
                       dq_ref, dk_ref, dv_ref, cdk_ref, cdv_ref, dz_ref, sink_ref,
                       oc_ref, od_ref):
    t = pl.program_id(1)
    _gqa(cq_ref, [(cck_ref[...], ccv_ref[...], None), (ck_ref[...], cv_ref[...], None)], cz_ref, oc_ref)
    span = 2 * TQ
    t0 = t * TQ
    ws = pl.multiple_of(jnp.clip(t0 - WINDOW, 0, DEC_SEQ - span), WINDOW)
    qpos = t0 + lax.broadcasted_iota(jnp.int32, (TQ, span), 0)
    kpos = ws + lax.broadcasted_iota(jnp.int32, (TQ, span), 1)
    valid = jnp.abs(qpos - kpos) <= WINDOW
    _gqa(dq_ref,
         [(cdk_ref[...], cdv_ref[...], None),
          (dk_ref[pl.ds(ws, span), :], dv_ref[pl.ds(ws, span), :], valid)],
         dz_ref, od_ref, sink_ref)


def _gqa_branch_prompt(cq, ck, cv, cz, dq, dk, dv, dz, sink):
    n = cq.shape[0]
    row = lambda i: (i, 0)
    wide = pl.BlockSpec((SEQ, C_W), row)
    kv = pl.BlockSpec((SEQ, KV_W), row)
    return pl.pallas_call(
        _gqa_prompt_kernel,
        grid=(n // SEQ,),
        in_specs=[wide, kv, kv, wide, wide, kv, kv, wide, pl.BlockSpec((1, 8), lambda i: (0, 0))],
        out_specs=[wide, wide],
        out_shape=[jax.ShapeDtypeStruct((n, C_W), F32), jax.ShapeDtypeStruct((n, D_W), F32)],
        compiler_params=_params("arbitrary"),
        name="gqa_prompt",
    )(cq, ck, cv, cz, dq, dk, dv, dz, sink)


def _gqa_branch_sample(cq, ck, cv, cz, dq, dk, dv, dz, sink, cck, ccv, cdk, cdv, layer_idx):
    n = cq.shape[0]
    nq = DEC_SEQ // TQ
    qrow = lambda b, t: (b * nq + t, 0)
    seq = lambda b, t: (b, 0)
    cache = lambda b, t: (b, layer_idx, 0, 0)
    wide = pl.BlockSpec((TQ, C_W), qrow)
    kv = pl.BlockSpec((DEC_SEQ, KV_W), seq)
    cb = pl.BlockSpec((None, None, PAST_LEN, KV_W), cache)
    return pl.pallas_call(
        _gqa_sample_kernel,
        grid=(DEC_BATCH, nq),
        in_specs=[wide, kv, kv, cb, cb, wide, wide, kv, kv, cb, cb, wide,
                  pl.BlockSpec((1, 8), lambda b, t: (0, 0))],
        out_specs=[wide, wide],
        out_shape=[jax.ShapeDtypeStruct((n, C_W), F32), jax.ShapeDtypeStruct((n, D_W), F32)],
        compiler_params=_params("arbitrary", "arbitrary"),
        name="gqa_sample",
    )(cq, ck, cv, cck, ccv, cz, dq, dk, dv, cdk, cdv, dz, sink)


def _out_kernel(ma_ref, mb_ref, x_ref, w_ref, g_ref, mod_ref, o_ref):
    half = ma_ref.shape[-1]
    o = _dot(ma_ref[...].astype(BF16), w_ref[0:half, :]) + _dot(mb_ref[...].astype(BF16), w_ref[half:, :])
    ms = jnp.mean(o * o, axis=-1, keepdims=True)
    r = o * lax.rsqrt(ms + NORM_EPS) * g_ref[...]
    gate = mod_ref[...][:, 2 * D_MODEL:]
    o_ref[...] = x_ref[...] + gate * r


def _out_proj(ma, mb, x, w_bf, g_post, mod4, layer, *, sample):
    n = x.shape[0]
    tiles_per_seq = DEC_SEQ // TM
    if sample:
        mod_map = lambda i: (layer, 1 + i // tiles_per_seq, 0, 0)
    else:
        mod_map = lambda i: (layer, 0, 0, 0)
    row = lambda i: (i, 0)
    fixed = lambda i: (0, 0)
    half = ma.shape[1]
    return pl.pallas_call(
        _out_kernel,
        grid=(n // TM,),
        in_specs=[
            pl.BlockSpec((TM, half), row),
            pl.BlockSpec((TM, half), row),
            pl.BlockSpec((TM, D_MODEL), row),
            pl.BlockSpec((2 * half, D_MODEL), fixed),
            pl.BlockSpec((1, D_MODEL), fixed),
            pl.BlockSpec((None, None, 1, 3 * D_MODEL), mod_map),
        ],
        out_specs=pl.BlockSpec((TM, D_MODEL), row),
        out_shape=jax.ShapeDtypeStruct((n, D_MODEL), F32),
        compiler_params=_params("arbitrary"),
        name=f"out_proj_{'s' if sample else 'p'}{layer}",
    )(ma, mb, x, w_bf, g_post, mod4)


def _rope_tables():
    nf = DH // 4
    t = jnp.arange(DEC_SEQ)
    row = (t // GRID_W).astype(F32)
    col = (t % GRID_W).astype(F32)
    inv = ROPE_THETA ** (-jnp.arange(nf, dtype=F32) / nf)
    d = jnp.arange(DH)
    axis = d // (2 * nf)
    second = (d % (2 * nf)) // nf
    f = d % nf
    pos = jnp.where(axis[None, :] == 0, row[:, None], col[:, None])
    ang = pos * inv[f][None, :]
    cos = jnp.cos(ang)
    sin = jnp.where(second[None, :] == 0, -jnp.sin(ang), jnp.sin(ang))
    reps = B_W // DH
    return jnp.tile(cos, (1, reps)), jnp.tile(sin, (1, reps))


def kernel(x_prompt, x_sample, cache_b_k, cache_b_v, cache_c_k, cache_c_v, cache_d_k, cache_d_v, c, c_ctx, norm_pre, norm_post, w_mod, b_mod, w_in_even, a_conv_w, a_conv_b, a_ln_g, a_ln_b, b_lambda, b_subln_g, w_out_even, w_in_odd, c_q_norm, c_k_norm, d_sink, w_out_odd):
    n_even = (DEPTH + 1) // 2
    n_odd = DEPTH // 2
    cond8 = jnp.zeros((SUBLANES, D_MODEL), F32).at[0].set(c_ctx).at[1:1 + DEC_BATCH].set(c)
    mod4 = _modulation(cond8, w_mod, b_mod).reshape(DEPTH, SUBLANES, 1, 3 * D_MODEL)
    tables = _rope_tables()

    xp = x_prompt.reshape(BATCH * SEQ, D_MODEL)
    xs = x_sample.reshape(DEC_BATCH * DEC_SEQ, D_MODEL)
    cbk = cache_b_k.reshape(DEC_BATCH, n_even, PAST_LEN, B_W)
    cbv = cache_b_v.reshape(DEC_BATCH, n_even, PAST_LEN, B_W)
    cck = cache_c_k.reshape(DEC_BATCH, n_odd, PAST_LEN, KV_W)
    ccv = cache_c_v.reshape(DEC_BATCH, n_odd, PAST_LEN, KV_W)
    cdk = cache_d_k.reshape(DEC_BATCH, n_odd, PAST_LEN, KV_W)
    cdv = cache_d_v.reshape(DEC_BATCH, n_odd, PAST_LEN, KV_W)

    nbk, nbv, nck, ncv, ndk, ndv = [], [], [], [], [], []
    for l in range(DEPTH):
        i = l // 2
        g_pre = norm_pre[l].reshape(1, D_MODEL)
        g_post = norm_post[l].reshape(1, D_MODEL)
        if l % 2 == 0:
            lam_init = 0.8 - 0.6 * math.exp(-0.3 * l)
            w_in = w_in_even[i].astype(BF16)
            w_out = w_out_even[i].astype(BF16)
            conv = (a_conv_w[i], a_conv_b[i], a_ln_g[i], a_ln_b[i])
            ug, az, q, k, v, bz = _in_proj(xp, g_pre, mod4, l, w_in, sample=False)
            ma = _conv_branch(ug, az, *conv, seq_len=SEQ)
            mb = _diff_branch_prompt(q, k, v, bz, b_lambda[i], b_subln_g[i], lam_init)
            xp = _out_proj(ma, mb, xp, w_out, g_post, mod4, l, sample=False)
            nbk.append(k.reshape(BATCH, SEQ, H_B, 2, DH))
            nbv.append(v.reshape(BATCH, SEQ, H_B, 2 * DH))
            ug, az, q, k, v, bz = _in_proj(xs, g_pre, mod4, l, w_in, sample=True, tables=tables)
            ma = _conv_branch(ug, az, *conv, seq_len=DEC_SEQ)
            mb = _diff_branch_sample(q, k, v, bz, cbk, cbv, i, b_lambda[i], b_subln_g[i], lam_init)
            xs = _out_proj(ma, mb, xs, w_out, g_post, mod4, l, sample=True)
        else:
            w_in = w_in_odd[i].astype(BF16)
            w_out = w_out_odd[i].astype(BF16)
            qn = jnp.tile(c_q_norm[i], C_W // DH).reshape(1, C_W)
            kn = jnp.tile(c_k_norm[i], KV_W // DH).reshape(1, KV_W)
            sink = d_sink[i].reshape(1, 8)
            cq, ck, cv, cz, dq, dk, dv, dz = _in_proj(xp, g_pre, mod4, l, w_in, sample=False,
                                                      extra=(qn, kn))
            mc, md = _gqa_branch_prompt(cq, ck, cv, cz, dq, dk, dv, dz, sink)
            xp = _out_proj(mc, md, xp, w_out, g_post, mod4, l, sample=False)
            nck.append(ck.reshape(BATCH, SEQ, 2, DH))
            ncv.append(cv.reshape(BATCH, SEQ, 2, DH))
            ndk.append(dk.reshape(BATCH, SEQ, 2, DH))
            ndv.append(dv.reshape(BATCH, SEQ, 2, DH))
            cq, ck, cv, cz, dq, dk, dv, dz = _in_proj(xs, g_pre, mod4, l, w_in, sample=True,
                                                      extra=(qn, kn), tables=tables)
            mc, md = _gqa_branch_sample(cq, ck, cv, cz, dq, dk, dv, dz, sink, cck, ccv, cdk, cdv, i)
            xs = _out_proj(mc, md, xs, w_out, g_post, mod4, l, sample=True)

    return (xp.reshape(BATCH, SEQ, D_MODEL), xs.reshape(DEC_BATCH, DEC_SEQ, D_MODEL),
            jnp.stack(nbk, axis=1), jnp.stack(nbv, axis=1), jnp.stack(nck, axis=1),
            jnp.stack(ncv, axis=1), jnp.stack(ndk, axis=1), jnp.stack(ndv, axis=1))
```

```python
import functools
import math

import jax
import jax.numpy as jnp
from jax import lax
from jax.experimental import pallas as pl
from jax.experimental.pallas import tpu as pltpu

F32 = jnp.float32
BF16 = jnp.bfloat16

D_MODEL = 1024
BATCH = 16
SEQ = 256
DEPTH = 4
DEC_BATCH = 2
DEC_SEQ = 1024
PAST_LEN = 512
GRID_W = 64
ROPE_THETA = 10000.0
NORM_EPS = 1e-6
DH = 64
A_W = 512
CONV_K = 31
H_B = 4
B_W = 512
C_W = 512
KV_W = 128
D_W = 512
WINDOW = 128
QK_SCALE = DH ** -0.5

LANES = 128
SUBLANES = 8
VMEM_LIMIT = 56 * 1024 * 1024

TM = 512
TA = 256
TQ = 256
HALO = 16
ROW_CHUNK = 64


def _params(*sem):
    return pltpu.CompilerParams(dimension_semantics=sem, vmem_limit_bytes=VMEM_LIMIT)


def _silu(x):
    return x * jax.nn.sigmoid(x)


def _dot(a, b):
    return jnp.dot(a, b, preferred_element_type=F32)


def _dot_nt(a, b):
    return lax.dot_general(a, b, (((1,), (1,)), ((), ())), preferred_element_type=F32)


def _mod_kernel(cond_ref, w_ref, b_ref, o_ref):
    a = _silu(cond_ref[...]).astype(BF16)
    o_ref[...] = _dot(a, w_ref[...].astype(BF16)) + b_ref[...]


def _modulation(cond8, w_mod, b_mod):
    nblk = 3
    return pl.pallas_call(
        _mod_kernel,
        grid=(DEPTH, nblk),
        in_specs=[
            pl.BlockSpec((SUBLANES, D_MODEL), lambda l, j: (0, 0)),
            pl.BlockSpec((None, D_MODEL, D_MODEL), lambda l, j: (l, 0, j)),
            pl.BlockSpec((None, 1, D_MODEL), lambda l, j: (l, 0, j)),
        ],
        out_specs=pl.BlockSpec((None, SUBLANES, D_MODEL), lambda l, j: (l, 0, j)),
        out_shape=jax.ShapeDtypeStruct((DEPTH, SUBLANES, 3 * D_MODEL), F32),
        compiler_params=_params("arbitrary", "arbitrary"),
        name="modulation",
    )(cond8, w_mod, b_mod.reshape(DEPTH, 1, 3 * D_MODEL))


def _pre_norm(x_ref, g_ref, mod_ref):
    x = x_ref[...]
    ms = jnp.mean(x * x, axis=-1, keepdims=True)
    mod = mod_ref[...]
    sh = mod[:, :D_MODEL]
    sc = mod[:, D_MODEL:2 * D_MODEL]
    h = (x * lax.rsqrt(ms + NORM_EPS) * g_ref[...]) * (1.0 + sc) + sh
    return h.astype(BF16)


def _rope(x, cos, sin_signed):
    w = x.shape[-1]
    lane = lax.broadcasted_iota(jnp.int32, (1, w), 1)
    first = (lane % 32) < 16
    partner = jnp.where(first, pltpu.roll(x, w - 16, 1), pltpu.roll(x, 16, 1))
    return x * cos + partner * sin_signed


def _rope_t(x, cos_t, sin_t):
    r = x.shape[0]
    row = lax.broadcasted_iota(jnp.int32, (r, 1), 0)
    first = (row % 32) < 16
    partner = jnp.where(first, pltpu.roll(x, r - 16, 0), pltpu.roll(x, 16, 0))
    return x * cos_t + partner * sin_t


def _store_chunks(ref, xt):
    for c in range(xt.shape[1] // LANES):
        ref[c] = xt[:, c * LANES:(c + 1) * LANES]


def _in_even_kernel(x_ref, g_ref, mod_ref, w_ref, wqt_ref, *rest, sample):
    if sample:
        wvt_ref, cos_ref, sin_ref, cost_ref, sint_ref, ug_ref, az_ref, qt_ref, k_ref, vt_ref, bz_ref = rest
    else:
        ug_ref, az_ref, qt_ref, k_ref, v_ref, bz_ref = rest
    hb = _pre_norm(x_ref, g_ref, mod_ref)
    ug_ref[...] = _dot(hb, w_ref[:, 0:2 * A_W])
    az_ref[...] = _dot(hb, w_ref[:, 2 * A_W:3 * A_W])
    o = 3 * A_W
    qt = _dot_nt(wqt_ref[...], hb)
    k = _dot(hb, w_ref[:, o + B_W:o + 2 * B_W])
    bz_ref[...] = _dot(hb, w_ref[:, o + 3 * B_W:o + 4 * B_W])
    if sample:
        qt = _rope_t(qt, cost_ref[...], sint_ref[...])
        k = _rope(k, cos_ref[...], sin_ref[...])
        vt_ref[...] = _dot_nt(wvt_ref[...], hb).astype(BF16)
    else:
        v_ref[...] = _dot(hb, w_ref[:, o + 2 * B_W:o + 3 * B_W])
    qt_ref[...] = (qt * QK_SCALE).astype(BF16)
    k_ref[...] = k.astype(k_ref.dtype)


def _group_mean_sq(x):
    width = x.shape[-1]
    xx = x * x
    hi = xx.astype(BF16)
    lo = (xx - hi.astype(F32)).astype(BF16)
    r = lax.broadcasted_iota(jnp.int32, (width, width), 0) // DH
    c = lax.broadcasted_iota(jnp.int32, (width, width), 1) // DH
    g = jnp.where(r == c, 1.0, 0.0).astype(BF16)
    return (_dot(hi, g) + _dot(lo, g)) * (1.0 / DH)


def _head_rms_t(xt, gain_col):
    parts = []
    for j in range(xt.shape[0] // DH):
        blk = xt[j * DH:(j + 1) * DH]
        ms = jnp.mean(blk * blk, axis=0, keepdims=True)
        parts.append(blk * lax.rsqrt(ms + NORM_EPS))
    return jnp.concatenate(parts, axis=0) * gain_col


def _in_odd_kernel(x_ref, g_ref, mod_ref, w_ref, wqt_ref, qn_ref, kn_ref, *rest, sample):
    if sample:
        (wvt_ref, cos_ref, sin_ref, cost_ref, sint_ref,
         cqt_ref, ck_ref, cvt_ref, cz_ref, dqt_ref, dk_ref, dvt_ref, dz_ref) = rest
    else:
        cqt_ref, ck_ref, cv_ref, cz_ref, dqt_ref, dk_ref, dv_ref, dz_ref = rest
    hb = _pre_norm(x_ref, g_ref, mod_ref)
    cqt = _head_rms_t(_dot_nt(wqt_ref[0:C_W, :], hb), qn_ref[...])
    dqt = _dot_nt(wqt_ref[C_W:C_W + D_W, :], hb)
    o = C_W
    ck = _dot(hb, w_ref[:, o:o + KV_W])
    ck = ck * lax.rsqrt(_group_mean_sq(ck) + NORM_EPS) * kn_ref[...]
    o += 2 * KV_W
    cz_ref[...] = _dot(hb, w_ref[:, o:o + C_W])
    o += C_W + D_W
    dk = _dot(hb, w_ref[:, o:o + KV_W])
    o += 2 * KV_W
    dz_ref[...] = _dot(hb, w_ref[:, o:o + D_W])
    if sample:
        cos_t = cost_ref[...]
        sin_t = sint_ref[...]
        cqt = _rope_t(cqt, cos_t, sin_t)
        dqt = _rope_t(dqt, cos_t, sin_t)
        cos = cos_ref[...][:, :KV_W]
        sin = sin_ref[...][:, :KV_W]
        ck = _rope(ck, cos, sin)
        dk = _rope(dk, cos, sin)
        vt = _dot_nt(wvt_ref[...], hb).astype(BF16)
        _store_chunks(cvt_ref, vt[0:KV_W])
        _store_chunks(dvt_ref, vt[KV_W:2 * KV_W])
    else:
        cv_ref[...] = _dot(hb, w_ref[:, C_W + KV_W:C_W + 2 * KV_W])
        o = 2 * C_W + 2 * KV_W + D_W + KV_W
        dv_ref[...] = _dot(hb, w_ref[:, o:o + KV_W])
    cqt_ref[...] = (cqt * QK_SCALE).astype(BF16)
    dqt_ref[...] = (dqt * QK_SCALE).astype(BF16)
    ck_ref[...] = ck.astype(ck_ref.dtype)
    dk_ref[...] = dk.astype(dk_ref.dtype)


def _in_proj(x, g_pre, mod4, layer, w_bf, wqt, *, sample, extra=(), wvt=None, tables=None):
    n = x.shape[0]
    even = layer % 2 == 0
    tiles_per_seq = DEC_SEQ // TM
    if sample:
        mod_map = lambda i: (layer, 1 + i // tiles_per_seq, 0, 0)
    else:
        mod_map = lambda i: (layer, 0, 0, 0)
    row = lambda i: (i, 0)
    col = lambda i: (0, i)
    fixed = lambda i: (0, 0)
    in_specs = [
        pl.BlockSpec((TM, D_MODEL), row),
        pl.BlockSpec((1, D_MODEL), fixed),
        pl.BlockSpec((None, None, 1, 3 * D_MODEL), mod_map),
        pl.BlockSpec(w_bf.shape, fixed),
        pl.BlockSpec(wqt.shape, fixed),
    ]
    args = [x, g_pre, mod4, w_bf, wqt]
    for e in extra:
        in_specs.append(pl.BlockSpec(e.shape, fixed))
        args.append(e)
    if sample:
        cos, sin, cos_t, sin_t = tables
        in_specs += [pl.BlockSpec(wvt.shape, fixed),
                     pl.BlockSpec((TM, B_W), lambda i: (i % tiles_per_seq, 0)),
                     pl.BlockSpec((TM, B_W), lambda i: (i % tiles_per_seq, 0)),
                     pl.BlockSpec((B_W, TM), lambda i: (0, i % tiles_per_seq)),
                     pl.BlockSpec((B_W, TM), lambda i: (0, i % tiles_per_seq))]
        args += [wvt, cos, sin, cos_t, sin_t]
    rows = lambda w, dt: (pl.BlockSpec((TM, w), row), jax.ShapeDtypeStruct((n, w), dt))
    cols = lambda w, dt: (pl.BlockSpec((w, TM), col), jax.ShapeDtypeStruct((w, n), dt))
    chunks = lambda w, dt: (pl.BlockSpec((TM // LANES, w, LANES), lambda i: (i, 0, 0)),
                            jax.ShapeDtypeStruct((n // LANES, w, LANES), dt))
    kdt = BF16 if sample else F32
    if even:
        outs = [rows(2 * A_W, F32), rows(A_W, F32), cols(B_W, BF16), rows(B_W, kdt),
                cols(B_W, BF16) if sample else rows(B_W, F32), rows(B_W, F32)]
        body = functools.partial(_in_even_kernel, sample=sample)
    else:
        vspec = (lambda: chunks(KV_W, BF16)) if sample else (lambda: rows(KV_W, F32))
        outs = [cols(C_W, BF16), rows(KV_W, kdt), vspec(), rows(C_W, F32),
                cols(D_W, BF16), rows(KV_W, kdt), vspec(), rows(D_W, F32)]
        body = functools.partial(_in_odd_kernel, sample=sample)
    return pl.pallas_call(
        body,
        grid=(n // TM,),
        in_specs=in_specs,
        out_specs=[o[0] for o in outs],
        out_shape=[o[1] for o in outs],
        compiler_params=_params("arbitrary"),
        name=f"in_proj_{'s' if sample else 'p'}{layer}",
    )(*args)


def _conv_kernel(ug_ref, prev_ref, next_ref, az_ref, cw_ref, cb_ref, lg_ref, lb_ref, o_ref, sh_ref,
                 *, tiles_per_seq):
    i = pl.program_id(0)
    pos = i % tiles_per_seq
    has_prev = pos != 0
    has_next = pos != tiles_per_seq - 1

    def glu(ref):
        v = ref[...]
        return v[:, :A_W] * jax.nn.sigmoid(v[:, A_W:])

    pad = jnp.concatenate([jnp.where(has_prev, glu(prev_ref), 0.0), glu(ug_ref),
                           jnp.where(has_next, glu(next_ref), 0.0)], axis=0)
    rows = sh_ref.shape[1]
    for b in range(SUBLANES):
        sh_ref[b] = pad[b:b + rows]

    base = HALO - CONV_K // 2
    for c0 in range(0, A_W, LANES):
        cs = slice(c0, c0 + LANES)
        for r0 in range(0, TA, ROW_CHUNK):
            acc = jnp.zeros((ROW_CHUNK, LANES), F32) + cb_ref[:, cs]
            for k in range(CONV_K):
                j = k + base
                s = r0 + (j // SUBLANES) * SUBLANES
                acc = acc + sh_ref[j % SUBLANES, s:s + ROW_CHUNK, cs] * cw_ref[k:k + 1, cs]
            o_ref[r0:r0 + ROW_CHUNK, cs] = acc.astype(o_ref.dtype)


def _conv_post_kernel(a_ref, az_ref, lg_ref, lb_ref, o_ref):
    a = a_ref[...]
    mu = jnp.mean(a, axis=-1, keepdims=True)
    d = a - mu
    var = jnp.mean(d * d, axis=-1, keepdims=True)
    y = d * lax.rsqrt(var + NORM_EPS) * lg_ref[...] + lb_ref[...]
    o_ref[...] = (_silu(y) * _silu(az_ref[...])).astype(o_ref.dtype)


def _conv_full_kernel(ug_ref, prev_ref, next_ref, az_ref, cw_ref, cb_ref, lg_ref, lb_ref, o_ref,
                      sh_ref, acc_ref, *, tiles_per_seq):
    _conv_kernel(ug_ref, prev_ref, next_ref, az_ref, cw_ref, cb_ref, lg_ref, lb_ref, acc_ref, sh_ref,
                 tiles_per_seq=tiles_per_seq)
    _conv_post_kernel(acc_ref, az_ref, lg_ref, lb_ref, o_ref)


def _conv_branch(ug, az, conv_w, conv_b, ln_g, ln_b, *, seq_len):
    n = ug.shape[0]
    tiles_per_seq = seq_len // TA
    hb = TA // HALO
    last = n // HALO - 1
    row = lambda i: (i, 0)
    fixed = lambda i: (0, 0)
    cw = jnp.zeros((4 * SUBLANES, A_W), F32).at[:CONV_K].set(conv_w)
    sh_rows = TA + 3 * SUBLANES
    return pl.pallas_call(
        functools.partial(_conv_full_kernel, tiles_per_seq=tiles_per_seq),
        grid=(n // TA,),
        in_specs=[
            pl.BlockSpec((TA, 2 * A_W), row),
            pl.BlockSpec((HALO, 2 * A_W), lambda i: (jnp.maximum(i * hb - 1, 0), 0)),
            pl.BlockSpec((HALO, 2 * A_W), lambda i: (jnp.minimum((i + 1) * hb, last), 0)),
            pl.BlockSpec((TA, A_W), row),
            pl.BlockSpec((4 * SUBLANES, A_W), fixed),
            pl.BlockSpec((1, A_W), fixed),
            pl.BlockSpec((1, A_W), fixed),
            pl.BlockSpec((1, A_W), fixed),
        ],
        out_specs=pl.BlockSpec((TA, A_W), row),
        out_shape=jax.ShapeDtypeStruct((n, A_W), BF16),
        scratch_shapes=[pltpu.VMEM((SUBLANES, sh_rows, A_W), F32), pltpu.VMEM((TA, A_W), F32)],
        compiler_params=_params("arbitrary"),
        name=f"conv_branch_{seq_len}",
    )(ug, ug, ug, az, cw, conv_b.reshape(1, A_W), ln_g.reshape(1, A_W), ln_b.reshape(1, A_W))


def _softmax_t(segs, extra=None):
    m = None
    for s in segs:
        mi = jnp.max(s, axis=0, keepdims=True)
        m = mi if m is None else jnp.maximum(m, mi)
    if extra is not None:
        m = jnp.maximum(m, extra)
    es = [jnp.exp(s - m) for s in segs]
    den = None
    for e in es:
        di = jnp.sum(e, axis=0, keepdims=True)
        den = di if den is None else den + di
    if extra is not None:
        den = den + jnp.exp(extra - m)
    return es, den


def _keep_rows(xt, lo, hi):
    zeros = lambda r: jnp.zeros((r, xt.shape[1]), xt.dtype)
    parts = []
    if lo > 0:
        parts.append(zeros(lo))
    parts.append(xt[lo:hi])
    if hi < xt.shape[0]:
        parts.append(zeros(xt.shape[0] - hi))
    return jnp.concatenate(parts, axis=0)


def _pipelined(n, scores, finish):
    cur = scores(0)
    for j in range(n):
        nxt = scores(j + 1) if j + 1 < n else None
        finish(j, cur)
        cur = nxt


def _diff_attn(qt_ref, kvs, z_ref, lam_ref, g_ref, o_ref, *, lam_init):
    lv = lam_ref[...]
    lam = (jnp.exp(jnp.sum(lv[0:1] * lv[1:2], axis=-1, keepdims=True))
           - jnp.exp(jnp.sum(lv[2:3] * lv[3:4], axis=-1, keepdims=True)) + lam_init)

    def scores(h):
        cs = slice(h * LANES, (h + 1) * LANES)
        qt = qt_ref[cs, :]
        ks = [get_k(cs) for get_k, _ in kvs]
        return [[_dot(kk, _keep_rows(qt, c * DH, (c + 1) * DH)) for kk in ks] for c in range(2)]

    def finish(h, ss):
        cs = slice(h * LANES, (h + 1) * LANES)
        es0, den0 = _softmax_t(ss[0])
        es1, den1 = _softmax_t(ss[1])
        r0 = 1.0 / den0
        r1 = lam / den1
        ot = None
        for e0, e1, (_, get_vt) in zip(es0, es1, kvs):
            w = e0 * r0 - e1 * r1
            oi = _dot(get_vt(cs), w.astype(BF16))
            ot = oi if ot is None else ot + oi
        ms = jnp.mean(ot * ot, axis=0, keepdims=True)
        o = (ot * lax.rsqrt(ms + NORM_EPS)).T
        o = (o * g_ref[...]) * (1.0 - lam_init)
        o_ref[:, cs] = (o * _silu(z_ref[:, cs])).astype(o_ref.dtype)

    _pipelined(H_B, scores, finish)


def _diff_prompt_kernel(qt_ref, k_ref, v_ref, z_ref, lam_ref, g_ref, o_ref, *, lam_init):
    kv = (lambda cs: k_ref[:, cs].astype(BF16), lambda cs: v_ref[:, cs].T.astype(BF16))
    _diff_attn(qt_ref, [kv], z_ref, lam_ref, g_ref, o_ref, lam_init=lam_init)


def _diff_sample_kernel(qt_ref, k_ref, vt_ref, ck_ref, cv_ref, z_ref, lam_ref, g_ref, o_ref,
                        ckb_ref, cvt_ref, *, lam_init):
    @pl.when(pl.program_id(1) == 0)
    def _():
        ckb_ref[...] = ck_ref[...].astype(BF16)
        for h in range(H_B):
            cs = slice(h * LANES, (h + 1) * LANES)
            cvt_ref[cs, :] = cv_ref[:, cs].T.astype(BF16)

    ctx = (lambda cs: ckb_ref[:, cs], lambda cs: cvt_ref[cs, :])
    loc = (lambda cs: k_ref[:, cs], lambda cs: vt_ref[cs, :])
    _diff_attn(qt_ref, [ctx, loc], z_ref, lam_ref, g_ref, o_ref, lam_init=lam_init)


def _diff_branch_prompt(qt, k, v, z, lam_vec, subln_g, lam_init):
    n = k.shape[0]
    row = lambda i: (i, 0)
    fixed = lambda i: (0, 0)
    blk = pl.BlockSpec((SEQ, B_W), row)
    return pl.pallas_call(
        functools.partial(_diff_prompt_kernel, lam_init=lam_init),
        grid=(n // SEQ,),
        in_specs=[pl.BlockSpec((B_W, SEQ), lambda i: (0, i)), blk, blk, blk,
                  pl.BlockSpec((4, DH), fixed), pl.BlockSpec((1, 2 * DH), fixed)],
        out_specs=blk,
        out_shape=jax.ShapeDtypeStruct((n, B_W), BF16),
        compiler_params=_params("arbitrary"),
        name="diff_attn_prompt",
    )(qt, k, v, z, lam_vec, subln_g.reshape(1, 2 * DH))


def _diff_branch_sample(qt, k, vt, z, cache_k, cache_v, layer_idx, lam_vec, subln_g, lam_init):
    n = k.shape[0]
    nq = DEC_SEQ // TQ
    qrow = lambda b, t: (b * nq + t, 0)
    cache = lambda b, t: (b, layer_idx, 0, 0)
    fixed = lambda b, t: (0, 0)
    return pl.pallas_call(
        functools.partial(_diff_sample_kernel, lam_init=lam_init),
        grid=(DEC_BATCH, nq),
        in_specs=[
            pl.BlockSpec((B_W, TQ), lambda b, t: (0, b * nq + t)),
            pl.BlockSpec((DEC_SEQ, B_W), lambda b, t: (b, 0)),
            pl.BlockSpec((B_W, DEC_SEQ), lambda b, t: (0, b)),
            pl.BlockSpec((None, None, PAST_LEN, B_W), cache),
            pl.BlockSpec((None, None, PAST_LEN, B_W), cache),
            pl.BlockSpec((TQ, B_W), qrow),
            pl.BlockSpec((4, DH), fixed),
            pl.BlockSpec((1, 2 * DH), fixed),
        ],
        out_specs=pl.BlockSpec((TQ, B_W), qrow),
        out_shape=jax.ShapeDtypeStruct((n, B_W), BF16),
        scratch_shapes=[pltpu.VMEM((PAST_LEN, B_W), BF16), pltpu.VMEM((B_W, PAST_LEN), BF16)],
        compiler_params=_params("arbitrary", "arbitrary"),
        name="diff_attn_sample",
    )(qt, k, vt, cache_k, cache_v, z, lam_vec, subln_g.reshape(1, 2 * DH))


def _gqa(qt_ref, segs, z_ref, o_ref, sink_ref=None):
    halves = []

    def scores(j):
        n = j // 4
        qj = qt_ref[j * DH:(j + 1) * DH, :]
        zero = jnp.zeros_like(qj)
        qz = jnp.concatenate([qj, zero] if n == 0 else [zero, qj], axis=0)
        return [_dot(k, qz) for k, _, _ in segs]

    def finish(j, ss):
        n = j // 4
        ss = [s if valid is None else jnp.where(valid, s, -jnp.inf)
              for s, (_, _, valid) in zip(ss, segs)]
        extra = None if sink_ref is None else sink_ref[:, j:j + 1]
        es, den = _softmax_t(ss, extra)
        ot = None
        for e, (_, vt, _) in zip(es, segs):
            oi = _dot(vt[n * DH:(n + 1) * DH], e.astype(BF16))
            ot = oi if ot is None else ot + oi
        halves.append(ot * (1.0 / den))
        if j % 2 == 1:
            cs = slice((j // 2) * LANES, (j // 2 + 1) * LANES)
            o_pair = jnp.concatenate(halves[-2:], axis=0).T
            o_ref[:, cs] = (o_pair * _silu(z_ref[:, cs])).astype(o_ref.dtype)

    _pipelined(2 * 4, scores, finish)


def _gqa_prompt_kernel(cqt_ref, ck_ref, cv_ref, cz_ref, dqt_ref, dk_ref, dv_ref, dz_ref, sink_ref,
                       oc_ref, od_ref):
    seg = lambda k_ref, v_ref: (k_ref[...].astype(BF16), v_ref[...].T.astype(BF16), None)
    _gqa(cqt_ref, [seg(ck_ref, cv_ref)], cz_ref, oc_ref)
    _gqa(dqt_ref, [seg(dk_ref, dv_ref)], dz_ref, od_ref, sink_ref)


def _gqa_sample_kernel(cqt_ref, ck_ref, cvt_ref, cck_ref, ccv_ref, cz_ref,
                       dqt_ref, dk_ref, dvt_ref, cdk_ref, cdv_ref, dz_ref, sink_ref,
                       oc_ref, od_ref):
    t = pl.program_id(1)
    ctx = lambda k_ref, v_ref: (k_ref[...].astype(BF16), v_ref[...].T.astype(BF16), None)
    n_chunks = DEC_SEQ // LANES
    cvt = jnp.concatenate([cvt_ref[c] for c in range(n_chunks)], axis=1)
    _gqa(cqt_ref, [ctx(cck_ref, ccv_ref), (ck_ref[...], cvt, None)], cz_ref, oc_ref)
    span = 2 * TQ
    t0 = t * TQ
    ws = pl.multiple_of(jnp.clip(t0 - WINDOW, 0, DEC_SEQ - span), WINDOW)
    kpos = ws + lax.broadcasted_iota(jnp.int32, (span, TQ), 0)
    qpos = t0 + lax.broadcasted_iota(jnp.int32, (span, TQ), 1)
    valid = jnp.abs(qpos - kpos) <= WINDOW
    c0 = ws // LANES
    dvt = jnp.concatenate([dvt_ref[c0 + c] for c in range(span // LANES)], axis=1)
    _gqa(dqt_ref, [ctx(cdk_ref, cdv_ref), (dk_ref[pl.ds(ws, span), :], dvt, valid)],
         dz_ref, od_ref, sink_ref)


def _gqa_branch_prompt(cqt, ck, cv, cz, dqt, dk, dv, dz, sink):
    n = ck.shape[0]
    row = lambda i: (i, 0)
    qt = pl.BlockSpec((C_W, SEQ), lambda i: (0, i))
    wide = pl.BlockSpec((SEQ, C_W), row)
    kv = pl.BlockSpec((SEQ, KV_W), row)
    return pl.pallas_call(
        _gqa_prompt_kernel,
        grid=(n // SEQ,),
        in_specs=[qt, kv, kv, wide, qt, kv, kv, wide, pl.BlockSpec((1, 8), lambda i: (0, 0))],
        out_specs=[wide, wide],
        out_shape=[jax.ShapeDtypeStruct((n, C_W), BF16), jax.ShapeDtypeStruct((n, D_W), BF16)],
        compiler_params=_params("arbitrary"),
        name="gqa_prompt",
    )(cqt, ck, cv, cz, dqt, dk, dv, dz, sink)


def _gqa_branch_sample(cqt, ck, cvt, cz, dqt, dk, dvt, dz, sink, cck, ccv, cdk, cdv, layer_idx):
    n = ck.shape[0]
    nq = DEC_SEQ // TQ
    qrow = lambda b, t: (b * nq + t, 0)
    cache = lambda b, t: (b, layer_idx, 0, 0)
    qt = pl.BlockSpec((C_W, TQ), lambda b, t: (0, b * nq + t))
    wide = pl.BlockSpec((TQ, C_W), qrow)
    kv = pl.BlockSpec((DEC_SEQ, KV_W), lambda b, t: (b, 0))
    vt = pl.BlockSpec((DEC_SEQ // LANES, KV_W, LANES), lambda b, t: (b, 0, 0))
    cb = pl.BlockSpec((None, None, PAST_LEN, KV_W), cache)
    return pl.pallas_call(
        _gqa_sample_kernel,
        grid=(DEC_BATCH, nq),
        in_specs=[qt, kv, vt, cb, cb, wide, qt, kv, vt, cb, cb, wide,
                  pl.BlockSpec((1, 8), lambda b, t: (0, 0))],
        out_specs=[wide, wide],
        out_shape=[jax.ShapeDtypeStruct((n, C_W), BF16), jax.ShapeDtypeStruct((n, D_W), BF16)],
        compiler_params=_params("arbitrary", "arbitrary"),
        name="gqa_sample",
    )(cqt, ck, cvt, cck, ccv, cz, dqt, dk, dvt, cdk, cdv, dz, sink)


def _out_kernel(ma_ref, mb_ref, x_ref, w_ref, g_ref, mod_ref, o_ref):
    half = ma_ref.shape[-1]
    o = _dot(ma_ref[...], w_ref[0:half, :]) + _dot(mb_ref[...], w_ref[half:, :])
    ms = jnp.mean(o * o, axis=-1, keepdims=True)
    r = o * lax.rsqrt(ms + NORM_EPS) * g_ref[...]
    gate = mod_ref[...][:, 2 * D_MODEL:]
    o_ref[...] = x_ref[...] + gate * r


def _out_proj(ma, mb, x, w_bf, g_post, mod4, layer, *, sample):
    n = x.shape[0]
    tiles_per_seq = DEC_SEQ // TM
    if sample:
        mod_map = lambda i: (layer, 1 + i // tiles_per_seq, 0, 0)
    else:
        mod_map = lambda i: (layer, 0, 0, 0)
    row = lambda i: (i, 0)
    fixed = lambda i: (0, 0)
    half = ma.shape[1]
    return pl.pallas_call(
        _out_kernel,
        grid=(n // TM,),
        in_specs=[
            pl.BlockSpec((TM, half), row),
            pl.BlockSpec((TM, half), row),
            pl.BlockSpec((TM, D_MODEL), row),
            pl.BlockSpec((2 * half, D_MODEL), fixed),
            pl.BlockSpec((1, D_MODEL), fixed),
            pl.BlockSpec((None, None, 1, 3 * D_MODEL), mod_map),
        ],
        out_specs=pl.BlockSpec((TM, D_MODEL), row),
        out_shape=jax.ShapeDtypeStruct((n, D_MODEL), F32),
        compiler_params=_params("arbitrary"),
        name=f"out_proj_{'s' if sample else 'p'}{layer}",
    )(ma, mb, x, w_bf, g_post, mod4)


def _rope_tables():
    nf = DH // 4
    t = jnp.arange(DEC_SEQ)
    row = (t // GRID_W).astype(F32)
    col = (t % GRID_W).astype(F32)
    inv = ROPE_THETA ** (-jnp.arange(nf, dtype=F32) / nf)
    d = jnp.arange(DH)
    axis = d // (2 * nf)
    second = (d % (2 * nf)) // nf
    f = d % nf
    pos = jnp.where(axis[None, :] == 0, row[:, None], col[:, None])
    ang = pos * inv[f][None, :]
    cos = jnp.cos(ang)
    sin = jnp.where(second[None, :] == 0, -jnp.sin(ang), jnp.sin(ang))
    reps = B_W // DH
    cos = jnp.tile(cos, (1, reps))
    sin = jnp.tile(sin, (1, reps))
    return cos, sin, cos.T, sin.T


def kernel(x_prompt, x_sample, cache_b_k, cache_b_v, cache_c_k, cache_c_v, cache_d_k, cache_d_v, c, c_ctx, norm_pre, norm_post, w_mod, b_mod, w_in_even, a_conv_w, a_conv_b, a_ln_g, a_ln_b, b_lambda, b_subln_g, w_out_even, w_in_odd, c_q_norm, c_k_norm, d_sink, w_out_odd):
    n_even = (DEPTH + 1) // 2
    n_odd = DEPTH // 2
    cond8 = jnp.zeros((SUBLANES, D_MODEL), F32).at[0].set(c_ctx).at[1:1 + DEC_BATCH].set(c)
    mod4 = _modulation(cond8, w_mod, b_mod).reshape(DEPTH, SUBLANES, 1, 3 * D_MODEL)
    tables = _rope_tables()

    xp = x_prompt.reshape(BATCH * SEQ, D_MODEL)
    xs = x_sample.reshape(DEC_BATCH * DEC_SEQ, D_MODEL)
    cbk = cache_b_k.reshape(DEC_BATCH, n_even, PAST_LEN, B_W)
    cbv = cache_b_v.reshape(DEC_BATCH, n_even, PAST_LEN, B_W)
    cck = cache_c_k.reshape(DEC_BATCH, n_odd, PAST_LEN, KV_W)
    ccv = cache_c_v.reshape(DEC_BATCH, n_odd, PAST_LEN, KV_W)
    cdk = cache_d_k.reshape(DEC_BATCH, n_odd, PAST_LEN, KV_W)
    cdv = cache_d_v.reshape(DEC_BATCH, n_odd, PAST_LEN, KV_W)

    nbk, nbv, nck, ncv, ndk, ndv = [], [], [], [], [], []
    for l in range(DEPTH):
        i = l // 2
        g_pre = norm_pre[l].reshape(1, D_MODEL)
        g_post = norm_post[l].reshape(1, D_MODEL)
        if l % 2 == 0:
            lam_init = 0.8 - 0.6 * math.exp(-0.3 * l)
            w_in = w_in_even[i].astype(BF16)
            w_out = w_out_even[i].astype(BF16)
            o = 3 * A_W
            wqt = w_in[:, o:o + B_W].T
            wvt = w_in[:, o + 2 * B_W:o + 3 * B_W].T
            conv = (a_conv_w[i], a_conv_b[i], a_ln_g[i], a_ln_b[i])
            ug, az, qt, k, v, bz = _in_proj(xp, g_pre, mod4, l, w_in, wqt, sample=False)
            ma = _conv_branch(ug, az, *conv, seq_len=SEQ)
            mb = _diff_branch_prompt(qt, k, v, bz, b_lambda[i], b_subln_g[i], lam_init)
            xp = _out_proj(ma, mb, xp, w_out, g_post, mod4, l, sample=False)
            nbk.append(k.reshape(BATCH, SEQ, H_B, 2, DH))
            nbv.append(v.reshape(BATCH, SEQ, H_B, 2 * DH))
            ug, az, qt, k, vt, bz = _in_proj(xs, g_pre, mod4, l, w_in, wqt, sample=True, wvt=wvt,
                                             tables=tables)
            ma = _conv_branch(ug, az, *conv, seq_len=DEC_SEQ)
            mb = _diff_branch_sample(qt, k, vt, bz, cbk, cbv, i, b_lambda[i], b_subln_g[i], lam_init)
            xs = _out_proj(ma, mb, xs, w_out, g_post, mod4, l, sample=True)
        else:
            w_in = w_in_odd[i].astype(BF16)
            w_out = w_out_odd[i].astype(BF16)
            od = C_W + 2 * KV_W + C_W
            wqt = jnp.concatenate([w_in[:, 0:C_W], w_in[:, od:od + D_W]], axis=1).T
            wvt = jnp.concatenate([w_in[:, C_W + KV_W:C_W + 2 * KV_W],
                                   w_in[:, od + D_W + KV_W:od + D_W + 2 * KV_W]], axis=1).T
            qn = jnp.tile(c_q_norm[i], C_W // DH).reshape(C_W, 1)
            kn = jnp.tile(c_k_norm[i], KV_W // DH).reshape(1, KV_W)
            sink = d_sink[i].reshape(1, 8)
            cqt, ck, cv, cz, dqt, dk, dv, dz = _in_proj(xp, g_pre, mod4, l, w_in, wqt, sample=False,
                                                        extra=(qn, kn))
            mc, md = _gqa_branch_prompt(cqt, ck, cv, cz, dqt, dk, dv, dz, sink)
            xp = _out_proj(mc, md, xp, w_out, g_post, mod4, l, sample=False)
            nck.append(ck.reshape(BATCH, SEQ, 2, DH))
            ncv.append(cv.reshape(BATCH, SEQ, 2, DH))
            ndk.append(dk.reshape(BATCH, SEQ, 2, DH))
            ndv.append(dv.reshape(BATCH, SEQ, 2, DH))
            cqt, ck, cvt, cz, dqt, dk, dvt, dz = _in_proj(xs, g_pre, mod4, l, w_in, wqt, sample=True,
                                                          extra=(qn, kn), wvt=wvt, tables=tables)
            mc, md = _gqa_branch_sample(cqt, ck, cvt, cz, dqt, dk, dvt, dz, sink, cck, ccv, cdk, cdv, i)
            xs = _out_proj(mc, md, xs, w_out, g_post, mod4, l, sample=True)

    return (xp.reshape(BATCH, SEQ, D_MODEL), xs.reshape(DEC_BATCH, DEC_SEQ, D_MODEL),
            jnp.stack(nbk, axis=1), jnp.stack(nbv, axis=1), jnp.stack(nck, axis=1),
            jnp.stack(ncv, axis=1), jnp.stack(ndk, axis=1), jnp.stack(ndv, axis=1))
```

```python
import functools
import math

import jax
import jax.numpy as jnp
from jax import lax
from jax.experimental import pallas as pl
from jax.experimental.pallas import tpu as pltpu

F32 = jnp.float32
BF16 = jnp.bfloat16

D_MODEL = 1024
BATCH = 16
SEQ = 256
DEPTH = 4
DEC_BATCH = 2
DEC_SEQ = 1024
PAST_LEN = 512
GRID_W = 64
ROPE_THETA = 10000.0
NORM_EPS = 1e-6
DH = 64
A_W = 512
CONV_K = 31
H_B = 4
B_W = 512
C_W = 512
KV_W = 128
D_W = 512
WINDOW = 128
QK_SCALE = DH ** -0.5

LANES = 128
SUBLANES = 8
VMEM_LIMIT = 56 * 1024 * 1024

TM = 512
TA = 256
TQ = 256
HALO = 16
ROW_CHUNK = 64


def _params(*sem):
    return pltpu.CompilerParams(dimension_semantics=sem, vmem_limit_bytes=VMEM_LIMIT)


def _silu(x):
    return x * jax.nn.sigmoid(x)


def _dot(a, b):
    return jnp.dot(a, b, preferred_element_type=F32)


def _dot_nt(a, b):
    return lax.dot_general(a, b, (((1,), (1,)), ((), ())), preferred_element_type=F32)


def _mod_kernel(cond_ref, w_ref, b_ref, o_ref):
    a = _silu(cond_ref[...]).astype(BF16)
    o_ref[...] = _dot(a, w_ref[...].astype(BF16)) + b_ref[...]


def _modulation(cond8, w_mod, b_mod):
    nblk = 3
    return pl.pallas_call(
        _mod_kernel,
        grid=(DEPTH, nblk),
        in_specs=[
            pl.BlockSpec((SUBLANES, D_MODEL), lambda l, j: (0, 0)),
            pl.BlockSpec((None, D_MODEL, D_MODEL), lambda l, j: (l, 0, j)),
            pl.BlockSpec((None, 1, D_MODEL), lambda l, j: (l, 0, j)),
        ],
        out_specs=pl.BlockSpec((None, SUBLANES, D_MODEL), lambda l, j: (l, 0, j)),
        out_shape=jax.ShapeDtypeStruct((DEPTH, SUBLANES, 3 * D_MODEL), F32),
        compiler_params=_params("arbitrary", "arbitrary"),
        name="modulation",
    )(cond8, w_mod, b_mod.reshape(DEPTH, 1, 3 * D_MODEL))


def _pre_norm(x_ref, g_ref, mod_ref):
    x = x_ref[...]
    ms = jnp.mean(x * x, axis=-1, keepdims=True)
    mod = mod_ref[...]
    sh = mod[:, :D_MODEL]
    sc = mod[:, D_MODEL:2 * D_MODEL]
    h = (x * lax.rsqrt(ms + NORM_EPS) * g_ref[...]) * (1.0 + sc) + sh
    return h.astype(BF16)


def _rope(x, cos, sin_signed):
    w = x.shape[-1]
    lane = lax.broadcasted_iota(jnp.int32, (1, w), 1)
    first = (lane % 32) < 16
    partner = jnp.where(first, pltpu.roll(x, w - 16, 1), pltpu.roll(x, 16, 1))
    return x * cos + partner * sin_signed


def _rope_t(x, cos_t, sin_t):
    r = x.shape[0]
    row = lax.broadcasted_iota(jnp.int32, (r, 1), 0)
    first = (row % 32) < 16
    partner = jnp.where(first, pltpu.roll(x, r - 16, 0), pltpu.roll(x, 16, 0))
    return x * cos_t + partner * sin_t


def _store_chunks(ref, xt):
    for c in range(xt.shape[1] // LANES):
        ref[c] = xt[:, c * LANES:(c + 1) * LANES]


def _store_per_seq(ref, xt):
    for s in range(xt.shape[1] // SEQ):
        ref[s] = xt[:, s * SEQ:(s + 1) * SEQ]


def _in_even_kernel(x_ref, g_ref, mod_ref, w_ref, *rest, sample):
    if sample:
        cos_ref, sin_ref, cost_ref, sint_ref, ug_ref, az_ref, qt_ref, k_ref, vt_ref, bz_ref = rest
    else:
        ug_ref, az_ref, qt_ref, k_ref, kt_ref, v_ref, vt_ref, bz_ref = rest
    hb = _pre_norm(x_ref, g_ref, mod_ref)
    ug_ref[...] = _dot(hb, w_ref[:, 0:2 * A_W])
    az_ref[...] = _dot(hb, w_ref[:, 2 * A_W:3 * A_W])
    o = 3 * A_W
    qt = _dot(hb, w_ref[:, o:o + B_W]).T
    k = _dot(hb, w_ref[:, o + B_W:o + 2 * B_W])
    v = _dot(hb, w_ref[:, o + 2 * B_W:o + 3 * B_W])
    bz_ref[...] = _dot(hb, w_ref[:, o + 3 * B_W:o + 4 * B_W])
    if sample:
        qt = _rope_t(qt, cost_ref[...], sint_ref[...])
        k = _rope(k, cos_ref[...], sin_ref[...])
    else:
        _store_per_seq(kt_ref, k.T)
        for s in range(TM // SEQ):
            for h in range(H_B):
                v_ref[s, pl.ds(h, SEQ, stride=H_B), :] = v[s * SEQ:(s + 1) * SEQ, h * LANES:(h + 1) * LANES]
    qt_ref[...] = (qt * QK_SCALE).astype(BF16)
    k_ref[...] = k.astype(BF16)
    vt_ref[...] = v.T.astype(BF16)


def _group_mean_sq(x):
    width = x.shape[-1]
    xx = x * x
    hi = xx.astype(BF16)
    lo = (xx - hi.astype(F32)).astype(BF16)
    r = lax.broadcasted_iota(jnp.int32, (width, width), 0) // DH
    c = lax.broadcasted_iota(jnp.int32, (width, width), 1) // DH
    g = jnp.where(r == c, 1.0, 0.0).astype(BF16)
    return (_dot(hi, g) + _dot(lo, g)) * (1.0 / DH)


def _head_rms_t(xt, gain_col):
    parts = []
    for j in range(xt.shape[0] // DH):
        blk = xt[j * DH:(j + 1) * DH]
        ms = jnp.mean(blk * blk, axis=0, keepdims=True)
        parts.append(blk * lax.rsqrt(ms + NORM_EPS))
    return jnp.concatenate(parts, axis=0) * gain_col


def _in_odd_kernel(x_ref, g_ref, mod_ref, w_ref, qn_ref, kn_ref, *rest, sample):
    if sample:
        (cos_ref, sin_ref, cost_ref, sint_ref,
         cqt_ref, ck_ref, cvt_ref, cz_ref, dqt_ref, dk_ref, dvt_ref, dz_ref) = rest
    else:
        (cqt_ref, ck_ref, ckt_ref, cvt_ref, cz_ref,
         dqt_ref, dk_ref, dkt_ref, dvt_ref, dz_ref) = rest
    hb = _pre_norm(x_ref, g_ref, mod_ref)
    o = 0
    cqt = _head_rms_t(_dot(hb, w_ref[:, o:o + C_W]).T, qn_ref[...])
    o += C_W
    ck = _dot(hb, w_ref[:, o:o + KV_W])
    ck = ck * lax.rsqrt(_group_mean_sq(ck) + NORM_EPS) * kn_ref[...]
    o += KV_W
    cvt = _dot(hb, w_ref[:, o:o + KV_W]).T
    o += KV_W
    cz_ref[...] = _dot(hb, w_ref[:, o:o + C_W])
    o += C_W
    dqt = _dot(hb, w_ref[:, o:o + D_W]).T
    o += D_W
    dk = _dot(hb, w_ref[:, o:o + KV_W])
    o += KV_W
    dvt = _dot(hb, w_ref[:, o:o + KV_W]).T
    o += KV_W
    dz_ref[...] = _dot(hb, w_ref[:, o:o + D_W])
    if sample:
        cos_t = cost_ref[...]
        sin_t = sint_ref[...]
        cqt = _rope_t(cqt, cos_t, sin_t)
        dqt = _rope_t(dqt, cos_t, sin_t)
        cos = cos_ref[...][:, :KV_W]
        sin = sin_ref[...][:, :KV_W]
        ck = _rope(ck, cos, sin)
        dk = _rope(dk, cos, sin)
        _store_chunks(cvt_ref, cvt.astype(BF16))
        _store_chunks(dvt_ref, dvt.astype(BF16))
    else:
        _store_per_seq(ckt_ref, ck.T)
        _store_per_seq(dkt_ref, dk.T)
        _store_per_seq(cvt_ref, cvt)
        _store_per_seq(dvt_ref, dvt)
    cqt_ref[...] = (cqt * QK_SCALE).astype(BF16)
    dqt_ref[...] = (dqt * QK_SCALE).astype(BF16)
    ck_ref[...] = ck.astype(BF16)
    dk_ref[...] = dk.astype(BF16)


def _in_proj(x, g_pre, mod4, layer, w_bf, *, sample, extra=(), tables=None):
    n = x.shape[0]
    even = layer % 2 == 0
    tiles_per_seq = DEC_SEQ // TM
    if sample:
        mod_map = lambda i: (layer, 1 + i // tiles_per_seq, 0, 0)
    else:
        mod_map = lambda i: (layer, 0, 0, 0)
    row = lambda i: (i, 0)
    col = lambda i: (0, i)
    fixed = lambda i: (0, 0)
    in_specs = [
        pl.BlockSpec((TM, D_MODEL), row),
        pl.BlockSpec((1, D_MODEL), fixed),
        pl.BlockSpec((None, None, 1, 3 * D_MODEL), mod_map),
        pl.BlockSpec(w_bf.shape, fixed),
    ]
    args = [x, g_pre, mod4, w_bf]
    for e in extra:
        in_specs.append(pl.BlockSpec(e.shape, fixed))
        args.append(e)
    if sample:
        cos, sin, cos_t, sin_t = tables
        in_specs += [pl.BlockSpec((TM, B_W), lambda i: (i % tiles_per_seq, 0)),
                     pl.BlockSpec((TM, B_W), lambda i: (i % tiles_per_seq, 0)),
                     pl.BlockSpec((B_W, TM), lambda i: (0, i % tiles_per_seq)),
                     pl.BlockSpec((B_W, TM), lambda i: (0, i % tiles_per_seq))]
        args += [cos, sin, cos_t, sin_t]
    rows = lambda w, dt: (pl.BlockSpec((TM, w), row), jax.ShapeDtypeStruct((n, w), dt))
    cols = lambda w, dt: (pl.BlockSpec((w, TM), col), jax.ShapeDtypeStruct((w, n), dt))
    chunks = lambda w, dt: (pl.BlockSpec((TM // LANES, w, LANES), lambda i: (i, 0, 0)),
                            jax.ShapeDtypeStruct((n // LANES, w, LANES), dt))
    per_seq = lambda r, c: (pl.BlockSpec((TM // SEQ, r, c), lambda i: (i, 0, 0)),
                            jax.ShapeDtypeStruct((n // SEQ, r, c), F32))
    if even and sample:
        outs = [rows(2 * A_W, F32), rows(A_W, F32), cols(B_W, BF16), rows(B_W, BF16),
                cols(B_W, BF16), rows(B_W, F32)]
    elif even:
        outs = [rows(2 * A_W, F32), rows(A_W, F32), cols(B_W, BF16), rows(B_W, BF16),
                per_seq(B_W, SEQ), per_seq(SEQ * H_B, LANES), cols(B_W, BF16), rows(B_W, F32)]
    elif sample:
        outs = [cols(C_W, BF16), rows(KV_W, BF16), chunks(KV_W, BF16), rows(C_W, F32),
                cols(D_W, BF16), rows(KV_W, BF16), chunks(KV_W, BF16), rows(D_W, F32)]
    else:
        outs = [cols(C_W, BF16), rows(KV_W, BF16), per_seq(KV_W, SEQ), per_seq(KV_W, SEQ), rows(C_W, F32),
                cols(D_W, BF16), rows(KV_W, BF16), per_seq(KV_W, SEQ), per_seq(KV_W, SEQ), rows(D_W, F32)]
    body = functools.partial(_in_even_kernel if even else _in_odd_kernel, sample=sample)
    return pl.pallas_call(
        body,
        grid=(n // TM,),
        in_specs=in_specs,
        out_specs=[o[0] for o in outs],
        out_shape=[o[1] for o in outs],
        compiler_params=_params("arbitrary"),
        name=f"in_proj_{'s' if sample else 'p'}{layer}",
    )(*args)


def _conv_kernel(ug_ref, prev_ref, next_ref, az_ref, cw_ref, cb_ref, lg_ref, lb_ref, o_ref, sh_ref,
                 *, tiles_per_seq):
    i = pl.program_id(0)
    pos = i % tiles_per_seq
    has_prev = pos != 0
    has_next = pos != tiles_per_seq - 1

    def glu(ref):
        v = ref[...]
        return v[:, :A_W] * jax.nn.sigmoid(v[:, A_W:])

    pad = jnp.concatenate([jnp.where(has_prev, glu(prev_ref), 0.0), glu(ug_ref),
                           jnp.where(has_next, glu(next_ref), 0.0)], axis=0)
    rows = sh_ref.shape[1]
    for b in range(SUBLANES):
        sh_ref[b] = pad[b:b + rows]

    base = HALO - CONV_K // 2
    for c0 in range(0, A_W, LANES):
        cs = slice(c0, c0 + LANES)
        for r0 in range(0, TA, ROW_CHUNK):
            acc = jnp.zeros((ROW_CHUNK, LANES), F32) + cb_ref[:, cs]
            for k in range(CONV_K):
                j = k + base
                s = r0 + (j // SUBLANES) * SUBLANES
                acc = acc + sh_ref[j % SUBLANES, s:s + ROW_CHUNK, cs] * cw_ref[k:k + 1, cs]
            o_ref[r0:r0 + ROW_CHUNK, cs] = acc.astype(o_ref.dtype)


def _conv_post_kernel(a_ref, az_ref, lg_ref, lb_ref, o_ref):
    a = a_ref[...]
    mu = jnp.mean(a, axis=-1, keepdims=True)
    d = a - mu
    var = jnp.mean(d * d, axis=-1, keepdims=True)
    y = d * lax.rsqrt(var + NORM_EPS) * lg_ref[...] + lb_ref[...]
    o_ref[...] = (_silu(y) * _silu(az_ref[...])).astype(o_ref.dtype)


def _conv_full_kernel(ug_ref, prev_ref, next_ref, az_ref, cw_ref, cb_ref, lg_ref, lb_ref, o_ref,
                      sh_ref, acc_ref, *, tiles_per_seq):
    _conv_kernel(ug_ref, prev_ref, next_ref, az_ref, cw_ref, cb_ref, lg_ref, lb_ref, acc_ref, sh_ref,
                 tiles_per_seq=tiles_per_seq)
    _conv_post_kernel(acc_ref, az_ref, lg_ref, lb_ref, o_ref)


def _conv_branch(ug, az, conv_w, conv_b, ln_g, ln_b, *, seq_len):
    n = ug.shape[0]
    tiles_per_seq = seq_len // TA
    hb = TA // HALO
    last = n // HALO - 1
    row = lambda i: (i, 0)
    fixed = lambda i: (0, 0)
    cw = jnp.zeros((4 * SUBLANES, A_W), F32).at[:CONV_K].set(conv_w)
    sh_rows = TA + 3 * SUBLANES
    return pl.pallas_call(
        functools.partial(_conv_full_kernel, tiles_per_seq=tiles_per_seq),
        grid=(n // TA,),
        in_specs=[
            pl.BlockSpec((TA, 2 * A_W), row),
            pl.BlockSpec((HALO, 2 * A_W), lambda i: (jnp.maximum(i * hb - 1, 0), 0)),
            pl.BlockSpec((HALO, 2 * A_W), lambda i: (jnp.minimum((i + 1) * hb, last), 0)),
            pl.BlockSpec((TA, A_W), row),
            pl.BlockSpec((4 * SUBLANES, A_W), fixed),
            pl.BlockSpec((1, A_W), fixed),
            pl.BlockSpec((1, A_W), fixed),
            pl.BlockSpec((1, A_W), fixed),
        ],
        out_specs=pl.BlockSpec((TA, A_W), row),
        out_shape=jax.ShapeDtypeStruct((n, A_W), BF16),
        scratch_shapes=[pltpu.VMEM((SUBLANES, sh_rows, A_W), F32), pltpu.VMEM((TA, A_W), F32)],
        compiler_params=_params("arbitrary"),
        name=f"conv_branch_{seq_len}",
    )(ug, ug, ug, az, cw, conv_b.reshape(1, A_W), ln_g.reshape(1, A_W), ln_b.reshape(1, A_W))


def _softmax_t(segs, extra=None):
    m = None
    for s in segs:
        mi = jnp.max(s, axis=0, keepdims=True)
        m = mi if m is None else jnp.maximum(m, mi)
    if extra is not None:
        m = jnp.maximum(m, extra)
    es = [jnp.exp(s - m) for s in segs]
    den = None
    for e in es:
        di = jnp.sum(e, axis=0, keepdims=True)
        den = di if den is None else den + di
    if extra is not None:
        den = den + jnp.exp(extra - m)
    return es, den


def _keep_rows(xt, lo, hi):
    zeros = lambda r: jnp.zeros((r, xt.shape[1]), xt.dtype)
    parts = []
    if lo > 0:
        parts.append(zeros(lo))
    parts.append(xt[lo:hi])
    if hi < xt.shape[0]:
        parts.append(zeros(xt.shape[0] - hi))
    return jnp.concatenate(parts, axis=0)


def _pipelined(n, scores, finish):
    cur = scores(0)
    for j in range(n):
        nxt = scores(j + 1) if j + 1 < n else None
        finish(j, cur)
        cur = nxt


def _diff_attn(qt_ref, kvs, z_ref, lam_ref, g_ref, o_ref, *, lam_init):
    lv = lam_ref[...]
    lam = (jnp.exp(jnp.sum(lv[0:1] * lv[1:2], axis=-1, keepdims=True))
           - jnp.exp(jnp.sum(lv[2:3] * lv[3:4], axis=-1, keepdims=True)) + lam_init)

    def scores(h):
        cs = slice(h * LANES, (h + 1) * LANES)
        qt = qt_ref[cs, :]
        ks = [get_k(cs) for get_k, _ in kvs]
        return [[_dot(kk, _keep_rows(qt, c * DH, (c + 1) * DH)) for kk in ks] for c in range(2)]

    def finish(h, ss):
        cs = slice(h * LANES, (h + 1) * LANES)
        es0, den0 = _softmax_t(ss[0])
        es1, den1 = _softmax_t(ss[1])
        r0 = 1.0 / den0
        r1 = lam / den1
        ot = None
        for e0, e1, (_, get_vt) in zip(es0, es1, kvs):
            w = e0 * r0 - e1 * r1
            oi = _dot(get_vt(cs), w.astype(BF16))
            ot = oi if ot is None else ot + oi
        ms = jnp.mean(ot * ot, axis=0, keepdims=True)
        o = (ot * lax.rsqrt(ms + NORM_EPS)).T
        o = (o * g_ref[...]) * (1.0 - lam_init)
        o_ref[:, cs] = (o * _silu(z_ref[:, cs])).astype(o_ref.dtype)

    _pipelined(H_B, scores, finish)


def _diff_prompt_kernel(qt_ref, k_ref, vt_ref, z_ref, lam_ref, g_ref, o_ref, *, lam_init):
    kv = (lambda cs: k_ref[:, cs], lambda cs: vt_ref[cs, :])
    _diff_attn(qt_ref, [kv], z_ref, lam_ref, g_ref, o_ref, lam_init=lam_init)


def _diff_sample_kernel(qt_ref, k_ref, vt_ref, ck_ref, cv_ref, z_ref, lam_ref, g_ref, o_ref,
                        ckb_ref, cvt_ref, *, lam_init):
    @pl.when(pl.program_id(1) == 0)
    def _():
        ckb_ref[...] = ck_ref[...].astype(BF16)
        for h in range(H_B):
            cs = slice(h * LANES, (h + 1) * LANES)
            cvt_ref[cs, :] = cv_ref[:, cs].T.astype(BF16)

    ctx = (lambda cs: ckb_ref[:, cs], lambda cs: cvt_ref[cs, :])
    loc = (lambda cs: k_ref[:, cs], lambda cs: vt_ref[cs, :])
    _diff_attn(qt_ref, [ctx, loc], z_ref, lam_ref, g_ref, o_ref, lam_init=lam_init)


def _diff_branch_prompt(qt, k, vt, z, lam_vec, subln_g, lam_init):
    n = k.shape[0]
    row = lambda i: (i, 0)
    fixed = lambda i: (0, 0)
    blk = pl.BlockSpec((SEQ, B_W), row)
    tblk = pl.BlockSpec((B_W, SEQ), lambda i: (0, i))
    return pl.pallas_call(
        functools.partial(_diff_prompt_kernel, lam_init=lam_init),
        grid=(n // SEQ,),
        in_specs=[tblk, blk, tblk, blk,
                  pl.BlockSpec((4, DH), fixed), pl.BlockSpec((1, 2 * DH), fixed)],
        out_specs=blk,
        out_shape=jax.ShapeDtypeStruct((n, B_W), BF16),
        compiler_params=_params("arbitrary"),
        name="diff_attn_prompt",
    )(qt, k, vt, z, lam_vec, subln_g.reshape(1, 2 * DH))


def _diff_branch_sample(qt, k, vt, z, cache_k, cache_v, layer_idx, lam_vec, subln_g, lam_init):
    n = k.shape[0]
    nq = DEC_SEQ // TQ
    qrow = lambda b, t: (b * nq + t, 0)
    cache = lambda b, t: (b, layer_idx, 0, 0)
    fixed = lambda b, t: (0, 0)
    return pl.pallas_call(
        functools.partial(_diff_sample_kernel, lam_init=lam_init),
        grid=(DEC_BATCH, nq),
        in_specs=[
            pl.BlockSpec((B_W, TQ), lambda b, t: (0, b * nq + t)),
            pl.BlockSpec((DEC_SEQ, B_W), lambda b, t: (b, 0)),
            pl.BlockSpec((B_W, DEC_SEQ), lambda b, t: (0, b)),
            pl.BlockSpec((None, None, PAST_LEN, B_W), cache),
            pl.BlockSpec((None, None, PAST_LEN, B_W), cache),
            pl.BlockSpec((TQ, B_W), qrow),
            pl.BlockSpec((4, DH), fixed),
            pl.BlockSpec((1, 2 * DH), fixed),
        ],
        out_specs=pl.BlockSpec((TQ, B_W), qrow),
        out_shape=jax.ShapeDtypeStruct((n, B_W), BF16),
        scratch_shapes=[pltpu.VMEM((PAST_LEN, B_W), BF16), pltpu.VMEM((B_W, PAST_LEN), BF16)],
        compiler_params=_params("arbitrary", "arbitrary"),
        name="diff_attn_sample",
    )(qt, k, vt, cache_k, cache_v, z, lam_vec, subln_g.reshape(1, 2 * DH))


def _gqa(qt_ref, segs, z_ref, o_ref, sink_ref=None):
    halves = []

    def scores(j):
        n = j // 4
        qj = qt_ref[j * DH:(j + 1) * DH, :]
        zero = jnp.zeros_like(qj)
        qz = jnp.concatenate([qj, zero] if n == 0 else [zero, qj], axis=0)
        return [_dot(k, qz) for k, _, _ in segs]

    def finish(j, ss):
        n = j // 4
        ss = [s if valid is None else jnp.where(valid, s, -jnp.inf)
              for s, (_, _, valid) in zip(ss, segs)]
        extra = None if sink_ref is None else sink_ref[:, j:j + 1]
        es, den = _softmax_t(ss, extra)
        ot = None
        for e, (_, vt, _) in zip(es, segs):
            oi = _dot(vt[n * DH:(n + 1) * DH], e.astype(BF16))
            ot = oi if ot is None else ot + oi
        halves.append(ot * (1.0 / den))
        if j % 2 == 1:
            cs = slice((j // 2) * LANES, (j // 2 + 1) * LANES)
            o_pair = jnp.concatenate(halves[-2:], axis=0).T
            o_ref[:, cs] = (o_pair * _silu(z_ref[:, cs])).astype(o_ref.dtype)

    _pipelined(2 * 4, scores, finish)


def _gqa_prompt_kernel(cqt_ref, ck_ref, cvt_ref, cz_ref, dqt_ref, dk_ref, dvt_ref, dz_ref, sink_ref,
                       oc_ref, od_ref):
    seg = lambda k_ref, vt_ref: (k_ref[...], vt_ref[...].astype(BF16), None)
    _gqa(cqt_ref, [seg(ck_ref, cvt_ref)], cz_ref, oc_ref)
    _gqa(dqt_ref, [seg(dk_ref, dvt_ref)], dz_ref, od_ref, sink_ref)


def _gqa_sample_kernel(cqt_ref, ck_ref, cvt_ref, cck_ref, ccv_ref, cz_ref,
                       dqt_ref, dk_ref, dvt_ref, cdk_ref, cdv_ref, dz_ref, sink_ref,
                       oc_ref, od_ref):
    t = pl.program_id(1)
    ctx = lambda k_ref, v_ref: (k_ref[...].astype(BF16), v_ref[...].T.astype(BF16), None)
    n_chunks = DEC_SEQ // LANES
    cvt = jnp.concatenate([cvt_ref[c] for c in range(n_chunks)], axis=1)
    _gqa(cqt_ref, [ctx(cck_ref, ccv_ref), (ck_ref[...], cvt, None)], cz_ref, oc_ref)
    span = 2 * TQ
    t0 = t * TQ
    ws = pl.multiple_of(jnp.clip(t0 - WINDOW, 0, DEC_SEQ - span), WINDOW)
    kpos = ws + lax.broadcasted_iota(jnp.int32, (span, TQ), 0)
    qpos = t0 + lax.broadcasted_iota(jnp.int32, (span, TQ), 1)
    valid = jnp.abs(qpos - kpos) <= WINDOW
    c0 = ws // LANES
    dvt = jnp.concatenate([dvt_ref[c0 + c] for c in range(span // LANES)], axis=1)
    _gqa(dqt_ref, [ctx(cdk_ref, cdv_ref), (dk_ref[pl.ds(ws, span), :], dvt, valid)],
         dz_ref, od_ref, sink_ref)


def _gqa_branch_prompt(cqt, ck, cvt, cz, dqt, dk, dvt, dz, sink):
    n = ck.shape[0]
    row = lambda i: (i, 0)
    qt = pl.BlockSpec((C_W, SEQ), lambda i: (0, i))
    wide = pl.BlockSpec((SEQ, C_W), row)
    kv = pl.BlockSpec((SEQ, KV_W), row)
    vt = pl.BlockSpec((None, KV_W, SEQ), lambda i: (i, 0, 0))
    return pl.pallas_call(
        _gqa_prompt_kernel,
        grid=(n // SEQ,),
        in_specs=[qt, kv, vt, wide, qt, kv, vt, wide, pl.BlockSpec((1, 8), lambda i: (0, 0))],
        out_specs=[wide, wide],
        out_shape=[jax.ShapeDtypeStruct((n, C_W), BF16), jax.ShapeDtypeStruct((n, D_W), BF16)],
        compiler_params=_params("arbitrary"),
        name="gqa_prompt",
    )(cqt, ck, cvt, cz, dqt, dk, dvt, dz, sink)


def _gqa_branch_sample(cqt, ck, cvt, cz, dqt, dk, dvt, dz, sink, cck, ccv, cdk, cdv, layer_idx):
    n = ck.shape[0]
    nq = DEC_SEQ // TQ
    qrow = lambda b, t: (b * nq + t, 0)
    cache = lambda b, t: (b, layer_idx, 0, 0)
    qt = pl.BlockSpec((C_W, TQ), lambda b, t: (0, b * nq + t))
    wide = pl.BlockSpec((TQ, C_W), qrow)
    kv = pl.BlockSpec((DEC_SEQ, KV_W), lambda b, t: (b, 0))
    vt = pl.BlockSpec((DEC_SEQ // LANES, KV_W, LANES), lambda b, t: (b, 0, 0))
    cb = pl.BlockSpec((None, None, PAST_LEN, KV_W), cache)
    return pl.pallas_call(
        _gqa_sample_kernel,
        grid=(DEC_BATCH, nq),
        in_specs=[qt, kv, vt, cb, cb, wide, qt, kv, vt, cb, cb, wide,
                  pl.BlockSpec((1, 8), lambda b, t: (0, 0))],
        out_specs=[wide, wide],
        out_shape=[jax.ShapeDtypeStruct((n, C_W), BF16), jax.ShapeDtypeStruct((n, D_W), BF16)],
        compiler_params=_params("arbitrary", "arbitrary"),
        name="gqa_sample",
    )(cqt, ck, cvt, cck, ccv, cz, dqt, dk, dvt, cdk, cdv, dz, sink)


def _out_kernel(ma_ref, mb_ref, x_ref, w_ref, g_ref, mod_ref, o_ref):
    half = ma_ref.shape[-1]
    o = _dot(ma_ref[...], w_ref[0:half, :]) + _dot(mb_ref[...], w_ref[half:, :])
    ms = jnp.mean(o * o, axis=-1, keepdims=True)
    r = o * lax.rsqrt(ms + NORM_EPS) * g_ref[...]
    gate = mod_ref[...][:, 2 * D_MODEL:]
    o_ref[...] = x_ref[...] + gate * r


def _out_proj(ma, mb, x, w_bf, g_post, mod4, layer, *, sample):
    n = x.shape[0]
    tiles_per_seq = DEC_SEQ // TM
    if sample:
        mod_map = lambda i: (layer, 1 + i // tiles_per_seq, 0, 0)
    else:
        mod_map = lambda i: (layer, 0, 0, 0)
    row = lambda i: (i, 0)
    fixed = lambda i: (0, 0)
    half = ma.shape[1]
    return pl.pallas_call(
        _out_kernel,
        grid=(n // TM,),
        in_specs=[
            pl.BlockSpec((TM, half), row),
            pl.BlockSpec((TM, half), row),
            pl.BlockSpec((TM, D_MODEL), row),
            pl.BlockSpec((2 * half, D_MODEL), fixed),
            pl.BlockSpec((1, D_MODEL), fixed),
            pl.BlockSpec((None, None, 1, 3 * D_MODEL), mod_map),
        ],
        out_specs=pl.BlockSpec((TM, D_MODEL), row),
        out_shape=jax.ShapeDtypeStruct((n, D_MODEL), F32),
        compiler_params=_params("arbitrary"),
        name=f"out_proj_{'s' if sample else 'p'}{layer}",
    )(ma, mb, x, w_bf, g_post, mod4)


def _rope_tables():
    nf = DH // 4
    t = jnp.arange(DEC_SEQ)
    row = (t // GRID_W).astype(F32)
    col = (t % GRID_W).astype(F32)
    inv = ROPE_THETA ** (-jnp.arange(nf, dtype=F32) / nf)
    d = jnp.arange(DH)
    axis = d // (2 * nf)
    second = (d % (2 * nf)) // nf
    f = d % nf
    pos = jnp.where(axis[None, :] == 0, row[:, None], col[:, None])
    ang = pos * inv[f][None, :]
    cos = jnp.cos(ang)
    sin = jnp.where(second[None, :] == 0, -jnp.sin(ang), jnp.sin(ang))
    reps = B_W // DH
    cos = jnp.tile(cos, (1, reps))
    sin = jnp.tile(sin, (1, reps))
    return cos, sin, cos.T, sin.T


def kernel(x_prompt, x_sample, cache_b_k, cache_b_v, cache_c_k, cache_c_v, cache_d_k, cache_d_v, c, c_ctx, norm_pre, norm_post, w_mod, b_mod, w_in_even, a_conv_w, a_conv_b, a_ln_g, a_ln_b, b_lambda, b_subln_g, w_out_even, w_in_odd, c_q_norm, c_k_norm, d_sink, w_out_odd):
    n_even = (DEPTH + 1) // 2
    n_odd = DEPTH // 2
    cond8 = jnp.zeros((SUBLANES, D_MODEL), F32).at[0].set(c_ctx).at[1:1 + DEC_BATCH].set(c)
    mod4 = _modulation(cond8, w_mod, b_mod).reshape(DEPTH, SUBLANES, 1, 3 * D_MODEL)
    tables = _rope_tables()

    xp = x_prompt.reshape(BATCH * SEQ, D_MODEL)
    xs = x_sample.reshape(DEC_BATCH * DEC_SEQ, D_MODEL)
    cbk = cache_b_k.reshape(DEC_BATCH, n_even, PAST_LEN, B_W)
    cbv = cache_b_v.reshape(DEC_BATCH, n_even, PAST_LEN, B_W)
    cck = cache_c_k.reshape(DEC_BATCH, n_odd, PAST_LEN, KV_W)
    ccv = cache_c_v.reshape(DEC_BATCH, n_odd, PAST_LEN, KV_W)
    cdk = cache_d_k.reshape(DEC_BATCH, n_odd, PAST_LEN, KV_W)
    cdv = cache_d_v.reshape(DEC_BATCH, n_odd, PAST_LEN, KV_W)

    nbk, nbv, nck, ncv, ndk, ndv = [], [], [], [], [], []
    for l in range(DEPTH):
        i = l // 2
        g_pre = norm_pre[l].reshape(1, D_MODEL)
        g_post = norm_post[l].reshape(1, D_MODEL)
        if l % 2 == 0:
            lam_init = 0.8 - 0.6 * math.exp(-0.3 * l)
            w_in = w_in_even[i].astype(BF16)
            w_out = w_out_even[i].astype(BF16)
            conv = (a_conv_w[i], a_conv_b[i], a_ln_g[i], a_ln_b[i])
            ug, az, qt, k, kt, v, vt, bz = _in_proj(xp, g_pre, mod4, l, w_in, sample=False)
            ma = _conv_branch(ug, az, *conv, seq_len=SEQ)
            mb = _diff_branch_prompt(qt, k, vt, bz, b_lambda[i], b_subln_g[i], lam_init)
            xp = _out_proj(ma, mb, xp, w_out, g_post, mod4, l, sample=False)
            nbk.append(kt)
            nbv.append(v)
            ug, az, qt, k, vt, bz = _in_proj(xs, g_pre, mod4, l, w_in, sample=True, tables=tables)
            ma = _conv_branch(ug, az, *conv, seq_len=DEC_SEQ)
            mb = _diff_branch_sample(qt, k, vt, bz, cbk, cbv, i, b_lambda[i], b_subln_g[i], lam_init)
            xs = _out_proj(ma, mb, xs, w_out, g_post, mod4, l, sample=True)
        else:
            w_in = w_in_odd[i].astype(BF16)
            w_out = w_out_odd[i].astype(BF16)
            qn = jnp.tile(c_q_norm[i], C_W // DH).reshape(C_W, 1)
            kn = jnp.tile(c_k_norm[i], KV_W // DH).reshape(1, KV_W)
            sink = d_sink[i].reshape(1, 8)
            cqt, ck, ckt, cvt, cz, dqt, dk, dkt, dvt, dz = _in_proj(xp, g_pre, mod4, l, w_in, sample=False,
                                                                    extra=(qn, kn))
            mc, md = _gqa_branch_prompt(cqt, ck, cvt, cz, dqt, dk, dvt, dz, sink)
            xp = _out_proj(mc, md, xp, w_out, g_post, mod4, l, sample=False)
            nck.append(ckt)
            ncv.append(cvt)
            ndk.append(dkt)
            ndv.append(dvt)
            cqt, ck, cvt, cz, dqt, dk, dvt, dz = _in_proj(xs, g_pre, mod4, l, w_in, sample=True,
                                                          extra=(qn, kn), tables=tables)
            mc, md = _gqa_branch_sample(cqt, ck, cvt, cz, dqt, dk, dvt, dz, sink, cck, ccv, cdk, cdv, i)
            xs = _out_proj(mc, md, xs, w_out, g_post, mod4, l, sample=True)

    def feature_major(parts, heads):
        a = jnp.stack(parts, axis=1).reshape((BATCH, len(parts)) + heads + (DH, SEQ))
        return jnp.moveaxis(a, -1, 2)

    new_b_v = jnp.stack(nbv, axis=1).reshape(BATCH, n_even, SEQ, H_B, 2 * DH)
    return (xp.reshape(BATCH, SEQ, D_MODEL), xs.reshape(DEC_BATCH, DEC_SEQ, D_MODEL),
            feature_major(nbk, (H_B, 2)), new_b_v, feature_major(nck, (2,)),
            feature_major(ncv, (2,)), feature_major(ndk, (2,)), feature_major(ndv, (2,)))
```

```python
import functools
import math

import jax
import jax.numpy as jnp
from jax import lax
from jax.experimental import pallas as pl
from jax.experimental.pallas import tpu as pltpu

F32 = jnp.float32
BF16 = jnp.bfloat16

D_MODEL = 1024
BATCH = 16
SEQ = 256
DEPTH = 4
DEC_BATCH = 2
DEC_SEQ = 1024
PAST_LEN = 512
GRID_W = 64
ROPE_THETA = 10000.0
NORM_EPS = 1e-6
DH = 64
A_W = 512
CONV_K = 31
H_B = 4
B_W = 512
C_W = 512
KV_W = 128
D_W = 512
WINDOW = 128
LOG2E = math.log2(math.e)
QK_SCALE = DH ** -0.5 * LOG2E

LANES = 128
SUBLANES = 8
VMEM_LIMIT = 56 * 1024 * 1024

TM = 512
TA = 256
TQ = 256
HALO = 16
ROW_CHUNK = 64


def _params(*sem):
    return pltpu.CompilerParams(dimension_semantics=sem, vmem_limit_bytes=VMEM_LIMIT)


def _silu(x):
    return x * jax.nn.sigmoid(x)


def _dot(a, b):
    return jnp.dot(a, b, preferred_element_type=F32)


def _dot_nt(a, b):
    return lax.dot_general(a, b, (((1,), (1,)), ((), ())), preferred_element_type=F32)


def _mod_kernel(cond_ref, w_ref, b_ref, o_ref):
    a = _silu(cond_ref[...]).astype(BF16)
    o_ref[...] = _dot(a, w_ref[...].astype(BF16)) + b_ref[...]


def _modulation(cond8, w_mod, b_mod):
    nblk = 3
    return pl.pallas_call(
        _mod_kernel,
        grid=(DEPTH, nblk),
        in_specs=[
            pl.BlockSpec((SUBLANES, D_MODEL), lambda l, j: (0, 0)),
            pl.BlockSpec((None, D_MODEL, D_MODEL), lambda l, j: (l, 0, j)),
            pl.BlockSpec((None, 1, D_MODEL), lambda l, j: (l, 0, j)),
        ],
        out_specs=pl.BlockSpec((None, SUBLANES, D_MODEL), lambda l, j: (l, 0, j)),
        out_shape=jax.ShapeDtypeStruct((DEPTH, SUBLANES, 3 * D_MODEL), F32),
        compiler_params=_params("arbitrary", "arbitrary"),
        name="modulation",
    )(cond8, w_mod, b_mod.reshape(DEPTH, 1, 3 * D_MODEL))


def _pre_norm(x_ref, g_ref, mod_ref):
    x = x_ref[...]
    ms = jnp.mean(x * x, axis=-1, keepdims=True)
    mod = mod_ref[...]
    sh = mod[:, :D_MODEL]
    sc = mod[:, D_MODEL:2 * D_MODEL]
    h = (x * lax.rsqrt(ms + NORM_EPS) * g_ref[...]) * (1.0 + sc) + sh
    return h.astype(BF16)


def _rope(x, cos, sin_signed):
    w = x.shape[-1]
    lane = lax.broadcasted_iota(jnp.int32, (1, w), 1)
    first = (lane % 32) < 16
    partner = jnp.where(first, pltpu.roll(x, w - 16, 1), pltpu.roll(x, 16, 1))
    return x * cos + partner * sin_signed


def _rope_t(x, cos_t, sin_t):
    r = x.shape[0]
    row = lax.broadcasted_iota(jnp.int32, (r, 1), 0)
    first = (row % 32) < 16
    partner = jnp.where(first, pltpu.roll(x, r - 16, 0), pltpu.roll(x, 16, 0))
    return x * cos_t + partner * sin_t


def _store_chunks(ref, xt):
    for c in range(xt.shape[1] // LANES):
        ref[c] = xt[:, c * LANES:(c + 1) * LANES]


def _store_per_seq(ref, xt):
    for s in range(xt.shape[1] // SEQ):
        ref[s] = xt[:, s * SEQ:(s + 1) * SEQ]


def _in_even_kernel(x_ref, g_ref, mod_ref, w_ref, *rest, sample, n_alias=0):
    if sample:
        cos_ref, sin_ref, cost_ref, sint_ref, ug_ref, az_ref, qt_ref, k_ref, vt_ref, bz_ref = rest
    else:
        ug_ref, az_ref, qt_ref, k_ref, kt_ref, v_ref, vt_ref, bz_ref = rest[n_alias:]
    hb = _pre_norm(x_ref, g_ref, mod_ref)
    ug_ref[...] = _dot(hb, w_ref[:, 0:2 * A_W])
    az_ref[...] = _dot(hb, w_ref[:, 2 * A_W:3 * A_W])
    o = 3 * A_W
    qt = _dot(hb, w_ref[:, o:o + B_W]).T
    k = _dot(hb, w_ref[:, o + B_W:o + 2 * B_W])
    v = _dot(hb, w_ref[:, o + 2 * B_W:o + 3 * B_W])
    bz_ref[...] = _dot(hb, w_ref[:, o + 3 * B_W:o + 4 * B_W])
    if sample:
        qt = _rope_t(qt, cost_ref[...], sint_ref[...])
        k = _rope(k, cos_ref[...], sin_ref[...])
    else:
        _store_per_seq(kt_ref, k.T)
        for s in range(TM // SEQ):
            for h in range(H_B):
                v_ref[s, pl.ds(h, SEQ, stride=H_B), :] = v[s * SEQ:(s + 1) * SEQ, h * LANES:(h + 1) * LANES]
    qt_ref[...] = (qt * QK_SCALE).astype(BF16)
    k_ref[...] = k.astype(BF16)
    vt_ref[...] = v.T.astype(BF16)


def _group_mean_sq(x):
    width = x.shape[-1]
    xx = x * x
    hi = xx.astype(BF16)
    lo = (xx - hi.astype(F32)).astype(BF16)
    r = lax.broadcasted_iota(jnp.int32, (width, width), 0) // DH
    c = lax.broadcasted_iota(jnp.int32, (width, width), 1) // DH
    g = jnp.where(r == c, 1.0, 0.0).astype(BF16)
    return (_dot(hi, g) + _dot(lo, g)) * (1.0 / DH)


def _head_rms_t(xt, gain_col):
    parts = []
    for j in range(xt.shape[0] // DH):
        blk = xt[j * DH:(j + 1) * DH]
        ms = jnp.mean(blk * blk, axis=0, keepdims=True)
        parts.append(blk * lax.rsqrt(ms + NORM_EPS))
    return jnp.concatenate(parts, axis=0) * gain_col


def _in_odd_kernel(x_ref, g_ref, mod_ref, w_ref, qn_ref, kn_ref, *rest, sample, n_alias=0):
    if sample:
        (cos_ref, sin_ref, cost_ref, sint_ref,
         cqt_ref, ck_ref, cvt_ref, cz_ref, dqt_ref, dk_ref, dvt_ref, dz_ref) = rest
    else:
        (cqt_ref, ck_ref, ckt_ref, cvt_ref, cz_ref,
         dqt_ref, dk_ref, dkt_ref, dvt_ref, dz_ref) = rest[n_alias:]
    hb = _pre_norm(x_ref, g_ref, mod_ref)
    o = 0
    cqt = _head_rms_t(_dot(hb, w_ref[:, o:o + C_W]).T, qn_ref[...])
    o += C_W
    ck = _dot(hb, w_ref[:, o:o + KV_W])
    ck = ck * lax.rsqrt(_group_mean_sq(ck) + NORM_EPS) * kn_ref[...]
    o += KV_W
    cvt = _dot(hb, w_ref[:, o:o + KV_W]).T
    o += KV_W
    cz_ref[...] = _dot(hb, w_ref[:, o:o + C_W])
    o += C_W
    dqt = _dot(hb, w_ref[:, o:o + D_W]).T
    o += D_W
    dk = _dot(hb, w_ref[:, o:o + KV_W])
    o += KV_W
    dvt = _dot(hb, w_ref[:, o:o + KV_W]).T
    o += KV_W
    dz_ref[...] = _dot(hb, w_ref[:, o:o + D_W])
    if sample:
        cos_t = cost_ref[...]
        sin_t = sint_ref[...]
        cqt = _rope_t(cqt, cos_t, sin_t)
        dqt = _rope_t(dqt, cos_t, sin_t)
        cos = cos_ref[...][:, :KV_W]
        sin = sin_ref[...][:, :KV_W]
        ck = _rope(ck, cos, sin)
        dk = _rope(dk, cos, sin)
        _store_chunks(cvt_ref, cvt.astype(BF16))
        _store_chunks(dvt_ref, dvt.astype(BF16))
    else:
        _store_per_seq(ckt_ref, ck.T)
        _store_per_seq(dkt_ref, dk.T)
        _store_per_seq(cvt_ref, cvt)
        _store_per_seq(dvt_ref, dvt)
    cqt_ref[...] = (cqt * QK_SCALE).astype(BF16)
    dqt_ref[...] = (dqt * QK_SCALE).astype(BF16)
    ck_ref[...] = ck.astype(BF16)
    dk_ref[...] = dk.astype(BF16)


def _in_proj(x, g_pre, mod4, layer, w_bf, *, sample, extra=(), tables=None, carry=()):
    n = x.shape[0]
    even = layer % 2 == 0
    tiles_per_seq = DEC_SEQ // TM
    if sample:
        mod_map = lambda i: (layer, 1 + i // tiles_per_seq, 0, 0)
    else:
        mod_map = lambda i: (layer, 0, 0, 0)
    row = lambda i: (i, 0)
    col = lambda i: (0, i)
    fixed = lambda i: (0, 0)
    in_specs = [
        pl.BlockSpec((TM, D_MODEL), row),
        pl.BlockSpec((1, D_MODEL), fixed),
        pl.BlockSpec((None, None, 1, 3 * D_MODEL), mod_map),
        pl.BlockSpec(w_bf.shape, fixed),
    ]
    args = [x, g_pre, mod4, w_bf]
    for e in extra:
        in_specs.append(pl.BlockSpec(e.shape, fixed))
        args.append(e)
    if sample:
        cos, sin, cos_t, sin_t = tables
        in_specs += [pl.BlockSpec((TM, B_W), lambda i: (i % tiles_per_seq, 0)),
                     pl.BlockSpec((TM, B_W), lambda i: (i % tiles_per_seq, 0)),
                     pl.BlockSpec((B_W, TM), lambda i: (0, i % tiles_per_seq)),
                     pl.BlockSpec((B_W, TM), lambda i: (0, i % tiles_per_seq))]
        args += [cos, sin, cos_t, sin_t]
    rows = lambda w, dt: (pl.BlockSpec((TM, w), row), jax.ShapeDtypeStruct((n, w), dt))
    cols = lambda w, dt: (pl.BlockSpec((w, TM), col), jax.ShapeDtypeStruct((w, n), dt))
    chunks = lambda w, dt: (pl.BlockSpec((TM // LANES, w, LANES), lambda i: (i, 0, 0)),
                            jax.ShapeDtypeStruct((n // LANES, w, LANES), dt))
    n_layers = (DEPTH + 1 - layer % 2) // 2
    per_seq = lambda r, c: (pl.BlockSpec((TM // SEQ, None, r, c), lambda i: (i, layer // 2, 0, 0)),
                            jax.ShapeDtypeStruct((n // SEQ, n_layers, r, c), F32))
    if even and sample:
        outs = [rows(2 * A_W, F32), rows(A_W, F32), cols(B_W, BF16), rows(B_W, BF16),
                cols(B_W, BF16), rows(B_W, F32)]
    elif even:
        outs = [rows(2 * A_W, F32), rows(A_W, F32), cols(B_W, BF16), rows(B_W, BF16),
                per_seq(B_W, SEQ), per_seq(SEQ * H_B, LANES), cols(B_W, BF16), rows(B_W, F32)]
    elif sample:
        outs = [cols(C_W, BF16), rows(KV_W, BF16), chunks(KV_W, BF16), rows(C_W, F32),
                cols(D_W, BF16), rows(KV_W, BF16), chunks(KV_W, BF16), rows(D_W, F32)]
    else:
        outs = [cols(C_W, BF16), rows(KV_W, BF16), per_seq(KV_W, SEQ), per_seq(KV_W, SEQ), rows(C_W, F32),
                cols(D_W, BF16), rows(KV_W, BF16), per_seq(KV_W, SEQ), per_seq(KV_W, SEQ), rows(D_W, F32)]
    aliases = {}
    for a in carry:
        out_idx = [j for j, o in enumerate(outs) if o[1].shape == a.shape and j not in aliases.values()][0]
        aliases[len(args)] = out_idx
        in_specs.append(pl.BlockSpec(memory_space=pl.ANY))
        args.append(a)
    body = functools.partial(_in_even_kernel if even else _in_odd_kernel, sample=sample,
                             n_alias=len(carry))
    return pl.pallas_call(
        body,
        grid=(n // TM,),
        in_specs=in_specs,
        out_specs=[o[0] for o in outs],
        out_shape=[o[1] for o in outs],
        input_output_aliases=aliases,
        compiler_params=_params("arbitrary"),
        name=f"in_proj_{'s' if sample else 'p'}{layer}",
    )(*args)


def _conv_kernel(ug_ref, prev_ref, next_ref, az_ref, cw_ref, cb_ref, lg_ref, lb_ref, o_ref, sh_ref,
                 *, tiles_per_seq):
    i = pl.program_id(0)
    pos = i % tiles_per_seq
    has_prev = pos != 0
    has_next = pos != tiles_per_seq - 1

    def glu(ref):
        v = ref[...]
        return v[:, :A_W] * jax.nn.sigmoid(v[:, A_W:])

    pad = jnp.concatenate([jnp.where(has_prev, glu(prev_ref), 0.0), glu(ug_ref),
                           jnp.where(has_next, glu(next_ref), 0.0)], axis=0)
    rows = sh_ref.shape[1]
    for b in range(SUBLANES):
        sh_ref[b] = pad[b:b + rows]

    base = HALO - CONV_K // 2
    for c0 in range(0, A_W, LANES):
        cs = slice(c0, c0 + LANES)
        for r0 in range(0, TA, ROW_CHUNK):
            acc = jnp.zeros((ROW_CHUNK, LANES), F32) + cb_ref[:, cs]
            for k in range(CONV_K):
                j = k + base
                s = r0 + (j // SUBLANES) * SUBLANES
                acc = acc + sh_ref[j % SUBLANES, s:s + ROW_CHUNK, cs] * cw_ref[k:k + 1, cs]
            o_ref[r0:r0 + ROW_CHUNK, cs] = acc.astype(o_ref.dtype)


def _conv_post_kernel(a_ref, az_ref, lg_ref, lb_ref, o_ref):
    a = a_ref[...]
    mu = jnp.mean(a, axis=-1, keepdims=True)
    d = a - mu
    var = jnp.mean(d * d, axis=-1, keepdims=True)
    y = d * lax.rsqrt(var + NORM_EPS) * lg_ref[...] + lb_ref[...]
    o_ref[...] = (_silu(y) * _silu(az_ref[...])).astype(o_ref.dtype)


def _conv_full_kernel(ug_ref, prev_ref, next_ref, az_ref, cw_ref, cb_ref, lg_ref, lb_ref, o_ref,
                      sh_ref, acc_ref, *, tiles_per_seq):
    _conv_kernel(ug_ref, prev_ref, next_ref, az_ref, cw_ref, cb_ref, lg_ref, lb_ref, acc_ref, sh_ref,
                 tiles_per_seq=tiles_per_seq)
    _conv_post_kernel(acc_ref, az_ref, lg_ref, lb_ref, o_ref)


def _conv_branch(ug, az, conv_w, conv_b, ln_g, ln_b, *, seq_len):
    n = ug.shape[0]
    tiles_per_seq = seq_len // TA
    hb = TA // HALO
    last = n // HALO - 1
    row = lambda i: (i, 0)
    fixed = lambda i: (0, 0)
    cw = jnp.zeros((4 * SUBLANES, A_W), F32).at[:CONV_K].set(conv_w)
    sh_rows = TA + 3 * SUBLANES
    return pl.pallas_call(
        functools.partial(_conv_full_kernel, tiles_per_seq=tiles_per_seq),
        grid=(n // TA,),
        in_specs=[
            pl.BlockSpec((TA, 2 * A_W), row),
            pl.BlockSpec((HALO, 2 * A_W), lambda i: (jnp.maximum(i * hb - 1, 0), 0)),
            pl.BlockSpec((HALO, 2 * A_W), lambda i: (jnp.minimum((i + 1) * hb, last), 0)),
            pl.BlockSpec((TA, A_W), row),
            pl.BlockSpec((4 * SUBLANES, A_W), fixed),
            pl.BlockSpec((1, A_W), fixed),
            pl.BlockSpec((1, A_W), fixed),
            pl.BlockSpec((1, A_W), fixed),
        ],
        out_specs=pl.BlockSpec((TA, A_W), row),
        out_shape=jax.ShapeDtypeStruct((n, A_W), BF16),
        scratch_shapes=[pltpu.VMEM((SUBLANES, sh_rows, A_W), F32), pltpu.VMEM((TA, A_W), F32)],
        compiler_params=_params("arbitrary"),
        name=f"conv_branch_{seq_len}",
    )(ug, ug, ug, az, cw, conv_b.reshape(1, A_W), ln_g.reshape(1, A_W), ln_b.reshape(1, A_W))


def _softmax_t(segs, extra=None):
    m = None
    for s in segs:
        mi = jnp.max(s, axis=0, keepdims=True)
        m = mi if m is None else jnp.maximum(m, mi)
    if extra is not None:
        extra = extra * LOG2E
        m = jnp.maximum(m, extra)
    es = [jnp.exp2(s - m) for s in segs]
    den = None
    for e in es:
        di = jnp.sum(e, axis=0, keepdims=True)
        den = di if den is None else den + di
    if extra is not None:
        den = den + jnp.exp2(extra - m)
    return es, den


def _keep_rows(xt, lo, hi):
    zeros = lambda r: jnp.zeros((r, xt.shape[1]), xt.dtype)
    parts = []
    if lo > 0:
        parts.append(zeros(lo))
    parts.append(xt[lo:hi])
    if hi < xt.shape[0]:
        parts.append(zeros(xt.shape[0] - hi))
    return jnp.concatenate(parts, axis=0)


def _pipelined(n, scores, finish):
    cur = scores(0)
    for j in range(n):
        nxt = scores(j + 1) if j + 1 < n else None
        finish(j, cur)
        cur = nxt


def _diff_attn(qt_ref, kvs, z_ref, lam_ref, g_ref, o_ref, *, lam_init):
    lv = lam_ref[...]
    lam = (jnp.exp(jnp.sum(lv[0:1] * lv[1:2], axis=-1, keepdims=True))
           - jnp.exp(jnp.sum(lv[2:3] * lv[3:4], axis=-1, keepdims=True)) + lam_init)

    def scores(h):
        cs = slice(h * LANES, (h + 1) * LANES)
        qt = qt_ref[cs, :]
        ks = [get_k(cs) for get_k, _ in kvs]
        return [[_dot(kk, _keep_rows(qt, c * DH, (c + 1) * DH)) for kk in ks] for c in range(2)]

    def finish(h, ss):
        cs = slice(h * LANES, (h + 1) * LANES)
        es0, den0 = _softmax_t(ss[0])
        es1, den1 = _softmax_t(ss[1])
        r0 = 1.0 / den0
        r1 = lam / den1
        ot = None
        for e0, e1, (_, get_vt) in zip(es0, es1, kvs):
            w = e0 * r0 - e1 * r1
            oi = _dot(get_vt(cs), w.astype(BF16))
            ot = oi if ot is None else ot + oi
        ms = jnp.mean(ot * ot, axis=0, keepdims=True)
        o = (ot * lax.rsqrt(ms + NORM_EPS)).T
        o = (o * g_ref[...]) * (1.0 - lam_init)
        o_ref[:, cs] = (o * _silu(z_ref[:, cs])).astype(o_ref.dtype)

    _pipelined(H_B, scores, finish)


def _diff_prompt_kernel(qt_ref, k_ref, vt_ref, z_ref, lam_ref, g_ref, o_ref, *, lam_init):
    kv = (lambda cs: k_ref[:, cs], lambda cs: vt_ref[cs, :])
    _diff_attn(qt_ref, [kv], z_ref, lam_ref, g_ref, o_ref, lam_init=lam_init)


def _diff_sample_kernel(qt_ref, k_ref, vt_ref, ck_ref, cv_ref, z_ref, lam_ref, g_ref, o_ref,
                        ckb_ref, cvt_ref, *, lam_init):
    @pl.when(pl.program_id(1) == 0)
    def _():
        ckb_ref[...] = ck_ref[...].T.astype(BF16)
        for h in range(H_B):
            cs = slice(h * LANES, (h + 1) * LANES)
            cvt_ref[cs, :] = cv_ref[pl.ds(h, PAST_LEN, stride=H_B), :].T.astype(BF16)

    ctx = (lambda cs: ckb_ref[:, cs], lambda cs: cvt_ref[cs, :])
    loc = (lambda cs: k_ref[:, cs], lambda cs: vt_ref[cs, :])
    _diff_attn(qt_ref, [ctx, loc], z_ref, lam_ref, g_ref, o_ref, lam_init=lam_init)


def _diff_branch_prompt(qt, k, vt, z, lam_vec, subln_g, lam_init):
    n = k.shape[0]
    row = lambda i: (i, 0)
    fixed = lambda i: (0, 0)
    blk = pl.BlockSpec((SEQ, B_W), row)
    tblk = pl.BlockSpec((B_W, SEQ), lambda i: (0, i))
    return pl.pallas_call(
        functools.partial(_diff_prompt_kernel, lam_init=lam_init),
        grid=(n // SEQ,),
        in_specs=[tblk, blk, tblk, blk,
                  pl.BlockSpec((4, DH), fixed), pl.BlockSpec((1, 2 * DH), fixed)],
        out_specs=blk,
        out_shape=jax.ShapeDtypeStruct((n, B_W), BF16),
        compiler_params=_params("arbitrary"),
        name="diff_attn_prompt",
    )(qt, k, vt, z, lam_vec, subln_g.reshape(1, 2 * DH))


def _diff_branch_sample(qt, k, vt, z, cache_k, cache_v, layer_idx, lam_vec, subln_g, lam_init):
    n = k.shape[0]
    nq = DEC_SEQ // TQ
    qrow = lambda b, t: (b * nq + t, 0)
    cache = lambda b, t: (b, layer_idx, 0, 0)
    fixed = lambda b, t: (0, 0)
    return pl.pallas_call(
        functools.partial(_diff_sample_kernel, lam_init=lam_init),
        grid=(DEC_BATCH, nq),
        in_specs=[
            pl.BlockSpec((B_W, TQ), lambda b, t: (0, b * nq + t)),
            pl.BlockSpec((DEC_SEQ, B_W), lambda b, t: (b, 0)),
            pl.BlockSpec((B_W, DEC_SEQ), lambda b, t: (0, b)),
            pl.BlockSpec((None, None, B_W, PAST_LEN), cache),
            pl.BlockSpec((None, None, PAST_LEN * H_B, LANES), cache),
            pl.BlockSpec((TQ, B_W), qrow),
            pl.BlockSpec((4, DH), fixed),
            pl.BlockSpec((1, 2 * DH), fixed),
        ],
        out_specs=pl.BlockSpec((TQ, B_W), qrow),
        out_shape=jax.ShapeDtypeStruct((n, B_W), BF16),
        scratch_shapes=[pltpu.VMEM((PAST_LEN, B_W), BF16), pltpu.VMEM((B_W, PAST_LEN), BF16)],
        compiler_params=_params("arbitrary", "arbitrary"),
        name="diff_attn_sample",
    )(qt, k, vt, cache_k, cache_v, z, lam_vec, subln_g.reshape(1, 2 * DH))


def _gqa(qt_ref, segs, z_ref, o_ref, sink_ref=None):
    halves = []

    def scores(j):
        n = j // 4
        qj = qt_ref[j * DH:(j + 1) * DH, :]
        zero = jnp.zeros_like(qj)
        qz = jnp.concatenate([qj, zero] if n == 0 else [zero, qj], axis=0)
        return [_dot(k, qz) for k, _, _ in segs]

    def finish(j, ss):
        n = j // 4
        ss = [s if valid is None else jnp.where(valid, s, -jnp.inf)
              for s, (_, _, valid) in zip(ss, segs)]
        extra = None if sink_ref is None else sink_ref[:, j:j + 1]
        es, den = _softmax_t(ss, extra)
        ot = None
        for e, (_, vt, _) in zip(es, segs):
            oi = _dot(vt[n * DH:(n + 1) * DH], e.astype(BF16))
            ot = oi if ot is None else ot + oi
        halves.append(ot * (1.0 / den))
        if j % 2 == 1:
            cs = slice((j // 2) * LANES, (j // 2 + 1) * LANES)
            o_pair = jnp.concatenate(halves[-2:], axis=0).T
            o_ref[:, cs] = (o_pair * _silu(z_ref[:, cs])).astype(o_ref.dtype)

    _pipelined(2 * 4, scores, finish)


def _gqa_prompt_kernel(cqt_ref, ck_ref, cvt_ref, cz_ref, dqt_ref, dk_ref, dvt_ref, dz_ref, sink_ref,
                       oc_ref, od_ref):
    seg = lambda k_ref, vt_ref: (k_ref[...], vt_ref[...].astype(BF16), None)
    _gqa(cqt_ref, [seg(ck_ref, cvt_ref)], cz_ref, oc_ref)
    _gqa(dqt_ref, [seg(dk_ref, dvt_ref)], dz_ref, od_ref, sink_ref)


def _gqa_sample_kernel(cqt_ref, ck_ref, cvt_ref, cck_ref, ccv_ref, cz_ref,
                       dqt_ref, dk_ref, dvt_ref, cdk_ref, cdv_ref, dz_ref, sink_ref,
                       oc_ref, od_ref):
    t = pl.program_id(1)
    ctx = lambda kt_ref, vt_ref: (kt_ref[...].T.astype(BF16), vt_ref[...].astype(BF16), None)
    n_chunks = DEC_SEQ // LANES
    cvt = jnp.concatenate([cvt_ref[c] for c in range(n_chunks)], axis=1)
    _gqa(cqt_ref, [ctx(cck_ref, ccv_ref), (ck_ref[...], cvt, None)], cz_ref, oc_ref)
    span = 2 * TQ
    t0 = t * TQ
    ws = pl.multiple_of(jnp.clip(t0 - WINDOW, 0, DEC_SEQ - span), WINDOW)
    kpos = ws + lax.broadcasted_iota(jnp.int32, (span, TQ), 0)
    qpos = t0 + lax.broadcasted_iota(jnp.int32, (span, TQ), 1)
    valid = jnp.abs(qpos - kpos) <= WINDOW
    c0 = ws // LANES
    dvt = jnp.concatenate([dvt_ref[c0 + c] for c in range(span // LANES)], axis=1)
    _gqa(dqt_ref, [ctx(cdk_ref, cdv_ref), (dk_ref[pl.ds(ws, span), :], dvt, valid)],
         dz_ref, od_ref, sink_ref)


def _gqa_branch_prompt(cqt, ck, cvt, cz, dqt, dk, dvt, dz, sink, layer_idx):
    n = ck.shape[0]
    row = lambda i: (i, 0)
    qt = pl.BlockSpec((C_W, SEQ), lambda i: (0, i))
    wide = pl.BlockSpec((SEQ, C_W), row)
    kv = pl.BlockSpec((SEQ, KV_W), row)
    vt = pl.BlockSpec((None, None, KV_W, SEQ), lambda i: (i, layer_idx, 0, 0))
    return pl.pallas_call(
        _gqa_prompt_kernel,
        grid=(n // SEQ,),
        in_specs=[qt, kv, vt, wide, qt, kv, vt, wide, pl.BlockSpec((1, 8), lambda i: (0, 0))],
        out_specs=[wide, wide],
        out_shape=[jax.ShapeDtypeStruct((n, C_W), BF16), jax.ShapeDtypeStruct((n, D_W), BF16)],
        compiler_params=_params("arbitrary"),
        name="gqa_prompt",
    )(cqt, ck, cvt, cz, dqt, dk, dvt, dz, sink)


def _gqa_branch_sample(cqt, ck, cvt, cz, dqt, dk, dvt, dz, sink, cck, ccv, cdk, cdv, layer_idx):
    n = ck.shape[0]
    nq = DEC_SEQ // TQ
    qrow = lambda b, t: (b * nq + t, 0)
    cache = lambda b, t: (b, layer_idx, 0, 0)
    qt = pl.BlockSpec((C_W, TQ), lambda b, t: (0, b * nq + t))
    wide = pl.BlockSpec((TQ, C_W), qrow)
    kv = pl.BlockSpec((DEC_SEQ, KV_W), lambda b, t: (b, 0))
    vt = pl.BlockSpec((DEC_SEQ // LANES, KV_W, LANES), lambda b, t: (b, 0, 0))
    cb = pl.BlockSpec((None, None, KV_W, PAST_LEN), cache)
    return pl.pallas_call(
        _gqa_sample_kernel,
        grid=(DEC_BATCH, nq),
        in_specs=[qt, kv, vt, cb, cb, wide, qt, kv, vt, cb, cb, wide,
                  pl.BlockSpec((1, 8), lambda b, t: (0, 0))],
        out_specs=[wide, wide],
        out_shape=[jax.ShapeDtypeStruct((n, C_W), BF16), jax.ShapeDtypeStruct((n, D_W), BF16)],
        compiler_params=_params("arbitrary", "arbitrary"),
        name="gqa_sample",
    )(cqt, ck, cvt, cck, ccv, cz, dqt, dk, dvt, cdk, cdv, dz, sink)


def _out_kernel(ma_ref, mb_ref, x_ref, w_ref, g_ref, mod_ref, o_ref):
    half = ma_ref.shape[-1]
    o = _dot(ma_ref[...], w_ref[0:half, :]) + _dot(mb_ref[...], w_ref[half:, :])
    ms = jnp.mean(o * o, axis=-1, keepdims=True)
    r = o * lax.rsqrt(ms + NORM_EPS) * g_ref[...]
    gate = mod_ref[...][:, 2 * D_MODEL:]
    o_ref[...] = x_ref[...] + gate * r


def _out_proj(ma, mb, x, w_bf, g_post, mod4, layer, *, sample):
    n = x.shape[0]
    tiles_per_seq = DEC_SEQ // TM
    if sample:
        mod_map = lambda i: (layer, 1 + i // tiles_per_seq, 0, 0)
    else:
        mod_map = lambda i: (layer, 0, 0, 0)
    row = lambda i: (i, 0)
    fixed = lambda i: (0, 0)
    half = ma.shape[1]
    return pl.pallas_call(
        _out_kernel,
        grid=(n // TM,),
        in_specs=[
            pl.BlockSpec((TM, half), row),
            pl.BlockSpec((TM, half), row),
            pl.BlockSpec((TM, D_MODEL), row),
            pl.BlockSpec((2 * half, D_MODEL), fixed),
            pl.BlockSpec((1, D_MODEL), fixed),
            pl.BlockSpec((None, None, 1, 3 * D_MODEL), mod_map),
        ],
        out_specs=pl.BlockSpec((TM, D_MODEL), row),
        out_shape=jax.ShapeDtypeStruct((n, D_MODEL), F32),
        compiler_params=_params("arbitrary"),
        name=f"out_proj_{'s' if sample else 'p'}{layer}",
    )(ma, mb, x, w_bf, g_post, mod4)


def _rope_tables():
    nf = DH // 4
    t = jnp.arange(DEC_SEQ)
    row = (t // GRID_W).astype(F32)
    col = (t % GRID_W).astype(F32)
    inv = ROPE_THETA ** (-jnp.arange(nf, dtype=F32) / nf)
    d = jnp.arange(DH)
    axis = d // (2 * nf)
    second = (d % (2 * nf)) // nf
    f = d % nf
    pos = jnp.where(axis[None, :] == 0, row[:, None], col[:, None])
    ang = pos * inv[f][None, :]
    cos = jnp.cos(ang)
    sin = jnp.where(second[None, :] == 0, -jnp.sin(ang), jnp.sin(ang))
    reps = B_W // DH
    cos = jnp.tile(cos, (1, reps))
    sin = jnp.tile(sin, (1, reps))
    return cos, sin, cos.T, sin.T


def kernel(x_prompt, x_sample, cache_b_k, cache_b_v, cache_c_k, cache_c_v, cache_d_k, cache_d_v, c, c_ctx, norm_pre, norm_post, w_mod, b_mod, w_in_even, a_conv_w, a_conv_b, a_ln_g, a_ln_b, b_lambda, b_subln_g, w_out_even, w_in_odd, c_q_norm, c_k_norm, d_sink, w_out_odd):
    n_even = (DEPTH + 1) // 2
    n_odd = DEPTH // 2
    cond8 = jnp.zeros((SUBLANES, D_MODEL), F32).at[0].set(c_ctx).at[1:1 + DEC_BATCH].set(c)
    mod4 = _modulation(cond8, w_mod, b_mod).reshape(DEPTH, SUBLANES, 1, 3 * D_MODEL)
    tables = _rope_tables()

    xp = x_prompt.reshape(BATCH * SEQ, D_MODEL)
    xs = x_sample.reshape(DEC_BATCH * DEC_SEQ, D_MODEL)
    feat = lambda a, w: jnp.moveaxis(a.reshape(a.shape[:3] + (w,)), 2, 3)
    cbk = feat(cache_b_k, B_W)
    cbv = cache_b_v.reshape(DEC_BATCH, n_even, PAST_LEN * H_B, 2 * DH)
    cck = feat(cache_c_k, KV_W)
    ccv = feat(cache_c_v, KV_W)
    cdk = feat(cache_d_k, KV_W)
    cdv = feat(cache_d_v, KV_W)

    new_even, new_odd = (), ()
    for l in range(DEPTH):
        i = l // 2
        g_pre = norm_pre[l].reshape(1, D_MODEL)
        g_post = norm_post[l].reshape(1, D_MODEL)
        if l % 2 == 0:
            lam_init = 0.8 - 0.6 * math.exp(-0.3 * l)
            w_in = w_in_even[i].astype(BF16)
            w_out = w_out_even[i].astype(BF16)
            conv = (a_conv_w[i], a_conv_b[i], a_ln_g[i], a_ln_b[i])
            ug, az, qt, k, kt, v, vt, bz = _in_proj(xp, g_pre, mod4, l, w_in, sample=False, carry=new_even)
            new_even = (kt, v)
            ma = _conv_branch(ug, az, *conv, seq_len=SEQ)
            mb = _diff_branch_prompt(qt, k, vt, bz, b_lambda[i], b_subln_g[i], lam_init)
            xp = _out_proj(ma, mb, xp, w_out, g_post, mod4, l, sample=False)
            ug, az, qt, k, vt, bz = _in_proj(xs, g_pre, mod4, l, w_in, sample=True, tables=tables)
            ma = _conv_branch(ug, az, *conv, seq_len=DEC_SEQ)
            mb = _diff_branch_sample(qt, k, vt, bz, cbk, cbv, i, b_lambda[i], b_subln_g[i], lam_init)
            xs = _out_proj(ma, mb, xs, w_out, g_post, mod4, l, sample=True)
        else:
            w_in = w_in_odd[i].astype(BF16)
            w_out = w_out_odd[i].astype(BF16)
            qn = jnp.tile(c_q_norm[i], C_W // DH).reshape(C_W, 1)
            kn = jnp.tile(c_k_norm[i], KV_W // DH).reshape(1, KV_W)
            sink = d_sink[i].reshape(1, 8)
            cqt, ck, ckt, cvt, cz, dqt, dk, dkt, dvt, dz = _in_proj(
                xp, g_pre, mod4, l, w_in, sample=False, extra=(qn, kn), carry=new_odd)
            new_odd = (ckt, cvt, dkt, dvt)
            mc, md = _gqa_branch_prompt(cqt, ck, cvt, cz, dqt, dk, dvt, dz, sink, i)
            xp = _out_proj(mc, md, xp, w_out, g_post, mod4, l, sample=False)
            cqt, ck, cvt, cz, dqt, dk, dvt, dz = _in_proj(xs, g_pre, mod4, l, w_in, sample=True,
                                                          extra=(qn, kn), tables=tables)
            mc, md = _gqa_branch_sample(cqt, ck, cvt, cz, dqt, dk, dvt, dz, sink, cck, ccv, cdk, cdv, i)
            xs = _out_proj(mc, md, xs, w_out, g_post, mod4, l, sample=True)

    def token_major(a, heads):
        return jnp.moveaxis(a.reshape(a.shape[:2] + heads + (DH, SEQ)), -1, 2)

    kt, v = new_even
    ckt, cvt, dkt, dvt = new_odd
    return (xp.reshape(BATCH, SEQ, D_MODEL), xs.reshape(DEC_BATCH, DEC_SEQ, D_MODEL),
            token_major(kt, (H_B, 2)), v.reshape(BATCH, n_even, SEQ, H_B, 2 * DH),
            token_major(ckt, (2,)), token_major(cvt, (2,)), token_major(dkt, (2,)), token_major(dvt, (2,)))
```

```python
import functools
import math

import jax
import jax.numpy as jnp
from jax import lax
from jax.experimental import pallas as pl
from jax.experimental.pallas import tpu as pltpu

F32 = jnp.float32
BF16 = jnp.bfloat16

D_MODEL = 1024
BATCH = 16
SEQ = 256
DEPTH = 4
DEC_BATCH = 2
DEC_SEQ = 1024
PAST_LEN = 512
GRID_W = 64
ROPE_THETA = 10000.0
NORM_EPS = 1e-6
DH = 64
A_W = 512
CONV_K = 31
H_B = 4
B_W = 512
C_W = 512
KV_W = 128
D_W = 512
WINDOW = 128
LOG2E = math.log2(math.e)
QK_SCALE = DH ** -0.5 * LOG2E

LANES = 128
SUBLANES = 8
VMEM_LIMIT = 56 * 1024 * 1024

TM = 512
TQ = SEQ
HALO = 16
ROW_CHUNK = 64


def _params(*sem):
    return pltpu.CompilerParams(dimension_semantics=sem, vmem_limit_bytes=VMEM_LIMIT)


def _silu(x):
    return x * jax.nn.sigmoid(x)


def _dot(a, b):
    return jnp.dot(a, b, preferred_element_type=F32)


def _dot_nt(a, b):
    return lax.dot_general(a, b, (((1,), (1,)), ((), ())), preferred_element_type=F32)


def _mod_kernel(cond_ref, w_ref, b_ref, o_ref):
    a = _silu(cond_ref[...]).astype(BF16)
    o_ref[...] = _dot(a, w_ref[...].astype(BF16)) + b_ref[...]


def _modulation(cond8, w_mod, b_mod):
    nblk = 3
    return pl.pallas_call(
        _mod_kernel,
        grid=(DEPTH, nblk),
        in_specs=[
            pl.BlockSpec((SUBLANES, D_MODEL), lambda l, j: (0, 0)),
            pl.BlockSpec((None, D_MODEL, D_MODEL), lambda l, j: (l, 0, j)),
            pl.BlockSpec((None, 1, D_MODEL), lambda l, j: (l, 0, j)),
        ],
        out_specs=pl.BlockSpec((None, SUBLANES, D_MODEL), lambda l, j: (l, 0, j)),
        out_shape=jax.ShapeDtypeStruct((DEPTH, SUBLANES, 3 * D_MODEL), F32),
        compiler_params=_params("arbitrary", "arbitrary"),
        name="modulation",
    )(cond8, w_mod, b_mod.reshape(DEPTH, 1, 3 * D_MODEL))


def _pre_norm(x_ref, g_ref, mod_ref):
    return _modulate(x_ref[...], g_ref, mod_ref)


def _modulate(x, g_ref, mod_ref):
    ms = jnp.mean(x * x, axis=-1, keepdims=True)
    mod = mod_ref[...]
    sh = mod[:, :D_MODEL]
    sc = mod[:, D_MODEL:2 * D_MODEL]
    h = (x * lax.rsqrt(ms + NORM_EPS) * g_ref[...]) * (1.0 + sc) + sh
    return h.astype(BF16)


def _rope(x, cos, sin_signed):
    w = x.shape[-1]
    lane = lax.broadcasted_iota(jnp.int32, (1, w), 1)
    first = (lane % 32) < 16
    partner = jnp.where(first, pltpu.roll(x, w - 16, 1), pltpu.roll(x, 16, 1))
    return x * cos + partner * sin_signed


def _rope_t(x, cos_t, sin_t):
    r = x.shape[0]
    row = lax.broadcasted_iota(jnp.int32, (r, 1), 0)
    first = (row % 32) < 16
    partner = jnp.where(first, pltpu.roll(x, r - 16, 0), pltpu.roll(x, 16, 0))
    return x * cos_t + partner * sin_t


def _store_chunks(ref, xt):
    for c in range(xt.shape[1] // LANES):
        ref[c] = xt[:, c * LANES:(c + 1) * LANES]


def _store_per_seq(ref, xt):
    for s in range(xt.shape[1] // SEQ):
        ref[s] = xt[:, s * SEQ:(s + 1) * SEQ]


def _glu(ug):
    return ug[:, :A_W] * jax.nn.sigmoid(ug[:, A_W:])


def _conv_mix(pad, az, cw_ref, cb_ref, lg_ref, lb_ref, sh_ref, acc_ref):
    rows = sh_ref.shape[1]
    for b in range(SUBLANES):
        sh_ref[b] = pad[b:b + rows]
    base = HALO - CONV_K // 2
    for c0 in range(0, A_W, LANES):
        cs = slice(c0, c0 + LANES)
        for r0 in range(0, SEQ, ROW_CHUNK):
            acc = jnp.zeros((ROW_CHUNK, LANES), F32) + cb_ref[:, cs]
            for k in range(CONV_K):
                j = k + base
                s = r0 + (j // SUBLANES) * SUBLANES
                acc = acc + sh_ref[j % SUBLANES, s:s + ROW_CHUNK, cs] * cw_ref[k:k + 1, cs]
            acc_ref[r0:r0 + ROW_CHUNK, cs] = acc
    a = acc_ref[...]
    mu = jnp.mean(a, axis=-1, keepdims=True)
    d = a - mu
    var = jnp.mean(d * d, axis=-1, keepdims=True)
    y = d * lax.rsqrt(var + NORM_EPS) * lg_ref[...] + lb_ref[...]
    return _silu(y) * _silu(az)


def _in_even_kernel(x_ref, g_ref, mod_ref, w_ref, cw_ref, cb_ref, lg_ref, lb_ref, *rest,
                    sample, n_alias=0):
    if sample:
        (xp_ref, xn_ref, cos_ref, sin_ref, cost_ref, sint_ref,
         ma_ref, qt_ref, k_ref, vt_ref, bz_ref, sh_ref, acc_ref) = rest
    else:
        ma_ref, qt_ref, k_ref, kt_ref, v_ref, vt_ref, bz_ref, sh_ref, acc_ref = rest[n_alias:]
    hb = _pre_norm(x_ref, g_ref, mod_ref)
    a = _glu(_dot(hb, w_ref[:, 0:2 * A_W]))
    az = _dot(hb, w_ref[:, 2 * A_W:3 * A_W])
    n_sub = TM // SEQ
    if sample:
        tiles_per_seq = DEC_SEQ // TM
        pos = pl.program_id(0) % tiles_per_seq
        xh = jnp.concatenate([xp_ref[...], xn_ref[...]], axis=0)
        ah = _glu(_dot(_modulate(xh, g_ref, mod_ref), w_ref[:, 0:2 * A_W]))
        prev = jnp.where(pos != 0, ah[:HALO], 0.0)
        nxt = jnp.where(pos != tiles_per_seq - 1, ah[HALO:], 0.0)
        full = jnp.concatenate([prev, a, nxt], axis=0)
        pads = [full[j * SEQ:(j + 1) * SEQ + 2 * HALO] for j in range(n_sub)]
    else:
        zeros = jnp.zeros((HALO, A_W), F32)
        pads = [jnp.concatenate([zeros, a[j * SEQ:(j + 1) * SEQ], zeros], axis=0) for j in range(n_sub)]
    o = 3 * A_W
    qt = _dot(hb, w_ref[:, o:o + B_W]).T
    k = _dot(hb, w_ref[:, o + B_W:o + 2 * B_W])
    v = _dot(hb, w_ref[:, o + 2 * B_W:o + 3 * B_W])
    bz_ref[...] = _dot(hb, w_ref[:, o + 3 * B_W:o + 4 * B_W])
    if sample:
        qt = _rope_t(qt, cost_ref[...], sint_ref[...])
        k = _rope(k, cos_ref[...], sin_ref[...])
    else:
        _store_per_seq(kt_ref, k.T)
        for s in range(TM // SEQ):
            for h in range(H_B):
                v_ref[s, pl.ds(h, SEQ, stride=H_B), :] = v[s * SEQ:(s + 1) * SEQ, h * LANES:(h + 1) * LANES]
    _store_per_seq(qt_ref, (qt * QK_SCALE).astype(BF16))
    k_ref[...] = k.astype(BF16)
    _store_per_seq(vt_ref, v.T.astype(BF16))
    for j, pad in enumerate(pads):
        rs = slice(j * SEQ, (j + 1) * SEQ)
        ma_ref[rs, :] = _conv_mix(pad, az[rs], cw_ref, cb_ref, lg_ref, lb_ref, sh_ref, acc_ref).astype(BF16)


def _group_mean_sq(x):
    width = x.shape[-1]
    xx = x * x
    hi = xx.astype(BF16)
    lo = (xx - hi.astype(F32)).astype(BF16)
    r = lax.broadcasted_iota(jnp.int32, (width, width), 0) // DH
    c = lax.broadcasted_iota(jnp.int32, (width, width), 1) // DH
    g = jnp.where(r == c, 1.0, 0.0).astype(BF16)
    return (_dot(hi, g) + _dot(lo, g)) * (1.0 / DH)


def _head_rms_t(xt, gain_col):
    parts = []
    for j in range(xt.shape[0] // DH):
        blk = xt[j * DH:(j + 1) * DH]
        ms = jnp.mean(blk * blk, axis=0, keepdims=True)
        parts.append(blk * lax.rsqrt(ms + NORM_EPS))
    return jnp.concatenate(parts, axis=0) * gain_col


def _in_odd_kernel(x_ref, g_ref, mod_ref, w_ref, qn_ref, kn_ref, *rest, sample, n_alias=0):
    if sample:
        (cos_ref, sin_ref, cost_ref, sint_ref,
         cqt_ref, ck_ref, cvt_ref, cz_ref, dqt_ref, dk_ref, dvt_ref, dz_ref) = rest
    else:
        (cqt_ref, ck_ref, ckt_ref, cvt_ref, cz_ref,
         dqt_ref, dk_ref, dkt_ref, dvt_ref, dz_ref) = rest[n_alias:]
    hb = _pre_norm(x_ref, g_ref, mod_ref)
    o = 0
    cqt = _head_rms_t(_dot(hb, w_ref[:, o:o + C_W]).T, qn_ref[...])
    o += C_W
    ck = _dot(hb, w_ref[:, o:o + KV_W])
    ck = ck * lax.rsqrt(_group_mean_sq(ck) + NORM_EPS) * kn_ref[...]
    o += KV_W
    cvt = _dot(hb, w_ref[:, o:o + KV_W]).T
    o += KV_W
    cz_ref[...] = _dot(hb, w_ref[:, o:o + C_W])
    o += C_W
    dqt = _dot(hb, w_ref[:, o:o + D_W]).T
    o += D_W
    dk = _dot(hb, w_ref[:, o:o + KV_W])
    o += KV_W
    dvt = _dot(hb, w_ref[:, o:o + KV_W]).T
    o += KV_W
    dz_ref[...] = _dot(hb, w_ref[:, o:o + D_W])
    if sample:
        cos_t = cost_ref[...]
        sin_t = sint_ref[...]
        cqt = _rope_t(cqt, cos_t, sin_t)
        dqt = _rope_t(dqt, cos_t, sin_t)
        cos = cos_ref[...][:, :KV_W]
        sin = sin_ref[...][:, :KV_W]
        ck = _rope(ck, cos, sin)
        dk = _rope(dk, cos, sin)
        _store_chunks(cvt_ref, cvt.astype(BF16))
        _store_chunks(dvt_ref, dvt.astype(BF16))
    else:
        _store_per_seq(ckt_ref, ck.T)
        _store_per_seq(dkt_ref, dk.T)
        _store_per_seq(cvt_ref, cvt)
        _store_per_seq(dvt_ref, dvt)
    _store_per_seq(cqt_ref, (cqt * QK_SCALE).astype(BF16))
    _store_per_seq(dqt_ref, (dqt * QK_SCALE).astype(BF16))
    ck_ref[...] = ck.astype(BF16)
    dk_ref[...] = dk.astype(BF16)


def _in_proj(x, g_pre, mod4, layer, w_bf, *, sample, extra=(), tables=None, carry=()):
    n = x.shape[0]
    even = layer % 2 == 0
    tiles_per_seq = DEC_SEQ // TM
    if sample:
        mod_map = lambda i: (layer, 1 + i // tiles_per_seq, 0, 0)
    else:
        mod_map = lambda i: (layer, 0, 0, 0)
    row = lambda i: (i, 0)
    fixed = lambda i: (0, 0)
    in_specs = [
        pl.BlockSpec((TM, D_MODEL), row),
        pl.BlockSpec((1, D_MODEL), fixed),
        pl.BlockSpec((None, None, 1, 3 * D_MODEL), mod_map),
        pl.BlockSpec(w_bf.shape, fixed),
    ]
    args = [x, g_pre, mod4, w_bf]
    for e in extra:
        in_specs.append(pl.BlockSpec(e.shape, fixed))
        args.append(e)
    if sample:
        if even:
            hb = TM // HALO
            last = n // HALO - 1
            in_specs += [pl.BlockSpec((HALO, D_MODEL), lambda i: (jnp.maximum(i * hb - 1, 0), 0)),
                         pl.BlockSpec((HALO, D_MODEL), lambda i: (jnp.minimum((i + 1) * hb, last), 0))]
            args += [x, x]
        cos, sin, cos_t, sin_t = tables
        in_specs += [pl.BlockSpec((TM, B_W), lambda i: (i % tiles_per_seq, 0)),
                     pl.BlockSpec((TM, B_W), lambda i: (i % tiles_per_seq, 0)),
                     pl.BlockSpec((B_W, TM), lambda i: (0, i % tiles_per_seq)),
                     pl.BlockSpec((B_W, TM), lambda i: (0, i % tiles_per_seq))]
        args += [cos, sin, cos_t, sin_t]
    rows = lambda w, dt: (pl.BlockSpec((TM, w), row), jax.ShapeDtypeStruct((n, w), dt))
    cols = lambda w, dt: (pl.BlockSpec((TM // SEQ, w, SEQ), lambda i: (i, 0, 0)),
                          jax.ShapeDtypeStruct((n // SEQ, w, SEQ), dt))
    chunks = lambda w, dt: (pl.BlockSpec((TM // LANES, w, LANES), lambda i: (i, 0, 0)),
                            jax.ShapeDtypeStruct((n // LANES, w, LANES), dt))
    n_layers = (DEPTH + 1 - layer % 2) // 2
    per_seq = lambda r, c: (pl.BlockSpec((TM // SEQ, None, r, c), lambda i: (i, layer // 2, 0, 0)),
                            jax.ShapeDtypeStruct((n // SEQ, n_layers, r, c), F32))
    scratch = []
    if even:
        scratch = [pltpu.VMEM((SUBLANES, SEQ + 3 * SUBLANES, A_W), F32), pltpu.VMEM((SEQ, A_W), F32)]
    if even and sample:
        outs = [rows(A_W, BF16), cols(B_W, BF16), rows(B_W, BF16), cols(B_W, BF16), rows(B_W, F32)]
    elif even:
        outs = [rows(A_W, BF16), cols(B_W, BF16), rows(B_W, BF16),
                per_seq(B_W, SEQ), per_seq(SEQ * H_B, LANES), cols(B_W, BF16), rows(B_W, F32)]
    elif sample:
        outs = [cols(C_W, BF16), rows(KV_W, BF16), chunks(KV_W, BF16), rows(C_W, F32),
                cols(D_W, BF16), rows(KV_W, BF16), chunks(KV_W, BF16), rows(D_W, F32)]
    else:
        outs = [cols(C_W, BF16), rows(KV_W, BF16), per_seq(KV_W, SEQ), per_seq(KV_W, SEQ), rows(C_W, F32),
                cols(D_W, BF16), rows(KV_W, BF16), per_seq(KV_W, SEQ), per_seq(KV_W, SEQ), rows(D_W, F32)]
    aliases = {}
    for a in carry:
        out_idx = [j for j, o in enumerate(outs) if o[1].shape == a.shape and j not in aliases.values()][0]
        aliases[len(args)] = out_idx
        in_specs.append(pl.BlockSpec(memory_space=pl.ANY))
        args.append(a)
    body = functools.partial(_in_even_kernel if even else _in_odd_kernel, sample=sample,
                             n_alias=len(carry))
    return pl.pallas_call(
        body,
        grid=(n // TM,),
        in_specs=in_specs,
        out_specs=[o[0] for o in outs],
        out_shape=[o[1] for o in outs],
        scratch_shapes=scratch,
        input_output_aliases=aliases,
        compiler_params=_params("arbitrary"),
        name=f"in_proj_{'s' if sample else 'p'}{layer}",
    )(*args)


def _softmax_t(segs, extra=None):
    m = None
    for s in segs:
        mi = jnp.max(s, axis=0, keepdims=True)
        m = mi if m is None else jnp.maximum(m, mi)
    if extra is not None:
        extra = extra * LOG2E
        m = jnp.maximum(m, extra)
    es = [jnp.exp2(s - m) for s in segs]
    den = None
    for e in es:
        di = jnp.sum(e, axis=0, keepdims=True)
        den = di if den is None else den + di
    if extra is not None:
        den = den + jnp.exp2(extra - m)
    return es, den


def _keep_rows(xt, lo, hi):
    zeros = lambda r: jnp.zeros((r, xt.shape[1]), xt.dtype)
    parts = []
    if lo > 0:
        parts.append(zeros(lo))
    parts.append(xt[lo:hi])
    if hi < xt.shape[0]:
        parts.append(zeros(xt.shape[0] - hi))
    return jnp.concatenate(parts, axis=0)


def _pipelined(n, scores, finish):
    cur = scores(0)
    for j in range(n):
        nxt = scores(j + 1) if j + 1 < n else None
        finish(j, cur)
        cur = nxt


def _diff_attn(qt_ref, kvs, z_ref, lam_ref, g_ref, o_ref, *, lam_init):
    lv = lam_ref[...]
    lam = (jnp.exp(jnp.sum(lv[0:1] * lv[1:2], axis=-1, keepdims=True))
           - jnp.exp(jnp.sum(lv[2:3] * lv[3:4], axis=-1, keepdims=True)) + lam_init)

    def scores(h):
        cs = slice(h * LANES, (h + 1) * LANES)
        qt = qt_ref[cs, :]
        ks = [get_k(cs) for get_k, _ in kvs]
        return [[_dot(kk, _keep_rows(qt, c * DH, (c + 1) * DH)) for kk in ks] for c in range(2)]

    def finish(h, ss):
        cs = slice(h * LANES, (h + 1) * LANES)
        es0, den0 = _softmax_t(ss[0])
        es1, den1 = _softmax_t(ss[1])
        r0 = 1.0 / den0
        r1 = lam / den1
        ot = None
        for e0, e1, (_, get_vt) in zip(es0, es1, kvs):
            w = e0 * r0 - e1 * r1
            oi = _dot(get_vt(cs), w.astype(BF16))
            ot = oi if ot is None else ot + oi
        ms = jnp.mean(ot * ot, axis=0, keepdims=True)
        o = (ot * lax.rsqrt(ms + NORM_EPS)).T
        o = (o * g_ref[...]) * (1.0 - lam_init)
        o_ref[:, cs] = (o * _silu(z_ref[:, cs])).astype(o_ref.dtype)

    _pipelined(H_B, scores, finish)


def _diff_prompt_kernel(qt_ref, k_ref, vt_ref, z_ref, lam_ref, g_ref, o_ref, *, lam_init):
    kv = (lambda cs: k_ref[:, cs], lambda cs: vt_ref[cs, :])
    _diff_attn(qt_ref, [kv], z_ref, lam_ref, g_ref, o_ref, lam_init=lam_init)


def _diff_sample_kernel(qt_ref, k_ref, vt_ref, ck_ref, cv_ref, z_ref, lam_ref, g_ref, o_ref,
                        ckb_ref, cvt_ref, *, lam_init):
    @pl.when(pl.program_id(1) == 0)
    def _():
        ckb_ref[...] = ck_ref[...].T.astype(BF16)
        for h in range(H_B):
            cs = slice(h * LANES, (h + 1) * LANES)
            cvt_ref[cs, :] = cv_ref[pl.ds(h, PAST_LEN, stride=H_B), :].T.astype(BF16)

    ctx = (lambda cs: ckb_ref[:, cs], lambda cs: cvt_ref[cs, :])
    loc = (lambda cs: k_ref[:, cs],
           lambda cs: jnp.concatenate([vt_ref[c, cs, :] for c in range(vt_ref.shape[0])], axis=1))
    _diff_attn(qt_ref, [ctx, loc], z_ref, lam_ref, g_ref, o_ref, lam_init=lam_init)


def _diff_branch_prompt(qt, k, vt, z, lam_vec, subln_g, lam_init):
    n = k.shape[0]
    row = lambda i: (i, 0)
    fixed = lambda i: (0, 0)
    blk = pl.BlockSpec((SEQ, B_W), row)
    tblk = pl.BlockSpec((None, B_W, SEQ), lambda i: (i, 0, 0))
    return pl.pallas_call(
        functools.partial(_diff_prompt_kernel, lam_init=lam_init),
        grid=(n // SEQ,),
        in_specs=[tblk, blk, tblk, blk,
                  pl.BlockSpec((4, DH), fixed), pl.BlockSpec((1, 2 * DH), fixed)],
        out_specs=blk,
        out_shape=jax.ShapeDtypeStruct((n, B_W), BF16),
        compiler_params=_params("arbitrary"),
        name="diff_attn_prompt",
    )(qt, k, vt, z, lam_vec, subln_g.reshape(1, 2 * DH))


def _diff_branch_sample(qt, k, vt, z, cache_k, cache_v, layer_idx, lam_vec, subln_g, lam_init):
    n = k.shape[0]
    nq = DEC_SEQ // TQ
    qrow = lambda b, t: (b * nq + t, 0)
    cache = lambda b, t: (b, layer_idx, 0, 0)
    fixed = lambda b, t: (0, 0)
    return pl.pallas_call(
        functools.partial(_diff_sample_kernel, lam_init=lam_init),
        grid=(DEC_BATCH, nq),
        in_specs=[
            pl.BlockSpec((None, B_W, TQ), lambda b, t: (b * nq + t, 0, 0)),
            pl.BlockSpec((DEC_SEQ, B_W), lambda b, t: (b, 0)),
            pl.BlockSpec((nq, B_W, TQ), lambda b, t: (b, 0, 0)),
            pl.BlockSpec((None, None, B_W, PAST_LEN), cache),
            pl.BlockSpec((None, None, PAST_LEN * H_B, LANES), cache),
            pl.BlockSpec((TQ, B_W), qrow),
            pl.BlockSpec((4, DH), fixed),
            pl.BlockSpec((1, 2 * DH), fixed),
        ],
        out_specs=pl.BlockSpec((TQ, B_W), qrow),
        out_shape=jax.ShapeDtypeStruct((n, B_W), BF16),
        scratch_shapes=[pltpu.VMEM((PAST_LEN, B_W), BF16), pltpu.VMEM((B_W, PAST_LEN), BF16)],
        compiler_params=_params("arbitrary", "arbitrary"),
        name="diff_attn_sample",
    )(qt, k, vt, cache_k, cache_v, z, lam_vec, subln_g.reshape(1, 2 * DH))


def _gqa(qt_ref, segs, z_ref, o_ref, sink_ref=None):
    halves = []

    def scores(j):
        n = j // 4
        qj = qt_ref[j * DH:(j + 1) * DH, :]
        zero = jnp.zeros_like(qj)
        qz = jnp.concatenate([qj, zero] if n == 0 else [zero, qj], axis=0)
        return [_dot(k, qz) for k, _, _ in segs]

    def finish(j, ss):
        n = j // 4
        ss = [s if valid is None else jnp.where(valid, s, -jnp.inf)
              for s, (_, _, valid) in zip(ss, segs)]
        extra = None if sink_ref is None else sink_ref[:, j:j + 1]
        es, den = _softmax_t(ss, extra)
        ot = None
        for e, (_, vt, _) in zip(es, segs):
            oi = _dot(vt[n * DH:(n + 1) * DH], e.astype(BF16))
            ot = oi if ot is None else ot + oi
        halves.append(ot * (1.0 / den))
        if j % 2 == 1:
            cs = slice((j // 2) * LANES, (j // 2 + 1) * LANES)
            o_pair = jnp.concatenate(halves[-2:], axis=0).T
            o_ref[:, cs] = (o_pair * _silu(z_ref[:, cs])).astype(o_ref.dtype)

    _pipelined(2 * 4, scores, finish)


def _gqa_prompt_kernel(cqt_ref, ck_ref, cvt_ref, cz_ref, dqt_ref, dk_ref, dvt_ref, dz_ref, sink_ref,
                       oc_ref, od_ref):
    seg = lambda k_ref, vt_ref: (k_ref[...], vt_ref[...].astype(BF16), None)
    _gqa(cqt_ref, [seg(ck_ref, cvt_ref)], cz_ref, oc_ref)
    _gqa(dqt_ref, [seg(dk_ref, dvt_ref)], dz_ref, od_ref, sink_ref)


def _gqa_sample_kernel(cqt_ref, ck_ref, cvt_ref, cck_ref, ccv_ref, cz_ref,
                       dqt_ref, dk_ref, dvt_ref, cdk_ref, cdv_ref, dz_ref, sink_ref,
                       oc_ref, od_ref):
    t = pl.program_id(1)
    ctx = lambda kt_ref, vt_ref: (kt_ref[...].T.astype(BF16), vt_ref[...].astype(BF16), None)
    n_chunks = DEC_SEQ // LANES
    cvt = jnp.concatenate([cvt_ref[c] for c in range(n_chunks)], axis=1)
    _gqa(cqt_ref, [ctx(cck_ref, ccv_ref), (ck_ref[...], cvt, None)], cz_ref, oc_ref)
    span = 2 * TQ
    t0 = t * TQ
    ws = pl.multiple_of(jnp.clip(t0 - WINDOW, 0, DEC_SEQ - span), WINDOW)
    kpos = ws + lax.broadcasted_iota(jnp.int32, (span, TQ), 0)
    qpos = t0 + lax.broadcasted_iota(jnp.int32, (span, TQ), 1)
    valid = jnp.abs(qpos - kpos) <= WINDOW
    c0 = ws // LANES
    dvt = jnp.concatenate([dvt_ref[c0 + c] for c in range(span // LANES)], axis=1)
    _gqa(dqt_ref, [ctx(cdk_ref, cdv_ref), (dk_ref[pl.ds(ws, span), :], dvt, valid)],
         dz_ref, od_ref, sink_ref)


def _gqa_branch_prompt(cqt, ck, cvt, cz, dqt, dk, dvt, dz, sink, layer_idx):
    n = ck.shape[0]
    row = lambda i: (i, 0)
    qt = pl.BlockSpec((None, C_W, SEQ), lambda i: (i, 0, 0))
    wide = pl.BlockSpec((SEQ, C_W), row)
    kv = pl.BlockSpec((SEQ, KV_W), row)
    vt = pl.BlockSpec((None, None, KV_W, SEQ), lambda i: (i, layer_idx, 0, 0))
    return pl.pallas_call(
        _gqa_prompt_kernel,
        grid=(n // SEQ,),
        in_specs=[qt, kv, vt, wide, qt, kv, vt, wide, pl.BlockSpec((1, 8), lambda i: (0, 0))],
        out_specs=[wide, wide],
        out_shape=[jax.ShapeDtypeStruct((n, C_W), BF16), jax.ShapeDtypeStruct((n, D_W), BF16)],
        compiler_params=_params("arbitrary"),
        name="gqa_prompt",
    )(cqt, ck, cvt, cz, dqt, dk, dvt, dz, sink)


def _gqa_branch_sample(cqt, ck, cvt, cz, dqt, dk, dvt, dz, sink, cck, ccv, cdk, cdv, layer_idx):
    n = ck.shape[0]
    nq = DEC_SEQ // TQ
    qrow = lambda b, t: (b * nq + t, 0)
    cache = lambda b, t: (b, layer_idx, 0, 0)
    qt = pl.BlockSpec((None, C_W, TQ), lambda b, t: (b * nq + t, 0, 0))
    wide = pl.BlockSpec((TQ, C_W), qrow)
    kv = pl.BlockSpec((DEC_SEQ, KV_W), lambda b, t: (b, 0))
    vt = pl.BlockSpec((DEC_SEQ // LANES, KV_W, LANES), lambda b, t: (b, 0, 0))
    cb = pl.BlockSpec((None, None, KV_W, PAST_LEN), cache)
    return pl.pallas_call(
        _gqa_sample_kernel,
        grid=(DEC_BATCH, nq),
        in_specs=[qt, kv, vt, cb, cb, wide, qt, kv, vt, cb, cb, wide,
                  pl.BlockSpec((1, 8), lambda b, t: (0, 0))],
        out_specs=[wide, wide],
        out_shape=[jax.ShapeDtypeStruct((n, C_W), BF16), jax.ShapeDtypeStruct((n, D_W), BF16)],
        compiler_params=_params("arbitrary", "arbitrary"),
        name="gqa_sample",
    )(cqt, ck, cvt, cck, ccv, cz, dqt, dk, dvt, cdk, cdv, dz, sink)


def _out_kernel(ma_ref, mb_ref, x_ref, w_ref, g_ref, mod_ref, o_ref):
    half = ma_ref.shape[-1]
    o = _dot(ma_ref[...], w_ref[0:half, :]) + _dot(mb_ref[...], w_ref[half:, :])
    ms = jnp.mean(o * o, axis=-1, keepdims=True)
    r = o * lax.rsqrt(ms + NORM_EPS) * g_ref[...]
    gate = mod_ref[...][:, 2 * D_MODEL:]
    o_ref[...] = x_ref[...] + gate * r


def _out_proj(ma, mb, x, w_bf, g_post, mod4, layer, *, sample):
    n = x.shape[0]
    tiles_per_seq = DEC_SEQ // TM
    if sample:
        mod_map = lambda i: (layer, 1 + i // tiles_per_seq, 0, 0)
    else:
        mod_map = lambda i: (layer, 0, 0, 0)
    row = lambda i: (i, 0)
    fixed = lambda i: (0, 0)
    half = ma.shape[1]
    return pl.pallas_call(
        _out_kernel,
        grid=(n // TM,),
        in_specs=[
            pl.BlockSpec((TM, half), row),
            pl.BlockSpec((TM, half), row),
            pl.BlockSpec((TM, D_MODEL), row),
            pl.BlockSpec((2 * half, D_MODEL), fixed),
            pl.BlockSpec((1, D_MODEL), fixed),
            pl.BlockSpec((None, None, 1, 3 * D_MODEL), mod_map),
        ],
        out_specs=pl.BlockSpec((TM, D_MODEL), row),
        out_shape=jax.ShapeDtypeStruct((n, D_MODEL), F32),
        compiler_params=_params("arbitrary"),
        name=f"out_proj_{'s' if sample else 'p'}{layer}",
    )(ma, mb, x, w_bf, g_post, mod4)


def _rope_tables():
    nf = DH // 4
    t = jnp.arange(DEC_SEQ)
    row = (t // GRID_W).astype(F32)
    col = (t % GRID_W).astype(F32)
    inv = ROPE_THETA ** (-jnp.arange(nf, dtype=F32) / nf)
    d = jnp.arange(DH)
    axis = d // (2 * nf)
    second = (d % (2 * nf)) // nf
    f = d % nf
    pos = jnp.where(axis[None, :] == 0, row[:, None], col[:, None])
    ang = pos * inv[f][None, :]
    cos = jnp.cos(ang)
    sin = jnp.where(second[None, :] == 0, -jnp.sin(ang), jnp.sin(ang))
    reps = B_W // DH
    cos = jnp.tile(cos, (1, reps))
    sin = jnp.tile(sin, (1, reps))
    return cos, sin, cos.T, sin.T


def kernel(x_prompt, x_sample, cache_b_k, cache_b_v, cache_c_k, cache_c_v, cache_d_k, cache_d_v, c, c_ctx, norm_pre, norm_post, w_mod, b_mod, w_in_even, a_conv_w, a_conv_b, a_ln_g, a_ln_b, b_lambda, b_subln_g, w_out_even, w_in_odd, c_q_norm, c_k_norm, d_sink, w_out_odd):
    n_even = (DEPTH + 1) // 2
    n_odd = DEPTH // 2
    cond8 = jnp.zeros((SUBLANES, D_MODEL), F32).at[0].set(c_ctx).at[1:1 + DEC_BATCH].set(c)
    mod4 = _modulation(cond8, w_mod, b_mod).reshape(DEPTH, SUBLANES, 1, 3 * D_MODEL)
    tables = _rope_tables()

    xp = x_prompt.reshape(BATCH * SEQ, D_MODEL)
    xs = x_sample.reshape(DEC_BATCH * DEC_SEQ, D_MODEL)
    feat = lambda a, w: jnp.moveaxis(a.reshape(a.shape[:3] + (w,)), 2, 3)
    cbk = feat(cache_b_k, B_W)
    cbv = cache_b_v.reshape(DEC_BATCH, n_even, PAST_LEN * H_B, 2 * DH)
    cck = feat(cache_c_k, KV_W)
    ccv = feat(cache_c_v, KV_W)
    cdk = feat(cache_d_k, KV_W)
    cdv = feat(cache_d_v, KV_W)

    new_even, new_odd = (), ()
    for l in range(DEPTH):
        i = l // 2
        g_pre = norm_pre[l].reshape(1, D_MODEL)
        g_post = norm_post[l].reshape(1, D_MODEL)
        if l % 2 == 0:
            lam_init = 0.8 - 0.6 * math.exp(-0.3 * l)
            w_in = w_in_even[i].astype(BF16)
            w_out = w_out_even[i].astype(BF16)
            conv = (jnp.zeros((4 * SUBLANES, A_W), F32).at[:CONV_K].set(a_conv_w[i]),
                    a_conv_b[i].reshape(1, A_W), a_ln_g[i].reshape(1, A_W), a_ln_b[i].reshape(1, A_W))
            ma, qt, k, kt, v, vt, bz = _in_proj(xp, g_pre, mod4, l, w_in, sample=False, extra=conv,
                                                carry=new_even)
            new_even = (kt, v)
            mb = _diff_branch_prompt(qt, k, vt, bz, b_lambda[i], b_subln_g[i], lam_init)
            xp = _out_proj(ma, mb, xp, w_out, g_post, mod4, l, sample=False)
            ma, qt, k, vt, bz = _in_proj(xs, g_pre, mod4, l, w_in, sample=True, extra=conv, tables=tables)
            mb = _diff_branch_sample(qt, k, vt, bz, cbk, cbv, i, b_lambda[i], b_subln_g[i], lam_init)
            xs = _out_proj(ma, mb, xs, w_out, g_post, mod4, l, sample=True)
        else:
            w_in = w_in_odd[i].astype(BF16)
            w_out = w_out_odd[i].astype(BF16)
            qn = jnp.tile(c_q_norm[i], C_W // DH).reshape(C_W, 1)
            kn = jnp.tile(c_k_norm[i], KV_W // DH).reshape(1, KV_W)
            sink = d_sink[i].reshape(1, 8)
            cqt, ck, ckt, cvt, cz, dqt, dk, dkt, dvt, dz = _in_proj(
                xp, g_pre, mod4, l, w_in, sample=False, extra=(qn, kn), carry=new_odd)
            new_odd = (ckt, cvt, dkt, dvt)
            mc, md = _gqa_branch_prompt(cqt, ck, cvt, cz, dqt, dk, dvt, dz, sink, i)
            xp = _out_proj(mc, md, xp, w_out, g_post, mod4, l, sample=False)
            cqt, ck, cvt, cz, dqt, dk, dvt, dz = _in_proj(xs, g_pre, mod4, l, w_in, sample=True,
                                                          extra=(qn, kn), tables=tables)
            mc, md = _gqa_branch_sample(cqt, ck, cvt, cz, dqt, dk, dvt, dz, sink, cck, ccv, cdk, cdv, i)
            xs = _out_proj(mc, md, xs, w_out, g_post, mod4, l, sample=True)

    def token_major(a, heads):
        return jnp.moveaxis(a.reshape(a.shape[:2] + heads + (DH, SEQ)), -1, 2)

    kt, v = new_even
    ckt, cvt, dkt, dvt = new_odd
    return (xp.reshape(BATCH, SEQ, D_MODEL), xs.reshape(DEC_BATCH, DEC_SEQ, D_MODEL),
            token_major(kt, (H_B, 2)), v.reshape(BATCH, n_even, SEQ, H_B, 2 * DH),
            token_major(ckt, (2,)), token_major(cvt, (2,)), token_major(dkt, (2,)), token_major(dvt, (2,)))
```

```python
import functools
import math

import jax
import jax.numpy as jnp
from jax import lax
from jax.experimental import pallas as pl
from jax.experimental.pallas import tpu as pltpu

F32 = jnp.float32
BF16 = jnp.bfloat16

D_MODEL = 1024
BATCH = 16
SEQ = 256
DEPTH = 4
DEC_BATCH = 2
DEC_SEQ = 1024
PAST_LEN = 512
GRID_W = 64
ROPE_THETA = 10000.0
NORM_EPS = 1e-6
DH = 64
A_W = 512
CONV_K = 31
H_B = 4
B_W = 512
C_W = 512
KV_W = 128
D_W = 512
WINDOW = 128
LOG2E = math.log2(math.e)
QK_SCALE = DH ** -0.5 * LOG2E

LANES = 128
SUBLANES = 8
VMEM_LIMIT = 56 * 1024 * 1024

TM = 512
TQ = SEQ
HALO = 16
ROW_CHUNK = 64


def _params(*sem):
    return pltpu.CompilerParams(dimension_semantics=sem, vmem_limit_bytes=VMEM_LIMIT)


def _silu(x):
    return x * jax.nn.sigmoid(x)


def _dot(a, b):
    return jnp.dot(a, b, preferred_element_type=F32)


def _dot_nt(a, b):
    return lax.dot_general(a, b, (((1,), (1,)), ((), ())), preferred_element_type=F32)


def _mod_kernel(cond_ref, w_ref, b_ref, o_ref):
    a = _silu(cond_ref[...]).astype(BF16)
    o_ref[...] = _dot(a, w_ref[...].astype(BF16)) + b_ref[...]


def _modulation(cond8, w_mod, b_mod):
    nblk = 3
    return pl.pallas_call(
        _mod_kernel,
        grid=(DEPTH, nblk),
        in_specs=[
            pl.BlockSpec((SUBLANES, D_MODEL), lambda l, j: (0, 0)),
            pl.BlockSpec((None, D_MODEL, D_MODEL), lambda l, j: (l, 0, j)),
            pl.BlockSpec((None, 1, D_MODEL), lambda l, j: (l, 0, j)),
        ],
        out_specs=pl.BlockSpec((None, SUBLANES, D_MODEL), lambda l, j: (l, 0, j)),
        out_shape=jax.ShapeDtypeStruct((DEPTH, SUBLANES, 3 * D_MODEL), F32),
        compiler_params=_params("arbitrary", "arbitrary"),
        name="modulation",
    )(cond8, w_mod, b_mod.reshape(DEPTH, 1, 3 * D_MODEL))


def _pre_norm(x_ref, g_ref, mod_ref):
    return _modulate(x_ref[...], g_ref, mod_ref)


def _modulate(x, g_ref, mod_ref):
    ms = jnp.mean(x * x, axis=-1, keepdims=True)
    mod = mod_ref[...]
    sh = mod[:, :D_MODEL]
    sc = mod[:, D_MODEL:2 * D_MODEL]
    h = (x * lax.rsqrt(ms + NORM_EPS) * g_ref[...]) * (1.0 + sc) + sh
    return h.astype(BF16)


def _rope(x, cos, sin_signed):
    w = x.shape[-1]
    lane = lax.broadcasted_iota(jnp.int32, (1, w), 1)
    first = (lane % 32) < 16
    partner = jnp.where(first, pltpu.roll(x, w - 16, 1), pltpu.roll(x, 16, 1))
    return x * cos + partner * sin_signed


def _rope_t(x, cos_t, sin_t):
    r = x.shape[0]
    row = lax.broadcasted_iota(jnp.int32, (r, 1), 0)
    first = (row % 32) < 16
    partner = jnp.where(first, pltpu.roll(x, r - 16, 0), pltpu.roll(x, 16, 0))
    return x * cos_t + partner * sin_t


def _store_chunks(ref, xt):
    for c in range(xt.shape[1] // LANES):
        ref[c] = xt[:, c * LANES:(c + 1) * LANES]


def _store_per_seq(ref, xt):
    for s in range(xt.shape[1] // SEQ):
        ref[s] = xt[:, s * SEQ:(s + 1) * SEQ]


def _glu(ug):
    return ug[:, :A_W] * jax.nn.sigmoid(ug[:, A_W:])


def _conv_mix(pad, az, cw_ref, cb_ref, lg_ref, lb_ref, sh_ref, acc_ref):
    rows = sh_ref.shape[1]
    for b in range(SUBLANES):
        sh_ref[b] = pad[b:b + rows]
    base = HALO - CONV_K // 2
    for c0 in range(0, A_W, LANES):
        cs = slice(c0, c0 + LANES)
        for r0 in range(0, SEQ, ROW_CHUNK):
            acc = jnp.zeros((ROW_CHUNK, LANES), F32) + cb_ref[:, cs]
            for k in range(CONV_K):
                j = k + base
                s = r0 + (j // SUBLANES) * SUBLANES
                acc = acc + sh_ref[j % SUBLANES, s:s + ROW_CHUNK, cs] * cw_ref[k:k + 1, cs]
            acc_ref[r0:r0 + ROW_CHUNK, cs] = acc
    a = acc_ref[...]
    mu = jnp.mean(a, axis=-1, keepdims=True)
    d = a - mu
    var = jnp.mean(d * d, axis=-1, keepdims=True)
    y = d * lax.rsqrt(var + NORM_EPS) * lg_ref[...] + lb_ref[...]
    return _silu(y) * _silu(az)


def _in_even_kernel(x_ref, g_ref, mod_ref, w_ref, cw_ref, cb_ref, lg_ref, lb_ref, *rest,
                    sample, n_alias=0):
    if sample:
        (xp_ref, xn_ref, cos_ref, sin_ref, cost_ref, sint_ref,
         ma_ref, qt_ref, k_ref, vt_ref, bz_ref, sh_ref, acc_ref) = rest
    else:
        ma_ref, qt_ref, k_ref, kt_ref, v_ref, vt_ref, bz_ref, sh_ref, acc_ref = rest[n_alias:]
    hb = _pre_norm(x_ref, g_ref, mod_ref)
    a = _glu(_dot(hb, w_ref[:, 0:2 * A_W]))
    az = _dot(hb, w_ref[:, 2 * A_W:3 * A_W])
    n_sub = TM // SEQ
    if sample:
        tiles_per_seq = DEC_SEQ // TM
        pos = pl.program_id(0) % tiles_per_seq
        xh = jnp.concatenate([xp_ref[...], xn_ref[...]], axis=0)
        ah = _glu(_dot(_modulate(xh, g_ref, mod_ref), w_ref[:, 0:2 * A_W]))
        prev = jnp.where(pos != 0, ah[:HALO], 0.0)
        nxt = jnp.where(pos != tiles_per_seq - 1, ah[HALO:], 0.0)
        full = jnp.concatenate([prev, a, nxt], axis=0)
        pads = [full[j * SEQ:(j + 1) * SEQ + 2 * HALO] for j in range(n_sub)]
    else:
        zeros = jnp.zeros((HALO, A_W), F32)
        pads = [jnp.concatenate([zeros, a[j * SEQ:(j + 1) * SEQ], zeros], axis=0) for j in range(n_sub)]
    o = 3 * A_W
    qt = _dot(hb, w_ref[:, o:o + B_W]).T
    k = _dot(hb, w_ref[:, o + B_W:o + 2 * B_W])
    v = _dot(hb, w_ref[:, o + 2 * B_W:o + 3 * B_W])
    bz_ref[...] = _dot(hb, w_ref[:, o + 3 * B_W:o + 4 * B_W])
    if sample:
        qt = _rope_t(qt, cost_ref[...], sint_ref[...])
        k = _rope(k, cos_ref[...], sin_ref[...])
    else:
        _store_per_seq(kt_ref, k.T)
        for s in range(TM // SEQ):
            for h in range(H_B):
                v_ref[s, pl.ds(h, SEQ, stride=H_B), :] = v[s * SEQ:(s + 1) * SEQ, h * LANES:(h + 1) * LANES]
    _store_per_seq(qt_ref, (qt * QK_SCALE).astype(BF16))
    k_ref[...] = k.astype(BF16)
    _store_per_seq(vt_ref, v.T.astype(BF16))
    for j, pad in enumerate(pads):
        rs = slice(j * SEQ, (j + 1) * SEQ)
        ma_ref[rs, :] = _conv_mix(pad, az[rs], cw_ref, cb_ref, lg_ref, lb_ref, sh_ref, acc_ref).astype(BF16)


def _group_mean_sq(x):
    width = x.shape[-1]
    xx = x * x
    hi = xx.astype(BF16)
    lo = (xx - hi.astype(F32)).astype(BF16)
    r = lax.broadcasted_iota(jnp.int32, (width, width), 0) // DH
    c = lax.broadcasted_iota(jnp.int32, (width, width), 1) // DH
    g = jnp.where(r == c, 1.0, 0.0).astype(BF16)
    return (_dot(hi, g) + _dot(lo, g)) * (1.0 / DH)


def _head_rms_t(xt, gain_col):
    parts = []
    for j in range(xt.shape[0] // DH):
        blk = xt[j * DH:(j + 1) * DH]
        ms = jnp.mean(blk * blk, axis=0, keepdims=True)
        parts.append(blk * lax.rsqrt(ms + NORM_EPS))
    return jnp.concatenate(parts, axis=0) * gain_col


def _in_odd_kernel(x_ref, g_ref, mod_ref, w_ref, qn_ref, kn_ref, *rest, sample, n_alias=0):
    if sample:
        (cos_ref, sin_ref, cost_ref, sint_ref,
         cqt_ref, ck_ref, cvt_ref, cz_ref, dqt_ref, dk_ref, dvt_ref, dz_ref) = rest
    else:
        (cqt_ref, ck_ref, ckt_ref, cvt_ref, cz_ref,
         dqt_ref, dk_ref, dkt_ref, dvt_ref, dz_ref) = rest[n_alias:]
    hb = _pre_norm(x_ref, g_ref, mod_ref)
    o = 0
    cqt = _head_rms_t(_dot(hb, w_ref[:, o:o + C_W]).T, qn_ref[...])
    o += C_W
    ck = _dot(hb, w_ref[:, o:o + KV_W])
    ck = ck * lax.rsqrt(_group_mean_sq(ck) + NORM_EPS) * kn_ref[...]
    o += KV_W
    cvt = _dot(hb, w_ref[:, o:o + KV_W]).T
    o += KV_W
    cz_ref[...] = _dot(hb, w_ref[:, o:o + C_W])
    o += C_W
    dqt = _dot(hb, w_ref[:, o:o + D_W]).T
    o += D_W
    dk = _dot(hb, w_ref[:, o:o + KV_W])
    o += KV_W
    dvt = _dot(hb, w_ref[:, o:o + KV_W]).T
    o += KV_W
    dz_ref[...] = _dot(hb, w_ref[:, o:o + D_W])
    if sample:
        cos_t = cost_ref[...]
        sin_t = sint_ref[...]
        cqt = _rope_t(cqt, cos_t, sin_t)
        dqt = _rope_t(dqt, cos_t, sin_t)
        cos = cos_ref[...][:, :KV_W]
        sin = sin_ref[...][:, :KV_W]
        ck = _rope(ck, cos, sin)
        dk = _rope(dk, cos, sin)
        _store_chunks(cvt_ref, cvt.astype(BF16))
        _store_chunks(dvt_ref, dvt.astype(BF16))
    else:
        _store_per_seq(ckt_ref, ck.T)
        _store_per_seq(dkt_ref, dk.T)
        _store_per_seq(cvt_ref, cvt)
        _store_per_seq(dvt_ref, dvt)
    _store_per_seq(cqt_ref, (cqt * QK_SCALE).astype(BF16))
    _store_per_seq(dqt_ref, (dqt * QK_SCALE).astype(BF16))
    ck_ref[...] = ck.astype(BF16)
    dk_ref[...] = dk.astype(BF16)


def _in_proj(x, g_pre, mod4, layer, w_bf, *, sample, extra=(), tables=None, carry=()):
    n = x.shape[0]
    even = layer % 2 == 0
    tiles_per_seq = DEC_SEQ // TM
    if sample:
        mod_map = lambda i: (layer, 1 + i // tiles_per_seq, 0, 0)
    else:
        mod_map = lambda i: (layer, 0, 0, 0)
    row = lambda i: (i, 0)
    fixed = lambda i: (0, 0)
    in_specs = [
        pl.BlockSpec((TM, D_MODEL), row),
        pl.BlockSpec((1, D_MODEL), fixed),
        pl.BlockSpec((None, None, 1, 3 * D_MODEL), mod_map),
        pl.BlockSpec(w_bf.shape, fixed),
    ]
    args = [x, g_pre, mod4, w_bf]
    for e in extra:
        in_specs.append(pl.BlockSpec(e.shape, fixed))
        args.append(e)
    if sample:
        if even:
            hb = TM // HALO
            last = n // HALO - 1
            in_specs += [pl.BlockSpec((HALO, D_MODEL), lambda i: (jnp.maximum(i * hb - 1, 0), 0)),
                         pl.BlockSpec((HALO, D_MODEL), lambda i: (jnp.minimum((i + 1) * hb, last), 0))]
            args += [x, x]
        cos, sin, cos_t, sin_t = tables
        in_specs += [pl.BlockSpec((TM, B_W), lambda i: (i % tiles_per_seq, 0)),
                     pl.BlockSpec((TM, B_W), lambda i: (i % tiles_per_seq, 0)),
                     pl.BlockSpec((B_W, TM), lambda i: (0, i % tiles_per_seq)),
                     pl.BlockSpec((B_W, TM), lambda i: (0, i % tiles_per_seq))]
        args += [cos, sin, cos_t, sin_t]
    rows = lambda w, dt: (pl.BlockSpec((TM, w), row), jax.ShapeDtypeStruct((n, w), dt))
    cols = lambda w, dt: (pl.BlockSpec((TM // SEQ, w, SEQ), lambda i: (i, 0, 0)),
                          jax.ShapeDtypeStruct((n // SEQ, w, SEQ), dt))
    chunks = lambda w, dt: (pl.BlockSpec((TM // LANES, w, LANES), lambda i: (i, 0, 0)),
                            jax.ShapeDtypeStruct((n // LANES, w, LANES), dt))
    n_layers = (DEPTH + 1 - layer % 2) // 2
    per_seq = lambda r, c: (pl.BlockSpec((TM // SEQ, None, r, c), lambda i: (i, layer // 2, 0, 0)),
                            jax.ShapeDtypeStruct((n // SEQ, n_layers, r, c), F32))
    scratch = []
    if even:
        scratch = [pltpu.VMEM((SUBLANES, SEQ + 3 * SUBLANES, A_W), F32), pltpu.VMEM((SEQ, A_W), F32)]
    if even and sample:
        outs = [rows(A_W, BF16), cols(B_W, BF16), rows(B_W, BF16), cols(B_W, BF16), rows(B_W, F32)]
    elif even:
        outs = [rows(A_W, BF16), cols(B_W, BF16), rows(B_W, BF16),
                per_seq(B_W, SEQ), per_seq(SEQ * H_B, LANES), cols(B_W, BF16), rows(B_W, F32)]
    elif sample:
        outs = [cols(C_W, BF16), rows(KV_W, BF16), chunks(KV_W, BF16), rows(C_W, F32),
                cols(D_W, BF16), rows(KV_W, BF16), chunks(KV_W, BF16), rows(D_W, F32)]
    else:
        outs = [cols(C_W, BF16), rows(KV_W, BF16), per_seq(KV_W, SEQ), per_seq(KV_W, SEQ), rows(C_W, F32),
                cols(D_W, BF16), rows(KV_W, BF16), per_seq(KV_W, SEQ), per_seq(KV_W, SEQ), rows(D_W, F32)]
    aliases = {}
    for a in carry:
        out_idx = [j for j, o in enumerate(outs) if o[1].shape == a.shape and j not in aliases.values()][0]
        aliases[len(args)] = out_idx
        in_specs.append(pl.BlockSpec(memory_space=pl.ANY))
        args.append(a)
    body = functools.partial(_in_even_kernel if even else _in_odd_kernel, sample=sample,
                             n_alias=len(carry))
    return pl.pallas_call(
        body,
        grid=(n // TM,),
        in_specs=in_specs,
        out_specs=[o[0] for o in outs],
        out_shape=[o[1] for o in outs],
        scratch_shapes=scratch,
        input_output_aliases=aliases,
        compiler_params=_params("arbitrary"),
        name=f"in_proj_{'s' if sample else 'p'}{layer}",
    )(*args)


def _softmax_t(segs, extra=None):
    m = None
    for s in segs:
        mi = jnp.max(s, axis=0, keepdims=True)
        m = mi if m is None else jnp.maximum(m, mi)
    if extra is not None:
        extra = extra * LOG2E
        m = jnp.maximum(m, extra)
    es = [jnp.exp2(s - m) for s in segs]
    den = None
    for e in es:
        di = jnp.sum(e, axis=0, keepdims=True)
        den = di if den is None else den + di
    if extra is not None:
        den = den + jnp.exp2(extra - m)
    return es, den


def _keep_rows(xt, lo, hi):
    zeros = lambda r: jnp.zeros((r, xt.shape[1]), xt.dtype)
    parts = []
    if lo > 0:
        parts.append(zeros(lo))
    parts.append(xt[lo:hi])
    if hi < xt.shape[0]:
        parts.append(zeros(xt.shape[0] - hi))
    return jnp.concatenate(parts, axis=0)


def _pipelined(n, scores, finish):
    cur = scores(0)
    for j in range(n):
        nxt = scores(j + 1) if j + 1 < n else None
        finish(j, cur)
        cur = nxt


def _diff_attn(qt_ref, kvs, z_ref, lam_ref, g_ref, o_ref, *, lam_init):
    lv = lam_ref[...]
    lam = (jnp.exp(jnp.sum(lv[0:1] * lv[1:2], axis=-1, keepdims=True))
           - jnp.exp(jnp.sum(lv[2:3] * lv[3:4], axis=-1, keepdims=True)) + lam_init)

    def scores(h):
        cs = slice(h * LANES, (h + 1) * LANES)
        qt = qt_ref[cs, :]
        ks = [get_k(cs) for get_k, _ in kvs]
        return [[_dot(kk, _keep_rows(qt, c * DH, (c + 1) * DH)) for kk in ks] for c in range(2)]

    def finish(h, ss):
        cs = slice(h * LANES, (h + 1) * LANES)
        es0, den0 = _softmax_t(ss[0])
        es1, den1 = _softmax_t(ss[1])
        r0 = 1.0 / den0
        r1 = lam / den1
        ot = None
        for e0, e1, (_, get_vt) in zip(es0, es1, kvs):
            w = e0 * r0 - e1 * r1
            oi = _dot(get_vt(cs), w.astype(BF16))
            ot = oi if ot is None else ot + oi
        ms = jnp.mean(ot * ot, axis=0, keepdims=True)
        o = (ot * lax.rsqrt(ms + NORM_EPS)).T
        o = (o * g_ref[...]) * (1.0 - lam_init)
        o_ref[:, cs] = (o * _silu(z_ref[:, cs])).astype(o_ref.dtype)

    _pipelined(H_B, scores, finish)


def _post_residual(o, x_ref, g_ref, mod_ref, o_ref):
    ms = jnp.mean(o * o, axis=-1, keepdims=True)
    r = o * lax.rsqrt(ms + NORM_EPS) * g_ref[...]
    gate = mod_ref[...][:, 2 * D_MODEL:]
    o_ref[...] = x_ref[...] + gate * r


def _diff_prompt_kernel(qt_ref, k_ref, vt_ref, z_ref, lam_ref, sg_ref, ma_ref, x_ref, w_ref, g_ref,
                        mod_ref, o_ref, mb_ref, *, lam_init):
    oa = _dot(ma_ref[...], w_ref[0:A_W, :])
    for s in range(qt_ref.shape[0]):
        rs = pl.ds(s * SEQ, SEQ)
        kv = (lambda cs, s=s: k_ref[s * SEQ:(s + 1) * SEQ, cs], lambda cs, s=s: vt_ref[s, cs, :])
        _diff_attn(qt_ref.at[s], [kv], z_ref.at[rs], lam_ref, sg_ref, mb_ref.at[rs], lam_init=lam_init)
    _post_residual(oa + _dot(mb_ref[...], w_ref[A_W:, :]), x_ref, g_ref, mod_ref, o_ref)


def _diff_sample_kernel(qt_ref, k_ref, vt_ref, ck_ref, cv_ref, z_ref, lam_ref, sg_ref, ma_ref, x_ref,
                        w_ref, g_ref, mod_ref, o_ref, ckb_ref, cvt_ref, mb_ref, *, lam_init):
    @pl.when(pl.program_id(1) == 0)
    def _():
        ckb_ref[...] = ck_ref[...].T.astype(BF16)
        for h in range(H_B):
            cs = slice(h * LANES, (h + 1) * LANES)
            cvt_ref[cs, :] = cv_ref[pl.ds(h, PAST_LEN, stride=H_B), :].T.astype(BF16)

    ctx = (lambda cs: ckb_ref[:, cs], lambda cs: cvt_ref[cs, :])
    loc = (lambda cs: k_ref[:, cs],
           lambda cs: jnp.concatenate([vt_ref[c, cs, :] for c in range(vt_ref.shape[0])], axis=1))
    oa = _dot(ma_ref[...], w_ref[0:A_W, :])
    _diff_attn(qt_ref, [ctx, loc], z_ref, lam_ref, sg_ref, mb_ref, lam_init=lam_init)
    _post_residual(oa + _dot(mb_ref[...], w_ref[A_W:, :]), x_ref, g_ref, mod_ref, o_ref)


def _out_specs_prompt(layer):
    return [pl.BlockSpec((TM, D_MODEL), lambda i: (i, 0)),
            pl.BlockSpec((D_MODEL, D_MODEL), lambda i: (0, 0)),
            pl.BlockSpec((1, D_MODEL), lambda i: (0, 0)),
            pl.BlockSpec((None, None, 1, 3 * D_MODEL), lambda i: (layer, 0, 0, 0))]


def _out_specs_sample(layer):
    nq = DEC_SEQ // TQ
    return [pl.BlockSpec((TQ, D_MODEL), lambda b, t: (b * nq + t, 0)),
            pl.BlockSpec((D_MODEL, D_MODEL), lambda b, t: (0, 0)),
            pl.BlockSpec((1, D_MODEL), lambda b, t: (0, 0)),
            pl.BlockSpec((None, None, 1, 3 * D_MODEL), lambda b, t: (layer, 1 + b, 0, 0))]


def _diff_layer_prompt(qt, k, vt, z, ma, x, w_out, g_post, mod4, layer, lam_vec, subln_g, lam_init):
    n = x.shape[0]
    row = lambda i: (i, 0)
    fixed = lambda i: (0, 0)
    blk = pl.BlockSpec((TM, B_W), row)
    tblk = pl.BlockSpec((TM // SEQ, B_W, SEQ), lambda i: (i, 0, 0))
    return pl.pallas_call(
        functools.partial(_diff_prompt_kernel, lam_init=lam_init),
        grid=(n // TM,),
        in_specs=[tblk, blk, tblk, blk, pl.BlockSpec((4, DH), fixed), pl.BlockSpec((1, 2 * DH), fixed),
                  blk] + _out_specs_prompt(layer),
        out_specs=pl.BlockSpec((TM, D_MODEL), row),
        out_shape=jax.ShapeDtypeStruct((n, D_MODEL), F32),
        scratch_shapes=[pltpu.VMEM((TM, B_W), BF16)],
        compiler_params=_params("arbitrary"),
        name=f"diff_layer_p{layer}",
    )(qt, k, vt, z, lam_vec, subln_g.reshape(1, 2 * DH), ma, x, w_out, g_post, mod4)


def _diff_layer_sample(qt, k, vt, z, ma, x, w_out, g_post, mod4, layer, cache_k, cache_v, lam_vec,
                       subln_g, lam_init):
    n = x.shape[0]
    nq = DEC_SEQ // TQ
    qrow = lambda b, t: (b * nq + t, 0)
    cache = lambda b, t: (b, layer // 2, 0, 0)
    fixed = lambda b, t: (0, 0)
    return pl.pallas_call(
        functools.partial(_diff_sample_kernel, lam_init=lam_init),
        grid=(DEC_BATCH, nq),
        in_specs=[
            pl.BlockSpec((None, B_W, TQ), lambda b, t: (b * nq + t, 0, 0)),
            pl.BlockSpec((DEC_SEQ, B_W), lambda b, t: (b, 0)),
            pl.BlockSpec((nq, B_W, TQ), lambda b, t: (b, 0, 0)),
            pl.BlockSpec((None, None, B_W, PAST_LEN), cache),
            pl.BlockSpec((None, None, PAST_LEN * H_B, LANES), cache),
            pl.BlockSpec((TQ, B_W), qrow),
            pl.BlockSpec((4, DH), fixed),
            pl.BlockSpec((1, 2 * DH), fixed),
            pl.BlockSpec((TQ, A_W), qrow),
        ] + _out_specs_sample(layer),
        out_specs=pl.BlockSpec((TQ, D_MODEL), qrow),
        out_shape=jax.ShapeDtypeStruct((n, D_MODEL), F32),
        scratch_shapes=[pltpu.VMEM((PAST_LEN, B_W), BF16), pltpu.VMEM((B_W, PAST_LEN), BF16),
                        pltpu.VMEM((TQ, B_W), BF16)],
        compiler_params=_params("arbitrary", "arbitrary"),
        name=f"diff_layer_s{layer}",
    )(qt, k, vt, cache_k, cache_v, z, lam_vec, subln_g.reshape(1, 2 * DH), ma, x, w_out, g_post, mod4)


def _gqa(qt_ref, segs, z_ref, o_ref, sink_ref=None):
    halves = []

    def scores(j):
        n = j // 4
        qj = qt_ref[j * DH:(j + 1) * DH, :]
        zero = jnp.zeros_like(qj)
        qz = jnp.concatenate([qj, zero] if n == 0 else [zero, qj], axis=0)
        return [_dot(k, qz) for k, _, _ in segs]

    def finish(j, ss):
        n = j // 4
        ss = [s if valid is None else jnp.where(valid, s, -jnp.inf)
              for s, (_, _, valid) in zip(ss, segs)]
        extra = None if sink_ref is None else sink_ref[:, j:j + 1]
        es, den = _softmax_t(ss, extra)
        ot = None
        for e, (_, vt, _) in zip(es, segs):
            oi = _dot(vt[n * DH:(n + 1) * DH], e.astype(BF16))
            ot = oi if ot is None else ot + oi
        halves.append(ot * (1.0 / den))
        if j % 2 == 1:
            cs = slice((j // 2) * LANES, (j // 2 + 1) * LANES)
            o_pair = jnp.concatenate(halves[-2:], axis=0).T
            o_ref[:, cs] = (o_pair * _silu(z_ref[:, cs])).astype(o_ref.dtype)

    _pipelined(2 * 4, scores, finish)


def _gqa_prompt_kernel(cqt_ref, ck_ref, cvt_ref, cz_ref, dqt_ref, dk_ref, dvt_ref, dz_ref, sink_ref,
                       x_ref, w_ref, g_ref, mod_ref, o_ref, m_ref):
    for s in range(cqt_ref.shape[0]):
        rows = slice(s * SEQ, (s + 1) * SEQ)
        rs = pl.ds(s * SEQ, SEQ)
        seg = lambda k_ref, vt_ref: (k_ref[rows, :], vt_ref[s].astype(BF16), None)
        _gqa(cqt_ref.at[s], [seg(ck_ref, cvt_ref)], cz_ref.at[rs], m_ref.at[rs, pl.ds(0, C_W)])
        _gqa(dqt_ref.at[s], [seg(dk_ref, dvt_ref)], dz_ref.at[rs], m_ref.at[rs, pl.ds(C_W, D_W)], sink_ref)
    _post_residual(_dot(m_ref[...], w_ref[...]), x_ref, g_ref, mod_ref, o_ref)


def _gqa_sample_kernel(cqt_ref, ck_ref, cvt_ref, cck_ref, ccv_ref, cz_ref,
                       dqt_ref, dk_ref, dvt_ref, cdk_ref, cdv_ref, dz_ref, sink_ref,
                       x_ref, w_ref, g_ref, mod_ref, o_ref, m_ref):
    oc_ref = m_ref.at[:, pl.ds(0, C_W)]
    od_ref = m_ref.at[:, pl.ds(C_W, D_W)]
    t = pl.program_id(1)
    ctx = lambda kt_ref, vt_ref: (kt_ref[...].T.astype(BF16), vt_ref[...].astype(BF16), None)
    n_chunks = DEC_SEQ // LANES
    cvt = jnp.concatenate([cvt_ref[c] for c in range(n_chunks)], axis=1)
    _gqa(cqt_ref, [ctx(cck_ref, ccv_ref), (ck_ref[...], cvt, None)], cz_ref, oc_ref)
    span = 2 * TQ
    t0 = t * TQ
    ws = pl.multiple_of(jnp.clip(t0 - WINDOW, 0, DEC_SEQ - span), WINDOW)
    kpos = ws + lax.broadcasted_iota(jnp.int32, (span, TQ), 0)
    qpos = t0 + lax.broadcasted_iota(jnp.int32, (span, TQ), 1)
    valid = jnp.abs(qpos - kpos) <= WINDOW
    c0 = ws // LANES
    dvt = jnp.concatenate([dvt_ref[c0 + c] for c in range(span // LANES)], axis=1)
    _gqa(dqt_ref, [ctx(cdk_ref, cdv_ref), (dk_ref[pl.ds(ws, span), :], dvt, valid)],
         dz_ref, od_ref, sink_ref)
    _post_residual(_dot(m_ref[...], w_ref[...]), x_ref, g_ref, mod_ref, o_ref)


def _gqa_layer_prompt(cqt, ck, cvt, cz, dqt, dk, dvt, dz, sink, x, w_out, g_post, mod4, layer):
    n = x.shape[0]
    row = lambda i: (i, 0)
    qt = pl.BlockSpec((TM // SEQ, C_W, SEQ), lambda i: (i, 0, 0))
    wide = pl.BlockSpec((TM, C_W), row)
    kv = pl.BlockSpec((TM, KV_W), row)
    vt = pl.BlockSpec((TM // SEQ, None, KV_W, SEQ), lambda i: (i, layer // 2, 0, 0))
    return pl.pallas_call(
        _gqa_prompt_kernel,
        grid=(n // TM,),
        in_specs=[qt, kv, vt, wide, qt, kv, vt, wide, pl.BlockSpec((1, 8), lambda i: (0, 0))]
        + _out_specs_prompt(layer),
        out_specs=pl.BlockSpec((TM, D_MODEL), row),
        out_shape=jax.ShapeDtypeStruct((n, D_MODEL), F32),
        scratch_shapes=[pltpu.VMEM((TM, C_W + D_W), BF16)],
        compiler_params=_params("arbitrary"),
        name=f"gqa_layer_p{layer}",
    )(cqt, ck, cvt, cz, dqt, dk, dvt, dz, sink, x, w_out, g_post, mod4)


def _gqa_layer_sample(cqt, ck, cvt, cz, dqt, dk, dvt, dz, sink, cck, ccv, cdk, cdv, x, w_out, g_post,
                      mod4, layer):
    n = x.shape[0]
    nq = DEC_SEQ // TQ
    qrow = lambda b, t: (b * nq + t, 0)
    cache = lambda b, t: (b, layer // 2, 0, 0)
    qt = pl.BlockSpec((None, C_W, TQ), lambda b, t: (b * nq + t, 0, 0))
    wide = pl.BlockSpec((TQ, C_W), qrow)
    kv = pl.BlockSpec((DEC_SEQ, KV_W), lambda b, t: (b, 0))
    vt = pl.BlockSpec((DEC_SEQ // LANES, KV_W, LANES), lambda b, t: (b, 0, 0))
    cb = pl.BlockSpec((None, None, KV_W, PAST_LEN), cache)
    return pl.pallas_call(
        _gqa_sample_kernel,
        grid=(DEC_BATCH, nq),
        in_specs=[qt, kv, vt, cb, cb, wide, qt, kv, vt, cb, cb, wide,
                  pl.BlockSpec((1, 8), lambda b, t: (0, 0))] + _out_specs_sample(layer),
        out_specs=pl.BlockSpec((TQ, D_MODEL), qrow),
        out_shape=jax.ShapeDtypeStruct((n, D_MODEL), F32),
        scratch_shapes=[pltpu.VMEM((TQ, C_W + D_W), BF16)],
        compiler_params=_params("arbitrary", "arbitrary"),
        name=f"gqa_layer_s{layer}",
    )(cqt, ck, cvt, cck, ccv, cz, dqt, dk, dvt, cdk, cdv, dz, sink, x, w_out, g_post, mod4)


def _rope_tables():
    nf = DH // 4
    t = jnp.arange(DEC_SEQ)
    row = (t // GRID_W).astype(F32)
    col = (t % GRID_W).astype(F32)
    inv = ROPE_THETA ** (-jnp.arange(nf, dtype=F32) / nf)
    d = jnp.arange(DH)
    axis = d // (2 * nf)
    second = (d % (2 * nf)) // nf
    f = d % nf
    pos = jnp.where(axis[None, :] == 0, row[:, None], col[:, None])
    ang = pos * inv[f][None, :]
    cos = jnp.cos(ang)
    sin = jnp.where(second[None, :] == 0, -jnp.sin(ang), jnp.sin(ang))
    reps = B_W // DH
    cos = jnp.tile(cos, (1, reps))
    sin = jnp.tile(sin, (1, reps))
    return cos, sin, cos.T, sin.T


def kernel(x_prompt, x_sample, cache_b_k, cache_b_v, cache_c_k, cache_c_v, cache_d_k, cache_d_v, c, c_ctx, norm_pre, norm_post, w_mod, b_mod, w_in_even, a_conv_w, a_conv_b, a_ln_g, a_ln_b, b_lambda, b_subln_g, w_out_even, w_in_odd, c_q_norm, c_k_norm, d_sink, w_out_odd):
    n_even = (DEPTH + 1) // 2
    n_odd = DEPTH // 2
    cond8 = jnp.zeros((SUBLANES, D_MODEL), F32).at[0].set(c_ctx).at[1:1 + DEC_BATCH].set(c)
    mod4 = _modulation(cond8, w_mod, b_mod).reshape(DEPTH, SUBLANES, 1, 3 * D_MODEL)
    tables = _rope_tables()

    xp = x_prompt.reshape(BATCH * SEQ, D_MODEL)
    xs = x_sample.reshape(DEC_BATCH * DEC_SEQ, D_MODEL)
    feat = lambda a, w: jnp.moveaxis(a.reshape(a.shape[:3] + (w,)), 2, 3)
    cbk = feat(cache_b_k, B_W)
    cbv = cache_b_v.reshape(DEC_BATCH, n_even, PAST_LEN * H_B, 2 * DH)
    cck = feat(cache_c_k, KV_W)
    ccv = feat(cache_c_v, KV_W)
    cdk = feat(cache_d_k, KV_W)
    cdv = feat(cache_d_v, KV_W)

    new_even, new_odd = (), ()
    for l in range(DEPTH):
        i = l // 2
        g_pre = norm_pre[l].reshape(1, D_MODEL)
        g_post = norm_post[l].reshape(1, D_MODEL)
        if l % 2 == 0:
            lam_init = 0.8 - 0.6 * math.exp(-0.3 * l)
            w_in = w_in_even[i].astype(BF16)
            w_out = w_out_even[i].astype(BF16)
            conv = (jnp.zeros((4 * SUBLANES, A_W), F32).at[:CONV_K].set(a_conv_w[i]),
                    a_conv_b[i].reshape(1, A_W), a_ln_g[i].reshape(1, A_W), a_ln_b[i].reshape(1, A_W))
            ma, qt, k, kt, v, vt, bz = _in_proj(xp, g_pre, mod4, l, w_in, sample=False, extra=conv,
                                                carry=new_even)
            new_even = (kt, v)
            xp = _diff_layer_prompt(qt, k, vt, bz, ma, xp, w_out, g_post, mod4, l,
                                    b_lambda[i], b_subln_g[i], lam_init)
            ma, qt, k, vt, bz = _in_proj(xs, g_pre, mod4, l, w_in, sample=True, extra=conv, tables=tables)
            xs = _diff_layer_sample(qt, k, vt, bz, ma, xs, w_out, g_post, mod4, l, cbk, cbv,
                                    b_lambda[i], b_subln_g[i], lam_init)
        else:
            w_in = w_in_odd[i].astype(BF16)
            w_out = w_out_odd[i].astype(BF16)
            qn = jnp.tile(c_q_norm[i], C_W // DH).reshape(C_W, 1)
            kn = jnp.tile(c_k_norm[i], KV_W // DH).reshape(1, KV_W)
            sink = d_sink[i].reshape(1, 8)
            cqt, ck, ckt, cvt, cz, dqt, dk, dkt, dvt, dz = _in_proj(
                xp, g_pre, mod4, l, w_in, sample=False, extra=(qn, kn), carry=new_odd)
            new_odd = (ckt, cvt, dkt, dvt)
            xp = _gqa_layer_prompt(cqt, ck, cvt, cz, dqt, dk, dvt, dz, sink, xp, w_out, g_post, mod4, l)
            cqt, ck, cvt, cz, dqt, dk, dvt, dz = _in_proj(xs, g_pre, mod4, l, w_in, sample=True,
                                                          extra=(qn, kn), tables=tables)
            xs = _gqa_layer_sample(cqt, ck, cvt, cz, dqt, dk, dvt, dz, sink, cck, ccv, cdk, cdv,
                                   xs, w_out, g_post, mod4, l)

    def token_major(a, heads):
        return jnp.moveaxis(a.reshape(a.shape[:2] + heads + (DH, SEQ)), -1, 2)

    kt, v = new_even
    ckt, cvt, dkt, dvt = new_odd
    return (xp.reshape(BATCH, SEQ, D_MODEL), xs.reshape(DEC_BATCH, DEC_SEQ, D_MODEL),
            token_major(kt, (H_B, 2)), v.reshape(BATCH, n_even, SEQ, H_B, 2 * DH),
            token_major(ckt, (2,)), token_major(cvt, (2,)), token_major(dkt, (2,)), token_major(dvt, (2,)))
```

```python
import functools
import math

import jax
import jax.numpy as jnp
from jax import lax
from jax.experimental import pallas as pl
from jax.experimental.pallas import tpu as pltpu

F32 = jnp.float32
BF16 = jnp.bfloat16

D_MODEL = 1024
BATCH = 16
SEQ = 256
DEPTH = 4
DEC_BATCH = 2
DEC_SEQ = 1024
PAST_LEN = 512
GRID_W = 64
ROPE_THETA = 10000.0
NORM_EPS = 1e-6
DH = 64
A_W = 512
CONV_K = 31
H_B = 4
B_W = 512
C_W = 512
KV_W = 128
D_W = 512
WINDOW = 128
LOG2E = math.log2(math.e)
QK_SCALE = DH ** -0.5 * LOG2E

LANES = 128
SUBLANES = 8
VMEM_LIMIT = 56 * 1024 * 1024

TM = 512
TQ = SEQ
HALO = 16
ROW_CHUNK = 64


def _params(*sem):
    return pltpu.CompilerParams(dimension_semantics=sem, vmem_limit_bytes=VMEM_LIMIT)


def _silu(x):
    return x * jax.nn.sigmoid(x)


def _dot(a, b):
    return jnp.dot(a, b, preferred_element_type=F32)


def _pick(stacked, idx):
    return pl.BlockSpec((None,) + stacked.shape[1:], lambda *_: (idx,) + (0,) * (stacked.ndim - 1))


def _mod_kernel(cond_ref, w_ref, b_ref, o_ref):
    a = _silu(cond_ref[...]).astype(BF16)
    o_ref[...] = _dot(a, w_ref[...].astype(BF16)) + b_ref[...]


def _modulation(cond8, w_mod, b_mod):
    nblk = 3
    return pl.pallas_call(
        _mod_kernel,
        grid=(DEPTH, nblk),
        in_specs=[
            pl.BlockSpec((SUBLANES, D_MODEL), lambda l, j: (0, 0)),
            pl.BlockSpec((None, D_MODEL, D_MODEL), lambda l, j: (l, 0, j)),
            pl.BlockSpec((None, 1, D_MODEL), lambda l, j: (l, 0, j)),
        ],
        out_specs=pl.BlockSpec((None, SUBLANES, D_MODEL), lambda l, j: (l, 0, j)),
        out_shape=jax.ShapeDtypeStruct((DEPTH, SUBLANES, 3 * D_MODEL), F32),
        compiler_params=_params("arbitrary", "arbitrary"),
        name="modulation",
    )(cond8, w_mod, b_mod.reshape(DEPTH, 1, 3 * D_MODEL))


def _pre_norm(x_ref, g_ref, mod_ref):
    return _modulate(x_ref[...], g_ref, mod_ref)


def _modulate(x, g_ref, mod_ref):
    ms = jnp.mean(x * x, axis=-1, keepdims=True)
    mod = mod_ref[...]
    sh = mod[:, :D_MODEL]
    sc = mod[:, D_MODEL:2 * D_MODEL]
    h = (x * lax.rsqrt(ms + NORM_EPS) * g_ref[...]) * (1.0 + sc) + sh
    return h.astype(BF16)


def _rope(x, cos, sin_signed):
    w = x.shape[-1]
    lane = lax.broadcasted_iota(jnp.int32, (1, w), 1)
    first = (lane % 32) < 16
    partner = jnp.where(first, pltpu.roll(x, w - 16, 1), pltpu.roll(x, 16, 1))
    return x * cos + partner * sin_signed


def _rope_t(x, cos_t, sin_t):
    r = x.shape[0]
    row = lax.broadcasted_iota(jnp.int32, (r, 1), 0)
    first = (row % 32) < 16
    partner = jnp.where(first, pltpu.roll(x, r - 16, 0), pltpu.roll(x, 16, 0))
    return x * cos_t + partner * sin_t


def _store_chunks(ref, xt):
    for c in range(xt.shape[1] // LANES):
        ref[c] = xt[:, c * LANES:(c + 1) * LANES]


def _store_per_seq(ref, xt):
    for s in range(xt.shape[1] // SEQ):
        ref[s] = xt[:, s * SEQ:(s + 1) * SEQ]


def _own_slot(ref, slot):
    if len(ref.shape) == 3:
        return ref
    for other in range(ref.shape[1]):
        if other != slot:
            ref[:, other] = jnp.zeros((ref.shape[0],) + tuple(ref.shape[2:]), ref.dtype)
    return ref.at[:, slot]


def _glu(ug):
    return ug[:, :A_W] * jax.nn.sigmoid(ug[:, A_W:])


def _conv_mix(pad, az, cw_ref, cb_ref, lg_ref, lb_ref, sh_ref, acc_ref):
    rows = sh_ref.shape[1]
    for b in range(SUBLANES):
        sh_ref[b] = pad[b:b + rows]
    base = HALO - CONV_K // 2
    for c0 in range(0, A_W, LANES):
        cs = slice(c0, c0 + LANES)
        for r0 in range(0, SEQ, ROW_CHUNK):
            acc = jnp.zeros((ROW_CHUNK, LANES), F32) + cb_ref[:, cs]
            for k in range(CONV_K):
                j = k + base
                s = r0 + (j // SUBLANES) * SUBLANES
                acc = acc + sh_ref[j % SUBLANES, s:s + ROW_CHUNK, cs] * cw_ref[k:k + 1, cs]
            acc_ref[r0:r0 + ROW_CHUNK, cs] = acc
    a = acc_ref[...]
    mu = jnp.mean(a, axis=-1, keepdims=True)
    d = a - mu
    var = jnp.mean(d * d, axis=-1, keepdims=True)
    y = d * lax.rsqrt(var + NORM_EPS) * lg_ref[...] + lb_ref[...]
    return _silu(y) * _silu(az)


def _in_even_kernel(x_ref, g_ref, mod_ref, w_ref, cw_ref, cb_ref, lg_ref, lb_ref, *rest,
                    sample, n_alias=0, slot=0):
    if sample:
        (xp_ref, xn_ref, cos_ref, sin_ref, cost_ref, sint_ref,
         ma_ref, qt_ref, k_ref, vt_ref, bz_ref, sh_ref, acc_ref) = rest
    else:
        ma_ref, qt_ref, k_ref, kt_ref, v_ref, vt_ref, bz_ref, sh_ref, acc_ref = rest[n_alias:]
        kt_ref = _own_slot(kt_ref, slot)
        v_ref = _own_slot(v_ref, slot)
    hb = _pre_norm(x_ref, g_ref, mod_ref)
    a = _glu(_dot(hb, w_ref[:, 0:2 * A_W]))
    az = _dot(hb, w_ref[:, 2 * A_W:3 * A_W])
    n_sub = TM // SEQ
    if sample:
        tiles_per_seq = DEC_SEQ // TM
        pos = pl.program_id(0) % tiles_per_seq
        xh = jnp.concatenate([xp_ref[...], xn_ref[...]], axis=0)
        ah = _glu(_dot(_modulate(xh, g_ref, mod_ref), w_ref[:, 0:2 * A_W]))
        prev = jnp.where(pos != 0, ah[:HALO], 0.0)
        nxt = jnp.where(pos != tiles_per_seq - 1, ah[HALO:], 0.0)
        full = jnp.concatenate([prev, a, nxt], axis=0)
        pads = [full[j * SEQ:(j + 1) * SEQ + 2 * HALO] for j in range(n_sub)]
    else:
        zeros = jnp.zeros((HALO, A_W), F32)
        pads = [jnp.concatenate([zeros, a[j * SEQ:(j + 1) * SEQ], zeros], axis=0) for j in range(n_sub)]
    o = 3 * A_W
    q = _dot(hb, w_ref[:, o:o + B_W])
    k = _dot(hb, w_ref[:, o + B_W:o + 2 * B_W])
    v = _dot(hb, w_ref[:, o + 2 * B_W:o + 3 * B_W])
    bz_ref[...] = _dot(hb, w_ref[:, o + 3 * B_W:o + 4 * B_W])
    qt = q.T
    if sample:
        qt = _rope_t(qt, cost_ref[...], sint_ref[...])
        k = _rope(k, cos_ref[...], sin_ref[...])
    else:
        _store_per_seq(kt_ref, k.T)
        for s in range(TM // SEQ):
            for h in range(H_B):
                v_ref[s, pl.ds(h, SEQ, stride=H_B), :] = v[s * SEQ:(s + 1) * SEQ, h * LANES:(h + 1) * LANES]
    _store_per_seq(qt_ref, (qt * QK_SCALE).astype(BF16))
    k_ref[...] = k.astype(BF16)
    _store_per_seq(vt_ref, v.T.astype(BF16))
    for j, pad in enumerate(pads):
        rs = slice(j * SEQ, (j + 1) * SEQ)
        ma_ref[rs, :] = _conv_mix(pad, az[rs], cw_ref, cb_ref, lg_ref, lb_ref, sh_ref, acc_ref).astype(BF16)


def _group_mean_sq(x):
    width = x.shape[-1]
    xx = x * x
    hi = xx.astype(BF16)
    lo = (xx - hi.astype(F32)).astype(BF16)
    r = lax.broadcasted_iota(jnp.int32, (width, width), 0) // DH
    c = lax.broadcasted_iota(jnp.int32, (width, width), 1) // DH
    g = jnp.where(r == c, 1.0, 0.0).astype(BF16)
    return (_dot(hi, g) + _dot(lo, g)) * (1.0 / DH)


def _head_rms_t(xt, gain_col):
    parts = []
    for j in range(xt.shape[0] // DH):
        blk = xt[j * DH:(j + 1) * DH]
        ms = jnp.mean(blk * blk, axis=0, keepdims=True)
        parts.append(blk * lax.rsqrt(ms + NORM_EPS))
    return jnp.concatenate(parts, axis=0) * gain_col


def _in_odd_kernel(x_ref, g_ref, mod_ref, w_ref, qn_ref, kn_ref, *rest, sample, n_alias=0, slot=0):
    if sample:
        (cos_ref, sin_ref, cost_ref, sint_ref,
         cqt_ref, ck_ref, cvt_ref, cz_ref, dqt_ref, dk_ref, dvt_ref, dz_ref) = rest
    else:
        (cqt_ref, ck_ref, ckt_ref, cvt_ref, cz_ref,
         dqt_ref, dk_ref, dkt_ref, dvt_ref, dz_ref) = rest[n_alias:]
        ckt_ref, cvt_ref, dkt_ref, dvt_ref = [_own_slot(r, slot) for r in (ckt_ref, cvt_ref, dkt_ref, dvt_ref)]
    hb = _pre_norm(x_ref, g_ref, mod_ref)
    y = _dot(hb, w_ref[...])
    o = 0
    cqt = _head_rms_t(y[:, o:o + C_W].T, qn_ref[...])
    o += C_W
    ck = y[:, o:o + KV_W]
    ck = ck * lax.rsqrt(_group_mean_sq(ck) + NORM_EPS) * kn_ref[...]
    o += KV_W
    cvt = y[:, o:o + KV_W].T
    o += KV_W
    cz_ref[...] = y[:, o:o + C_W]
    o += C_W
    dqt = y[:, o:o + D_W].T
    o += D_W
    dk = y[:, o:o + KV_W]
    o += KV_W
    dvt = y[:, o:o + KV_W].T
    o += KV_W
    dz_ref[...] = y[:, o:o + D_W]
    if sample:
        cos_t = cost_ref[...]
        sin_t = sint_ref[...]
        cqt = _rope_t(cqt, cos_t, sin_t)
        dqt = _rope_t(dqt, cos_t, sin_t)
        cos = cos_ref[...][:, :KV_W]
        sin = sin_ref[...][:, :KV_W]
        ck = _rope(ck, cos, sin)
        dk = _rope(dk, cos, sin)
        _store_chunks(cvt_ref, cvt.astype(BF16))
        _store_chunks(dvt_ref, dvt.astype(BF16))
    else:
        _store_per_seq(ckt_ref, ck.T)
        _store_per_seq(dkt_ref, dk.T)
        _store_per_seq(cvt_ref, cvt)
        _store_per_seq(dvt_ref, dvt)
    _store_per_seq(cqt_ref, (cqt * QK_SCALE).astype(BF16))
    _store_per_seq(dqt_ref, (dqt * QK_SCALE).astype(BF16))
    ck_ref[...] = ck.astype(BF16)
    dk_ref[...] = dk.astype(BF16)


def _in_proj(x, g_pre, mod4, layer, w_bf, *, sample, extra=(), tables=None, carry=()):
    n = x.shape[0]
    even = layer % 2 == 0
    tiles_per_seq = DEC_SEQ // TM
    if sample:
        mod_map = lambda i: (layer, 1 + i // tiles_per_seq, 0, 0)
    else:
        mod_map = lambda i: (layer, 0, 0, 0)
    row = lambda i: (i, 0)
    fixed = lambda i: (0, 0)
    in_specs = [
        pl.BlockSpec((TM, D_MODEL), row),
        _pick(g_pre, layer),
        pl.BlockSpec((None, None, 1, 3 * D_MODEL), mod_map),
        _pick(w_bf, layer // 2),
    ]
    args = [x, g_pre, mod4, w_bf]
    for e in extra:
        in_specs.append(_pick(e, layer // 2))
        args.append(e)
    if sample:
        if even:
            hb = TM // HALO
            last = n // HALO - 1
            in_specs += [pl.BlockSpec((HALO, D_MODEL), lambda i: (jnp.maximum(i * hb - 1, 0), 0)),
                         pl.BlockSpec((HALO, D_MODEL), lambda i: (jnp.minimum((i + 1) * hb, last), 0))]
            args += [x, x]
        cos, sin, cos_t, sin_t = tables
        in_specs += [pl.BlockSpec((TM, B_W), lambda i: (i % tiles_per_seq, 0)),
                     pl.BlockSpec((TM, B_W), lambda i: (i % tiles_per_seq, 0)),
                     pl.BlockSpec((B_W, TM), lambda i: (0, i % tiles_per_seq)),
                     pl.BlockSpec((B_W, TM), lambda i: (0, i % tiles_per_seq))]
        args += [cos, sin, cos_t, sin_t]
    rows = lambda w, dt: (pl.BlockSpec((TM, w), row), jax.ShapeDtypeStruct((n, w), dt))
    cols = lambda w, dt: (pl.BlockSpec((TM // SEQ, w, SEQ), lambda i: (i, 0, 0)),
                          jax.ShapeDtypeStruct((n // SEQ, w, SEQ), dt))
    chunks = lambda w, dt: (pl.BlockSpec((TM // LANES, w, LANES), lambda i: (i, 0, 0)),
                            jax.ShapeDtypeStruct((n // LANES, w, LANES), dt))
    n_layers = (DEPTH + 1 - layer % 2) // 2
    if carry:
        per_seq_spec = lambda r, c: pl.BlockSpec((TM // SEQ, None, r, c), lambda i: (i, layer // 2, 0, 0))
    else:
        per_seq_spec = lambda r, c: pl.BlockSpec((TM // SEQ, n_layers, r, c), lambda i: (i, 0, 0, 0))
    per_seq = lambda r, c: (per_seq_spec(r, c), jax.ShapeDtypeStruct((n // SEQ, n_layers, r, c), F32))
    scratch = []
    if even:
        scratch = [pltpu.VMEM((SUBLANES, SEQ + 3 * SUBLANES, A_W), F32), pltpu.VMEM((SEQ, A_W), F32)]
    if even and sample:
        outs = [rows(A_W, BF16), cols(B_W, BF16), rows(B_W, BF16), cols(B_W, BF16), rows(B_W, F32)]
    elif even:
        outs = [rows(A_W, BF16), cols(B_W, BF16), rows(B_W, BF16),
                per_seq(B_W, SEQ), per_seq(SEQ * H_B, LANES), cols(B_W, BF16), rows(B_W, F32)]
    elif sample:
        outs = [cols(C_W, BF16), rows(KV_W, BF16), chunks(KV_W, BF16), rows(C_W, F32),
                cols(D_W, BF16), rows(KV_W, BF16), chunks(KV_W, BF16), rows(D_W, F32)]
    else:
        outs = [cols(C_W, BF16), rows(KV_W, BF16), per_seq(KV_W, SEQ), per_seq(KV_W, SEQ), rows(C_W, F32),
                cols(D_W, BF16), rows(KV_W, BF16), per_seq(KV_W, SEQ), per_seq(KV_W, SEQ), rows(D_W, F32)]
    aliases = {}
    for a in carry:
        out_idx = [j for j, o in enumerate(outs) if o[1].shape == a.shape and j not in aliases.values()][0]
        aliases[len(args)] = out_idx
        in_specs.append(pl.BlockSpec(memory_space=pl.ANY))
        args.append(a)
    body = functools.partial(_in_even_kernel if even else _in_odd_kernel, sample=sample,
                             n_alias=len(carry), slot=layer // 2)
    return pl.pallas_call(
        body,
        grid=(n // TM,),
        in_specs=in_specs,
        out_specs=[o[0] for o in outs],
        out_shape=[o[1] for o in outs],
        scratch_shapes=scratch,
        input_output_aliases=aliases,
        compiler_params=_params("arbitrary"),
        name=f"in_proj_{'s' if sample else 'p'}{layer}",
    )(*args)


def _softmax_t(segs, extra=None):
    m = None
    for s in segs:
        mi = jnp.max(s, axis=0, keepdims=True)
        m = mi if m is None else jnp.maximum(m, mi)
    if extra is not None:
        extra = extra * LOG2E
        m = jnp.maximum(m, extra)
    es = [jnp.exp2(s - m) for s in segs]
    den = None
    for e in es:
        di = jnp.sum(e, axis=0, keepdims=True)
        den = di if den is None else den + di
    if extra is not None:
        den = den + jnp.exp2(extra - m)
    return es, den


def _keep_rows(xt, lo, hi):
    zeros = lambda r: jnp.zeros((r, xt.shape[1]), xt.dtype)
    parts = []
    if lo > 0:
        parts.append(zeros(lo))
    parts.append(xt[lo:hi])
    if hi < xt.shape[0]:
        parts.append(zeros(xt.shape[0] - hi))
    return jnp.concatenate(parts, axis=0)


def _pipelined(n, scores, finish):
    cur = scores(0)
    for j in range(n):
        nxt = scores(j + 1) if j + 1 < n else None
        finish(j, cur)
        cur = nxt


def _diff_attn(qt_ref, kvs, z_ref, lam_ref, g_ref, o_ref, *, lam_init):
    lv = lam_ref[...]
    lam = (jnp.exp(jnp.sum(lv[0:1] * lv[1:2], axis=-1, keepdims=True))
           - jnp.exp(jnp.sum(lv[2:3] * lv[3:4], axis=-1, keepdims=True)) + lam_init)

    def scores(h):
        cs = slice(h * LANES, (h + 1) * LANES)
        qt = qt_ref[cs, :]
        ks = [get_k(cs) for get_k, _ in kvs]
        return [[_dot(kk, _keep_rows(qt, c * DH, (c + 1) * DH)) for kk in ks] for c in range(2)]

    def finish(h, ss):
        cs = slice(h * LANES, (h + 1) * LANES)
        es0, den0 = _softmax_t(ss[0])
        es1, den1 = _softmax_t(ss[1])
        r0 = 1.0 / den0
        r1 = lam / den1
        ot = None
        for e0, e1, (_, get_vt) in zip(es0, es1, kvs):
            w = e0 * r0 - e1 * r1
            oi = _dot(get_vt(cs), w.astype(BF16))
            ot = oi if ot is None else ot + oi
        ms = jnp.mean(ot * ot, axis=0, keepdims=True)
        o = (ot * lax.rsqrt(ms + NORM_EPS)).T
        o = (o * g_ref[...]) * (1.0 - lam_init)
        o_ref[:, cs] = (o * _silu(z_ref[:, cs])).astype(o_ref.dtype)

    _pipelined(H_B, scores, finish)


def _post_residual(o, x_ref, g_ref, mod_ref, o_ref):
    ms = jnp.mean(o * o, axis=-1, keepdims=True)
    r = o * lax.rsqrt(ms + NORM_EPS) * g_ref[...]
    gate = mod_ref[...][:, 2 * D_MODEL:]
    o_ref[...] = x_ref[...] + gate * r


def _diff_prompt_kernel(qt_ref, k_ref, vt_ref, z_ref, lam_ref, sg_ref, ma_ref, x_ref, w_ref, g_ref,
                        mod_ref, o_ref, mb_ref, *, lam_init):
    oa = _dot(ma_ref[...], w_ref[0:A_W, :])
    for s in range(qt_ref.shape[0]):
        rs = pl.ds(s * SEQ, SEQ)
        kv = (lambda cs, s=s: k_ref[s * SEQ:(s + 1) * SEQ, cs], lambda cs, s=s: vt_ref[s, cs, :])
        _diff_attn(qt_ref.at[s], [kv], z_ref.at[rs], lam_ref, sg_ref, mb_ref.at[rs], lam_init=lam_init)
    _post_residual(oa + _dot(mb_ref[...], w_ref[A_W:, :]), x_ref, g_ref, mod_ref, o_ref)


def _diff_sample_kernel(qt_ref, k_ref, vt_ref, ck_ref, cv_ref, z_ref, lam_ref, sg_ref, ma_ref, x_ref,
                        w_ref, g_ref, mod_ref, o_ref, ckb_ref, cvt_ref, mb_ref, *, lam_init):
    @pl.when(pl.program_id(1) == 0)
    def _():
        ckb_ref[...] = ck_ref[...].T.astype(BF16)
        for h in range(H_B):
            cs = slice(h * LANES, (h + 1) * LANES)
            cvt_ref[cs, :] = cv_ref[pl.ds(h, PAST_LEN, stride=H_B), :].T.astype(BF16)

    ctx = (lambda cs: ckb_ref[:, cs], lambda cs: cvt_ref[cs, :])
    loc = (lambda cs: k_ref[:, cs],
           lambda cs: jnp.concatenate([vt_ref[c, cs, :] for c in range(vt_ref.shape[0])], axis=1))
    oa = _dot(ma_ref[...], w_ref[0:A_W, :])
    _diff_attn(qt_ref, [ctx, loc], z_ref, lam_ref, sg_ref, mb_ref, lam_init=lam_init)
    _post_residual(oa + _dot(mb_ref[...], w_ref[A_W:, :]), x_ref, g_ref, mod_ref, o_ref)


def _out_specs_prompt(layer, w_out, g_post):
    return [pl.BlockSpec((TM, D_MODEL), lambda i: (i, 0)),
            _pick(w_out, layer // 2),
            _pick(g_post, layer),
            pl.BlockSpec((None, None, 1, 3 * D_MODEL), lambda i: (layer, 0, 0, 0))]


def _out_specs_sample(layer, w_out, g_post):
    nq = DEC_SEQ // TQ
    return [pl.BlockSpec((TQ, D_MODEL), lambda b, t: (b * nq + t, 0)),
            _pick(w_out, layer // 2),
            _pick(g_post, layer),
            pl.BlockSpec((None, None, 1, 3 * D_MODEL), lambda b, t: (layer, 1 + b, 0, 0))]


def _diff_layer_prompt(qt, k, vt, z, ma, x, w_out, g_post, mod4, layer, lam_vec, subln_g, lam_init):
    n = x.shape[0]
    row = lambda i: (i, 0)
    fixed = lambda i: (0, 0)
    blk = pl.BlockSpec((TM, B_W), row)
    tblk = pl.BlockSpec((TM // SEQ, B_W, SEQ), lambda i: (i, 0, 0))
    return pl.pallas_call(
        functools.partial(_diff_prompt_kernel, lam_init=lam_init),
        grid=(n // TM,),
        in_specs=[tblk, blk, tblk, blk, _pick(lam_vec, layer // 2), _pick(subln_g, layer // 2),
                  blk] + _out_specs_prompt(layer, w_out, g_post),
        out_specs=pl.BlockSpec((TM, D_MODEL), row),
        out_shape=jax.ShapeDtypeStruct((n, D_MODEL), F32),
        scratch_shapes=[pltpu.VMEM((TM, B_W), BF16)],
        compiler_params=_params("arbitrary"),
        name=f"diff_layer_p{layer}",
    )(qt, k, vt, z, lam_vec, subln_g, ma, x, w_out, g_post, mod4)


def _diff_layer_sample(qt, k, vt, z, ma, x, w_out, g_post, mod4, layer, cache_k, cache_v, lam_vec,
                       subln_g, lam_init):
    n = x.shape[0]
    nq = DEC_SEQ // TQ
    qrow = lambda b, t: (b * nq + t, 0)
    cache = lambda b, t: (b, layer // 2, 0, 0)
    fixed = lambda b, t: (0, 0)
    return pl.pallas_call(
        functools.partial(_diff_sample_kernel, lam_init=lam_init),
        grid=(DEC_BATCH, nq),
        in_specs=[
            pl.BlockSpec((None, B_W, TQ), lambda b, t: (b * nq + t, 0, 0)),
            pl.BlockSpec((DEC_SEQ, B_W), lambda b, t: (b, 0)),
            pl.BlockSpec((nq, B_W, TQ), lambda b, t: (b, 0, 0)),
            pl.BlockSpec((None, None, B_W, PAST_LEN), cache),
            pl.BlockSpec((None, None, PAST_LEN * H_B, LANES), cache),
            pl.BlockSpec((TQ, B_W), qrow),
            _pick(lam_vec, layer // 2),
            _pick(subln_g, layer // 2),
            pl.BlockSpec((TQ, A_W), qrow),
        ] + _out_specs_sample(layer, w_out, g_post),
        out_specs=pl.BlockSpec((TQ, D_MODEL), qrow),
        out_shape=jax.ShapeDtypeStruct((n, D_MODEL), F32),
        scratch_shapes=[pltpu.VMEM((PAST_LEN, B_W), BF16), pltpu.VMEM((B_W, PAST_LEN), BF16),
                        pltpu.VMEM((TQ, B_W), BF16)],
        compiler_params=_params("arbitrary", "arbitrary"),
        name=f"diff_layer_s{layer}",
    )(qt, k, vt, cache_k, cache_v, z, lam_vec, subln_g, ma, x, w_out, g_post, mod4)


def _gqa(qt_ref, segs, z_ref, o_ref, sink_ref=None):
    halves = []

    def scores(j):
        n = j // 4
        qj = qt_ref[j * DH:(j + 1) * DH, :]
        zero = jnp.zeros_like(qj)
        qz = jnp.concatenate([qj, zero] if n == 0 else [zero, qj], axis=0)
        return [_dot(k, qz) for k, _, _ in segs]

    def finish(j, ss):
        n = j // 4
        ss = [s if valid is None else jnp.where(valid, s, -jnp.inf)
              for s, (_, _, valid) in zip(ss, segs)]
        extra = None if sink_ref is None else sink_ref[:, j:j + 1]
        es, den = _softmax_t(ss, extra)
        ot = None
        for e, (_, vt, _) in zip(es, segs):
            oi = _dot(vt[n * DH:(n + 1) * DH], e.astype(BF16))
            ot = oi if ot is None else ot + oi
        halves.append(ot * (1.0 / den))
        if j % 2 == 1:
            cs = slice((j // 2) * LANES, (j // 2 + 1) * LANES)
            o_pair = jnp.concatenate(halves[-2:], axis=0).T
            o_ref[:, cs] = (o_pair * _silu(z_ref[:, cs])).astype(o_ref.dtype)

    _pipelined(2 * 4, scores, finish)


def _gqa_prompt_kernel(cqt_ref, ck_ref, cvt_ref, cz_ref, dqt_ref, dk_ref, dvt_ref, dz_ref, sink_ref,
                       x_ref, w_ref, g_ref, mod_ref, o_ref, m_ref):
    for s in range(cqt_ref.shape[0]):
        rows = slice(s * SEQ, (s + 1) * SEQ)
        rs = pl.ds(s * SEQ, SEQ)
        seg = lambda k_ref, vt_ref: (k_ref[rows, :], vt_ref[s].astype(BF16), None)
        _gqa(cqt_ref.at[s], [seg(ck_ref, cvt_ref)], cz_ref.at[rs], m_ref.at[rs, pl.ds(0, C_W)])
        _gqa(dqt_ref.at[s], [seg(dk_ref, dvt_ref)], dz_ref.at[rs], m_ref.at[rs, pl.ds(C_W, D_W)], sink_ref)
    _post_residual(_dot(m_ref[...], w_ref[...]), x_ref, g_ref, mod_ref, o_ref)


def _gqa_sample_kernel(cqt_ref, ck_ref, cvt_ref, cck_ref, ccv_ref, cz_ref,
                       dqt_ref, dk_ref, dvt_ref, cdk_ref, cdv_ref, dz_ref, sink_ref,
                       x_ref, w_ref, g_ref, mod_ref, o_ref, m_ref):
    oc_ref = m_ref.at[:, pl.ds(0, C_W)]
    od_ref = m_ref.at[:, pl.ds(C_W, D_W)]
    t = pl.program_id(1)
    ctx = lambda kt_ref, vt_ref: (kt_ref[...].T.astype(BF16), vt_ref[...].astype(BF16), None)
    n_chunks = DEC_SEQ // LANES
    cvt = jnp.concatenate([cvt_ref[c] for c in range(n_chunks)], axis=1)
    _gqa(cqt_ref, [ctx(cck_ref, ccv_ref), (ck_ref[...], cvt, None)], cz_ref, oc_ref)
    span = 2 * TQ
    t0 = t * TQ
    ws = pl.multiple_of(jnp.clip(t0 - WINDOW, 0, DEC_SEQ - span), WINDOW)
    kpos = ws + lax.broadcasted_iota(jnp.int32, (span, TQ), 0)
    qpos = t0 + lax.broadcasted_iota(jnp.int32, (span, TQ), 1)
    valid = jnp.abs(qpos - kpos) <= WINDOW
    c0 = ws // LANES
    dvt = jnp.concatenate([dvt_ref[c0 + c] for c in range(span // LANES)], axis=1)
    _gqa(dqt_ref, [ctx(cdk_ref, cdv_ref), (dk_ref[pl.ds(ws, span), :], dvt, valid)],
         dz_ref, od_ref, sink_ref)
    _post_residual(_dot(m_ref[...], w_ref[...]), x_ref, g_ref, mod_ref, o_ref)


def _gqa_layer_prompt(cqt, ck, cvt, cz, dqt, dk, dvt, dz, sink, x, w_out, g_post, mod4, layer):
    n = x.shape[0]
    row = lambda i: (i, 0)
    qt = pl.BlockSpec((TM // SEQ, C_W, SEQ), lambda i: (i, 0, 0))
    wide = pl.BlockSpec((TM, C_W), row)
    kv = pl.BlockSpec((TM, KV_W), row)
    vt = pl.BlockSpec((TM // SEQ, None, KV_W, SEQ), lambda i: (i, layer // 2, 0, 0))
    return pl.pallas_call(
        _gqa_prompt_kernel,
        grid=(n // TM,),
        in_specs=[qt, kv, vt, wide, qt, kv, vt, wide, _pick(sink, layer // 2)]
        + _out_specs_prompt(layer, w_out, g_post),
        out_specs=pl.BlockSpec((TM, D_MODEL), row),
        out_shape=jax.ShapeDtypeStruct((n, D_MODEL), F32),
        scratch_shapes=[pltpu.VMEM((TM, C_W + D_W), BF16)],
        compiler_params=_params("arbitrary"),
        name=f"gqa_layer_p{layer}",
    )(cqt, ck, cvt, cz, dqt, dk, dvt, dz, sink, x, w_out, g_post, mod4)


def _gqa_layer_sample(cqt, ck, cvt, cz, dqt, dk, dvt, dz, sink, cck, ccv, cdk, cdv, x, w_out, g_post,
                      mod4, layer):
    n = x.shape[0]
    nq = DEC_SEQ // TQ
    qrow = lambda b, t: (b * nq + t, 0)
    cache = lambda b, t: (b, layer // 2, 0, 0)
    qt = pl.BlockSpec((None, C_W, TQ), lambda b, t: (b * nq + t, 0, 0))
    wide = pl.BlockSpec((TQ, C_W), qrow)
    kv = pl.BlockSpec((DEC_SEQ, KV_W), lambda b, t: (b, 0))
    vt = pl.BlockSpec((DEC_SEQ // LANES, KV_W, LANES), lambda b, t: (b, 0, 0))
    cb = pl.BlockSpec((None, None, KV_W, PAST_LEN), cache)
    return pl.pallas_call(
        _gqa_sample_kernel,
        grid=(DEC_BATCH, nq),
        in_specs=[qt, kv, vt, cb, cb, wide, qt, kv, vt, cb, cb, wide,
                  _pick(sink, layer // 2)] + _out_specs_sample(layer, w_out, g_post),
        out_specs=pl.BlockSpec((TQ, D_MODEL), qrow),
        out_shape=jax.ShapeDtypeStruct((n, D_MODEL), F32),
        scratch_shapes=[pltpu.VMEM((TQ, C_W + D_W), BF16)],
        compiler_params=_params("arbitrary", "arbitrary"),
        name=f"gqa_layer_s{layer}",
    )(cqt, ck, cvt, cck, ccv, cz, dqt, dk, dvt, cdk, cdv, dz, sink, x, w_out, g_post, mod4)


def _rope_tables():
    nf = DH // 4
    t = jnp.arange(DEC_SEQ)
    row = (t // GRID_W).astype(F32)
    col = (t % GRID_W).astype(F32)
    inv = ROPE_THETA ** (-jnp.arange(nf, dtype=F32) / nf)
    d = jnp.arange(DH)
    axis = d // (2 * nf)
    second = (d % (2 * nf)) // nf
    f = d % nf
    pos = jnp.where(axis[None, :] == 0, row[:, None], col[:, None])
    ang = pos * inv[f][None, :]
    cos = jnp.cos(ang)
    sin = jnp.where(second[None, :] == 0, -jnp.sin(ang), jnp.sin(ang))
    reps = B_W // DH
    cos = jnp.tile(cos, (1, reps))
    sin = jnp.tile(sin, (1, reps))
    return cos, sin, cos.T, sin.T


def kernel(x_prompt, x_sample, cache_b_k, cache_b_v, cache_c_k, cache_c_v, cache_d_k, cache_d_v, c, c_ctx, norm_pre, norm_post, w_mod, b_mod, w_in_even, a_conv_w, a_conv_b, a_ln_g, a_ln_b, b_lambda, b_subln_g, w_out_even, w_in_odd, c_q_norm, c_k_norm, d_sink, w_out_odd):
    n_even = (DEPTH + 1) // 2
    n_odd = DEPTH // 2
    cond8 = jnp.zeros((SUBLANES, D_MODEL), F32).at[0].set(c_ctx).at[1:1 + DEC_BATCH].set(c)
    mod4 = _modulation(cond8, w_mod, b_mod).reshape(DEPTH, SUBLANES, 1, 3 * D_MODEL)
    tables = _rope_tables()

    xp = x_prompt.reshape(BATCH * SEQ, D_MODEL)
    xs = x_sample.reshape(DEC_BATCH * DEC_SEQ, D_MODEL)
    feat = lambda a, w: jnp.moveaxis(a.reshape(a.shape[:3] + (w,)), 2, 3)
    cbk = feat(cache_b_k, B_W)
    cbv = cache_b_v.reshape(DEC_BATCH, n_even, PAST_LEN * H_B, 2 * DH)
    cck = feat(cache_c_k, KV_W)
    ccv = feat(cache_c_v, KV_W)
    cdk = feat(cache_d_k, KV_W)
    cdv = feat(cache_d_v, KV_W)

    g_pre = norm_pre.reshape(DEPTH, 1, D_MODEL)
    g_post = norm_post.reshape(DEPTH, 1, D_MODEL)
    w_in_e = w_in_even.astype(BF16)
    w_out_e = w_out_even.astype(BF16)
    w_in_o = w_in_odd.astype(BF16)
    w_out_o = w_out_odd.astype(BF16)
    conv = (jnp.zeros((n_even, 4 * SUBLANES, A_W), F32).at[:, :CONV_K].set(a_conv_w),
            a_conv_b.reshape(n_even, 1, A_W), a_ln_g.reshape(n_even, 1, A_W), a_ln_b.reshape(n_even, 1, A_W))
    subln = b_subln_g.reshape(n_even, 1, 2 * DH)
    qkn = (jnp.tile(c_q_norm, (1, C_W // DH)).reshape(n_odd, C_W, 1),
           jnp.tile(c_k_norm, (1, KV_W // DH)).reshape(n_odd, 1, KV_W))
    sink = d_sink.reshape(n_odd, 1, 8)

    new_even, new_odd = (), ()
    for l in range(DEPTH):
        if l % 2 == 0:
            lam_init = 0.8 - 0.6 * math.exp(-0.3 * l)
            w_in, w_out = w_in_e, w_out_e
            ma, qt, k, kt, v, vt, bz = _in_proj(xp, g_pre, mod4, l, w_in, sample=False, extra=conv,
                                                carry=new_even)
            new_even = (kt, v)
            xp = _diff_layer_prompt(qt, k, vt, bz, ma, xp, w_out, g_post, mod4, l,
                                    b_lambda, subln, lam_init)
            ma, qt, k, vt, bz = _in_proj(xs, g_pre, mod4, l, w_in, sample=True, extra=conv, tables=tables)
            xs = _diff_layer_sample(qt, k, vt, bz, ma, xs, w_out, g_post, mod4, l, cbk, cbv,
                                    b_lambda, subln, lam_init)
        else:
            w_in, w_out = w_in_o, w_out_o
            cqt, ck, ckt, cvt, cz, dqt, dk, dkt, dvt, dz = _in_proj(
                xp, g_pre, mod4, l, w_in, sample=False, extra=qkn, carry=new_odd)
            new_odd = (ckt, cvt, dkt, dvt)
            xp = _gqa_layer_prompt(cqt, ck, cvt, cz, dqt, dk, dvt, dz, sink, xp, w_out, g_post, mod4, l)
            cqt, ck, cvt, cz, dqt, dk, dvt, dz = _in_proj(xs, g_pre, mod4, l, w_in, sample=True,
                                                          extra=qkn, tables=tables)
            xs = _gqa_layer_sample(cqt, ck, cvt, cz, dqt, dk, dvt, dz, sink, cck, ccv, cdk, cdv,
                                   xs, w_out, g_post, mod4, l)

    def token_major(a, heads):
        return jnp.moveaxis(a.reshape(a.shape[:2] + heads + (DH, SEQ)), -1, 2)

    kt, v = new_even
    ckt, cvt, dkt, dvt = new_odd
    return (xp.reshape(BATCH, SEQ, D_MODEL), xs.reshape(DEC_BATCH, DEC_SEQ, D_MODEL),
            token_major(kt, (H_B, 2)), v.reshape(BATCH, n_even, SEQ, H_B, 2 * DH),
            token_major(ckt, (2,)), token_major(cvt, (2,)), token_major(dkt, (2,)), token_major(dvt, (2,)))
```

```python
import functools
import math

import jax
import jax.numpy as jnp
from jax import lax
from jax.experimental import pallas as pl
from jax.experimental.pallas import tpu as pltpu

F32 = jnp.float32
BF16 = jnp.bfloat16

D_MODEL = 1024
BATCH = 16
SEQ = 256
DEPTH = 4
DEC_BATCH = 2
DEC_SEQ = 1024
PAST_LEN = 512
GRID_W = 64
ROPE_THETA = 10000.0
NORM_EPS = 1e-6
DH = 64
A_W = 512
CONV_K = 31
H_B = 4
B_W = 512
C_W = 512
KV_W = 128
D_W = 512
WINDOW = 128
LOG2E = math.log2(math.e)
QK_SCALE = DH ** -0.5 * LOG2E

LANES = 128
SUBLANES = 8
VMEM_LIMIT = 56 * 1024 * 1024

TM = 512
TQ = SEQ
HALO = 16
GQA_AHEAD = (8, 4)
DIFF_AHEAD = (3, 2)
ROW_CHUNK = 64


def _params(*sem):
    return pltpu.CompilerParams(dimension_semantics=sem, vmem_limit_bytes=VMEM_LIMIT)


def _silu(x):
    return x * jax.nn.sigmoid(x)


def _dot(a, b):
    return jnp.dot(a, b, preferred_element_type=F32)


def _pick(stacked, idx):
    return pl.BlockSpec((None,) + stacked.shape[1:], lambda *_: (idx,) + (0,) * (stacked.ndim - 1))


def _mod_kernel(cond_ref, w_ref, b_ref, o_ref):
    a = _silu(cond_ref[...]).astype(BF16)
    o_ref[...] = _dot(a, w_ref[...].astype(BF16)) + b_ref[...]


def _modulation(cond8, w_mod, b_mod):
    nblk = 3
    return pl.pallas_call(
        _mod_kernel,
        grid=(DEPTH, nblk),
        in_specs=[
            pl.BlockSpec((SUBLANES, D_MODEL), lambda l, j: (0, 0)),
            pl.BlockSpec((None, D_MODEL, D_MODEL), lambda l, j: (l, 0, j)),
            pl.BlockSpec((None, 1, D_MODEL), lambda l, j: (l, 0, j)),
        ],
        out_specs=pl.BlockSpec((None, SUBLANES, D_MODEL), lambda l, j: (l, 0, j)),
        out_shape=jax.ShapeDtypeStruct((DEPTH, SUBLANES, 3 * D_MODEL), F32),
        compiler_params=_params("arbitrary", "arbitrary"),
        name="modulation",
    )(cond8, w_mod, b_mod.reshape(DEPTH, 1, 3 * D_MODEL))


def _pre_norm(x_ref, g_ref, mod_ref):
    return _modulate(x_ref[...], g_ref, mod_ref)


def _modulate(x, g_ref, mod_ref):
    ms = jnp.mean(x * x, axis=-1, keepdims=True)
    mod = mod_ref[...]
    sh = mod[:, :D_MODEL]
    sc = mod[:, D_MODEL:2 * D_MODEL]
    h = (x * lax.rsqrt(ms + NORM_EPS) * g_ref[...]) * (1.0 + sc) + sh
    return h.astype(BF16)


def _rope(x, cos, sin_signed):
    w = x.shape[-1]
    lane = lax.broadcasted_iota(jnp.int32, (1, w), 1)
    first = (lane % 32) < 16
    partner = jnp.where(first, pltpu.roll(x, w - 16, 1), pltpu.roll(x, 16, 1))
    return x * cos + partner * sin_signed


def _rope_t(x, cos_t, sin_t):
    r = x.shape[0]
    row = lax.broadcasted_iota(jnp.int32, (r, 1), 0)
    first = (row % 32) < 16
    partner = jnp.where(first, pltpu.roll(x, r - 16, 0), pltpu.roll(x, 16, 0))
    return x * cos_t + partner * sin_t


def _store_chunks(ref, xt):
    for c in range(xt.shape[1] // LANES):
        ref[c] = xt[:, c * LANES:(c + 1) * LANES]


def _store_per_seq(ref, xt):
    for s in range(xt.shape[1] // SEQ):
        ref[s] = xt[:, s * SEQ:(s + 1) * SEQ]


def _own_slot(ref, slot):
    if len(ref.shape) == 3:
        return ref
    for other in range(ref.shape[1]):
        if other != slot:
            ref[:, other] = jnp.zeros((ref.shape[0],) + tuple(ref.shape[2:]), ref.dtype)
    return ref.at[:, slot]


def _glu(ug):
    return ug[:, :A_W] * jax.nn.sigmoid(ug[:, A_W:])


def _conv_mix(pad, az, cw_ref, cb_ref, lg_ref, lb_ref, sh_ref, acc_ref):
    rows = sh_ref.shape[1]
    for b in range(SUBLANES):
        sh_ref[b] = pad[b:b + rows]
    base = HALO - CONV_K // 2
    for c0 in range(0, A_W, LANES):
        cs = slice(c0, c0 + LANES)
        for r0 in range(0, SEQ, ROW_CHUNK):
            acc = jnp.zeros((ROW_CHUNK, LANES), F32) + cb_ref[:, cs]
            for k in range(CONV_K):
                j = k + base
                s = r0 + (j // SUBLANES) * SUBLANES
                acc = acc + sh_ref[j % SUBLANES, s:s + ROW_CHUNK, cs] * cw_ref[k:k + 1, cs]
            acc_ref[r0:r0 + ROW_CHUNK, cs] = acc
    a = acc_ref[...]
    mu = jnp.mean(a, axis=-1, keepdims=True)
    d = a - mu
    var = jnp.mean(d * d, axis=-1, keepdims=True)
    y = d * lax.rsqrt(var + NORM_EPS) * lg_ref[...] + lb_ref[...]
    return _silu(y) * _silu(az)


def _in_even_kernel(x_ref, g_ref, mod_ref, w_ref, cw_ref, cb_ref, lg_ref, lb_ref, *rest,
                    sample, n_alias=0, slot=0):
    if sample:
        (xp_ref, xn_ref, cos_ref, sin_ref, cost_ref, sint_ref,
         ma_ref, qt_ref, k_ref, vt_ref, bz_ref, sh_ref, acc_ref) = rest
    else:
        ma_ref, qt_ref, k_ref, kt_ref, v_ref, vt_ref, bz_ref, sh_ref, acc_ref = rest[n_alias:]
        kt_ref = _own_slot(kt_ref, slot)
        v_ref = _own_slot(v_ref, slot)
    hb = _pre_norm(x_ref, g_ref, mod_ref)
    a = _glu(_dot(hb, w_ref[:, 0:2 * A_W]))
    az = _dot(hb, w_ref[:, 2 * A_W:3 * A_W])
    n_sub = TM // SEQ
    if sample:
        tiles_per_seq = DEC_SEQ // TM
        pos = pl.program_id(0) % tiles_per_seq
        xh = jnp.concatenate([xp_ref[...], xn_ref[...]], axis=0)
        ah = _glu(_dot(_modulate(xh, g_ref, mod_ref), w_ref[:, 0:2 * A_W]))
        prev = jnp.where(pos != 0, ah[:HALO], 0.0)
        nxt = jnp.where(pos != tiles_per_seq - 1, ah[HALO:], 0.0)
        full = jnp.concatenate([prev, a, nxt], axis=0)
        pads = [full[j * SEQ:(j + 1) * SEQ + 2 * HALO] for j in range(n_sub)]
    else:
        zeros = jnp.zeros((HALO, A_W), F32)
        pads = [jnp.concatenate([zeros, a[j * SEQ:(j + 1) * SEQ], zeros], axis=0) for j in range(n_sub)]
    o = 3 * A_W
    q = _dot(hb, w_ref[:, o:o + B_W])
    k = _dot(hb, w_ref[:, o + B_W:o + 2 * B_W])
    v = _dot(hb, w_ref[:, o + 2 * B_W:o + 3 * B_W])
    bz_ref[...] = _dot(hb, w_ref[:, o + 3 * B_W:o + 4 * B_W])
    qt = q.T
    if sample:
        qt = _rope_t(qt, cost_ref[...], sint_ref[...])
        k = _rope(k, cos_ref[...], sin_ref[...])
    else:
        _store_per_seq(kt_ref, k.T)
        for s in range(TM // SEQ):
            for h in range(H_B):
                v_ref[s, pl.ds(h, SEQ, stride=H_B), :] = v[s * SEQ:(s + 1) * SEQ, h * LANES:(h + 1) * LANES]
    _store_per_seq(qt_ref, (qt * QK_SCALE).astype(BF16))
    k_ref[...] = k.astype(BF16)
    _store_per_seq(vt_ref, v.T.astype(BF16))
    for j, pad in enumerate(pads):
        rs = slice(j * SEQ, (j + 1) * SEQ)
        ma_ref[rs, :] = _conv_mix(pad, az[rs], cw_ref, cb_ref, lg_ref, lb_ref, sh_ref, acc_ref).astype(BF16)


def _group_mean_sq(x):
    width = x.shape[-1]
    xx = x * x
    hi = xx.astype(BF16)
    lo = (xx - hi.astype(F32)).astype(BF16)
    r = lax.broadcasted_iota(jnp.int32, (width, width), 0) // DH
    c = lax.broadcasted_iota(jnp.int32, (width, width), 1) // DH
    g = jnp.where(r == c, 1.0, 0.0).astype(BF16)
    return (_dot(hi, g) + _dot(lo, g)) * (1.0 / DH)


def _head_rms_t(xt, gain_col):
    parts = []
    for j in range(xt.shape[0] // DH):
        blk = xt[j * DH:(j + 1) * DH]
        ms = jnp.mean(blk * blk, axis=0, keepdims=True)
        parts.append(blk * lax.rsqrt(ms + NORM_EPS))
    return jnp.concatenate(parts, axis=0) * gain_col


def _in_odd_kernel(x_ref, g_ref, mod_ref, w_ref, qn_ref, kn_ref, *rest, sample, n_alias=0, slot=0):
    if sample:
        (cos_ref, sin_ref, cost_ref, sint_ref,
         cqt_ref, ck_ref, cvt_ref, cz_ref, dqt_ref, dk_ref, dvt_ref, dz_ref) = rest
    else:
        (cqt_ref, ck_ref, ckt_ref, cvt_ref, cz_ref,
         dqt_ref, dk_ref, dkt_ref, dvt_ref, dz_ref) = rest[n_alias:]
        ckt_ref, cvt_ref, dkt_ref, dvt_ref = [_own_slot(r, slot) for r in (ckt_ref, cvt_ref, dkt_ref, dvt_ref)]
    hb = _pre_norm(x_ref, g_ref, mod_ref)
    y = _dot(hb, w_ref[...])
    o = 0
    cqt = _head_rms_t(y[:, o:o + C_W].T, qn_ref[...])
    o += C_W
    ck = y[:, o:o + KV_W]
    ck = ck * lax.rsqrt(_group_mean_sq(ck) + NORM_EPS) * kn_ref[...]
    o += KV_W
    cvt = y[:, o:o + KV_W].T
    o += KV_W
    cz_ref[...] = y[:, o:o + C_W]
    o += C_W
    dqt = y[:, o:o + D_W].T
    o += D_W
    dk = y[:, o:o + KV_W]
    o += KV_W
    dvt = y[:, o:o + KV_W].T
    o += KV_W
    dz_ref[...] = y[:, o:o + D_W]
    if sample:
        cos_t = cost_ref[...]
        sin_t = sint_ref[...]
        cqt = _rope_t(cqt, cos_t, sin_t)
        dqt = _rope_t(dqt, cos_t, sin_t)
        cos = cos_ref[...][:, :KV_W]
        sin = sin_ref[...][:, :KV_W]
        ck = _rope(ck, cos, sin)
        dk = _rope(dk, cos, sin)
        _store_chunks(cvt_ref, cvt.astype(BF16))
        _store_chunks(dvt_ref, dvt.astype(BF16))
    else:
        _store_per_seq(ckt_ref, ck.T)
        _store_per_seq(dkt_ref, dk.T)
        _store_per_seq(cvt_ref, cvt)
        _store_per_seq(dvt_ref, dvt)
    _store_per_seq(cqt_ref, (cqt * QK_SCALE).astype(BF16))
    _store_per_seq(dqt_ref, (dqt * QK_SCALE).astype(BF16))
    ck_ref[...] = ck.astype(BF16)
    dk_ref[...] = dk.astype(BF16)


def _in_proj(x, g_pre, mod4, layer, w_bf, *, sample, extra=(), tables=None, carry=()):
    n = x.shape[0]
    even = layer % 2 == 0
    tiles_per_seq = DEC_SEQ // TM
    if sample:
        mod_map = lambda i: (layer, 1 + i // tiles_per_seq, 0, 0)
    else:
        mod_map = lambda i: (layer, 0, 0, 0)
    row = lambda i: (i, 0)
    fixed = lambda i: (0, 0)
    in_specs = [
        pl.BlockSpec((TM, D_MODEL), row),
        _pick(g_pre, layer),
        pl.BlockSpec((None, None, 1, 3 * D_MODEL), mod_map),
        _pick(w_bf, layer // 2),
    ]
    args = [x, g_pre, mod4, w_bf]
    for e in extra:
        in_specs.append(_pick(e, layer // 2))
        args.append(e)
    if sample:
        if even:
            hb = TM // HALO
            last = n // HALO - 1
            in_specs += [pl.BlockSpec((HALO, D_MODEL), lambda i: (jnp.maximum(i * hb - 1, 0), 0)),
                         pl.BlockSpec((HALO, D_MODEL), lambda i: (jnp.minimum((i + 1) * hb, last), 0))]
            args += [x, x]
        cos, sin, cos_t, sin_t = tables
        in_specs += [pl.BlockSpec((TM, B_W), lambda i: (i % tiles_per_seq, 0)),
                     pl.BlockSpec((TM, B_W), lambda i: (i % tiles_per_seq, 0)),
                     pl.BlockSpec((B_W, TM), lambda i: (0, i % tiles_per_seq)),
                     pl.BlockSpec((B_W, TM), lambda i: (0, i % tiles_per_seq))]
        args += [cos, sin, cos_t, sin_t]
    rows = lambda w, dt: (pl.BlockSpec((TM, w), row), jax.ShapeDtypeStruct((n, w), dt))
    cols = lambda w, dt: (pl.BlockSpec((TM // SEQ, w, SEQ), lambda i: (i, 0, 0)),
                          jax.ShapeDtypeStruct((n // SEQ, w, SEQ), dt))
    chunks = lambda w, dt: (pl.BlockSpec((TM // LANES, w, LANES), lambda i: (i, 0, 0)),
                            jax.ShapeDtypeStruct((n // LANES, w, LANES), dt))
    n_layers = (DEPTH + 1 - layer % 2) // 2
    if carry:
        per_seq_spec = lambda r, c: pl.BlockSpec((TM // SEQ, None, r, c), lambda i: (i, layer // 2, 0, 0))
    else:
        per_seq_spec = lambda r, c: pl.BlockSpec((TM // SEQ, n_layers, r, c), lambda i: (i, 0, 0, 0))
    per_seq = lambda r, c: (per_seq_spec(r, c), jax.ShapeDtypeStruct((n // SEQ, n_layers, r, c), F32))
    scratch = []
    if even:
        scratch = [pltpu.VMEM((SUBLANES, SEQ + 3 * SUBLANES, A_W), F32), pltpu.VMEM((SEQ, A_W), F32)]
    if even and sample:
        outs = [rows(A_W, BF16), cols(B_W, BF16), rows(B_W, BF16), cols(B_W, BF16), rows(B_W, F32)]
    elif even:
        outs = [rows(A_W, BF16), cols(B_W, BF16), rows(B_W, BF16),
                per_seq(B_W, SEQ), per_seq(SEQ * H_B, LANES), cols(B_W, BF16), rows(B_W, F32)]
    elif sample:
        outs = [cols(C_W, BF16), rows(KV_W, BF16), chunks(KV_W, BF16), rows(C_W, F32),
                cols(D_W, BF16), rows(KV_W, BF16), chunks(KV_W, BF16), rows(D_W, F32)]
    else:
        outs = [cols(C_W, BF16), rows(KV_W, BF16), per_seq(KV_W, SEQ), per_seq(KV_W, SEQ), rows(C_W, F32),
                cols(D_W, BF16), rows(KV_W, BF16), per_seq(KV_W, SEQ), per_seq(KV_W, SEQ), rows(D_W, F32)]
    aliases = {}
    for a in carry:
        out_idx = [j for j, o in enumerate(outs) if o[1].shape == a.shape and j not in aliases.values()][0]
        aliases[len(args)] = out_idx
        in_specs.append(pl.BlockSpec(memory_space=pl.ANY))
        args.append(a)
    body = functools.partial(_in_even_kernel if even else _in_odd_kernel, sample=sample,
                             n_alias=len(carry), slot=layer // 2)
    return pl.pallas_call(
        body,
        grid=(n // TM,),
        in_specs=in_specs,
        out_specs=[o[0] for o in outs],
        out_shape=[o[1] for o in outs],
        scratch_shapes=scratch,
        input_output_aliases=aliases,
        compiler_params=_params("arbitrary"),
        name=f"in_proj_{'s' if sample else 'p'}{layer}",
    )(*args)


def _softmax_t(segs, extra=None):
    m = None
    for s in segs:
        mi = jnp.max(s, axis=0, keepdims=True)
        m = mi if m is None else jnp.maximum(m, mi)
    if extra is not None:
        extra = extra * LOG2E
        m = jnp.maximum(m, extra)
    es = [jnp.exp2(s - m) for s in segs]
    den = None
    for e in es:
        di = jnp.sum(e, axis=0, keepdims=True)
        den = di if den is None else den + di
    if extra is not None:
        den = den + jnp.exp2(extra - m)
    return es, den


def _keep_rows(xt, lo, hi):
    zeros = lambda r: jnp.zeros((r, xt.shape[1]), xt.dtype)
    parts = []
    if lo > 0:
        parts.append(zeros(lo))
    parts.append(xt[lo:hi])
    if hi < xt.shape[0]:
        parts.append(zeros(xt.shape[0] - hi))
    return jnp.concatenate(parts, axis=0)


def _pipelined(n, scores, finish, ahead):
    ready = [scores(j) for j in range(min(ahead, n))]
    for j in range(n):
        if j + ahead < n:
            ready.append(scores(j + ahead))
        finish(j, ready.pop(0))


def _diff_attn(qt_ref, kvs, z_ref, lam_ref, g_ref, o_ref, *, lam_init):
    lv = lam_ref[...]
    lam = (jnp.exp(jnp.sum(lv[0:1] * lv[1:2], axis=-1, keepdims=True))
           - jnp.exp(jnp.sum(lv[2:3] * lv[3:4], axis=-1, keepdims=True)) + lam_init)

    def scores(h):
        cs = slice(h * LANES, (h + 1) * LANES)
        qt = qt_ref[cs, :]
        ks = [get_k(cs) for get_k, _ in kvs]
        return [[_dot(kk, _keep_rows(qt, c * DH, (c + 1) * DH)) for kk in ks] for c in range(2)]

    def finish(h, ss):
        cs = slice(h * LANES, (h + 1) * LANES)
        es0, den0 = _softmax_t(ss[0])
        es1, den1 = _softmax_t(ss[1])
        r0 = 1.0 / den0
        r1 = lam / den1
        ot = None
        for e0, e1, (_, get_vt) in zip(es0, es1, kvs):
            w = e0 * r0 - e1 * r1
            oi = _dot(get_vt(cs), w.astype(BF16))
            ot = oi if ot is None else ot + oi
        ms = jnp.mean(ot * ot, axis=0, keepdims=True)
        o = (ot * lax.rsqrt(ms + NORM_EPS)).T
        o = (o * g_ref[...]) * (1.0 - lam_init)
        o_ref[:, cs] = (o * _silu(z_ref[:, cs])).astype(o_ref.dtype)

    _pipelined(H_B, scores, finish, DIFF_AHEAD[len(kvs) - 1])


def _post_residual(o, x_ref, g_ref, mod_ref, o_ref):
    ms = jnp.mean(o * o, axis=-1, keepdims=True)
    r = o * lax.rsqrt(ms + NORM_EPS) * g_ref[...]
    gate = mod_ref[...][:, 2 * D_MODEL:]
    o_ref[...] = x_ref[...] + gate * r


def _diff_prompt_kernel(qt_ref, k_ref, vt_ref, z_ref, lam_ref, sg_ref, ma_ref, x_ref, w_ref, g_ref,
                        mod_ref, o_ref, mb_ref, *, lam_init):
    oa = _dot(ma_ref[...], w_ref[0:A_W, :])
    for s in range(qt_ref.shape[0]):
        rs = pl.ds(s * SEQ, SEQ)
        kv = (lambda cs, s=s: k_ref[s * SEQ:(s + 1) * SEQ, cs], lambda cs, s=s: vt_ref[s, cs, :])
        _diff_attn(qt_ref.at[s], [kv], z_ref.at[rs], lam_ref, sg_ref, mb_ref.at[rs], lam_init=lam_init)
    _post_residual(oa + _dot(mb_ref[...], w_ref[A_W:, :]), x_ref, g_ref, mod_ref, o_ref)


def _diff_sample_kernel(qt_ref, k_ref, vt_ref, ck_ref, cv_ref, z_ref, lam_ref, sg_ref, ma_ref, x_ref,
                        w_ref, g_ref, mod_ref, o_ref, ckb_ref, cvt_ref, mb_ref, *, lam_init):
    @pl.when(pl.program_id(1) == 0)
    def _():
        ckb_ref[...] = ck_ref[...].T.astype(BF16)
        for h in range(H_B):
            cs = slice(h * LANES, (h + 1) * LANES)
            cvt_ref[cs, :] = cv_ref[pl.ds(h, PAST_LEN, stride=H_B), :].T.astype(BF16)

    ctx = (lambda cs: ckb_ref[:, cs], lambda cs: cvt_ref[cs, :])
    loc = (lambda cs: k_ref[:, cs],
           lambda cs: jnp.concatenate([vt_ref[c, cs, :] for c in range(vt_ref.shape[0])], axis=1))
    oa = _dot(ma_ref[...], w_ref[0:A_W, :])
    _diff_attn(qt_ref, [ctx, loc], z_ref, lam_ref, sg_ref, mb_ref, lam_init=lam_init)
    _post_residual(oa + _dot(mb_ref[...], w_ref[A_W:, :]), x_ref, g_ref, mod_ref, o_ref)


def _out_specs_prompt(layer, w_out, g_post):
    return [pl.BlockSpec((TM, D_MODEL), lambda i: (i, 0)),
            _pick(w_out, layer // 2),
            _pick(g_post, layer),
            pl.BlockSpec((None, None, 1, 3 * D_MODEL), lambda i: (layer, 0, 0, 0))]


def _out_specs_sample(layer, w_out, g_post):
    nq = DEC_SEQ // TQ
    return [pl.BlockSpec((TQ, D_MODEL), lambda b, t: (b * nq + t, 0)),
            _pick(w_out, layer // 2),
            _pick(g_post, layer),
            pl.BlockSpec((None, None, 1, 3 * D_MODEL), lambda b, t: (layer, 1 + b, 0, 0))]


def _diff_layer_prompt(qt, k, vt, z, ma, x, w_out, g_post, mod4, layer, lam_vec, subln_g, lam_init):
    n = x.shape[0]
    row = lambda i: (i, 0)
    fixed = lambda i: (0, 0)
    blk = pl.BlockSpec((TM, B_W), row)
    tblk = pl.BlockSpec((TM // SEQ, B_W, SEQ), lambda i: (i, 0, 0))
    return pl.pallas_call(
        functools.partial(_diff_prompt_kernel, lam_init=lam_init),
        grid=(n // TM,),
        in_specs=[tblk, blk, tblk, blk, _pick(lam_vec, layer // 2), _pick(subln_g, layer // 2),
                  blk] + _out_specs_prompt(layer, w_out, g_post),
        out_specs=pl.BlockSpec((TM, D_MODEL), row),
        out_shape=jax.ShapeDtypeStruct((n, D_MODEL), F32),
        scratch_shapes=[pltpu.VMEM((TM, B_W), BF16)],
        compiler_params=_params("arbitrary"),
        name=f"diff_layer_p{layer}",
    )(qt, k, vt, z, lam_vec, subln_g, ma, x, w_out, g_post, mod4)


def _diff_layer_sample(qt, k, vt, z, ma, x, w_out, g_post, mod4, layer, cache_k, cache_v, lam_vec,
                       subln_g, lam_init):
    n = x.shape[0]
    nq = DEC_SEQ // TQ
    qrow = lambda b, t: (b * nq + t, 0)
    cache = lambda b, t: (b, layer // 2, 0, 0)
    fixed = lambda b, t: (0, 0)
    return pl.pallas_call(
        functools.partial(_diff_sample_kernel, lam_init=lam_init),
        grid=(DEC_BATCH, nq),
        in_specs=[
            pl.BlockSpec((None, B_W, TQ), lambda b, t: (b * nq + t, 0, 0)),
            pl.BlockSpec((DEC_SEQ, B_W), lambda b, t: (b, 0)),
            pl.BlockSpec((nq, B_W, TQ), lambda b, t: (b, 0, 0)),
            pl.BlockSpec((None, None, B_W, PAST_LEN), cache),
            pl.BlockSpec((None, None, PAST_LEN * H_B, LANES), cache),
            pl.BlockSpec((TQ, B_W), qrow),
            _pick(lam_vec, layer // 2),
            _pick(subln_g, layer // 2),
            pl.BlockSpec((TQ, A_W), qrow),
        ] + _out_specs_sample(layer, w_out, g_post),
        out_specs=pl.BlockSpec((TQ, D_MODEL), qrow),
        out_shape=jax.ShapeDtypeStruct((n, D_MODEL), F32),
        scratch_shapes=[pltpu.VMEM((PAST_LEN, B_W), BF16), pltpu.VMEM((B_W, PAST_LEN), BF16),
                        pltpu.VMEM((TQ, B_W), BF16)],
        compiler_params=_params("arbitrary", "arbitrary"),
        name=f"diff_layer_s{layer}",
    )(qt, k, vt, cache_k, cache_v, z, lam_vec, subln_g, ma, x, w_out, g_post, mod4)


def _gqa(qt_ref, segs, z_ref, o_ref, sink_ref=None):
    halves = []

    def scores(j):
        n = j // 4
        qj = qt_ref[j * DH:(j + 1) * DH, :]
        zero = jnp.zeros_like(qj)
        qz = jnp.concatenate([qj, zero] if n == 0 else [zero, qj], axis=0)
        return [_dot(k, qz) for k, _, _ in segs]

    def finish(j, ss):
        n = j // 4
        ss = [s if valid is None else jnp.where(valid, s, -jnp.inf)
              for s, (_, _, valid) in zip(ss, segs)]
        extra = None if sink_ref is None else sink_ref[:, j:j + 1]
        es, den = _softmax_t(ss, extra)
        ot = None
        for e, (_, vt, _) in zip(es, segs):
            oi = _dot(vt[n * DH:(n + 1) * DH], e.astype(BF16))
            ot = oi if ot is None else ot + oi
        halves.append(ot * (1.0 / den))
        if j % 2 == 1:
            cs = slice((j // 2) * LANES, (j // 2 + 1) * LANES)
            o_pair = jnp.concatenate(halves[-2:], axis=0).T
            o_ref[:, cs] = (o_pair * _silu(z_ref[:, cs])).astype(o_ref.dtype)

    _pipelined(2 * 4, scores, finish, GQA_AHEAD[len(segs) - 1])


def _gqa_prompt_kernel(cqt_ref, ck_ref, cvt_ref, cz_ref, dqt_ref, dk_ref, dvt_ref, dz_ref, sink_ref,
                       x_ref, w_ref, g_ref, mod_ref, o_ref, m_ref):
    for s in range(cqt_ref.shape[0]):
        rows = slice(s * SEQ, (s + 1) * SEQ)
        rs = pl.ds(s * SEQ, SEQ)
        seg = lambda k_ref, vt_ref: (k_ref[rows, :], vt_ref[s].astype(BF16), None)
        _gqa(cqt_ref.at[s], [seg(ck_ref, cvt_ref)], cz_ref.at[rs], m_ref.at[rs, pl.ds(0, C_W)])
        _gqa(dqt_ref.at[s], [seg(dk_ref, dvt_ref)], dz_ref.at[rs], m_ref.at[rs, pl.ds(C_W, D_W)], sink_ref)
    _post_residual(_dot(m_ref[...], w_ref[...]), x_ref, g_ref, mod_ref, o_ref)


def _gqa_sample_kernel(cqt_ref, ck_ref, cvt_ref, cck_ref, ccv_ref, cz_ref,
                       dqt_ref, dk_ref, dvt_ref, cdk_ref, cdv_ref, dz_ref, sink_ref,
                       x_ref, w_ref, g_ref, mod_ref, o_ref, m_ref):
    oc_ref = m_ref.at[:, pl.ds(0, C_W)]
    od_ref = m_ref.at[:, pl.ds(C_W, D_W)]
    t = pl.program_id(1)
    ctx = lambda kt_ref, vt_ref: (kt_ref[...].T.astype(BF16), vt_ref[...].astype(BF16), None)
    n_chunks = DEC_SEQ // LANES
    cvt = jnp.concatenate([cvt_ref[c] for c in range(n_chunks)], axis=1)
    _gqa(cqt_ref, [ctx(cck_ref, ccv_ref), (ck_ref[...], cvt, None)], cz_ref, oc_ref)
    span = 2 * TQ
    t0 = t * TQ
    ws = pl.multiple_of(jnp.clip(t0 - WINDOW, 0, DEC_SEQ - span), WINDOW)
    kpos = ws + lax.broadcasted_iota(jnp.int32, (span, TQ), 0)
    qpos = t0 + lax.broadcasted_iota(jnp.int32, (span, TQ), 1)
    valid = jnp.abs(qpos - kpos) <= WINDOW
    c0 = ws // LANES
    dvt = jnp.concatenate([dvt_ref[c0 + c] for c in range(span // LANES)], axis=1)
    _gqa(dqt_ref, [ctx(cdk_ref, cdv_ref), (dk_ref[pl.ds(ws, span), :], dvt, valid)],
         dz_ref, od_ref, sink_ref)
    _post_residual(_dot(m_ref[...], w_ref[...]), x_ref, g_ref, mod_ref, o_ref)


def _gqa_layer_prompt(cqt, ck, cvt, cz, dqt, dk, dvt, dz, sink, x, w_out, g_post, mod4, layer):
    n = x.shape[0]
    row = lambda i: (i, 0)
    qt = pl.BlockSpec((TM // SEQ, C_W, SEQ), lambda i: (i, 0, 0))
    wide = pl.BlockSpec((TM, C_W), row)
    kv = pl.BlockSpec((TM, KV_W), row)
    vt = pl.BlockSpec((TM // SEQ, None, KV_W, SEQ), lambda i: (i, layer // 2, 0, 0))
    return pl.pallas_call(
        _gqa_prompt_kernel,
        grid=(n // TM,),
        in_specs=[qt, kv, vt, wide, qt, kv, vt, wide, _pick(sink, layer // 2)]
        + _out_specs_prompt(layer, w_out, g_post),
        out_specs=pl.BlockSpec((TM, D_MODEL), row),
        out_shape=jax.ShapeDtypeStruct((n, D_MODEL), F32),
        scratch_shapes=[pltpu.VMEM((TM, C_W + D_W), BF16)],
        compiler_params=_params("arbitrary"),
        name=f"gqa_layer_p{layer}",
    )(cqt, ck, cvt, cz, dqt, dk, dvt, dz, sink, x, w_out, g_post, mod4)


def _gqa_layer_sample(cqt, ck, cvt, cz, dqt, dk, dvt, dz, sink, cck, ccv, cdk, cdv, x, w_out, g_post,
                      mod4, layer):
    n = x.shape[0]
    nq = DEC_SEQ // TQ
    qrow = lambda b, t: (b * nq + t, 0)
    cache = lambda b, t: (b, layer // 2, 0, 0)
    qt = pl.BlockSpec((None, C_W, TQ), lambda b, t: (b * nq + t, 0, 0))
    wide = pl.BlockSpec((TQ, C_W), qrow)
    kv = pl.BlockSpec((DEC_SEQ, KV_W), lambda b, t: (b, 0))
    vt = pl.BlockSpec((DEC_SEQ // LANES, KV_W, LANES), lambda b, t: (b, 0, 0))
    cb = pl.BlockSpec((None, None, KV_W, PAST_LEN), cache)
    return pl.pallas_call(
        _gqa_sample_kernel,
        grid=(DEC_BATCH, nq),
        in_specs=[qt, kv, vt, cb, cb, wide, qt, kv, vt, cb, cb, wide,
                  _pick(sink, layer // 2)] + _out_specs_sample(layer, w_out, g_post),
        out_specs=pl.BlockSpec((TQ, D_MODEL), qrow),
        out_shape=jax.ShapeDtypeStruct((n, D_MODEL), F32),
        scratch_shapes=[pltpu.VMEM((TQ, C_W + D_W), BF16)],
        compiler_params=_params("arbitrary", "arbitrary"),
        name=f"gqa_layer_s{layer}",
    )(cqt, ck, cvt, cck, ccv, cz, dqt, dk, dvt, cdk, cdv, dz, sink, x, w_out, g_post, mod4)


def _rope_tables():
    nf = DH // 4
    t = jnp.arange(DEC_SEQ)
    row = (t // GRID_W).astype(F32)
    col = (t % GRID_W).astype(F32)
    inv = ROPE_THETA ** (-jnp.arange(nf, dtype=F32) / nf)
    d = jnp.arange(DH)
    axis = d // (2 * nf)
    second = (d % (2 * nf)) // nf
    f = d % nf
    pos = jnp.where(axis[None, :] == 0, row[:, None], col[:, None])
    ang = pos * inv[f][None, :]
    cos = jnp.cos(ang)
    sin = jnp.where(second[None, :] == 0, -jnp.sin(ang), jnp.sin(ang))
    reps = B_W // DH
    cos = jnp.tile(cos, (1, reps))
    sin = jnp.tile(sin, (1, reps))
    return cos, sin, cos.T, sin.T


def kernel(x_prompt, x_sample, cache_b_k, cache_b_v, cache_c_k, cache_c_v, cache_d_k, cache_d_v, c, c_ctx, norm_pre, norm_post, w_mod, b_mod, w_in_even, a_conv_w, a_conv_b, a_ln_g, a_ln_b, b_lambda, b_subln_g, w_out_even, w_in_odd, c_q_norm, c_k_norm, d_sink, w_out_odd):
    n_even = (DEPTH + 1) // 2
    n_odd = DEPTH // 2
    cond8 = jnp.zeros((SUBLANES, D_MODEL), F32).at[0].set(c_ctx).at[1:1 + DEC_BATCH].set(c)
    mod4 = _modulation(cond8, w_mod, b_mod).reshape(DEPTH, SUBLANES, 1, 3 * D_MODEL)
    tables = _rope_tables()

    xp = x_prompt.reshape(BATCH * SEQ, D_MODEL)
    xs = x_sample.reshape(DEC_BATCH * DEC_SEQ, D_MODEL)
    feat = lambda a, w: jnp.moveaxis(a.reshape(a.shape[:3] + (w,)), 2, 3)
    cbk = feat(cache_b_k, B_W)
    cbv = cache_b_v.reshape(DEC_BATCH, n_even, PAST_LEN * H_B, 2 * DH)
    cck = feat(cache_c_k, KV_W)
    ccv = feat(cache_c_v, KV_W)
    cdk = feat(cache_d_k, KV_W)
    cdv = feat(cache_d_v, KV_W)

    g_pre = norm_pre.reshape(DEPTH, 1, D_MODEL)
    g_post = norm_post.reshape(DEPTH, 1, D_MODEL)
    w_in_e = w_in_even.astype(BF16)
    w_out_e = w_out_even.astype(BF16)
    w_in_o = w_in_odd.astype(BF16)
    w_out_o = w_out_odd.astype(BF16)
    conv = (jnp.zeros((n_even, 4 * SUBLANES, A_W), F32).at[:, :CONV_K].set(a_conv_w),
            a_conv_b.reshape(n_even, 1, A_W), a_ln_g.reshape(n_even, 1, A_W), a_ln_b.reshape(n_even, 1, A_W))
    subln = b_subln_g.reshape(n_even, 1, 2 * DH)
    qkn = (jnp.tile(c_q_norm, (1, C_W // DH)).reshape(n_odd, C_W, 1),
           jnp.tile(c_k_norm, (1, KV_W // DH)).reshape(n_odd, 1, KV_W))
    sink = d_sink.reshape(n_odd, 1, 8)

    new_even, new_odd = (), ()
    for l in range(DEPTH):
        if l % 2 == 0:
            lam_init = 0.8 - 0.6 * math.exp(-0.3 * l)
            w_in, w_out = w_in_e, w_out_e
            ma, qt, k, kt, v, vt, bz = _in_proj(xp, g_pre, mod4, l, w_in, sample=False, extra=conv,
                                                carry=new_even)
            new_even = (kt, v)
            xp = _diff_layer_prompt(qt, k, vt, bz, ma, xp, w_out, g_post, mod4, l,
                                    b_lambda, subln, lam_init)
            ma, qt, k, vt, bz = _in_proj(xs, g_pre, mod4, l, w_in, sample=True, extra=conv, tables=tables)
            xs = _diff_layer_sample(qt, k, vt, bz, ma, xs, w_out, g_post, mod4, l, cbk, cbv,
                                    b_lambda, subln, lam_init)
        else:
            w_in, w_out = w_in_o, w_out_o
            cqt, ck, ckt, cvt, cz, dqt, dk, dkt, dvt, dz = _in_proj(
                xp, g_pre, mod4, l, w_in, sample=False, extra=qkn, carry=new_odd)
            new_odd = (ckt, cvt, dkt, dvt)
            xp = _gqa_layer_prompt(cqt, ck, cvt, cz, dqt, dk, dvt, dz, sink, xp, w_out, g_post, mod4, l)
            cqt, ck, cvt, cz, dqt, dk, dvt, dz = _in_proj(xs, g_pre, mod4, l, w_in, sample=True,
                                                          extra=qkn, tables=tables)
            xs = _gqa_layer_sample(cqt, ck, cvt, cz, dqt, dk, dvt, dz, sink, cck, ccv, cdk, cdv,
                                   xs, w_out, g_post, mod4, l)

    def token_major(a, heads):
        return jnp.moveaxis(a.reshape(a.shape[:2] + heads + (DH, SEQ)), -1, 2)

    kt, v = new_even
    ckt, cvt, dkt, dvt = new_odd
    return (xp.reshape(BATCH, SEQ, D_MODEL), xs.reshape(DEC_BATCH, DEC_SEQ, D_MODEL),
            token_major(kt, (H_B, 2)), v.reshape(BATCH, n_even, SEQ, H_B, 2 * DH),
            token_major(ckt, (2,)), token_major(cvt, (2,)), token_major(dkt, (2,)), token_major(dvt, (2,)))
```

```python
import functools
import math

import jax
import jax.numpy as jnp
from jax import lax
from jax.experimental import pallas as pl
from jax.experimental.pallas import tpu as pltpu

F32 = jnp.float32
BF16 = jnp.bfloat16

D_MODEL = 1024
BATCH = 16
SEQ = 256
DEPTH = 4
DEC_BATCH = 2
DEC_SEQ = 1024
PAST_LEN = 512
GRID_W = 64
ROPE_THETA = 10000.0
NORM_EPS = 1e-6
DH = 64
A_W = 512
CONV_K = 31
H_B = 4
B_W = 512
C_W = 512
KV_W = 128
D_W = 512
WINDOW = 128
LOG2E = math.log2(math.e)
QK_SCALE = DH ** -0.5 * LOG2E

LANES = 128
SUBLANES = 8
VMEM_LIMIT = 56 * 1024 * 1024

TM = 512
TQ = SEQ
HALO = 16
GQA_AHEAD = (8, 4)
DIFF_AHEAD = (3, 2)
ROW_CHUNK = 64


def _params(*sem):
    return pltpu.CompilerParams(dimension_semantics=sem, vmem_limit_bytes=VMEM_LIMIT)


def _silu(x):
    return x * jax.nn.sigmoid(x)


def _dot(a, b):
    return jnp.dot(a, b, preferred_element_type=F32)


def _pick(stacked, idx):
    return pl.BlockSpec((None,) + stacked.shape[1:], lambda *_: (idx,) + (0,) * (stacked.ndim - 1))


def _mod_kernel(cond_ref, w_ref, b_ref, o_ref):
    a = _silu(cond_ref[...]).astype(BF16)
    o_ref[...] = _dot(a, w_ref[...].astype(BF16)) + b_ref[...]


def _modulation(cond8, w_mod, b_mod):
    nblk = 3
    return pl.pallas_call(
        _mod_kernel,
        grid=(DEPTH, nblk),
        in_specs=[
            pl.BlockSpec((SUBLANES, D_MODEL), lambda l, j: (0, 0)),
            pl.BlockSpec((None, D_MODEL, D_MODEL), lambda l, j: (l, 0, j)),
            pl.BlockSpec((None, 1, D_MODEL), lambda l, j: (l, 0, j)),
        ],
        out_specs=pl.BlockSpec((None, SUBLANES, D_MODEL), lambda l, j: (l, 0, j)),
        out_shape=jax.ShapeDtypeStruct((DEPTH, SUBLANES, 3 * D_MODEL), F32),
        compiler_params=_params("arbitrary", "arbitrary"),
        name="modulation",
    )(cond8, w_mod, b_mod.reshape(DEPTH, 1, 3 * D_MODEL))


def _pre_norm(x_ref, g_ref, mod_ref):
    return _modulate(x_ref[...], g_ref, mod_ref)


def _modulate(x, g_ref, mod_ref):
    ms = jnp.mean(x * x, axis=-1, keepdims=True)
    mod = mod_ref[...]
    sh = mod[:, :D_MODEL]
    sc = mod[:, D_MODEL:2 * D_MODEL]
    h = (x * lax.rsqrt(ms + NORM_EPS) * g_ref[...]) * (1.0 + sc) + sh
    return h.astype(BF16)


def _rope(x, cos, sin_signed):
    w = x.shape[-1]
    lane = lax.broadcasted_iota(jnp.int32, (1, w), 1)
    first = (lane % 32) < 16
    partner = jnp.where(first, pltpu.roll(x, w - 16, 1), pltpu.roll(x, 16, 1))
    return x * cos + partner * sin_signed


def _rope_t(x, cos_t, sin_t):
    r = x.shape[0]
    row = lax.broadcasted_iota(jnp.int32, (r, 1), 0)
    first = (row % 32) < 16
    partner = jnp.where(first, pltpu.roll(x, r - 16, 0), pltpu.roll(x, 16, 0))
    return x * cos_t + partner * sin_t


def _store_chunks(ref, xt):
    for c in range(xt.shape[1] // LANES):
        ref[c] = xt[:, c * LANES:(c + 1) * LANES]


def _store_per_seq(ref, xt):
    for s in range(xt.shape[1] // SEQ):
        ref[s] = xt[:, s * SEQ:(s + 1) * SEQ]


def _own_slot(ref, slot, fill=True):
    if len(ref.shape) == 3:
        return ref
    for other in range(ref.shape[1]):
        if fill and other != slot:
            ref[:, other] = jnp.zeros((ref.shape[0],) + tuple(ref.shape[2:]), ref.dtype)
    return ref.at[:, slot]


def _glu(ug):
    return ug[:, :A_W] * jax.nn.sigmoid(ug[:, A_W:])


def _conv_mix(pad, az, cw_ref, cb_ref, lg_ref, lb_ref, sh_ref, acc_ref):
    rows = sh_ref.shape[1]
    for b in range(SUBLANES):
        sh_ref[b] = pad[b:b + rows]
    base = HALO - CONV_K // 2
    for c0 in range(0, A_W, LANES):
        cs = slice(c0, c0 + LANES)
        for r0 in range(0, SEQ, ROW_CHUNK):
            acc = jnp.zeros((ROW_CHUNK, LANES), F32) + cb_ref[:, cs]
            for k in range(CONV_K):
                j = k + base
                s = r0 + (j // SUBLANES) * SUBLANES
                acc = acc + sh_ref[j % SUBLANES, s:s + ROW_CHUNK, cs] * cw_ref[k:k + 1, cs]
            acc_ref[r0:r0 + ROW_CHUNK, cs] = acc
    a = acc_ref[...]
    mu = jnp.mean(a, axis=-1, keepdims=True)
    d = a - mu
    var = jnp.mean(d * d, axis=-1, keepdims=True)
    y = d * lax.rsqrt(var + NORM_EPS) * lg_ref[...] + lb_ref[...]
    return _silu(y) * _silu(az)


def _in_even_kernel(x_ref, g_ref, mod_ref, w_ref, cw_ref, cb_ref, lg_ref, lb_ref, *rest,
                    sample, n_alias=0, slot=0):
    if sample:
        (xp_ref, xn_ref, cos_ref, sin_ref, cost_ref, sint_ref,
         ma_ref, qt_ref, k_ref, vt_ref, bz_ref, sh_ref, acc_ref) = rest
    else:
        ma_ref, qt_ref, k_ref, kt_ref, v_ref, vt_ref, bz_ref, sh_ref, acc_ref = rest[n_alias:]
        kt_ref = _own_slot(kt_ref, slot)
        v_ref = _own_slot(v_ref, slot)
    hb = _pre_norm(x_ref, g_ref, mod_ref)
    a = _glu(_dot(hb, w_ref[:, 0:2 * A_W]))
    az = _dot(hb, w_ref[:, 2 * A_W:3 * A_W])
    n_sub = TM // SEQ
    if sample:
        tiles_per_seq = DEC_SEQ // TM
        pos = pl.program_id(0) % tiles_per_seq
        xh = jnp.concatenate([xp_ref[...], xn_ref[...]], axis=0)
        ah = _glu(_dot(_modulate(xh, g_ref, mod_ref), w_ref[:, 0:2 * A_W]))
        prev = jnp.where(pos != 0, ah[:HALO], 0.0)
        nxt = jnp.where(pos != tiles_per_seq - 1, ah[HALO:], 0.0)
        full = jnp.concatenate([prev, a, nxt], axis=0)
        pads = [full[j * SEQ:(j + 1) * SEQ + 2 * HALO] for j in range(n_sub)]
    else:
        zeros = jnp.zeros((HALO, A_W), F32)
        pads = [jnp.concatenate([zeros, a[j * SEQ:(j + 1) * SEQ], zeros], axis=0) for j in range(n_sub)]
    o = 3 * A_W
    q = _dot(hb, w_ref[:, o:o + B_W])
    k = _dot(hb, w_ref[:, o + B_W:o + 2 * B_W])
    v = _dot(hb, w_ref[:, o + 2 * B_W:o + 3 * B_W])
    bz_ref[...] = _dot(hb, w_ref[:, o + 3 * B_W:o + 4 * B_W])
    qt = q.T
    if sample:
        qt = _rope_t(qt, cost_ref[...], sint_ref[...])
        k = _rope(k, cos_ref[...], sin_ref[...])
    else:
        _store_per_seq(kt_ref, k.T)
        for s in range(TM // SEQ):
            for h in range(H_B):
                v_ref[s, pl.ds(h, SEQ, stride=H_B), :] = v[s * SEQ:(s + 1) * SEQ, h * LANES:(h + 1) * LANES]
    _store_per_seq(qt_ref, (qt * QK_SCALE).astype(BF16))
    k_ref[...] = k.astype(BF16)
    _store_per_seq(vt_ref, v.T.astype(BF16))
    for j, pad in enumerate(pads):
        rs = slice(j * SEQ, (j + 1) * SEQ)
        ma_ref[rs, :] = _conv_mix(pad, az[rs], cw_ref, cb_ref, lg_ref, lb_ref, sh_ref, acc_ref).astype(BF16)


def _group_mean_sq(x):
    width = x.shape[-1]
    xx = x * x
    hi = xx.astype(BF16)
    lo = (xx - hi.astype(F32)).astype(BF16)
    r = lax.broadcasted_iota(jnp.int32, (width, width), 0) // DH
    c = lax.broadcasted_iota(jnp.int32, (width, width), 1) // DH
    g = jnp.where(r == c, 1.0, 0.0).astype(BF16)
    return (_dot(hi, g) + _dot(lo, g)) * (1.0 / DH)


def _head_rms_t(xt, gain_col):
    parts = []
    for j in range(xt.shape[0] // DH):
        blk = xt[j * DH:(j + 1) * DH]
        ms = jnp.mean(blk * blk, axis=0, keepdims=True)
        parts.append(blk * lax.rsqrt(ms + NORM_EPS))
    return jnp.concatenate(parts, axis=0) * gain_col


def _in_odd_kernel(x_ref, g_ref, mod_ref, w_ref, qn_ref, kn_ref, *rest, sample, n_alias=0, slot=0):
    if sample:
        (cos_ref, sin_ref, cost_ref, sint_ref,
         cqt_ref, ck_ref, cvt_ref, cz_ref, dqt_ref, dk_ref, dvt_ref, dz_ref) = rest
    else:
        (cqt_ref, ck_ref, ckt_ref, cvt_ref, cz_ref,
         dqt_ref, dk_ref, dkt_ref, dvt_ref, dz_ref) = rest[n_alias:]
        ckt_ref, cvt_ref, dkt_ref, dvt_ref = [_own_slot(r, slot) for r in (ckt_ref, cvt_ref, dkt_ref, dvt_ref)]
    hb = _pre_norm(x_ref, g_ref, mod_ref)
    y = _dot(hb, w_ref[...])
    o = 0
    cqt = _head_rms_t(y[:, o:o + C_W].T, qn_ref[...])
    o += C_W
    ck = y[:, o:o + KV_W]
    ck = ck * lax.rsqrt(_group_mean_sq(ck) + NORM_EPS) * kn_ref[...]
    o += KV_W
    cvt = y[:, o:o + KV_W].T
    o += KV_W
    cz_ref[...] = y[:, o:o + C_W]
    o += C_W
    dqt = y[:, o:o + D_W].T
    o += D_W
    dk = y[:, o:o + KV_W]
    o += KV_W
    dvt = y[:, o:o + KV_W].T
    o += KV_W
    dz_ref[...] = y[:, o:o + D_W]
    if sample:
        cos_t = cost_ref[...]
        sin_t = sint_ref[...]
        cqt = _rope_t(cqt, cos_t, sin_t)
        dqt = _rope_t(dqt, cos_t, sin_t)
        cos = cos_ref[...][:, :KV_W]
        sin = sin_ref[...][:, :KV_W]
        ck = _rope(ck, cos, sin)
        dk = _rope(dk, cos, sin)
        _store_chunks(cvt_ref, cvt.astype(BF16))
        _store_chunks(dvt_ref, dvt.astype(BF16))
    else:
        _store_per_seq(ckt_ref, ck.T)
        _store_per_seq(dkt_ref, dk.T)
        _store_per_seq(cvt_ref, cvt)
        _store_per_seq(dvt_ref, dvt)
    _store_per_seq(cqt_ref, (cqt * QK_SCALE).astype(BF16))
    _store_per_seq(dqt_ref, (dqt * QK_SCALE).astype(BF16))
    ck_ref[...] = ck.astype(BF16)
    dk_ref[...] = dk.astype(BF16)


def _in_proj(x, g_pre, mod4, layer, w_bf, *, sample, extra=(), tables=None, carry=()):
    n = x.shape[0]
    even = layer % 2 == 0
    tiles_per_seq = DEC_SEQ // TM
    if sample:
        mod_map = lambda i: (layer, 1 + i // tiles_per_seq, 0, 0)
    else:
        mod_map = lambda i: (layer, 0, 0, 0)
    row = lambda i: (i, 0)
    fixed = lambda i: (0, 0)
    in_specs = [
        pl.BlockSpec((TM, D_MODEL), row),
        _pick(g_pre, layer),
        pl.BlockSpec((None, None, 1, 3 * D_MODEL), mod_map),
        _pick(w_bf, layer // 2),
    ]
    args = [x, g_pre, mod4, w_bf]
    for e in extra:
        in_specs.append(_pick(e, layer // 2))
        args.append(e)
    if sample:
        if even:
            hb = TM // HALO
            last = n // HALO - 1
            in_specs += [pl.BlockSpec((HALO, D_MODEL), lambda i: (jnp.maximum(i * hb - 1, 0), 0)),
                         pl.BlockSpec((HALO, D_MODEL), lambda i: (jnp.minimum((i + 1) * hb, last), 0))]
            args += [x, x]
        cos, sin, cos_t, sin_t = tables
        in_specs += [pl.BlockSpec((TM, B_W), lambda i: (i % tiles_per_seq, 0)),
                     pl.BlockSpec((TM, B_W), lambda i: (i % tiles_per_seq, 0)),
                     pl.BlockSpec((B_W, TM), lambda i: (0, i % tiles_per_seq)),
                     pl.BlockSpec((B_W, TM), lambda i: (0, i % tiles_per_seq))]
        args += [cos, sin, cos_t, sin_t]
    rows = lambda w, dt: (pl.BlockSpec((TM, w), row), jax.ShapeDtypeStruct((n, w), dt))
    cols = lambda w, dt: (pl.BlockSpec((TM // SEQ, w, SEQ), lambda i: (i, 0, 0)),
                          jax.ShapeDtypeStruct((n // SEQ, w, SEQ), dt))
    chunks = lambda w, dt: (pl.BlockSpec((TM // LANES, w, LANES), lambda i: (i, 0, 0)),
                            jax.ShapeDtypeStruct((n // LANES, w, LANES), dt))
    n_layers = (DEPTH + 1 - layer % 2) // 2
    if carry:
        per_seq_spec = lambda r, c: pl.BlockSpec((TM // SEQ, None, r, c), lambda i: (i, layer // 2, 0, 0))
    else:
        per_seq_spec = lambda r, c: pl.BlockSpec((TM // SEQ, n_layers, r, c), lambda i: (i, 0, 0, 0))
    per_seq = lambda r, c: (per_seq_spec(r, c), jax.ShapeDtypeStruct((n // SEQ, n_layers, r, c), F32))
    scratch = []
    if even:
        scratch = [pltpu.VMEM((SUBLANES, SEQ + 3 * SUBLANES, A_W), F32), pltpu.VMEM((SEQ, A_W), F32)]
    if even and sample:
        outs = [rows(A_W, BF16), cols(B_W, BF16), rows(B_W, BF16), cols(B_W, BF16), rows(B_W, F32)]
    elif even:
        outs = [rows(A_W, BF16), cols(B_W, BF16), rows(B_W, BF16),
                per_seq(B_W, SEQ), per_seq(SEQ * H_B, LANES), cols(B_W, BF16), rows(B_W, F32)]
    elif sample:
        outs = [cols(C_W, BF16), rows(KV_W, BF16), chunks(KV_W, BF16), rows(C_W, F32),
                cols(D_W, BF16), rows(KV_W, BF16), chunks(KV_W, BF16), rows(D_W, F32)]
    else:
        outs = [cols(C_W, BF16), rows(KV_W, BF16), per_seq(KV_W, SEQ), per_seq(KV_W, SEQ), rows(C_W, F32),
                cols(D_W, BF16), rows(KV_W, BF16), per_seq(KV_W, SEQ), per_seq(KV_W, SEQ), rows(D_W, F32)]
    aliases = {}
    for a in carry:
        out_idx = [j for j, o in enumerate(outs) if o[1].shape == a.shape and j not in aliases.values()][0]
        aliases[len(args)] = out_idx
        in_specs.append(pl.BlockSpec(memory_space=pl.ANY))
        args.append(a)
    body = functools.partial(_in_even_kernel if even else _in_odd_kernel, sample=sample,
                             n_alias=len(carry), slot=layer // 2)
    return pl.pallas_call(
        body,
        grid=(n // TM,),
        in_specs=in_specs,
        out_specs=[o[0] for o in outs],
        out_shape=[o[1] for o in outs],
        scratch_shapes=scratch,
        input_output_aliases=aliases,
        compiler_params=_params("arbitrary"),
        name=f"in_proj_{'s' if sample else 'p'}{layer}",
    )(*args)


def _softmax_t(segs, extra=None):
    m = None
    for s in segs:
        mi = jnp.max(s, axis=0, keepdims=True)
        m = mi if m is None else jnp.maximum(m, mi)
    if extra is not None:
        extra = extra * LOG2E
        m = jnp.maximum(m, extra)
    es = [jnp.exp2(s - m) for s in segs]
    den = None
    for e in es:
        di = jnp.sum(e, axis=0, keepdims=True)
        den = di if den is None else den + di
    if extra is not None:
        den = den + jnp.exp2(extra - m)
    return es, den


def _keep_rows(xt, lo, hi):
    zeros = lambda r: jnp.zeros((r, xt.shape[1]), xt.dtype)
    parts = []
    if lo > 0:
        parts.append(zeros(lo))
    parts.append(xt[lo:hi])
    if hi < xt.shape[0]:
        parts.append(zeros(xt.shape[0] - hi))
    return jnp.concatenate(parts, axis=0)


def _pipelined(n, scores, finish, ahead):
    ready = [scores(j) for j in range(min(ahead, n))]
    for j in range(n):
        if j + ahead < n:
            ready.append(scores(j + ahead))
        finish(j, ready.pop(0))


def _diff_attn(qt_ref, kvs, z_ref, lam_ref, g_ref, o_ref, *, lam_init):
    lv = lam_ref[...]
    lam = (jnp.exp(jnp.sum(lv[0:1] * lv[1:2], axis=-1, keepdims=True))
           - jnp.exp(jnp.sum(lv[2:3] * lv[3:4], axis=-1, keepdims=True)) + lam_init)

    def scores(h):
        cs = slice(h * LANES, (h + 1) * LANES)
        qt = qt_ref[cs, :]
        ks = [get_k(cs) for get_k, _ in kvs]
        return [[_dot(kk, _keep_rows(qt, c * DH, (c + 1) * DH)) for kk in ks] for c in range(2)]

    def finish(h, ss):
        cs = slice(h * LANES, (h + 1) * LANES)
        es0, den0 = _softmax_t(ss[0])
        es1, den1 = _softmax_t(ss[1])
        r0 = 1.0 / den0
        r1 = lam / den1
        ot = None
        for e0, e1, (_, get_vt) in zip(es0, es1, kvs):
            w = e0 * r0 - e1 * r1
            oi = _dot(get_vt(cs), w.astype(BF16))
            ot = oi if ot is None else ot + oi
        ms = jnp.mean(ot * ot, axis=0, keepdims=True)
        o = (ot * lax.rsqrt(ms + NORM_EPS)).T
        o = (o * g_ref[...]) * (1.0 - lam_init)
        o_ref[:, cs] = (o * _silu(z_ref[:, cs])).astype(o_ref.dtype)

    _pipelined(H_B, scores, finish, DIFF_AHEAD[len(kvs) - 1])


def _post_residual(o, x_ref, g_ref, mod_ref, o_ref):
    ms = jnp.mean(o * o, axis=-1, keepdims=True)
    r = o * lax.rsqrt(ms + NORM_EPS) * g_ref[...]
    gate = mod_ref[...][:, 2 * D_MODEL:]
    o_ref[...] = x_ref[...] + gate * r


def _diff_prompt_kernel(qt_ref, k_ref, vt_ref, z_ref, lam_ref, sg_ref, ma_ref, x_ref, w_ref, g_ref,
                        mod_ref, o_ref, mb_ref, *, lam_init):
    oa = _dot(ma_ref[...], w_ref[0:A_W, :])
    for s in range(qt_ref.shape[0]):
        rs = pl.ds(s * SEQ, SEQ)
        kv = (lambda cs, s=s: k_ref[s * SEQ:(s + 1) * SEQ, cs], lambda cs, s=s: vt_ref[s, cs, :])
        _diff_attn(qt_ref.at[s], [kv], z_ref.at[rs], lam_ref, sg_ref, mb_ref.at[rs], lam_init=lam_init)
    _post_residual(oa + _dot(mb_ref[...], w_ref[A_W:, :]), x_ref, g_ref, mod_ref, o_ref)


def _diff_sample_kernel(qt_ref, k_ref, vt_ref, ck_ref, cv_ref, z_ref, lam_ref, sg_ref, ma_ref, x_ref,
                        w_ref, g_ref, mod_ref, o_ref, ckb_ref, cvt_ref, mb_ref, *, lam_init):
    @pl.when(pl.program_id(1) == 0)
    def _():
        ckb_ref[...] = ck_ref[...].T.astype(BF16)
        for h in range(H_B):
            cs = slice(h * LANES, (h + 1) * LANES)
            cvt_ref[cs, :] = cv_ref[pl.ds(h, PAST_LEN, stride=H_B), :].T.astype(BF16)

    ctx = (lambda cs: ckb_ref[:, cs], lambda cs: cvt_ref[cs, :])
    loc = (lambda cs: k_ref[:, cs],
           lambda cs: jnp.concatenate([vt_ref[c, cs, :] for c in range(vt_ref.shape[0])], axis=1))
    oa = _dot(ma_ref[...], w_ref[0:A_W, :])
    _diff_attn(qt_ref, [ctx, loc], z_ref, lam_ref, sg_ref, mb_ref, lam_init=lam_init)
    _post_residual(oa + _dot(mb_ref[...], w_ref[A_W:, :]), x_ref, g_ref, mod_ref, o_ref)


def _out_specs_sample(layer, w_out, g_post):
    nq = DEC_SEQ // TQ
    return [pl.BlockSpec((TQ, D_MODEL), lambda b, t: (b * nq + t, 0)),
            _pick(w_out, layer // 2),
            _pick(g_post, layer),
            pl.BlockSpec((None, None, 1, 3 * D_MODEL), lambda b, t: (layer, 1 + b, 0, 0))]


def _diff_layer_sample(qt, k, vt, z, ma, x, w_out, g_post, mod4, layer, cache_k, cache_v, lam_vec,
                       subln_g, lam_init):
    n = x.shape[0]
    nq = DEC_SEQ // TQ
    qrow = lambda b, t: (b * nq + t, 0)
    cache = lambda b, t: (b, layer // 2, 0, 0)
    fixed = lambda b, t: (0, 0)
    return pl.pallas_call(
        functools.partial(_diff_sample_kernel, lam_init=lam_init),
        grid=(DEC_BATCH, nq),
        in_specs=[
            pl.BlockSpec((None, B_W, TQ), lambda b, t: (b * nq + t, 0, 0)),
            pl.BlockSpec((DEC_SEQ, B_W), lambda b, t: (b, 0)),
            pl.BlockSpec((nq, B_W, TQ), lambda b, t: (b, 0, 0)),
            pl.BlockSpec((None, None, B_W, PAST_LEN), cache),
            pl.BlockSpec((None, None, PAST_LEN * H_B, LANES), cache),
            pl.BlockSpec((TQ, B_W), qrow),
            _pick(lam_vec, layer // 2),
            _pick(subln_g, layer // 2),
            pl.BlockSpec((TQ, A_W), qrow),
        ] + _out_specs_sample(layer, w_out, g_post),
        out_specs=pl.BlockSpec((TQ, D_MODEL), qrow),
        out_shape=jax.ShapeDtypeStruct((n, D_MODEL), F32),
        scratch_shapes=[pltpu.VMEM((PAST_LEN, B_W), BF16), pltpu.VMEM((B_W, PAST_LEN), BF16),
                        pltpu.VMEM((TQ, B_W), BF16)],
        compiler_params=_params("arbitrary", "arbitrary"),
        name=f"diff_layer_s{layer}",
    )(qt, k, vt, cache_k, cache_v, z, lam_vec, subln_g, ma, x, w_out, g_post, mod4)


def _gqa(qt_ref, segs, z_ref, o_ref, sink_ref=None):
    halves = []

    def scores(j):
        n = j // 4
        qj = qt_ref[j * DH:(j + 1) * DH, :]
        zero = jnp.zeros_like(qj)
        qz = jnp.concatenate([qj, zero] if n == 0 else [zero, qj], axis=0)
        return [_dot(k, qz) for k, _, _ in segs]

    def finish(j, ss):
        n = j // 4
        ss = [s if valid is None else jnp.where(valid, s, -jnp.inf)
              for s, (_, _, valid) in zip(ss, segs)]
        extra = None if sink_ref is None else sink_ref[:, j:j + 1]
        es, den = _softmax_t(ss, extra)
        ot = None
        for e, (_, vt, _) in zip(es, segs):
            oi = _dot(vt[n * DH:(n + 1) * DH], e.astype(BF16))
            ot = oi if ot is None else ot + oi
        halves.append(ot * (1.0 / den))
        if j % 2 == 1:
            cs = slice((j // 2) * LANES, (j // 2 + 1) * LANES)
            o_pair = jnp.concatenate(halves[-2:], axis=0).T
            o_ref[:, cs] = (o_pair * _silu(z_ref[:, cs])).astype(o_ref.dtype)

    _pipelined(2 * 4, scores, finish, GQA_AHEAD[len(segs) - 1])


def _gqa_prompt_kernel(cqt_ref, ck_ref, cvt_ref, cz_ref, dqt_ref, dk_ref, dvt_ref, dz_ref, sink_ref,
                       x_ref, w_ref, g_ref, mod_ref, o_ref, m_ref):
    for s in range(cqt_ref.shape[0]):
        rows = slice(s * SEQ, (s + 1) * SEQ)
        rs = pl.ds(s * SEQ, SEQ)
        seg = lambda k_ref, vt_ref: (k_ref[rows, :], vt_ref[s].astype(BF16), None)
        _gqa(cqt_ref.at[s], [seg(ck_ref, cvt_ref)], cz_ref.at[rs], m_ref.at[rs, pl.ds(0, C_W)])
        _gqa(dqt_ref.at[s], [seg(dk_ref, dvt_ref)], dz_ref.at[rs], m_ref.at[rs, pl.ds(C_W, D_W)], sink_ref)
    _post_residual(_dot(m_ref[...], w_ref[...]), x_ref, g_ref, mod_ref, o_ref)


def _gqa_sample_kernel(cqt_ref, ck_ref, cvt_ref, cck_ref, ccv_ref, cz_ref,
                       dqt_ref, dk_ref, dvt_ref, cdk_ref, cdv_ref, dz_ref, sink_ref,
                       x_ref, w_ref, g_ref, mod_ref, o_ref, m_ref):
    oc_ref = m_ref.at[:, pl.ds(0, C_W)]
    od_ref = m_ref.at[:, pl.ds(C_W, D_W)]
    t = pl.program_id(1)
    ctx = lambda kt_ref, vt_ref: (kt_ref[...].T.astype(BF16), vt_ref[...].astype(BF16), None)
    n_chunks = DEC_SEQ // LANES
    cvt = jnp.concatenate([cvt_ref[c] for c in range(n_chunks)], axis=1)
    _gqa(cqt_ref, [ctx(cck_ref, ccv_ref), (ck_ref[...], cvt, None)], cz_ref, oc_ref)
    span = 2 * TQ
    t0 = t * TQ
    ws = pl.multiple_of(jnp.clip(t0 - WINDOW, 0, DEC_SEQ - span), WINDOW)
    kpos = ws + lax.broadcasted_iota(jnp.int32, (span, TQ), 0)
    qpos = t0 + lax.broadcasted_iota(jnp.int32, (span, TQ), 1)
    valid = jnp.abs(qpos - kpos) <= WINDOW
    c0 = ws // LANES
    dvt = jnp.concatenate([dvt_ref[c0 + c] for c in range(span // LANES)], axis=1)
    _gqa(dqt_ref, [ctx(cdk_ref, cdv_ref), (dk_ref[pl.ds(ws, span), :], dvt, valid)],
         dz_ref, od_ref, sink_ref)
    _post_residual(_dot(m_ref[...], w_ref[...]), x_ref, g_ref, mod_ref, o_ref)


def _gqa_layer_sample(cqt, ck, cvt, cz, dqt, dk, dvt, dz, sink, cck, ccv, cdk, cdv, x, w_out, g_post,
                      mod4, layer):
    n = x.shape[0]
    nq = DEC_SEQ // TQ
    qrow = lambda b, t: (b * nq + t, 0)
    cache = lambda b, t: (b, layer // 2, 0, 0)
    qt = pl.BlockSpec((None, C_W, TQ), lambda b, t: (b * nq + t, 0, 0))
    wide = pl.BlockSpec((TQ, C_W), qrow)
    kv = pl.BlockSpec((DEC_SEQ, KV_W), lambda b, t: (b, 0))
    vt = pl.BlockSpec((DEC_SEQ // LANES, KV_W, LANES), lambda b, t: (b, 0, 0))
    cb = pl.BlockSpec((None, None, KV_W, PAST_LEN), cache)
    return pl.pallas_call(
        _gqa_sample_kernel,
        grid=(DEC_BATCH, nq),
        in_specs=[qt, kv, vt, cb, cb, wide, qt, kv, vt, cb, cb, wide,
                  _pick(sink, layer // 2)] + _out_specs_sample(layer, w_out, g_post),
        out_specs=pl.BlockSpec((TQ, D_MODEL), qrow),
        out_shape=jax.ShapeDtypeStruct((n, D_MODEL), F32),
        scratch_shapes=[pltpu.VMEM((TQ, C_W + D_W), BF16)],
        compiler_params=_params("arbitrary", "arbitrary"),
        name=f"gqa_layer_s{layer}",
    )(cqt, ck, cvt, cck, ccv, cz, dqt, dk, dvt, cdk, cdv, dz, sink, x, w_out, g_post, mod4)


def _prompt_even_kernel(x_ref, g_ref, mod_ref, w_ref, cw_ref, cb_ref, lg_ref, lb_ref, lam_ref, sg_ref,
                        wo_ref, gp_ref, *rest, lam_init, n_alias, slot):
    (o_ref, kt_ref, v_ref,
     ma_ref, qt_ref, k_ref, vt_ref, bz_ref, sh_ref, acc_ref, mb_ref) = rest[n_alias:]
    _in_even_kernel(x_ref, g_ref, mod_ref, w_ref, cw_ref, cb_ref, lg_ref, lb_ref,
                    ma_ref, qt_ref, k_ref, kt_ref, v_ref, vt_ref, bz_ref, sh_ref, acc_ref,
                    sample=False, slot=slot)
    _diff_prompt_kernel(qt_ref, k_ref, vt_ref, bz_ref, lam_ref, sg_ref, ma_ref, x_ref, wo_ref, gp_ref,
                        mod_ref, o_ref, mb_ref, lam_init=lam_init)


def _prompt_odd_kernel(x_ref, g_ref, mod_ref, w_ref, qn_ref, kn_ref, sink_ref, wo_ref, gp_ref, *rest,
                       n_alias, slot):
    (o_ref, ckt_ref, cvt_ref, dkt_ref, dvt_ref,
     cqt_ref, ck_ref, cz_ref, dqt_ref, dk_ref, dz_ref, m_ref) = rest[n_alias:]
    _in_odd_kernel(x_ref, g_ref, mod_ref, w_ref, qn_ref, kn_ref,
                   cqt_ref, ck_ref, ckt_ref, cvt_ref, cz_ref, dqt_ref, dk_ref, dkt_ref, dvt_ref, dz_ref,
                   sample=False, slot=slot)
    _gqa_prompt_kernel(cqt_ref, ck_ref, _own_slot(cvt_ref, slot, fill=False), cz_ref,
                       dqt_ref, dk_ref, _own_slot(dvt_ref, slot, fill=False), dz_ref, sink_ref,
                       x_ref, wo_ref, gp_ref, mod_ref, o_ref, m_ref)


def _prompt_layer(x, layer, g_pre, g_post, mod4, w_in, w_out, head, tail, carry, lam_init=None):
    n = x.shape[0]
    even = layer % 2 == 0
    row = lambda i: (i, 0)
    n_seq = TM // SEQ
    params = (g_pre, w_in) + tuple(head) + tuple(tail) + (w_out, g_post)
    idx = (layer, layer // 2) + (layer // 2,) * (len(head) + len(tail)) + (layer // 2, layer)
    specs = [_pick(p, i) for p, i in zip(params, idx)]
    in_specs = [pl.BlockSpec((TM, D_MODEL), row), specs[0],
                pl.BlockSpec((None, None, 1, 3 * D_MODEL), lambda i: (layer, 0, 0, 0))] + specs[1:]
    args = [x, g_pre, mod4, w_in] + list(head) + list(tail) + [w_out, g_post]
    n_layers = (DEPTH + 1 - layer % 2) // 2
    if carry:
        cache_spec = lambda r, c: pl.BlockSpec((n_seq, None, r, c), lambda i: (i, layer // 2, 0, 0))
    else:
        cache_spec = lambda r, c: pl.BlockSpec((n_seq, n_layers, r, c), lambda i: (i, 0, 0, 0))
    cache = lambda r, c: (cache_spec(r, c), jax.ShapeDtypeStruct((n // SEQ, n_layers, r, c), F32))
    outs = [(pl.BlockSpec((TM, D_MODEL), row), jax.ShapeDtypeStruct((n, D_MODEL), F32))]
    wide = pltpu.VMEM((TM, B_W), BF16)
    slab = pltpu.VMEM((n_seq, B_W, SEQ), BF16)
    if even:
        outs += [cache(B_W, SEQ), cache(SEQ * H_B, LANES)]
        scratch = [wide, slab, wide, slab, pltpu.VMEM((TM, B_W), F32),
                   pltpu.VMEM((SUBLANES, SEQ + 3 * SUBLANES, A_W), F32), pltpu.VMEM((SEQ, A_W), F32), wide]
        body = functools.partial(_prompt_even_kernel, lam_init=lam_init)
    else:
        outs += [cache(KV_W, SEQ)] * 4
        narrow = pltpu.VMEM((TM, KV_W), BF16)
        gate = pltpu.VMEM((TM, C_W), F32)
        scratch = [slab, narrow, gate, slab, narrow, gate, pltpu.VMEM((TM, C_W + D_W), BF16)]
        body = _prompt_odd_kernel
    aliases = {}
    for j, a in enumerate(carry):
        aliases[len(args)] = 1 + j
        in_specs.append(pl.BlockSpec(memory_space=pl.ANY))
        args.append(a)
    return pl.pallas_call(
        functools.partial(body, n_alias=len(carry), slot=layer // 2),
        grid=(n // TM,),
        in_specs=in_specs,
        out_specs=[o[0] for o in outs],
        out_shape=[o[1] for o in outs],
        scratch_shapes=scratch,
        input_output_aliases=aliases,
        compiler_params=_params("arbitrary"),
        name=f"prompt_layer{layer}",
    )(*args)


def _rope_tables():
    nf = DH // 4
    t = jnp.arange(DEC_SEQ)
    row = (t // GRID_W).astype(F32)
    col = (t % GRID_W).astype(F32)
    inv = ROPE_THETA ** (-jnp.arange(nf, dtype=F32) / nf)
    d = jnp.arange(DH)
    axis = d // (2 * nf)
    second = (d % (2 * nf)) // nf
    f = d % nf
    pos = jnp.where(axis[None, :] == 0, row[:, None], col[:, None])
    ang = pos * inv[f][None, :]
    cos = jnp.cos(ang)
    sin = jnp.where(second[None, :] == 0, -jnp.sin(ang), jnp.sin(ang))
    reps = B_W // DH
    cos = jnp.tile(cos, (1, reps))
    sin = jnp.tile(sin, (1, reps))
    return cos, sin, cos.T, sin.T


def kernel(x_prompt, x_sample, cache_b_k, cache_b_v, cache_c_k, cache_c_v, cache_d_k, cache_d_v, c, c_ctx, norm_pre, norm_post, w_mod, b_mod, w_in_even, a_conv_w, a_conv_b, a_ln_g, a_ln_b, b_lambda, b_subln_g, w_out_even, w_in_odd, c_q_norm, c_k_norm, d_sink, w_out_odd):
    n_even = (DEPTH + 1) // 2
    n_odd = DEPTH // 2
    cond8 = jnp.zeros((SUBLANES, D_MODEL), F32).at[0].set(c_ctx).at[1:1 + DEC_BATCH].set(c)
    mod4 = _modulation(cond8, w_mod, b_mod).reshape(DEPTH, SUBLANES, 1, 3 * D_MODEL)
    tables = _rope_tables()

    xp = x_prompt.reshape(BATCH * SEQ, D_MODEL)
    xs = x_sample.reshape(DEC_BATCH * DEC_SEQ, D_MODEL)
    feat = lambda a, w: jnp.moveaxis(a.reshape(a.shape[:3] + (w,)), 2, 3)
    cbk = feat(cache_b_k, B_W)
    cbv = cache_b_v.reshape(DEC_BATCH, n_even, PAST_LEN * H_B, 2 * DH)
    cck = feat(cache_c_k, KV_W)
    ccv = feat(cache_c_v, KV_W)
    cdk = feat(cache_d_k, KV_W)
    cdv = feat(cache_d_v, KV_W)

    g_pre = norm_pre.reshape(DEPTH, 1, D_MODEL)
    g_post = norm_post.reshape(DEPTH, 1, D_MODEL)
    w_in_e = w_in_even.astype(BF16)
    w_out_e = w_out_even.astype(BF16)
    w_in_o = w_in_odd.astype(BF16)
    w_out_o = w_out_odd.astype(BF16)
    conv = (jnp.zeros((n_even, 4 * SUBLANES, A_W), F32).at[:, :CONV_K].set(a_conv_w),
            a_conv_b.reshape(n_even, 1, A_W), a_ln_g.reshape(n_even, 1, A_W), a_ln_b.reshape(n_even, 1, A_W))
    subln = b_subln_g.reshape(n_even, 1, 2 * DH)
    qkn = (jnp.tile(c_q_norm, (1, C_W // DH)).reshape(n_odd, C_W, 1),
           jnp.tile(c_k_norm, (1, KV_W // DH)).reshape(n_odd, 1, KV_W))
    sink = d_sink.reshape(n_odd, 1, 8)

    new_even, new_odd = (), ()
    for l in range(DEPTH):
        if l % 2 == 0:
            lam_init = 0.8 - 0.6 * math.exp(-0.3 * l)
            w_in, w_out = w_in_e, w_out_e
            xp, *new_even = _prompt_layer(xp, l, g_pre, g_post, mod4, w_in, w_out, conv,
                                          (b_lambda, subln), new_even, lam_init)
            ma, qt, k, vt, bz = _in_proj(xs, g_pre, mod4, l, w_in, sample=True, extra=conv, tables=tables)
            xs = _diff_layer_sample(qt, k, vt, bz, ma, xs, w_out, g_post, mod4, l, cbk, cbv,
                                    b_lambda, subln, lam_init)
        else:
            w_in, w_out = w_in_o, w_out_o
            xp, *new_odd = _prompt_layer(xp, l, g_pre, g_post, mod4, w_in, w_out, qkn, (sink,), new_odd)
            cqt, ck, cvt, cz, dqt, dk, dvt, dz = _in_proj(xs, g_pre, mod4, l, w_in, sample=True,
                                                          extra=qkn, tables=tables)
            xs = _gqa_layer_sample(cqt, ck, cvt, cz, dqt, dk, dvt, dz, sink, cck, ccv, cdk, cdv,
                                   xs, w_out, g_post, mod4, l)

    def token_major(a, heads):
        return jnp.moveaxis(a.reshape(a.shape[:2] + heads + (DH, SEQ)), -1, 2)

    kt, v = new_even
    ckt, cvt, dkt, dvt = new_odd
    return (xp.reshape(BATCH, SEQ, D_MODEL), xs.reshape(DEC_BATCH, DEC_SEQ, D_MODEL),
            token_major(kt, (H_B, 2)), v.reshape(BATCH, n_even, SEQ, H_B, 2 * DH),
            token_major(ckt, (2,)), token_major(cvt, (2,)), token_major(dkt, (2,)), token_major(dvt, (2,)))
```

```python
import functools
import math

import jax
import jax.numpy as jnp
from jax import lax
from jax.experimental import pallas as pl
from jax.experimental.pallas import tpu as pltpu

F32 = jnp.float32
BF16 = jnp.bfloat16

D_MODEL = 1024
BATCH = 16
SEQ = 256
DEPTH = 4
DEC_BATCH = 2
DEC_SEQ = 1024
PAST_LEN = 512
GRID_W = 64
ROPE_THETA = 10000.0
NORM_EPS = 1e-6
DH = 64
A_W = 512
CONV_K = 31
H_B = 4
B_W = 512
C_W = 512
KV_W = 128
D_W = 512
WINDOW = 128
LOG2E = math.log2(math.e)
QK_SCALE = DH ** -0.5 * LOG2E

LANES = 128
SUBLANES = 8
VMEM_LIMIT = 56 * 1024 * 1024

TM = 512
TQ = SEQ
HALO = 16
GQA_AHEAD = (8, 4)
DIFF_AHEAD = (3, 2)
ROW_CHUNK = 64


def _params(*sem):
    return pltpu.CompilerParams(dimension_semantics=sem, vmem_limit_bytes=VMEM_LIMIT)


def _silu(x):
    return x * jax.nn.sigmoid(x)


def _dot(a, b):
    return jnp.dot(a, b, preferred_element_type=F32)


def _pick(stacked, idx):
    return pl.BlockSpec((None,) + stacked.shape[1:], lambda *_: (idx,) + (0,) * (stacked.ndim - 1))


def _mod_kernel(cond_ref, w_ref, b_ref, o_ref):
    a = _silu(cond_ref[...]).astype(BF16)
    o_ref[...] = _dot(a, w_ref[...].astype(BF16)) + b_ref[...]


def _modulation(cond8, w_mod, b_mod):
    nblk = 3
    return pl.pallas_call(
        _mod_kernel,
        grid=(DEPTH, nblk),
        in_specs=[
            pl.BlockSpec((SUBLANES, D_MODEL), lambda l, j: (0, 0)),
            pl.BlockSpec((None, D_MODEL, D_MODEL), lambda l, j: (l, 0, j)),
            pl.BlockSpec((None, 1, D_MODEL), lambda l, j: (l, 0, j)),
        ],
        out_specs=pl.BlockSpec((None, SUBLANES, D_MODEL), lambda l, j: (l, 0, j)),
        out_shape=jax.ShapeDtypeStruct((DEPTH, SUBLANES, 3 * D_MODEL), F32),
        compiler_params=_params("arbitrary", "arbitrary"),
        name="modulation",
    )(cond8, w_mod, b_mod.reshape(DEPTH, 1, 3 * D_MODEL))


def _pre_norm(x_ref, g_ref, mod_ref):
    return _modulate(x_ref[...], g_ref, mod_ref)


def _modulate(x, g_ref, mod_ref):
    ms = jnp.mean(x * x, axis=-1, keepdims=True)
    mod = mod_ref[...]
    sh = mod[:, :D_MODEL]
    sc = mod[:, D_MODEL:2 * D_MODEL]
    h = (x * lax.rsqrt(ms + NORM_EPS) * g_ref[...]) * (1.0 + sc) + sh
    return h.astype(BF16)


def _rope(x, cos, sin_signed):
    w = x.shape[-1]
    lane = lax.broadcasted_iota(jnp.int32, (1, w), 1)
    first = (lane % 32) < 16
    partner = jnp.where(first, pltpu.roll(x, w - 16, 1), pltpu.roll(x, 16, 1))
    return x * cos + partner * sin_signed


def _rope_t(x, cos_t, sin_t):
    r = x.shape[0]
    row = lax.broadcasted_iota(jnp.int32, (r, 1), 0)
    first = (row % 32) < 16
    partner = jnp.where(first, pltpu.roll(x, r - 16, 0), pltpu.roll(x, 16, 0))
    return x * cos_t + partner * sin_t


def _store_chunks(ref, xt):
    for c in range(xt.shape[1] // LANES):
        ref[c] = xt[:, c * LANES:(c + 1) * LANES]


def _store_per_seq(ref, xt):
    for s in range(xt.shape[1] // SEQ):
        ref[s] = xt[:, s * SEQ:(s + 1) * SEQ]


def _own_slot(ref, slot, fill=True):
    if len(ref.shape) == 3:
        return ref
    for other in range(ref.shape[1]):
        if fill and other != slot:
            ref[:, other] = jnp.zeros((ref.shape[0],) + tuple(ref.shape[2:]), ref.dtype)
    return ref.at[:, slot]


def _glu(ug):
    return ug[:, :A_W] * jax.nn.sigmoid(ug[:, A_W:])


def _conv_mix(pad, az, cw_ref, cb_ref, lg_ref, lb_ref, sh_ref, acc_ref):
    rows = sh_ref.shape[1]
    for b in range(SUBLANES):
        sh_ref[b] = pad[b:b + rows]
    base = HALO - CONV_K // 2
    for c0 in range(0, A_W, LANES):
        cs = slice(c0, c0 + LANES)
        for r0 in range(0, SEQ, ROW_CHUNK):
            acc = jnp.zeros((ROW_CHUNK, LANES), F32) + cb_ref[:, cs]
            for k in range(CONV_K):
                j = k + base
                s = r0 + (j // SUBLANES) * SUBLANES
                acc = acc + sh_ref[j % SUBLANES, s:s + ROW_CHUNK, cs] * cw_ref[k:k + 1, cs]
            acc_ref[r0:r0 + ROW_CHUNK, cs] = acc
    a = acc_ref[...]
    mu = jnp.mean(a, axis=-1, keepdims=True)
    d = a - mu
    var = jnp.mean(d * d, axis=-1, keepdims=True)
    y = d * lax.rsqrt(var + NORM_EPS) * lg_ref[...] + lb_ref[...]
    return _silu(y) * _silu(az)


def _in_even_kernel(x_ref, g_ref, mod_ref, w_ref, cw_ref, cb_ref, lg_ref, lb_ref, *rest,
                    sample, n_alias=0, slot=0, pos=None):
    if sample:
        (xp_ref, xn_ref, cos_ref, sin_ref, cost_ref, sint_ref,
         ma_ref, qt_ref, k_ref, vt_ref, bz_ref, sh_ref, acc_ref) = rest
    else:
        ma_ref, qt_ref, k_ref, kt_ref, v_ref, vt_ref, bz_ref, sh_ref, acc_ref = rest[n_alias:]
        kt_ref = _own_slot(kt_ref, slot)
        v_ref = _own_slot(v_ref, slot)
    hb = _pre_norm(x_ref, g_ref, mod_ref)
    a = _glu(_dot(hb, w_ref[:, 0:2 * A_W]))
    az = _dot(hb, w_ref[:, 2 * A_W:3 * A_W])
    n_sub = TM // SEQ
    if sample:
        tiles_per_seq = DEC_SEQ // TM
        xh = jnp.concatenate([xp_ref[...], xn_ref[...]], axis=0)
        ah = _glu(_dot(_modulate(xh, g_ref, mod_ref), w_ref[:, 0:2 * A_W]))
        prev = jnp.where(pos != 0, ah[:HALO], 0.0)
        nxt = jnp.where(pos != tiles_per_seq - 1, ah[HALO:], 0.0)
        full = jnp.concatenate([prev, a, nxt], axis=0)
        pads = [full[j * SEQ:(j + 1) * SEQ + 2 * HALO] for j in range(n_sub)]
    else:
        zeros = jnp.zeros((HALO, A_W), F32)
        pads = [jnp.concatenate([zeros, a[j * SEQ:(j + 1) * SEQ], zeros], axis=0) for j in range(n_sub)]
    o = 3 * A_W
    q = _dot(hb, w_ref[:, o:o + B_W])
    k = _dot(hb, w_ref[:, o + B_W:o + 2 * B_W])
    v = _dot(hb, w_ref[:, o + 2 * B_W:o + 3 * B_W])
    bz_ref[...] = _dot(hb, w_ref[:, o + 3 * B_W:o + 4 * B_W])
    qt = q.T
    if sample:
        qt = _rope_t(qt, cost_ref[...], sint_ref[...])
        k = _rope(k, cos_ref[...], sin_ref[...])
    else:
        _store_per_seq(kt_ref, k.T)
        for s in range(TM // SEQ):
            for h in range(H_B):
                v_ref[s, pl.ds(h, SEQ, stride=H_B), :] = v[s * SEQ:(s + 1) * SEQ, h * LANES:(h + 1) * LANES]
    _store_per_seq(qt_ref, (qt * QK_SCALE).astype(BF16))
    k_ref[...] = k.astype(BF16)
    _store_per_seq(vt_ref, v.T.astype(BF16))
    for j, pad in enumerate(pads):
        rs = slice(j * SEQ, (j + 1) * SEQ)
        ma_ref[rs, :] = _conv_mix(pad, az[rs], cw_ref, cb_ref, lg_ref, lb_ref, sh_ref, acc_ref).astype(BF16)


def _group_mean_sq(x):
    width = x.shape[-1]
    xx = x * x
    hi = xx.astype(BF16)
    lo = (xx - hi.astype(F32)).astype(BF16)
    r = lax.broadcasted_iota(jnp.int32, (width, width), 0) // DH
    c = lax.broadcasted_iota(jnp.int32, (width, width), 1) // DH
    g = jnp.where(r == c, 1.0, 0.0).astype(BF16)
    return (_dot(hi, g) + _dot(lo, g)) * (1.0 / DH)


def _head_rms_t(xt, gain_col):
    parts = []
    for j in range(xt.shape[0] // DH):
        blk = xt[j * DH:(j + 1) * DH]
        ms = jnp.mean(blk * blk, axis=0, keepdims=True)
        parts.append(blk * lax.rsqrt(ms + NORM_EPS))
    return jnp.concatenate(parts, axis=0) * gain_col


def _in_odd_kernel(x_ref, g_ref, mod_ref, w_ref, qn_ref, kn_ref, *rest, sample, n_alias=0, slot=0):
    if sample:
        (cos_ref, sin_ref, cost_ref, sint_ref,
         cqt_ref, ck_ref, cvt_ref, cz_ref, dqt_ref, dk_ref, dvt_ref, dz_ref) = rest
    else:
        (cqt_ref, ck_ref, ckt_ref, cvt_ref, cz_ref,
         dqt_ref, dk_ref, dkt_ref, dvt_ref, dz_ref) = rest[n_alias:]
        ckt_ref, cvt_ref, dkt_ref, dvt_ref = [_own_slot(r, slot) for r in (ckt_ref, cvt_ref, dkt_ref, dvt_ref)]
    hb = _pre_norm(x_ref, g_ref, mod_ref)
    y = _dot(hb, w_ref[...])
    o = 0
    cqt = _head_rms_t(y[:, o:o + C_W].T, qn_ref[...])
    o += C_W
    ck = y[:, o:o + KV_W]
    ck = ck * lax.rsqrt(_group_mean_sq(ck) + NORM_EPS) * kn_ref[...]
    o += KV_W
    cvt = y[:, o:o + KV_W].T
    o += KV_W
    cz_ref[...] = y[:, o:o + C_W]
    o += C_W
    dqt = y[:, o:o + D_W].T
    o += D_W
    dk = y[:, o:o + KV_W]
    o += KV_W
    dvt = y[:, o:o + KV_W].T
    o += KV_W
    dz_ref[...] = y[:, o:o + D_W]
    if sample:
        cos_t = cost_ref[...]
        sin_t = sint_ref[...]
        cqt = _rope_t(cqt, cos_t, sin_t)
        dqt = _rope_t(dqt, cos_t, sin_t)
        cos = cos_ref[...][:, :KV_W]
        sin = sin_ref[...][:, :KV_W]
        ck = _rope(ck, cos, sin)
        dk = _rope(dk, cos, sin)
        _store_chunks(cvt_ref, cvt.astype(BF16))
        _store_chunks(dvt_ref, dvt.astype(BF16))
    else:
        _store_per_seq(ckt_ref, ck.T)
        _store_per_seq(dkt_ref, dk.T)
        _store_per_seq(cvt_ref, cvt)
        _store_per_seq(dvt_ref, dvt)
    _store_per_seq(cqt_ref, (cqt * QK_SCALE).astype(BF16))
    _store_per_seq(dqt_ref, (dqt * QK_SCALE).astype(BF16))
    ck_ref[...] = ck.astype(BF16)
    dk_ref[...] = dk.astype(BF16)


def _softmax_t(segs, extra=None):
    m = None
    for s in segs:
        mi = jnp.max(s, axis=0, keepdims=True)
        m = mi if m is None else jnp.maximum(m, mi)
    if extra is not None:
        extra = extra * LOG2E
        m = jnp.maximum(m, extra)
    es = [jnp.exp2(s - m) for s in segs]
    den = None
    for e in es:
        di = jnp.sum(e, axis=0, keepdims=True)
        den = di if den is None else den + di
    if extra is not None:
        den = den + jnp.exp2(extra - m)
    return es, den


def _keep_rows(xt, lo, hi):
    zeros = lambda r: jnp.zeros((r, xt.shape[1]), xt.dtype)
    parts = []
    if lo > 0:
        parts.append(zeros(lo))
    parts.append(xt[lo:hi])
    if hi < xt.shape[0]:
        parts.append(zeros(xt.shape[0] - hi))
    return jnp.concatenate(parts, axis=0)


def _pipelined(n, scores, finish, ahead):
    ready = [scores(j) for j in range(min(ahead, n))]
    for j in range(n):
        if j + ahead < n:
            ready.append(scores(j + ahead))
        finish(j, ready.pop(0))


def _diff_attn(qt_ref, kvs, z_ref, lam_ref, g_ref, o_ref, *, lam_init):
    lv = lam_ref[...]
    lam = (jnp.exp(jnp.sum(lv[0:1] * lv[1:2], axis=-1, keepdims=True))
           - jnp.exp(jnp.sum(lv[2:3] * lv[3:4], axis=-1, keepdims=True)) + lam_init)

    def scores(h):
        cs = slice(h * LANES, (h + 1) * LANES)
        qt = qt_ref[cs, :]
        ks = [get_k(cs) for get_k, _ in kvs]
        return [[_dot(kk, _keep_rows(qt, c * DH, (c + 1) * DH)) for kk in ks] for c in range(2)]

    def finish(h, ss):
        cs = slice(h * LANES, (h + 1) * LANES)
        es0, den0 = _softmax_t(ss[0])
        es1, den1 = _softmax_t(ss[1])
        r0 = 1.0 / den0
        r1 = lam / den1
        ot = None
        for e0, e1, (_, get_vt) in zip(es0, es1, kvs):
            w = e0 * r0 - e1 * r1
            oi = _dot(get_vt(cs), w.astype(BF16))
            ot = oi if ot is None else ot + oi
        ms = jnp.mean(ot * ot, axis=0, keepdims=True)
        o = (ot * lax.rsqrt(ms + NORM_EPS)).T
        o = (o * g_ref[...]) * (1.0 - lam_init)
        o_ref[:, cs] = (o * _silu(z_ref[:, cs])).astype(o_ref.dtype)

    _pipelined(H_B, scores, finish, DIFF_AHEAD[len(kvs) - 1])


def _post_residual(o, x_ref, g_ref, mod_ref, o_ref):
    ms = jnp.mean(o * o, axis=-1, keepdims=True)
    r = o * lax.rsqrt(ms + NORM_EPS) * g_ref[...]
    gate = mod_ref[...][:, 2 * D_MODEL:]
    o_ref[...] = x_ref[...] + gate * r


def _diff_prompt_kernel(qt_ref, k_ref, vt_ref, z_ref, lam_ref, sg_ref, ma_ref, x_ref, w_ref, g_ref,
                        mod_ref, o_ref, mb_ref, *, lam_init):
    oa = _dot(ma_ref[...], w_ref[0:A_W, :])
    for s in range(qt_ref.shape[0]):
        rs = pl.ds(s * SEQ, SEQ)
        kv = (lambda cs, s=s: k_ref[s * SEQ:(s + 1) * SEQ, cs], lambda cs, s=s: vt_ref[s, cs, :])
        _diff_attn(qt_ref.at[s], [kv], z_ref.at[rs], lam_ref, sg_ref, mb_ref.at[rs], lam_init=lam_init)
    _post_residual(oa + _dot(mb_ref[...], w_ref[A_W:, :]), x_ref, g_ref, mod_ref, o_ref)


def _prep_diff_cache(ck_ref, cv_ref, ckb_ref, cvt_ref):
    ckb_ref[...] = ck_ref[...].T.astype(BF16)
    for h in range(H_B):
        cs = slice(h * LANES, (h + 1) * LANES)
        cvt_ref[cs, :] = cv_ref[pl.ds(h, PAST_LEN, stride=H_B), :].T.astype(BF16)


def _diff_sample_body(qt_ref, k_ref, vt_ref, z_ref, lam_ref, sg_ref, ma_ref, x_ref, w_ref, g_ref,
                      mod_ref, o_ref, ckb_ref, cvt_ref, mb_ref, *, lam_init):
    ctx = (lambda cs: ckb_ref[:, cs], lambda cs: cvt_ref[cs, :])
    loc = (lambda cs: k_ref[:, cs],
           lambda cs: jnp.concatenate([vt_ref[c, cs, :] for c in range(vt_ref.shape[0])], axis=1))
    oa = _dot(ma_ref[...], w_ref[0:A_W, :])
    _diff_attn(qt_ref, [ctx, loc], z_ref, lam_ref, sg_ref, mb_ref, lam_init=lam_init)
    _post_residual(oa + _dot(mb_ref[...], w_ref[A_W:, :]), x_ref, g_ref, mod_ref, o_ref)


def _gqa(qt_ref, segs, z_ref, o_ref, sink_ref=None):
    halves = []

    def scores(j):
        n = j // 4
        qj = qt_ref[j * DH:(j + 1) * DH, :]
        zero = jnp.zeros_like(qj)
        qz = jnp.concatenate([qj, zero] if n == 0 else [zero, qj], axis=0)
        return [_dot(k, qz) for k, _, _ in segs]

    def finish(j, ss):
        n = j // 4
        ss = [s if valid is None else jnp.where(valid, s, -jnp.inf)
              for s, (_, _, valid) in zip(ss, segs)]
        extra = None if sink_ref is None else sink_ref[:, j:j + 1]
        es, den = _softmax_t(ss, extra)
        ot = None
        for e, (_, vt, _) in zip(es, segs):
            oi = _dot(vt[n * DH:(n + 1) * DH], e.astype(BF16))
            ot = oi if ot is None else ot + oi
        halves.append(ot * (1.0 / den))
        if j % 2 == 1:
            cs = slice((j // 2) * LANES, (j // 2 + 1) * LANES)
            o_pair = jnp.concatenate(halves[-2:], axis=0).T
            o_ref[:, cs] = (o_pair * _silu(z_ref[:, cs])).astype(o_ref.dtype)

    _pipelined(2 * 4, scores, finish, GQA_AHEAD[len(segs) - 1])


def _gqa_prompt_kernel(cqt_ref, ck_ref, cvt_ref, cz_ref, dqt_ref, dk_ref, dvt_ref, dz_ref, sink_ref,
                       x_ref, w_ref, g_ref, mod_ref, o_ref, m_ref):
    for s in range(cqt_ref.shape[0]):
        rows = slice(s * SEQ, (s + 1) * SEQ)
        rs = pl.ds(s * SEQ, SEQ)
        seg = lambda k_ref, vt_ref: (k_ref[rows, :], vt_ref[s].astype(BF16), None)
        _gqa(cqt_ref.at[s], [seg(ck_ref, cvt_ref)], cz_ref.at[rs], m_ref.at[rs, pl.ds(0, C_W)])
        _gqa(dqt_ref.at[s], [seg(dk_ref, dvt_ref)], dz_ref.at[rs], m_ref.at[rs, pl.ds(C_W, D_W)], sink_ref)
    _post_residual(_dot(m_ref[...], w_ref[...]), x_ref, g_ref, mod_ref, o_ref)


def _gqa_sample_body(cqt_ref, ck_ref, cvt_ref, cck_ref, ccv_ref, cz_ref,
                     dqt_ref, dk_ref, dvt_ref, cdk_ref, cdv_ref, dz_ref, sink_ref,
                     x_ref, w_ref, g_ref, mod_ref, o_ref, m_ref, *, t):
    oc_ref = m_ref.at[:, pl.ds(0, C_W)]
    od_ref = m_ref.at[:, pl.ds(C_W, D_W)]
    ctx = lambda kt_ref, vt_ref: (kt_ref[...].T.astype(BF16), vt_ref[...].astype(BF16), None)
    n_chunks = DEC_SEQ // LANES
    cvt = jnp.concatenate([cvt_ref[c] for c in range(n_chunks)], axis=1)
    _gqa(cqt_ref, [ctx(cck_ref, ccv_ref), (ck_ref[...], cvt, None)], cz_ref, oc_ref)
    span = 2 * TQ
    t0 = t * TQ
    ws = pl.multiple_of(jnp.clip(t0 - WINDOW, 0, DEC_SEQ - span), WINDOW)
    kpos = ws + lax.broadcasted_iota(jnp.int32, (span, TQ), 0)
    qpos = t0 + lax.broadcasted_iota(jnp.int32, (span, TQ), 1)
    valid = jnp.abs(qpos - kpos) <= WINDOW
    c0 = ws // LANES
    dvt = jnp.concatenate([dvt_ref[c0 + c] for c in range(span // LANES)], axis=1)
    _gqa(dqt_ref, [ctx(cdk_ref, cdv_ref), (dk_ref[pl.ds(ws, span), :], dvt, valid)],
         dz_ref, od_ref, sink_ref)
    _post_residual(_dot(m_ref[...], w_ref[...]), x_ref, g_ref, mod_ref, o_ref)


def _prompt_even_kernel(x_ref, g_ref, mod_ref, w_ref, cw_ref, cb_ref, lg_ref, lb_ref, lam_ref, sg_ref,
                        wo_ref, gp_ref, *rest, lam_init, n_alias, slot):
    (o_ref, kt_ref, v_ref,
     ma_ref, qt_ref, k_ref, vt_ref, bz_ref, sh_ref, acc_ref, mb_ref) = rest[n_alias:]
    _in_even_kernel(x_ref, g_ref, mod_ref, w_ref, cw_ref, cb_ref, lg_ref, lb_ref,
                    ma_ref, qt_ref, k_ref, kt_ref, v_ref, vt_ref, bz_ref, sh_ref, acc_ref,
                    sample=False, slot=slot)
    _diff_prompt_kernel(qt_ref, k_ref, vt_ref, bz_ref, lam_ref, sg_ref, ma_ref, x_ref, wo_ref, gp_ref,
                        mod_ref, o_ref, mb_ref, lam_init=lam_init)


def _prompt_odd_kernel(x_ref, g_ref, mod_ref, w_ref, qn_ref, kn_ref, sink_ref, wo_ref, gp_ref, *rest,
                       n_alias, slot):
    (o_ref, ckt_ref, cvt_ref, dkt_ref, dvt_ref,
     cqt_ref, ck_ref, cz_ref, dqt_ref, dk_ref, dz_ref, m_ref) = rest[n_alias:]
    _in_odd_kernel(x_ref, g_ref, mod_ref, w_ref, qn_ref, kn_ref,
                   cqt_ref, ck_ref, ckt_ref, cvt_ref, cz_ref, dqt_ref, dk_ref, dkt_ref, dvt_ref, dz_ref,
                   sample=False, slot=slot)
    _gqa_prompt_kernel(cqt_ref, ck_ref, _own_slot(cvt_ref, slot, fill=False), cz_ref,
                       dqt_ref, dk_ref, _own_slot(dvt_ref, slot, fill=False), dz_ref, sink_ref,
                       x_ref, wo_ref, gp_ref, mod_ref, o_ref, m_ref)


def _prompt_layer(x, layer, g_pre, g_post, mod4, w_in, w_out, head, tail, carry, lam_init=None):
    n = x.shape[0]
    even = layer % 2 == 0
    row = lambda i: (i, 0)
    n_seq = TM // SEQ
    params = (g_pre, w_in) + tuple(head) + tuple(tail) + (w_out, g_post)
    idx = (layer, layer // 2) + (layer // 2,) * (len(head) + len(tail)) + (layer // 2, layer)
    specs = [_pick(p, i) for p, i in zip(params, idx)]
    in_specs = [pl.BlockSpec((TM, D_MODEL), row), specs[0],
                pl.BlockSpec((None, None, 1, 3 * D_MODEL), lambda i: (layer, 0, 0, 0))] + specs[1:]
    args = [x, g_pre, mod4, w_in] + list(head) + list(tail) + [w_out, g_post]
    n_layers = (DEPTH + 1 - layer % 2) // 2
    if carry:
        cache_spec = lambda r, c: pl.BlockSpec((n_seq, None, r, c), lambda i: (i, layer // 2, 0, 0))
    else:
        cache_spec = lambda r, c: pl.BlockSpec((n_seq, n_layers, r, c), lambda i: (i, 0, 0, 0))
    cache = lambda r, c: (cache_spec(r, c), jax.ShapeDtypeStruct((n // SEQ, n_layers, r, c), F32))
    outs = [(pl.BlockSpec((TM, D_MODEL), row), jax.ShapeDtypeStruct((n, D_MODEL), F32))]
    wide = pltpu.VMEM((TM, B_W), BF16)
    slab = pltpu.VMEM((n_seq, B_W, SEQ), BF16)
    if even:
        outs += [cache(B_W, SEQ), cache(SEQ * H_B, LANES)]
        scratch = [wide, slab, wide, slab, pltpu.VMEM((TM, B_W), F32),
                   pltpu.VMEM((SUBLANES, SEQ + 3 * SUBLANES, A_W), F32), pltpu.VMEM((SEQ, A_W), F32), wide]
        body = functools.partial(_prompt_even_kernel, lam_init=lam_init)
    else:
        outs += [cache(KV_W, SEQ)] * 4
        narrow = pltpu.VMEM((TM, KV_W), BF16)
        gate = pltpu.VMEM((TM, C_W), F32)
        scratch = [slab, narrow, gate, slab, narrow, gate, pltpu.VMEM((TM, C_W + D_W), BF16)]
        body = _prompt_odd_kernel
    aliases = {}
    for j, a in enumerate(carry):
        aliases[len(args)] = 1 + j
        in_specs.append(pl.BlockSpec(memory_space=pl.ANY))
        args.append(a)
    return pl.pallas_call(
        functools.partial(body, n_alias=len(carry), slot=layer // 2),
        grid=(n // TM,),
        in_specs=in_specs,
        out_specs=[o[0] for o in outs],
        out_shape=[o[1] for o in outs],
        scratch_shapes=scratch,
        input_output_aliases=aliases,
        compiler_params=_params("arbitrary"),
        name=f"prompt_layer{layer}",
    )(*args)


N_IN = DEC_SEQ // TM
N_Q = DEC_SEQ // TQ


def _rows(ref, start, size):
    return ref.at[pl.ds(pl.multiple_of(start, size), size)]


def _sample_even_kernel(x_ref, xp_ref, xn_ref, g_ref, mod_ref, w_ref, cw_ref, cb_ref, lg_ref, lb_ref,
                        cos_ref, sin_ref, cost_ref, sint_ref, ck_ref, cv_ref, lam_ref, sg_ref,
                        xr_ref, wo_ref, gp_ref, o_ref,
                        ma_ref, qt_ref, k_ref, vt_ref, bz_ref, sh_ref, acc_ref, ckb_ref, cvt_ref, mb_ref,
                        *, lam_init):
    ph = pl.program_id(1)

    @pl.when(ph == 0)
    def _():
        _prep_diff_cache(ck_ref, cv_ref, ckb_ref, cvt_ref)

    @pl.when(ph < N_IN)
    def _():
        per = TM // SEQ
        _in_even_kernel(x_ref, g_ref, mod_ref, w_ref, cw_ref, cb_ref, lg_ref, lb_ref,
                        xp_ref, xn_ref, cos_ref, sin_ref, cost_ref, sint_ref,
                        _rows(ma_ref, ph * TM, TM), qt_ref.at[pl.ds(ph * per, per)],
                        _rows(k_ref, ph * TM, TM), vt_ref.at[pl.ds(ph * per, per)],
                        _rows(bz_ref, ph * TM, TM), sh_ref, acc_ref, sample=True, pos=ph)

    @pl.when(ph >= N_IN)
    def _():
        t = ph - N_IN
        _diff_sample_body(qt_ref.at[t], k_ref, vt_ref, _rows(bz_ref, t * TQ, TQ), lam_ref, sg_ref,
                          _rows(ma_ref, t * TQ, TQ), xr_ref, wo_ref, gp_ref, mod_ref, o_ref,
                          ckb_ref, cvt_ref, mb_ref, lam_init=lam_init)


def _sample_odd_kernel(x_ref, g_ref, mod_ref, w_ref, qn_ref, kn_ref,
                       cos_ref, sin_ref, cost_ref, sint_ref, cck_ref, ccv_ref, cdk_ref, cdv_ref, sink_ref,
                       xr_ref, wo_ref, gp_ref, o_ref,
                       cqt_ref, ck_ref, cvt_ref, cz_ref, dqt_ref, dk_ref, dvt_ref, dz_ref, m_ref):
    ph = pl.program_id(1)

    @pl.when(ph < N_IN)
    def _():
        per = TM // SEQ
        chunks = TM // LANES
        _in_odd_kernel(x_ref, g_ref, mod_ref, w_ref, qn_ref, kn_ref,
                       cos_ref, sin_ref, cost_ref, sint_ref,
                       cqt_ref.at[pl.ds(ph * per, per)], _rows(ck_ref, ph * TM, TM),
                       cvt_ref.at[pl.ds(ph * chunks, chunks)], _rows(cz_ref, ph * TM, TM),
                       dqt_ref.at[pl.ds(ph * per, per)], _rows(dk_ref, ph * TM, TM),
                       dvt_ref.at[pl.ds(ph * chunks, chunks)], _rows(dz_ref, ph * TM, TM), sample=True)

    @pl.when(ph >= N_IN)
    def _():
        t = ph - N_IN
        _gqa_sample_body(cqt_ref.at[t], ck_ref, cvt_ref, cck_ref, ccv_ref, _rows(cz_ref, t * TQ, TQ),
                         dqt_ref.at[t], dk_ref, dvt_ref, cdk_ref, cdv_ref, _rows(dz_ref, t * TQ, TQ),
                         sink_ref, xr_ref, wo_ref, gp_ref, mod_ref, o_ref, m_ref, t=t)


def _sample_layer(x, layer, g_pre, g_post, mod4, w_in, w_out, head, tail, caches, tables, lam_init=None):
    n = x.shape[0]
    even = layer % 2 == 0
    in_tile = lambda b, ph: b * N_IN + jnp.minimum(ph, N_IN - 1)
    q_tile = lambda b, ph: b * N_Q + jnp.maximum(ph - N_IN, 0)
    tab = lambda b, ph: jnp.minimum(ph, N_IN - 1)
    cos, sin, cos_t, sin_t = tables
    x_spec = pl.BlockSpec((TM, D_MODEL), lambda b, ph: (in_tile(b, ph), 0))
    mod_spec = pl.BlockSpec((None, None, 1, 3 * D_MODEL), lambda b, ph: (layer, 1 + b, 0, 0))
    table_specs = [pl.BlockSpec((TM, B_W), lambda b, ph: (tab(b, ph), 0)),
                   pl.BlockSpec((TM, B_W), lambda b, ph: (tab(b, ph), 0)),
                   pl.BlockSpec((B_W, TM), lambda b, ph: (0, tab(b, ph))),
                   pl.BlockSpec((B_W, TM), lambda b, ph: (0, tab(b, ph)))]
    cache_specs = [pl.BlockSpec((None, None) + c.shape[2:], lambda b, ph: (b, layer // 2, 0, 0))
                   for c in caches]
    res_spec = pl.BlockSpec((TQ, D_MODEL), lambda b, ph: (q_tile(b, ph), 0))
    in_specs = [x_spec]
    args = [x]
    if even:
        hb = TM // HALO
        last = n // HALO - 1
        in_specs += [pl.BlockSpec((HALO, D_MODEL), lambda b, ph: (jnp.maximum(in_tile(b, ph) * hb - 1, 0), 0)),
                     pl.BlockSpec((HALO, D_MODEL),
                                  lambda b, ph: (jnp.minimum((in_tile(b, ph) + 1) * hb, last), 0))]
        args += [x, x]
    in_specs += [_pick(g_pre, layer), mod_spec, _pick(w_in, layer // 2)] + [_pick(h, layer // 2) for h in head]
    args += [g_pre, mod4, w_in] + list(head)
    in_specs += table_specs + cache_specs + [_pick(t, layer // 2) for t in tail]
    args += [cos, sin, cos_t, sin_t] + list(caches) + list(tail)
    in_specs += [res_spec, _pick(w_out, layer // 2), _pick(g_post, layer)]
    args += [x, w_out, g_post]
    seq_wide = lambda w, dt: pltpu.VMEM((DEC_SEQ, w), dt)
    slab = pltpu.VMEM((DEC_SEQ // SEQ, B_W, SEQ), BF16)
    if even:
        scratch = [seq_wide(A_W, BF16), slab, seq_wide(B_W, BF16), slab, seq_wide(B_W, F32),
                   pltpu.VMEM((SUBLANES, SEQ + 3 * SUBLANES, A_W), F32), pltpu.VMEM((SEQ, A_W), F32),
                   pltpu.VMEM((PAST_LEN, B_W), BF16), pltpu.VMEM((B_W, PAST_LEN), BF16),
                   pltpu.VMEM((TQ, B_W), BF16)]
        body = functools.partial(_sample_even_kernel, lam_init=lam_init)
    else:
        chunk = pltpu.VMEM((DEC_SEQ // LANES, KV_W, LANES), BF16)
        scratch = [slab, seq_wide(KV_W, BF16), chunk, seq_wide(C_W, F32),
                   slab, seq_wide(KV_W, BF16), chunk, seq_wide(D_W, F32),
                   pltpu.VMEM((TQ, C_W + D_W), BF16)]
        body = _sample_odd_kernel
    return pl.pallas_call(
        body,
        grid=(DEC_BATCH, N_IN + N_Q),
        in_specs=in_specs,
        out_specs=pl.BlockSpec((TQ, D_MODEL), lambda b, ph: (q_tile(b, ph), 0)),
        out_shape=jax.ShapeDtypeStruct((n, D_MODEL), F32),
        scratch_shapes=scratch,
        compiler_params=_params("arbitrary", "arbitrary"),
        name=f"sample_layer{layer}",
    )(*args)


def _rope_tables():
    nf = DH // 4
    t = jnp.arange(DEC_SEQ)
    row = (t // GRID_W).astype(F32)
    col = (t % GRID_W).astype(F32)
    inv = ROPE_THETA ** (-jnp.arange(nf, dtype=F32) / nf)
    d = jnp.arange(DH)
    axis = d // (2 * nf)
    second = (d % (2 * nf)) // nf
    f = d % nf
    pos = jnp.where(axis[None, :] == 0, row[:, None], col[:, None])
    ang = pos * inv[f][None, :]
    cos = jnp.cos(ang)
    sin = jnp.where(second[None, :] == 0, -jnp.sin(ang), jnp.sin(ang))
    reps = B_W // DH
    cos = jnp.tile(cos, (1, reps))
    sin = jnp.tile(sin, (1, reps))
    return cos, sin, cos.T, sin.T


def kernel(x_prompt, x_sample, cache_b_k, cache_b_v, cache_c_k, cache_c_v, cache_d_k, cache_d_v, c, c_ctx, norm_pre, norm_post, w_mod, b_mod, w_in_even, a_conv_w, a_conv_b, a_ln_g, a_ln_b, b_lambda, b_subln_g, w_out_even, w_in_odd, c_q_norm, c_k_norm, d_sink, w_out_odd):
    n_even = (DEPTH + 1) // 2
    n_odd = DEPTH // 2
    cond8 = jnp.zeros((SUBLANES, D_MODEL), F32).at[0].set(c_ctx).at[1:1 + DEC_BATCH].set(c)
    mod4 = _modulation(cond8, w_mod, b_mod).reshape(DEPTH, SUBLANES, 1, 3 * D_MODEL)
    tables = _rope_tables()

    xp = x_prompt.reshape(BATCH * SEQ, D_MODEL)
    xs = x_sample.reshape(DEC_BATCH * DEC_SEQ, D_MODEL)
    feat = lambda a, w: jnp.moveaxis(a.reshape(a.shape[:3] + (w,)), 2, 3)
    cbk = feat(cache_b_k, B_W)
    cbv = cache_b_v.reshape(DEC_BATCH, n_even, PAST_LEN * H_B, 2 * DH)
    cck = feat(cache_c_k, KV_W)
    ccv = feat(cache_c_v, KV_W)
    cdk = feat(cache_d_k, KV_W)
    cdv = feat(cache_d_v, KV_W)

    g_pre = norm_pre.reshape(DEPTH, 1, D_MODEL)
    g_post = norm_post.reshape(DEPTH, 1, D_MODEL)
    w_in_e = w_in_even.astype(BF16)
    w_out_e = w_out_even.astype(BF16)
    w_in_o = w_in_odd.astype(BF16)
    w_out_o = w_out_odd.astype(BF16)
    conv = (jnp.zeros((n_even, 4 * SUBLANES, A_W), F32).at[:, :CONV_K].set(a_conv_w),
            a_conv_b.reshape(n_even, 1, A_W), a_ln_g.reshape(n_even, 1, A_W), a_ln_b.reshape(n_even, 1, A_W))
    subln = b_subln_g.reshape(n_even, 1, 2 * DH)
    qkn = (jnp.tile(c_q_norm, (1, C_W // DH)).reshape(n_odd, C_W, 1),
           jnp.tile(c_k_norm, (1, KV_W // DH)).reshape(n_odd, 1, KV_W))
    sink = d_sink.reshape(n_odd, 1, 8)

    new_even, new_odd = (), ()
    for l in range(DEPTH):
        if l % 2 == 0:
            lam_init = 0.8 - 0.6 * math.exp(-0.3 * l)
            w_in, w_out = w_in_e, w_out_e
            xp, *new_even = _prompt_layer(xp, l, g_pre, g_post, mod4, w_in, w_out, conv,
                                          (b_lambda, subln), new_even, lam_init)
            xs = _sample_layer(xs, l, g_pre, g_post, mod4, w_in, w_out, conv, (b_lambda, subln),
                               (cbk, cbv), tables, lam_init)
        else:
            w_in, w_out = w_in_o, w_out_o
            xp, *new_odd = _prompt_layer(xp, l, g_pre, g_post, mod4, w_in, w_out, qkn, (sink,), new_odd)
            xs = _sample_layer(xs, l, g_pre, g_post, mod4, w_in, w_out, qkn, (sink,),
                               (cck, ccv, cdk, cdv), tables)

    def token_major(a, heads):
        return jnp.moveaxis(a.reshape(a.shape[:2] + heads + (DH, SEQ)), -1, 2)

    kt, v = new_even
    ckt, cvt, dkt, dvt = new_odd
    return (xp.reshape(BATCH, SEQ, D_MODEL), xs.reshape(DEC_BATCH, DEC_SEQ, D_MODEL),
            token_major(kt, (H_B, 2)), v.reshape(BATCH, n_even, SEQ, H_B, 2 * DH),
            token_major(ckt, (2,)), token_major(cvt, (2,)), token_major(dkt, (2,)), token_major(dvt, (2,)))
```

```python
import functools
import math

import jax
import jax.numpy as jnp
from jax import lax
from jax.experimental import pallas as pl
from jax.experimental.pallas import tpu as pltpu

F32 = jnp.float32
BF16 = jnp.bfloat16

D_MODEL = 1024
BATCH = 16
SEQ = 256
DEPTH = 4
DEC_BATCH = 2
DEC_SEQ = 1024
PAST_LEN = 512
GRID_W = 64
ROPE_THETA = 10000.0
NORM_EPS = 1e-6
DH = 64
A_W = 512
CONV_K = 31
H_B = 4
B_W = 512
C_W = 512
KV_W = 128
D_W = 512
WINDOW = 128
LOG2E = math.log2(math.e)
QK_SCALE = DH ** -0.5 * LOG2E

LANES = 128
SUBLANES = 8
VMEM_LIMIT = 56 * 1024 * 1024

TM = 512
TQ = SEQ
HALO = 16
GQA_AHEAD = (8, 4)
DIFF_AHEAD = (3, 2)
ROW_CHUNK = 64
DEN_ROWS = 16


def _params(*sem):
    return pltpu.CompilerParams(dimension_semantics=sem, vmem_limit_bytes=VMEM_LIMIT)


def _silu(x):
    return x * jax.nn.sigmoid(x)


def _dot(a, b):
    return jnp.dot(a, b, preferred_element_type=F32)


def _pick(stacked, idx):
    return pl.BlockSpec((None,) + stacked.shape[1:], lambda *_: (idx,) + (0,) * (stacked.ndim - 1))


def _mod_kernel(cond_ref, w_ref, b_ref, o_ref):
    a = _silu(cond_ref[...]).astype(BF16)
    o_ref[...] = _dot(a, w_ref[...].astype(BF16)) + b_ref[...]


def _modulation(cond8, w_mod, b_mod):
    return pl.pallas_call(
        _mod_kernel,
        grid=(DEPTH,),
        in_specs=[
            pl.BlockSpec((SUBLANES, D_MODEL), lambda l: (0, 0)),
            pl.BlockSpec((None, D_MODEL, 3 * D_MODEL), lambda l: (l, 0, 0)),
            pl.BlockSpec((None, 1, 3 * D_MODEL), lambda l: (l, 0, 0)),
        ],
        out_specs=pl.BlockSpec((None, SUBLANES, 3 * D_MODEL), lambda l: (l, 0, 0)),
        out_shape=jax.ShapeDtypeStruct((DEPTH, SUBLANES, 3 * D_MODEL), F32),
        compiler_params=_params("arbitrary"),
        name="modulation",
    )(cond8, w_mod, b_mod.reshape(DEPTH, 1, 3 * D_MODEL))


def _pre_norm(x_ref, g_ref, mod_ref):
    return _modulate(x_ref[...], g_ref, mod_ref)


def _modulate(x, g_ref, mod_ref):
    ms = jnp.mean(x * x, axis=-1, keepdims=True)
    mod = mod_ref[...]
    sh = mod[:, :D_MODEL]
    sc = mod[:, D_MODEL:2 * D_MODEL]
    h = (x * lax.rsqrt(ms + NORM_EPS)) * (g_ref[...] * (1.0 + sc)) + sh
    return h.astype(BF16)


def _rope(x, cos, sin_signed):
    w = x.shape[-1]
    lane = lax.broadcasted_iota(jnp.int32, (1, w), 1)
    first = (lane % 32) < 16
    partner = jnp.where(first, pltpu.roll(x, w - 16, 1), pltpu.roll(x, 16, 1))
    return x * cos + partner * sin_signed


def _rope_t(x, cos_t, sin_t):
    r = x.shape[0]
    row = lax.broadcasted_iota(jnp.int32, (r, 1), 0)
    first = (row % 32) < 16
    partner = jnp.where(first, pltpu.roll(x, r - 16, 0), pltpu.roll(x, 16, 0))
    return x * cos_t + partner * sin_t


def _store_chunks(ref, xt):
    for c in range(xt.shape[1] // LANES):
        ref[c] = xt[:, c * LANES:(c + 1) * LANES]


def _store_per_seq(ref, xt):
    for s in range(xt.shape[1] // SEQ):
        ref[s] = xt[:, s * SEQ:(s + 1) * SEQ]


def _own_slot(ref, slot, fill=True):
    if len(ref.shape) == 3:
        return ref
    for other in range(ref.shape[1]):
        if fill and other != slot:
            ref[:, other] = jnp.zeros((ref.shape[0],) + tuple(ref.shape[2:]), ref.dtype)
    return ref.at[:, slot]


def _glu(ug):
    return ug[:, :A_W] * jax.nn.sigmoid(ug[:, A_W:])


def _conv_mix(pad, az, cw_ref, cb_ref, lg_ref, lb_ref, sh_ref, acc_ref):
    rows = sh_ref.shape[1]
    for b in range(SUBLANES):
        sh_ref[b] = pad[b:b + rows]
    base = HALO - CONV_K // 2
    for c0 in range(0, A_W, LANES):
        cs = slice(c0, c0 + LANES)
        for r0 in range(0, SEQ, ROW_CHUNK):
            acc = jnp.zeros((ROW_CHUNK, LANES), F32) + cb_ref[:, cs]
            for k in range(CONV_K):
                j = k + base
                s = r0 + (j // SUBLANES) * SUBLANES
                acc = acc + sh_ref[j % SUBLANES, s:s + ROW_CHUNK, cs] * cw_ref[k:k + 1, cs]
            acc_ref[r0:r0 + ROW_CHUNK, cs] = acc
    a = acc_ref[...]
    mu = jnp.mean(a, axis=-1, keepdims=True)
    d = a - mu
    var = jnp.mean(d * d, axis=-1, keepdims=True)
    y = d * lax.rsqrt(var + NORM_EPS) * lg_ref[...] + lb_ref[...]
    return _silu(y) * _silu(az)


def _in_even_kernel(x_ref, g_ref, mod_ref, w_ref, cw_ref, cb_ref, lg_ref, lb_ref, *rest,
                    sample, n_alias=0, slot=0, pos=None):
    if sample:
        (xp_ref, xn_ref, cos_ref, sin_ref, cost_ref, sint_ref,
         ma_ref, qt_ref, k_ref, vt_ref, bz_ref, sh_ref, acc_ref) = rest
    else:
        ma_ref, qt_ref, k_ref, kt_ref, v_ref, vt_ref, bz_ref, sh_ref, acc_ref = rest[n_alias:]
        kt_ref = _own_slot(kt_ref, slot)
        v_ref = _own_slot(v_ref, slot)
    hb = _pre_norm(x_ref, g_ref, mod_ref)
    a = _glu(_dot(hb, w_ref[:, 0:2 * A_W]))
    az = _dot(hb, w_ref[:, 2 * A_W:3 * A_W])
    n_sub = TM // SEQ
    if sample:
        tiles_per_seq = DEC_SEQ // TM
        xh = jnp.concatenate([xp_ref[...], xn_ref[...]], axis=0)
        ah = _glu(_dot(_modulate(xh, g_ref, mod_ref), w_ref[:, 0:2 * A_W]))
        prev = jnp.where(pos != 0, ah[:HALO], 0.0)
        nxt = jnp.where(pos != tiles_per_seq - 1, ah[HALO:], 0.0)
        full = jnp.concatenate([prev, a, nxt], axis=0)
        pads = [full[j * SEQ:(j + 1) * SEQ + 2 * HALO] for j in range(n_sub)]
    else:
        zeros = jnp.zeros((HALO, A_W), F32)
        pads = [jnp.concatenate([zeros, a[j * SEQ:(j + 1) * SEQ], zeros], axis=0) for j in range(n_sub)]
    o = 3 * A_W
    q = _dot(hb, w_ref[:, o:o + B_W])
    k = _dot(hb, w_ref[:, o + B_W:o + 2 * B_W])
    v = _dot(hb, w_ref[:, o + 2 * B_W:o + 3 * B_W])
    bz_ref[...] = _dot(hb, w_ref[:, o + 3 * B_W:o + 4 * B_W])
    qt = q.T
    if sample:
        qt = _rope_t(qt, cost_ref[...], sint_ref[...])
        k = _rope(k, cos_ref[...], sin_ref[...])
    else:
        _store_per_seq(kt_ref, k.T)
        for s in range(TM // SEQ):
            for h in range(H_B):
                v_ref[s, pl.ds(h, SEQ, stride=H_B), :] = v[s * SEQ:(s + 1) * SEQ, h * LANES:(h + 1) * LANES]
    _store_per_seq(qt_ref, (qt * QK_SCALE).astype(BF16))
    k_ref[...] = k.astype(BF16)
    _store_per_seq(vt_ref, v.T.astype(BF16))
    for j, pad in enumerate(pads):
        rs = slice(j * SEQ, (j + 1) * SEQ)
        ma_ref[rs, :] = _conv_mix(pad, az[rs], cw_ref, cb_ref, lg_ref, lb_ref, sh_ref, acc_ref).astype(BF16)


def _group_mean_sq(x):
    width = x.shape[-1]
    xx = x * x
    hi = xx.astype(BF16)
    lo = (xx - hi.astype(F32)).astype(BF16)
    r = lax.broadcasted_iota(jnp.int32, (width, width), 0) // DH
    c = lax.broadcasted_iota(jnp.int32, (width, width), 1) // DH
    g = jnp.where(r == c, 1.0, 0.0).astype(BF16)
    return (_dot(hi, g) + _dot(lo, g)) * (1.0 / DH)


def _head_rms_t(xt, gain_col):
    parts = []
    for j in range(xt.shape[0] // DH):
        blk = xt[j * DH:(j + 1) * DH]
        ms = jnp.mean(blk * blk, axis=0, keepdims=True)
        parts.append(blk * lax.rsqrt(ms + NORM_EPS))
    return jnp.concatenate(parts, axis=0) * gain_col


def _in_odd_kernel(x_ref, g_ref, mod_ref, w_ref, qn_ref, kn_ref, *rest, sample, n_alias=0, slot=0):
    if sample:
        (cos_ref, sin_ref, cost_ref, sint_ref,
         cqt_ref, ck_ref, cvt_ref, cz_ref, dqt_ref, dk_ref, dvt_ref, dz_ref) = rest
    else:
        (cqt_ref, ck_ref, ckt_ref, cvt_ref, cz_ref,
         dqt_ref, dk_ref, dkt_ref, dvt_ref, dz_ref) = rest[n_alias:]
        ckt_ref, cvt_ref, dkt_ref, dvt_ref = [_own_slot(r, slot) for r in (ckt_ref, cvt_ref, dkt_ref, dvt_ref)]
    hb = _pre_norm(x_ref, g_ref, mod_ref)
    y = _dot(hb, w_ref[...])
    o = 0
    cqt = _head_rms_t(y[:, o:o + C_W].T, qn_ref[...])
    o += C_W
    ck = y[:, o:o + KV_W]
    ck = ck * lax.rsqrt(_group_mean_sq(ck) + NORM_EPS) * kn_ref[...]
    o += KV_W
    cvt = y[:, o:o + KV_W].T
    o += KV_W
    cz_ref[...] = y[:, o:o + C_W]
    o += C_W
    dqt = y[:, o:o + D_W].T
    o += D_W
    dk = y[:, o:o + KV_W]
    o += KV_W
    dvt = y[:, o:o + KV_W].T
    o += KV_W
    dz_ref[...] = y[:, o:o + D_W]
    if sample:
        cos_t = cost_ref[...]
        sin_t = sint_ref[...]
        cqt = _rope_t(cqt, cos_t, sin_t)
        dqt = _rope_t(dqt, cos_t, sin_t)
        cos = cos_ref[...][:, :KV_W]
        sin = sin_ref[...][:, :KV_W]
        ck = _rope(ck, cos, sin)
        dk = _rope(dk, cos, sin)
        _store_chunks(cvt_ref, cvt.astype(BF16))
        _store_chunks(dvt_ref, dvt.astype(BF16))
    else:
        _store_per_seq(ckt_ref, ck.T)
        _store_per_seq(dkt_ref, dk.T)
        _store_per_seq(cvt_ref, cvt)
        _store_per_seq(dvt_ref, dvt)
    _store_per_seq(cqt_ref, (cqt * QK_SCALE).astype(BF16))
    _store_per_seq(dqt_ref, (dqt * QK_SCALE).astype(BF16))
    ck_ref[...] = ck.astype(BF16)
    dk_ref[...] = dk.astype(BF16)


def _exp_terms(segs, extra=None):
    m = None
    for s in segs:
        mi = jnp.max(s, axis=0, keepdims=True)
        m = mi if m is None else jnp.maximum(m, mi)
    if extra is not None:
        extra = extra * LOG2E
        m = jnp.maximum(m, extra)
    es = [jnp.exp2(s - m) for s in segs]
    return es, (None if extra is None else jnp.exp2(extra - m))


def _softmax_t(segs):
    es, _ = _exp_terms(segs)
    den = None
    for e in es:
        di = jnp.sum(e, axis=0, keepdims=True)
        den = di if den is None else den + di
    return es, den


def _keep_rows(xt, lo, hi):
    zeros = lambda r: jnp.zeros((r, xt.shape[1]), xt.dtype)
    parts = []
    if lo > 0:
        parts.append(zeros(lo))
    parts.append(xt[lo:hi])
    if hi < xt.shape[0]:
        parts.append(zeros(xt.shape[0] - hi))
    return jnp.concatenate(parts, axis=0)


def _pipelined(n, scores, finish, ahead):
    ready = [scores(j) for j in range(min(ahead, n))]
    for j in range(n):
        if j + ahead < n:
            ready.append(scores(j + ahead))
        finish(j, ready.pop(0))


def _diff_attn(qt_ref, kvs, z_ref, lam_ref, g_ref, o_ref, *, lam_init):
    lv = lam_ref[...]
    lam = (jnp.exp(jnp.sum(lv[0:1] * lv[1:2], axis=-1, keepdims=True))
           - jnp.exp(jnp.sum(lv[2:3] * lv[3:4], axis=-1, keepdims=True)) + lam_init)

    def scores(h):
        cs = slice(h * LANES, (h + 1) * LANES)
        qt = qt_ref[cs, :]
        ks = [get_k(cs) for get_k, _ in kvs]
        return [[_dot(kk, _keep_rows(qt, c * DH, (c + 1) * DH)) for kk in ks] for c in range(2)]

    def finish(h, ss):
        cs = slice(h * LANES, (h + 1) * LANES)
        es0, den0 = _softmax_t(ss[0])
        es1, den1 = _softmax_t(ss[1])
        r0 = 1.0 / den0
        r1 = lam / den1
        ot = None
        for e0, e1, (_, get_vt) in zip(es0, es1, kvs):
            w = e0 * r0 - e1 * r1
            oi = _dot(get_vt(cs), w.astype(BF16))
            ot = oi if ot is None else ot + oi
        ms = jnp.mean(ot * ot, axis=0, keepdims=True)
        o = (ot * lax.rsqrt(ms + NORM_EPS)).T
        o = (o * g_ref[...]) * (1.0 - lam_init)
        o_ref[:, cs] = (o * _silu(z_ref[:, cs])).astype(o_ref.dtype)

    _pipelined(H_B, scores, finish, DIFF_AHEAD[len(kvs) - 1])


def _post_residual(o, x_ref, g_ref, mod_ref, o_ref):
    ms = jnp.mean(o * o, axis=-1, keepdims=True)
    r = o * lax.rsqrt(ms + NORM_EPS) * g_ref[...]
    gate = mod_ref[...][:, 2 * D_MODEL:]
    o_ref[...] = x_ref[...] + gate * r


def _diff_prompt_kernel(qt_ref, k_ref, vt_ref, z_ref, lam_ref, sg_ref, ma_ref, x_ref, w_ref, g_ref,
                        mod_ref, o_ref, mb_ref, *, lam_init):
    oa = _dot(ma_ref[...], w_ref[0:A_W, :])
    for s in range(qt_ref.shape[0]):
        rs = pl.ds(s * SEQ, SEQ)
        kv = (lambda cs, s=s: k_ref[s * SEQ:(s + 1) * SEQ, cs], lambda cs, s=s: vt_ref[s, cs, :])
        _diff_attn(qt_ref.at[s], [kv], z_ref.at[rs], lam_ref, sg_ref, mb_ref.at[rs], lam_init=lam_init)
    _post_residual(oa + _dot(mb_ref[...], w_ref[A_W:, :]), x_ref, g_ref, mod_ref, o_ref)


def _prep_diff_cache(ck_ref, cv_ref, ckb_ref, cvt_ref):
    ckb_ref[...] = ck_ref[...].T.astype(BF16)
    for h in range(H_B):
        cs = slice(h * LANES, (h + 1) * LANES)
        cvt_ref[cs, :] = cv_ref[pl.ds(h, PAST_LEN, stride=H_B), :].T.astype(BF16)


def _diff_sample_body(qt_ref, k_ref, vt_ref, z_ref, lam_ref, sg_ref, ma_ref, x_ref, w_ref, g_ref,
                      mod_ref, o_ref, ckb_ref, cvt_ref, mb_ref, *, lam_init):
    ctx = (lambda cs: ckb_ref[:, cs], lambda cs: cvt_ref[cs, :])
    loc = (lambda cs: k_ref[:, cs],
           lambda cs: jnp.concatenate([vt_ref[c, cs, :] for c in range(vt_ref.shape[0])], axis=1))
    oa = _dot(ma_ref[...], w_ref[0:A_W, :])
    _diff_attn(qt_ref, [ctx, loc], z_ref, lam_ref, sg_ref, mb_ref, lam_init=lam_init)
    _post_residual(oa + _dot(mb_ref[...], w_ref[A_W:, :]), x_ref, g_ref, mod_ref, o_ref)


def _gqa(qt_ref, segs, z_ref, o_ref, sink_ref=None):
    halves = []

    def scores(j):
        n = j // 4
        qj = qt_ref[j * DH:(j + 1) * DH, :]
        zero = jnp.zeros_like(qj)
        qz = jnp.concatenate([qj, zero] if n == 0 else [zero, qj], axis=0)
        return [_dot(k, qz) for k, _, _ in segs]

    def finish(j, ss):
        n = j // 4
        ss = [s if valid is None else jnp.where(valid, s, -jnp.inf)
              for s, (_, _, valid) in zip(ss, segs)]
        extra = None if sink_ref is None else sink_ref[:, j:j + 1]
        es, den = _exp_terms(ss, extra)
        ot = None
        for e, (_, vt, _) in zip(es, segs):
            ones = jnp.ones((DEN_ROWS, vt.shape[1]), BF16)
            vt1 = jnp.concatenate([vt[n * DH:(n + 1) * DH], ones], axis=0)
            oi = _dot(vt1, e.astype(BF16))
            ot = oi if ot is None else ot + oi
        den = ot[DH:DH + 1] if den is None else den + ot[DH:DH + 1]
        halves.append(ot[:DH] * (1.0 / den))
        if j % 2 == 1:
            cs = slice((j // 2) * LANES, (j // 2 + 1) * LANES)
            o_pair = jnp.concatenate(halves[-2:], axis=0).T
            o_ref[:, cs] = (o_pair * _silu(z_ref[:, cs])).astype(o_ref.dtype)

    _pipelined(2 * 4, scores, finish, GQA_AHEAD[len(segs) - 1])


def _gqa_prompt_kernel(cqt_ref, ck_ref, cvt_ref, cz_ref, dqt_ref, dk_ref, dvt_ref, dz_ref, sink_ref,
                       x_ref, w_ref, g_ref, mod_ref, o_ref, m_ref):
    for s in range(cqt_ref.shape[0]):
        rows = slice(s * SEQ, (s + 1) * SEQ)
        rs = pl.ds(s * SEQ, SEQ)
        seg = lambda k_ref, vt_ref: (k_ref[rows, :], vt_ref[s].astype(BF16), None)
        _gqa(cqt_ref.at[s], [seg(ck_ref, cvt_ref)], cz_ref.at[rs], m_ref.at[rs, pl.ds(0, C_W)])
        _gqa(dqt_ref.at[s], [seg(dk_ref, dvt_ref)], dz_ref.at[rs], m_ref.at[rs, pl.ds(C_W, D_W)], sink_ref)
    _post_residual(_dot(m_ref[...], w_ref[...]), x_ref, g_ref, mod_ref, o_ref)


def _gqa_sample_body(cqt_ref, ck_ref, cvt_ref, cck_ref, ccv_ref, cz_ref,
                     dqt_ref, dk_ref, dvt_ref, cdk_ref, cdv_ref, dz_ref, sink_ref,
                     x_ref, w_ref, g_ref, mod_ref, o_ref, m_ref, *, t):
    oc_ref = m_ref.at[:, pl.ds(0, C_W)]
    od_ref = m_ref.at[:, pl.ds(C_W, D_W)]
    ctx = lambda kt_ref, vt_ref: (kt_ref[...].T.astype(BF16), vt_ref[...].astype(BF16), None)
    n_chunks = DEC_SEQ // LANES
    cvt = jnp.concatenate([cvt_ref[c] for c in range(n_chunks)], axis=1)
    _gqa(cqt_ref, [ctx(cck_ref, ccv_ref), (ck_ref[...], cvt, None)], cz_ref, oc_ref)
    span = 2 * TQ
    t0 = t * TQ
    ws = pl.multiple_of(jnp.clip(t0 - WINDOW, 0, DEC_SEQ - span), WINDOW)
    kpos = ws + lax.broadcasted_iota(jnp.int32, (span, TQ), 0)
    qpos = t0 + lax.broadcasted_iota(jnp.int32, (span, TQ), 1)
    valid = jnp.abs(qpos - kpos) <= WINDOW
    c0 = ws // LANES
    dvt = jnp.concatenate([dvt_ref[c0 + c] for c in range(span // LANES)], axis=1)
    _gqa(dqt_ref, [ctx(cdk_ref, cdv_ref), (dk_ref[pl.ds(ws, span), :], dvt, valid)],
         dz_ref, od_ref, sink_ref)
    _post_residual(_dot(m_ref[...], w_ref[...]), x_ref, g_ref, mod_ref, o_ref)


def _prompt_even_kernel(x_ref, g_ref, mod_ref, w_ref, cw_ref, cb_ref, lg_ref, lb_ref, lam_ref, sg_ref,
                        wo_ref, gp_ref, *rest, lam_init, n_alias, slot):
    (o_ref, kt_ref, v_ref,
     ma_ref, qt_ref, k_ref, vt_ref, bz_ref, sh_ref, acc_ref, mb_ref) = rest[n_alias:]
    _in_even_kernel(x_ref, g_ref, mod_ref, w_ref, cw_ref, cb_ref, lg_ref, lb_ref,
                    ma_ref, qt_ref, k_ref, kt_ref, v_ref, vt_ref, bz_ref, sh_ref, acc_ref,
                    sample=False, slot=slot)
    _diff_prompt_kernel(qt_ref, k_ref, vt_ref, bz_ref, lam_ref, sg_ref, ma_ref, x_ref, wo_ref, gp_ref,
                        mod_ref, o_ref, mb_ref, lam_init=lam_init)


def _prompt_odd_kernel(x_ref, g_ref, mod_ref, w_ref, qn_ref, kn_ref, sink_ref, wo_ref, gp_ref, *rest,
                       n_alias, slot):
    (o_ref, ckt_ref, cvt_ref, dkt_ref, dvt_ref,
     cqt_ref, ck_ref, cz_ref, dqt_ref, dk_ref, dz_ref, m_ref) = rest[n_alias:]
    _in_odd_kernel(x_ref, g_ref, mod_ref, w_ref, qn_ref, kn_ref,
                   cqt_ref, ck_ref, ckt_ref, cvt_ref, cz_ref, dqt_ref, dk_ref, dkt_ref, dvt_ref, dz_ref,
                   sample=False, slot=slot)
    _gqa_prompt_kernel(cqt_ref, ck_ref, _own_slot(cvt_ref, slot, fill=False), cz_ref,
                       dqt_ref, dk_ref, _own_slot(dvt_ref, slot, fill=False), dz_ref, sink_ref,
                       x_ref, wo_ref, gp_ref, mod_ref, o_ref, m_ref)


def _prompt_layer(x, layer, g_pre, g_post, mod4, w_in, w_out, head, tail, carry, lam_init=None):
    n = x.shape[0]
    even = layer % 2 == 0
    row = lambda i: (i, 0)
    n_seq = TM // SEQ
    params = (g_pre, w_in) + tuple(head) + tuple(tail) + (w_out, g_post)
    idx = (layer, layer // 2) + (layer // 2,) * (len(head) + len(tail)) + (layer // 2, layer)
    specs = [_pick(p, i) for p, i in zip(params, idx)]
    in_specs = [pl.BlockSpec((TM, D_MODEL), row), specs[0],
                pl.BlockSpec((None, None, 1, 3 * D_MODEL), lambda i: (layer, 0, 0, 0))] + specs[1:]
    args = [x, g_pre, mod4, w_in] + list(head) + list(tail) + [w_out, g_post]
    n_layers = (DEPTH + 1 - layer % 2) // 2
    if carry:
        cache_spec = lambda r, c: pl.BlockSpec((n_seq, None, r, c), lambda i: (i, layer // 2, 0, 0))
    else:
        cache_spec = lambda r, c: pl.BlockSpec((n_seq, n_layers, r, c), lambda i: (i, 0, 0, 0))
    cache = lambda r, c: (cache_spec(r, c), jax.ShapeDtypeStruct((n // SEQ, n_layers, r, c), F32))
    outs = [(pl.BlockSpec((TM, D_MODEL), row), jax.ShapeDtypeStruct((n, D_MODEL), F32))]
    wide = pltpu.VMEM((TM, B_W), BF16)
    slab = pltpu.VMEM((n_seq, B_W, SEQ), BF16)
    if even:
        outs += [cache(B_W, SEQ), cache(SEQ * H_B, LANES)]
        scratch = [wide, slab, wide, slab, pltpu.VMEM((TM, B_W), F32),
                   pltpu.VMEM((SUBLANES, SEQ + 3 * SUBLANES, A_W), F32), pltpu.VMEM((SEQ, A_W), F32), wide]
        body = functools.partial(_prompt_even_kernel, lam_init=lam_init)
    else:
        outs += [cache(KV_W, SEQ)] * 4
        narrow = pltpu.VMEM((TM, KV_W), BF16)
        gate = pltpu.VMEM((TM, C_W), F32)
        scratch = [slab, narrow, gate, slab, narrow, gate, pltpu.VMEM((TM, C_W + D_W), BF16)]
        body = _prompt_odd_kernel
    aliases = {}
    for j, a in enumerate(carry):
        aliases[len(args)] = 1 + j
        in_specs.append(pl.BlockSpec(memory_space=pl.ANY))
        args.append(a)
    return pl.pallas_call(
        functools.partial(body, n_alias=len(carry), slot=layer // 2),
        grid=(n // TM,),
        in_specs=in_specs,
        out_specs=[o[0] for o in outs],
        out_shape=[o[1] for o in outs],
        scratch_shapes=scratch,
        input_output_aliases=aliases,
        compiler_params=_params("arbitrary"),
        name=f"prompt_layer{layer}",
    )(*args)


N_IN = DEC_SEQ // TM
N_Q = DEC_SEQ // TQ


def _rows(ref, start, size):
    return ref.at[pl.ds(pl.multiple_of(start, size), size)]


def _sample_even_kernel(x_ref, xp_ref, xn_ref, g_ref, mod_ref, w_ref, cw_ref, cb_ref, lg_ref, lb_ref,
                        cos_ref, sin_ref, cost_ref, sint_ref, ck_ref, cv_ref, lam_ref, sg_ref,
                        xr_ref, wo_ref, gp_ref, o_ref,
                        ma_ref, qt_ref, k_ref, vt_ref, bz_ref, sh_ref, acc_ref, ckb_ref, cvt_ref, mb_ref,
                        *, lam_init):
    ph = pl.program_id(1)

    @pl.when(ph == 0)
    def _():
        _prep_diff_cache(ck_ref, cv_ref, ckb_ref, cvt_ref)

    @pl.when(ph < N_IN)
    def _():
        per = TM // SEQ
        _in_even_kernel(x_ref, g_ref, mod_ref, w_ref, cw_ref, cb_ref, lg_ref, lb_ref,
                        xp_ref, xn_ref, cos_ref, sin_ref, cost_ref, sint_ref,
                        _rows(ma_ref, ph * TM, TM), qt_ref.at[pl.ds(ph * per, per)],
                        _rows(k_ref, ph * TM, TM), vt_ref.at[pl.ds(ph * per, per)],
                        _rows(bz_ref, ph * TM, TM), sh_ref, acc_ref, sample=True, pos=ph)

    @pl.when(ph >= N_IN)
    def _():
        t = ph - N_IN
        _diff_sample_body(qt_ref.at[t], k_ref, vt_ref, _rows(bz_ref, t * TQ, TQ), lam_ref, sg_ref,
                          _rows(ma_ref, t * TQ, TQ), xr_ref, wo_ref, gp_ref, mod_ref, o_ref,
                          ckb_ref, cvt_ref, mb_ref, lam_init=lam_init)


def _sample_odd_kernel(x_ref, g_ref, mod_ref, w_ref, qn_ref, kn_ref,
                       cos_ref, sin_ref, cost_ref, sint_ref, cck_ref, ccv_ref, cdk_ref, cdv_ref, sink_ref,
                       xr_ref, wo_ref, gp_ref, o_ref,
                       cqt_ref, ck_ref, cvt_ref, cz_ref, dqt_ref, dk_ref, dvt_ref, dz_ref, m_ref):
    ph = pl.program_id(1)

    @pl.when(ph < N_IN)
    def _():
        per = TM // SEQ
        chunks = TM // LANES
        _in_odd_kernel(x_ref, g_ref, mod_ref, w_ref, qn_ref, kn_ref,
                       cos_ref, sin_ref, cost_ref, sint_ref,
                       cqt_ref.at[pl.ds(ph * per, per)], _rows(ck_ref, ph * TM, TM),
                       cvt_ref.at[pl.ds(ph * chunks, chunks)], _rows(cz_ref, ph * TM, TM),
                       dqt_ref.at[pl.ds(ph * per, per)], _rows(dk_ref, ph * TM, TM),
                       dvt_ref.at[pl.ds(ph * chunks, chunks)], _rows(dz_ref, ph * TM, TM), sample=True)

    @pl.when(ph >= N_IN)
    def _():
        t = ph - N_IN
        _gqa_sample_body(cqt_ref.at[t], ck_ref, cvt_ref, cck_ref, ccv_ref, _rows(cz_ref, t * TQ, TQ),
                         dqt_ref.at[t], dk_ref, dvt_ref, cdk_ref, cdv_ref, _rows(dz_ref, t * TQ, TQ),
                         sink_ref, xr_ref, wo_ref, gp_ref, mod_ref, o_ref, m_ref, t=t)


def _sample_layer(x, layer, g_pre, g_post, mod4, w_in, w_out, head, tail, caches, tables, lam_init=None):
    n = x.shape[0]
    even = layer % 2 == 0
    in_tile = lambda b, ph: b * N_IN + jnp.minimum(ph, N_IN - 1)
    q_tile = lambda b, ph: b * N_Q + jnp.maximum(ph - N_IN, 0)
    tab = lambda b, ph: jnp.minimum(ph, N_IN - 1)
    cos, sin, cos_t, sin_t = tables
    x_spec = pl.BlockSpec((TM, D_MODEL), lambda b, ph: (in_tile(b, ph), 0))
    mod_spec = pl.BlockSpec((None, None, 1, 3 * D_MODEL), lambda b, ph: (layer, 1 + b, 0, 0))
    table_specs = [pl.BlockSpec((TM, B_W), lambda b, ph: (tab(b, ph), 0)),
                   pl.BlockSpec((TM, B_W), lambda b, ph: (tab(b, ph), 0)),
                   pl.BlockSpec((B_W, TM), lambda b, ph: (0, tab(b, ph))),
                   pl.BlockSpec((B_W, TM), lambda b, ph: (0, tab(b, ph)))]
    cache_specs = [pl.BlockSpec((None, None) + c.shape[2:], lambda b, ph: (b, layer // 2, 0, 0))
                   for c in caches]
    res_spec = pl.BlockSpec((TQ, D_MODEL), lambda b, ph: (q_tile(b, ph), 0))
    in_specs = [x_spec]
    args = [x]
    if even:
        hb = TM // HALO
        last = n // HALO - 1
        in_specs += [pl.BlockSpec((HALO, D_MODEL), lambda b, ph: (jnp.maximum(in_tile(b, ph) * hb - 1, 0), 0)),
                     pl.BlockSpec((HALO, D_MODEL),
                                  lambda b, ph: (jnp.minimum((in_tile(b, ph) + 1) * hb, last), 0))]
        args += [x, x]
    in_specs += [_pick(g_pre, layer), mod_spec, _pick(w_in, layer // 2)] + [_pick(h, layer // 2) for h in head]
    args += [g_pre, mod4, w_in] + list(head)
    in_specs += table_specs + cache_specs + [_pick(t, layer // 2) for t in tail]
    args += [cos, sin, cos_t, sin_t] + list(caches) + list(tail)
    in_specs += [res_spec, _pick(w_out, layer // 2), _pick(g_post, layer)]
    args += [x, w_out, g_post]
    seq_wide = lambda w, dt: pltpu.VMEM((DEC_SEQ, w), dt)
    slab = pltpu.VMEM((DEC_SEQ // SEQ, B_W, SEQ), BF16)
    if even:
        scratch = [seq_wide(A_W, BF16), slab, seq_wide(B_W, BF16), slab, seq_wide(B_W, F32),
                   pltpu.VMEM((SUBLANES, SEQ + 3 * SUBLANES, A_W), F32), pltpu.VMEM((SEQ, A_W), F32),
                   pltpu.VMEM((PAST_LEN, B_W), BF16), pltpu.VMEM((B_W, PAST_LEN), BF16),
                   pltpu.VMEM((TQ, B_W), BF16)]
        body = functools.partial(_sample_even_kernel, lam_init=lam_init)
    else:
        chunk = pltpu.VMEM((DEC_SEQ // LANES, KV_W, LANES), BF16)
        scratch = [slab, seq_wide(KV_W, BF16), chunk, seq_wide(C_W, F32),
                   slab, seq_wide(KV_W, BF16), chunk, seq_wide(D_W, F32),
                   pltpu.VMEM((TQ, C_W + D_W), BF16)]
        body = _sample_odd_kernel
    return pl.pallas_call(
        body,
        grid=(DEC_BATCH, N_IN + N_Q),
        in_specs=in_specs,
        out_specs=pl.BlockSpec((TQ, D_MODEL), lambda b, ph: (q_tile(b, ph), 0)),
        out_shape=jax.ShapeDtypeStruct((n, D_MODEL), F32),
        scratch_shapes=scratch,
        compiler_params=_params("arbitrary", "arbitrary"),
        name=f"sample_layer{layer}",
    )(*args)


def _rope_tables():
    nf = DH // 4
    t = jnp.arange(DEC_SEQ)
    row = (t // GRID_W).astype(F32)
    col = (t % GRID_W).astype(F32)
    inv = ROPE_THETA ** (-jnp.arange(nf, dtype=F32) / nf)
    d = jnp.arange(DH)
    axis = d // (2 * nf)
    second = (d % (2 * nf)) // nf
    f = d % nf
    pos = jnp.where(axis[None, :] == 0, row[:, None], col[:, None])
    ang = pos * inv[f][None, :]
    cos = jnp.cos(ang)
    sin = jnp.where(second[None, :] == 0, -jnp.sin(ang), jnp.sin(ang))
    reps = B_W // DH
    cos = jnp.tile(cos, (1, reps))
    sin = jnp.tile(sin, (1, reps))
    return cos, sin, cos.T, sin.T


def kernel(x_prompt, x_sample, cache_b_k, cache_b_v, cache_c_k, cache_c_v, cache_d_k, cache_d_v, c, c_ctx, norm_pre, norm_post, w_mod, b_mod, w_in_even, a_conv_w, a_conv_b, a_ln_g, a_ln_b, b_lambda, b_subln_g, w_out_even, w_in_odd, c_q_norm, c_k_norm, d_sink, w_out_odd):
    n_even = (DEPTH + 1) // 2
    n_odd = DEPTH // 2
    cond8 = jnp.zeros((SUBLANES, D_MODEL), F32).at[0].set(c_ctx).at[1:1 + DEC_BATCH].set(c)
    mod4 = _modulation(cond8, w_mod, b_mod).reshape(DEPTH, SUBLANES, 1, 3 * D_MODEL)
    tables = _rope_tables()

    xp = x_prompt.reshape(BATCH * SEQ, D_MODEL)
    xs = x_sample.reshape(DEC_BATCH * DEC_SEQ, D_MODEL)
    feat = lambda a, w: jnp.moveaxis(a.reshape(a.shape[:3] + (w,)), 2, 3)
    cbk = feat(cache_b_k, B_W)
    cbv = cache_b_v.reshape(DEC_BATCH, n_even, PAST_LEN * H_B, 2 * DH)
    cck = feat(cache_c_k, KV_W)
    ccv = feat(cache_c_v, KV_W)
    cdk = feat(cache_d_k, KV_W)
    cdv = feat(cache_d_v, KV_W)

    g_pre = norm_pre.reshape(DEPTH, 1, D_MODEL)
    g_post = norm_post.reshape(DEPTH, 1, D_MODEL)
    w_in_e = w_in_even.astype(BF16)
    w_out_e = w_out_even.astype(BF16)
    w_in_o = w_in_odd.astype(BF16)
    w_out_o = w_out_odd.astype(BF16)
    conv = (jnp.zeros((n_even, 4 * SUBLANES, A_W), F32).at[:, :CONV_K].set(a_conv_w),
            a_conv_b.reshape(n_even, 1, A_W), a_ln_g.reshape(n_even, 1, A_W), a_ln_b.reshape(n_even, 1, A_W))
    subln = b_subln_g.reshape(n_even, 1, 2 * DH)
    qkn = (jnp.tile(c_q_norm, (1, C_W // DH)).reshape(n_odd, C_W, 1),
           jnp.tile(c_k_norm, (1, KV_W // DH)).reshape(n_odd, 1, KV_W))
    sink = d_sink.reshape(n_odd, 1, 8)

    new_even, new_odd = (), ()
    for l in range(DEPTH):
        if l % 2 == 0:
            lam_init = 0.8 - 0.6 * math.exp(-0.3 * l)
            w_in, w_out = w_in_e, w_out_e
            xp, *new_even = _prompt_layer(xp, l, g_pre, g_post, mod4, w_in, w_out, conv,
                                          (b_lambda, subln), new_even, lam_init)
            xs = _sample_layer(xs, l, g_pre, g_post, mod4, w_in, w_out, conv, (b_lambda, subln),
                               (cbk, cbv), tables, lam_init)
        else:
            w_in, w_out = w_in_o, w_out_o
            xp, *new_odd = _prompt_layer(xp, l, g_pre, g_post, mod4, w_in, w_out, qkn, (sink,), new_odd)
            xs = _sample_layer(xs, l, g_pre, g_post, mod4, w_in, w_out, qkn, (sink,),
                               (cck, ccv, cdk, cdv), tables)

    def token_major(a, heads):
        return jnp.moveaxis(a.reshape(a.shape[:2] + heads + (DH, SEQ)), -1, 2)

    kt, v = new_even
    ckt, cvt, dkt, dvt = new_odd
    return (xp.reshape(BATCH, SEQ, D_MODEL), xs.reshape(DEC_BATCH, DEC_SEQ, D_MODEL),
            token_major(kt, (H_B, 2)), v.reshape(BATCH, n_even, SEQ, H_B, 2 * DH),
            token_major(ckt, (2,)), token_major(cvt, (2,)), token_major(dkt, (2,)), token_major(dvt, (2,)))
```

```python
import functools
import math

import jax
import jax.numpy as jnp
from jax import lax
from jax.experimental import pallas as pl
from jax.experimental.pallas import tpu as pltpu

F32 = jnp.float32
BF16 = jnp.bfloat16

D_MODEL = 1024
BATCH = 16
SEQ = 256
DEPTH = 4
DEC_BATCH = 2
DEC_SEQ = 1024
PAST_LEN = 512
GRID_W = 64
ROPE_THETA = 10000.0
NORM_EPS = 1e-6
DH = 64
A_W = 512
CONV_K = 31
H_B = 4
B_W = 512
C_W = 512
KV_W = 128
D_W = 512
WINDOW = 128
LOG2E = math.log2(math.e)
QK_SCALE = DH ** -0.5 * LOG2E

LANES = 128
SUBLANES = 8
VMEM_LIMIT = 56 * 1024 * 1024

TM = 512
TQ = SEQ
HALO = 16
GQA_AHEAD = (8, 4)
DIFF_AHEAD = (3, 2)
ROW_CHUNK = 64
DEN_ROWS = 16


def _params(*sem):
    return pltpu.CompilerParams(dimension_semantics=sem, vmem_limit_bytes=VMEM_LIMIT)


def _silu(x):
    return x * jax.nn.sigmoid(x)


def _dot(a, b):
    return jnp.dot(a, b, preferred_element_type=F32)


def _pick(stacked, idx):
    return pl.BlockSpec((None,) + stacked.shape[1:], lambda *_: (idx,) + (0,) * (stacked.ndim - 1))


def _mod_kernel(cond_ref, w_ref, b_ref, o_ref):
    a = _silu(cond_ref[...]).astype(BF16)
    o_ref[...] = _dot(a, w_ref[...].astype(BF16)) + b_ref[...]


def _modulation(cond8, w_mod, b_mod):
    return pl.pallas_call(
        _mod_kernel,
        grid=(DEPTH,),
        in_specs=[
            pl.BlockSpec((SUBLANES, D_MODEL), lambda l: (0, 0)),
            pl.BlockSpec((None, D_MODEL, 3 * D_MODEL), lambda l: (l, 0, 0)),
            pl.BlockSpec((None, 1, 3 * D_MODEL), lambda l: (l, 0, 0)),
        ],
        out_specs=pl.BlockSpec((None, SUBLANES, 3 * D_MODEL), lambda l: (l, 0, 0)),
        out_shape=jax.ShapeDtypeStruct((DEPTH, SUBLANES, 3 * D_MODEL), F32),
        compiler_params=_params("arbitrary"),
        name="modulation",
    )(cond8, w_mod, b_mod.reshape(DEPTH, 1, 3 * D_MODEL))


def _pre_norm(x_ref, g_ref, mod_ref):
    return _modulate(x_ref[...], g_ref, mod_ref)


def _modulate(x, g_ref, mod_ref):
    ms = jnp.mean(x * x, axis=-1, keepdims=True)
    mod = mod_ref[...]
    sh = mod[:, :D_MODEL]
    sc = mod[:, D_MODEL:2 * D_MODEL]
    h = (x * lax.rsqrt(ms + NORM_EPS)) * (g_ref[...] * (1.0 + sc)) + sh
    return h.astype(BF16)


def _rope(x, cos, sin_signed):
    w = x.shape[-1]
    lane = lax.broadcasted_iota(jnp.int32, (1, w), 1)
    first = (lane % 32) < 16
    partner = jnp.where(first, pltpu.roll(x, w - 16, 1), pltpu.roll(x, 16, 1))
    return x * cos + partner * sin_signed


def _rope_t(x, cos_t, sin_t):
    r = x.shape[0]
    row = lax.broadcasted_iota(jnp.int32, (r, 1), 0)
    first = (row % 32) < 16
    partner = jnp.where(first, pltpu.roll(x, r - 16, 0), pltpu.roll(x, 16, 0))
    return x * cos_t + partner * sin_t


def _store_chunks(ref, xt):
    for c in range(xt.shape[1] // LANES):
        ref[c] = xt[:, c * LANES:(c + 1) * LANES]


def _store_per_seq(ref, xt):
    for s in range(xt.shape[1] // SEQ):
        ref[s] = xt[:, s * SEQ:(s + 1) * SEQ]


def _own_slot(ref, slot, fill=True):
    if len(ref.shape) == 3:
        return ref
    for other in range(ref.shape[1]):
        if fill and other != slot:
            ref[:, other] = jnp.zeros((ref.shape[0],) + tuple(ref.shape[2:]), ref.dtype)
    return ref.at[:, slot]


def _glu(ug):
    return ug[:, :A_W] * jax.nn.sigmoid(ug[:, A_W:])


def _conv_mix(pad, az, cw_ref, cb_ref, lg_ref, lb_ref, sh_ref, acc_ref):
    rows = sh_ref.shape[1]
    for b in range(SUBLANES):
        sh_ref[b] = pad[b:b + rows]
    base = HALO - CONV_K // 2
    for c0 in range(0, A_W, LANES):
        cs = slice(c0, c0 + LANES)
        for r0 in range(0, SEQ, ROW_CHUNK):
            acc = jnp.zeros((ROW_CHUNK, LANES), F32) + cb_ref[:, cs]
            for k in range(CONV_K):
                j = k + base
                s = r0 + (j // SUBLANES) * SUBLANES
                acc = acc + sh_ref[j % SUBLANES, s:s + ROW_CHUNK, cs] * cw_ref[k:k + 1, cs]
            acc_ref[r0:r0 + ROW_CHUNK, cs] = acc
    a = acc_ref[...]
    mu = jnp.mean(a, axis=-1, keepdims=True)
    d = a - mu
    var = jnp.mean(d * d, axis=-1, keepdims=True)
    y = d * lax.rsqrt(var + NORM_EPS) * lg_ref[...] + lb_ref[...]
    return _silu(y) * _silu(az)


def _in_even_kernel(x_ref, g_ref, mod_ref, w_ref, cw_ref, cb_ref, lg_ref, lb_ref, *rest,
                    sample, n_alias=0, slot=0, pos=None):
    if sample:
        (xp_ref, xn_ref, cos_ref, sin_ref, cost_ref, sint_ref,
         ma_ref, qt_ref, k_ref, vt_ref, bz_ref, sh_ref, acc_ref) = rest
    else:
        ma_ref, qt_ref, k_ref, kt_ref, v_ref, vt_ref, bz_ref, sh_ref, acc_ref = rest[n_alias:]
        kt_ref = _own_slot(kt_ref, slot)
        v_ref = _own_slot(v_ref, slot)
    hb = _pre_norm(x_ref, g_ref, mod_ref)
    a = _glu(_dot(hb, w_ref[:, 0:2 * A_W]))
    az = _dot(hb, w_ref[:, 2 * A_W:3 * A_W])
    n_sub = TM // SEQ
    if sample:
        tiles_per_seq = DEC_SEQ // TM
        xh = jnp.concatenate([xp_ref[...], xn_ref[...]], axis=0)
        ah = _glu(_dot(_modulate(xh, g_ref, mod_ref), w_ref[:, 0:2 * A_W]))
        prev = jnp.where(pos != 0, ah[:HALO], 0.0)
        nxt = jnp.where(pos != tiles_per_seq - 1, ah[HALO:], 0.0)
        full = jnp.concatenate([prev, a, nxt], axis=0)
        pads = [full[j * SEQ:(j + 1) * SEQ + 2 * HALO] for j in range(n_sub)]
    else:
        zeros = jnp.zeros((HALO, A_W), F32)
        pads = [jnp.concatenate([zeros, a[j * SEQ:(j + 1) * SEQ], zeros], axis=0) for j in range(n_sub)]
    o = 3 * A_W
    q = _dot(hb, w_ref[:, o:o + B_W])
    k = _dot(hb, w_ref[:, o + B_W:o + 2 * B_W])
    v = _dot(hb, w_ref[:, o + 2 * B_W:o + 3 * B_W])
    bz_ref[...] = _dot(hb, w_ref[:, o + 3 * B_W:o + 4 * B_W])
    qt = q.T
    if sample:
        qt = _rope_t(qt, cost_ref[...], sint_ref[...])
        k = _rope(k, cos_ref[...], sin_ref[...])
    else:
        _store_per_seq(kt_ref, k.T)
        for s in range(TM // SEQ):
            for h in range(H_B):
                v_ref[s, pl.ds(h, SEQ, stride=H_B), :] = v[s * SEQ:(s + 1) * SEQ, h * LANES:(h + 1) * LANES]
    _store_per_seq(qt_ref, (qt * QK_SCALE).astype(BF16))
    k_ref[...] = k.astype(BF16)
    _store_per_seq(vt_ref, v.T.astype(BF16))
    for j, pad in enumerate(pads):
        rs = slice(j * SEQ, (j + 1) * SEQ)
        ma_ref[rs, :] = _conv_mix(pad, az[rs], cw_ref, cb_ref, lg_ref, lb_ref, sh_ref, acc_ref).astype(BF16)


def _group_mean_sq(x):
    width = x.shape[-1]
    xx = x * x
    hi = xx.astype(BF16)
    lo = (xx - hi.astype(F32)).astype(BF16)
    r = lax.broadcasted_iota(jnp.int32, (width, width), 0) // DH
    c = lax.broadcasted_iota(jnp.int32, (width, width), 1) // DH
    g = jnp.where(r == c, 1.0, 0.0).astype(BF16)
    return (_dot(hi, g) + _dot(lo, g)) * (1.0 / DH)


def _head_rms_t(xt, gain_col):
    parts = []
    for j in range(xt.shape[0] // DH):
        blk = xt[j * DH:(j + 1) * DH]
        ms = jnp.mean(blk * blk, axis=0, keepdims=True)
        parts.append(blk * lax.rsqrt(ms + NORM_EPS))
    return jnp.concatenate(parts, axis=0) * gain_col


def _in_odd_kernel(x_ref, g_ref, mod_ref, w_ref, qn_ref, kn_ref, *rest, sample, n_alias=0, slot=0):
    if sample:
        (cos_ref, sin_ref, cost_ref, sint_ref,
         cqt_ref, ck_ref, cvt_ref, cz_ref, dqt_ref, dk_ref, dvt_ref, dz_ref) = rest
    else:
        (cqt_ref, ck_ref, ckt_ref, cvt_ref, cz_ref,
         dqt_ref, dk_ref, dkt_ref, dvt_ref, dz_ref) = rest[n_alias:]
        ckt_ref, cvt_ref, dkt_ref, dvt_ref = [_own_slot(r, slot) for r in (ckt_ref, cvt_ref, dkt_ref, dvt_ref)]
    hb = _pre_norm(x_ref, g_ref, mod_ref)
    y = _dot(hb, w_ref[...])
    o = 0
    cqt = _head_rms_t(y[:, o:o + C_W].T, qn_ref[...])
    o += C_W
    ck = y[:, o:o + KV_W]
    ck = ck * lax.rsqrt(_group_mean_sq(ck) + NORM_EPS) * kn_ref[...]
    o += KV_W
    cvt = y[:, o:o + KV_W].T
    o += KV_W
    cz_ref[...] = y[:, o:o + C_W]
    o += C_W
    dqt = y[:, o:o + D_W].T
    o += D_W
    dk = y[:, o:o + KV_W]
    o += KV_W
    dvt = y[:, o:o + KV_W].T
    o += KV_W
    dz_ref[...] = y[:, o:o + D_W]
    if sample:
        cos_t = cost_ref[...]
        sin_t = sint_ref[...]
        cqt = _rope_t(cqt, cos_t, sin_t)
        dqt = _rope_t(dqt, cos_t, sin_t)
        cos = cos_ref[...][:, :KV_W]
        sin = sin_ref[...][:, :KV_W]
        ck = _rope(ck, cos, sin)
        dk = _rope(dk, cos, sin)
        _store_chunks(cvt_ref, cvt.astype(BF16))
        _store_chunks(dvt_ref, dvt.astype(BF16))
    else:
        _store_per_seq(ckt_ref, ck.T)
        _store_per_seq(dkt_ref, dk.T)
        _store_per_seq(cvt_ref, cvt)
        _store_per_seq(dvt_ref, dvt)
    _store_per_seq(cqt_ref, (cqt * QK_SCALE).astype(BF16))
    _store_per_seq(dqt_ref, (dqt * QK_SCALE).astype(BF16))
    ck_ref[...] = ck.astype(BF16)
    dk_ref[...] = dk.astype(BF16)


def _exp_terms(segs, extra=None):
    m = None
    for s in segs:
        mi = jnp.max(s, axis=0, keepdims=True)
        m = mi if m is None else jnp.maximum(m, mi)
    if extra is not None:
        extra = extra * LOG2E
        m = jnp.maximum(m, extra)
    es = [jnp.exp2(s - m) for s in segs]
    return es, (None if extra is None else jnp.exp2(extra - m))


def _softmax_t(segs):
    es, _ = _exp_terms(segs)
    den = None
    for e in es:
        di = jnp.sum(e, axis=0, keepdims=True)
        den = di if den is None else den + di
    return es, den


def _keep_rows(xt, lo, hi):
    zeros = lambda r: jnp.zeros((r, xt.shape[1]), xt.dtype)
    parts = []
    if lo > 0:
        parts.append(zeros(lo))
    parts.append(xt[lo:hi])
    if hi < xt.shape[0]:
        parts.append(zeros(xt.shape[0] - hi))
    return jnp.concatenate(parts, axis=0)


def _pipelined(n, scores, finish, ahead):
    ready = [scores(j) for j in range(min(ahead, n))]
    for j in range(n):
        if j + ahead < n:
            ready.append(scores(j + ahead))
        finish(j, ready.pop(0))


def _diff_attn(qt_ref, kvs, z_ref, lam_ref, g_ref, o_ref, *, lam_init):
    lv = lam_ref[...]
    lam = (jnp.exp(jnp.sum(lv[0:1] * lv[1:2], axis=-1, keepdims=True))
           - jnp.exp(jnp.sum(lv[2:3] * lv[3:4], axis=-1, keepdims=True)) + lam_init)

    def scores(h):
        cs = slice(h * LANES, (h + 1) * LANES)
        qt = qt_ref[cs, :]
        ks = [get_k(cs) for get_k, _ in kvs]
        return [[_dot(kk, _keep_rows(qt, c * DH, (c + 1) * DH)) for kk in ks] for c in range(2)]

    def finish(h, ss):
        cs = slice(h * LANES, (h + 1) * LANES)
        es0, den0 = _softmax_t(ss[0])
        es1, den1 = _softmax_t(ss[1])
        r0 = 1.0 / den0
        r1 = lam / den1
        ot = None
        for e0, e1, (_, get_vt) in zip(es0, es1, kvs):
            w = e0 * r0 - e1 * r1
            oi = _dot(get_vt(cs), w.astype(BF16))
            ot = oi if ot is None else ot + oi
        ms = jnp.mean(ot * ot, axis=0, keepdims=True)
        o = (ot * lax.rsqrt(ms + NORM_EPS)).T
        o = (o * g_ref[...]) * (1.0 - lam_init)
        o_ref[:, cs] = (o * _silu(z_ref[:, cs])).astype(o_ref.dtype)

    _pipelined(H_B, scores, finish, DIFF_AHEAD[len(kvs) - 1])


def _post_residual(o, x_ref, g_ref, mod_ref, o_ref):
    ms = jnp.mean(o * o, axis=-1, keepdims=True)
    r = o * lax.rsqrt(ms + NORM_EPS) * g_ref[...]
    gate = mod_ref[...][:, 2 * D_MODEL:]
    o_ref[...] = x_ref[...] + gate * r


def _diff_prompt_kernel(qt_ref, k_ref, vt_ref, z_ref, lam_ref, sg_ref, ma_ref, x_ref, w_ref, g_ref,
                        mod_ref, o_ref, mb_ref, *, lam_init):
    oa = _dot(ma_ref[...], w_ref[0:A_W, :])
    for s in range(qt_ref.shape[0]):
        rs = pl.ds(s * SEQ, SEQ)
        kv = (lambda cs, s=s: k_ref[s * SEQ:(s + 1) * SEQ, cs], lambda cs, s=s: vt_ref[s, cs, :])
        _diff_attn(qt_ref.at[s], [kv], z_ref.at[rs], lam_ref, sg_ref, mb_ref.at[rs], lam_init=lam_init)
    _post_residual(oa + _dot(mb_ref[...], w_ref[A_W:, :]), x_ref, g_ref, mod_ref, o_ref)


def _prep_diff_cache(ck_ref, cv_ref, ckb_ref, cvt_ref):
    ckb_ref[...] = ck_ref[...].T.astype(BF16)
    for h in range(H_B):
        cs = slice(h * LANES, (h + 1) * LANES)
        cvt_ref[cs, :] = cv_ref[pl.ds(h, PAST_LEN, stride=H_B), :].T.astype(BF16)


def _diff_sample_body(qt_ref, k_ref, vt_ref, z_ref, lam_ref, sg_ref, ma_ref, x_ref, w_ref, g_ref,
                      mod_ref, o_ref, ckb_ref, cvt_ref, mb_ref, *, lam_init):
    ctx = (lambda cs: ckb_ref[:, cs], lambda cs: cvt_ref[cs, :])
    loc = (lambda cs: k_ref[:, cs],
           lambda cs: jnp.concatenate([vt_ref[c, cs, :] for c in range(vt_ref.shape[0])], axis=1))
    oa = _dot(ma_ref[...], w_ref[0:A_W, :])
    _diff_attn(qt_ref, [ctx, loc], z_ref, lam_ref, sg_ref, mb_ref, lam_init=lam_init)
    _post_residual(oa + _dot(mb_ref[...], w_ref[A_W:, :]), x_ref, g_ref, mod_ref, o_ref)


def _gqa(qt_ref, segs, z_ref, o_ref, sink_ref=None):
    halves = []

    def scores(j):
        n = j // 4
        qj = qt_ref[j * DH:(j + 1) * DH, :]
        zero = jnp.zeros_like(qj)
        qz = jnp.concatenate([qj, zero] if n == 0 else [zero, qj], axis=0)
        return [_dot(k, qz) for k, _, _ in segs]

    def finish(j, ss):
        n = j // 4
        ss = [s if valid is None else jnp.where(valid, s, -jnp.inf)
              for s, (_, _, valid) in zip(ss, segs)]
        extra = None if sink_ref is None else sink_ref[:, j:j + 1]
        es, den = _exp_terms(ss, extra)
        ot = None
        for e, (_, vt, _) in zip(es, segs):
            ones = jnp.ones((DEN_ROWS, vt.shape[1]), BF16)
            vt1 = jnp.concatenate([vt[n * DH:(n + 1) * DH], ones], axis=0)
            oi = _dot(vt1, e.astype(BF16))
            ot = oi if ot is None else ot + oi
        den = ot[DH:DH + 1] if den is None else den + ot[DH:DH + 1]
        halves.append(ot[:DH] * (1.0 / den))
        if j % 2 == 1:
            cs = slice((j // 2) * LANES, (j // 2 + 1) * LANES)
            o_pair = jnp.concatenate(halves[-2:], axis=0).T
            o_ref[:, cs] = (o_pair * _silu(z_ref[:, cs])).astype(o_ref.dtype)

    _pipelined(2 * 4, scores, finish, GQA_AHEAD[len(segs) - 1])


def _gqa_prompt_kernel(cqt_ref, ck_ref, cvt_ref, cz_ref, dqt_ref, dk_ref, dvt_ref, dz_ref, sink_ref,
                       x_ref, w_ref, g_ref, mod_ref, o_ref, m_ref):
    for s in range(cqt_ref.shape[0]):
        rows = slice(s * SEQ, (s + 1) * SEQ)
        rs = pl.ds(s * SEQ, SEQ)
        seg = lambda k_ref, vt_ref: (k_ref[rows, :], vt_ref[s].astype(BF16), None)
        _gqa(cqt_ref.at[s], [seg(ck_ref, cvt_ref)], cz_ref.at[rs], m_ref.at[rs, pl.ds(0, C_W)])
        _gqa(dqt_ref.at[s], [seg(dk_ref, dvt_ref)], dz_ref.at[rs], m_ref.at[rs, pl.ds(C_W, D_W)], sink_ref)
    _post_residual(_dot(m_ref[...], w_ref[...]), x_ref, g_ref, mod_ref, o_ref)


def _gqa_sample_body(cqt_ref, ck_ref, cvt_ref, cck_ref, ccv_ref, cz_ref,
                     dqt_ref, dk_ref, dvt_ref, cdk_ref, cdv_ref, dz_ref, sink_ref,
                     x_ref, w_ref, g_ref, mod_ref, o_ref, m_ref, *, t):
    oc_ref = m_ref.at[:, pl.ds(0, C_W)]
    od_ref = m_ref.at[:, pl.ds(C_W, D_W)]
    ctx = lambda kt_ref, vt_ref: (kt_ref[...].T.astype(BF16), vt_ref[...].astype(BF16), None)
    n_chunks = DEC_SEQ // LANES
    cvt = jnp.concatenate([cvt_ref[c] for c in range(n_chunks)], axis=1)
    _gqa(cqt_ref, [ctx(cck_ref, ccv_ref), (ck_ref[...], cvt, None)], cz_ref, oc_ref)
    span = 2 * TQ
    t0 = t * TQ
    ws = pl.multiple_of(jnp.clip(t0 - WINDOW, 0, DEC_SEQ - span), WINDOW)
    kpos = ws + lax.broadcasted_iota(jnp.int32, (span, TQ), 0)
    qpos = t0 + lax.broadcasted_iota(jnp.int32, (span, TQ), 1)
    valid = jnp.abs(qpos - kpos) <= WINDOW
    c0 = ws // LANES
    dvt = jnp.concatenate([dvt_ref[c0 + c] for c in range(span // LANES)], axis=1)
    _gqa(dqt_ref, [ctx(cdk_ref, cdv_ref), (dk_ref[pl.ds(ws, span), :], dvt, valid)],
         dz_ref, od_ref, sink_ref)
    _post_residual(_dot(m_ref[...], w_ref[...]), x_ref, g_ref, mod_ref, o_ref)


def _cast_next_weights(rest, n_cast, n_alias, n_out):
    srcs = rest[:n_cast]
    rest = rest[n_cast + n_alias:]
    dsts = rest[n_out:n_out + n_cast]
    for src, dst in zip(srcs, dsts):
        dst[...] = src[...].astype(BF16)
    return rest[:n_out] + rest[n_out + n_cast:]


def _prompt_even_kernel(x_ref, g_ref, mod_ref, w_ref, cw_ref, cb_ref, lg_ref, lb_ref, lam_ref, sg_ref,
                        wo_ref, gp_ref, *rest, lam_init, n_alias, slot, n_cast):
    rest = _cast_next_weights(rest, n_cast, n_alias, n_out=3)
    (o_ref, kt_ref, v_ref,
     ma_ref, qt_ref, k_ref, vt_ref, bz_ref, sh_ref, acc_ref, mb_ref) = rest
    _in_even_kernel(x_ref, g_ref, mod_ref, w_ref, cw_ref, cb_ref, lg_ref, lb_ref,
                    ma_ref, qt_ref, k_ref, kt_ref, v_ref, vt_ref, bz_ref, sh_ref, acc_ref,
                    sample=False, slot=slot)
    _diff_prompt_kernel(qt_ref, k_ref, vt_ref, bz_ref, lam_ref, sg_ref, ma_ref, x_ref, wo_ref, gp_ref,
                        mod_ref, o_ref, mb_ref, lam_init=lam_init)


def _prompt_odd_kernel(x_ref, g_ref, mod_ref, w_ref, qn_ref, kn_ref, sink_ref, wo_ref, gp_ref, *rest,
                       n_alias, slot, n_cast):
    rest = _cast_next_weights(rest, n_cast, n_alias, n_out=5)
    (o_ref, ckt_ref, cvt_ref, dkt_ref, dvt_ref,
     cqt_ref, ck_ref, cz_ref, dqt_ref, dk_ref, dz_ref, m_ref) = rest
    _in_odd_kernel(x_ref, g_ref, mod_ref, w_ref, qn_ref, kn_ref,
                   cqt_ref, ck_ref, ckt_ref, cvt_ref, cz_ref, dqt_ref, dk_ref, dkt_ref, dvt_ref, dz_ref,
                   sample=False, slot=slot)
    _gqa_prompt_kernel(cqt_ref, ck_ref, _own_slot(cvt_ref, slot, fill=False), cz_ref,
                       dqt_ref, dk_ref, _own_slot(dvt_ref, slot, fill=False), dz_ref, sink_ref,
                       x_ref, wo_ref, gp_ref, mod_ref, o_ref, m_ref)


def _prompt_layer(x, layer, g_pre, g_post, mod4, w_in, w_out, w_idx, head, tail, carry, lam_init=None,
                  cast_next=None):
    n = x.shape[0]
    even = layer % 2 == 0
    row = lambda i: (i, 0)
    n_seq = TM // SEQ
    params = (g_pre, w_in) + tuple(head) + tuple(tail) + (w_out, g_post)
    idx = (layer, w_idx) + (layer // 2,) * (len(head) + len(tail)) + (w_idx, layer)
    specs = [_pick(p, i) for p, i in zip(params, idx)]
    in_specs = [pl.BlockSpec((TM, D_MODEL), row), specs[0],
                pl.BlockSpec((None, None, 1, 3 * D_MODEL), lambda i: (layer, 0, 0, 0))] + specs[1:]
    args = [x, g_pre, mod4, w_in] + list(head) + list(tail) + [w_out, g_post]
    n_layers = (DEPTH + 1 - layer % 2) // 2
    if carry:
        cache_spec = lambda r, c: pl.BlockSpec((n_seq, None, r, c), lambda i: (i, layer // 2, 0, 0))
    else:
        cache_spec = lambda r, c: pl.BlockSpec((n_seq, n_layers, r, c), lambda i: (i, 0, 0, 0))
    cache = lambda r, c: (cache_spec(r, c), jax.ShapeDtypeStruct((n // SEQ, n_layers, r, c), F32))
    outs = [(pl.BlockSpec((TM, D_MODEL), row), jax.ShapeDtypeStruct((n, D_MODEL), F32))]
    wide = pltpu.VMEM((TM, B_W), BF16)
    slab = pltpu.VMEM((n_seq, B_W, SEQ), BF16)
    if even:
        outs += [cache(B_W, SEQ), cache(SEQ * H_B, LANES)]
        scratch = [wide, slab, wide, slab, pltpu.VMEM((TM, B_W), F32),
                   pltpu.VMEM((SUBLANES, SEQ + 3 * SUBLANES, A_W), F32), pltpu.VMEM((SEQ, A_W), F32), wide]
        body = functools.partial(_prompt_even_kernel, lam_init=lam_init)
    else:
        outs += [cache(KV_W, SEQ)] * 4
        narrow = pltpu.VMEM((TM, KV_W), BF16)
        gate = pltpu.VMEM((TM, C_W), F32)
        scratch = [slab, narrow, gate, slab, narrow, gate, pltpu.VMEM((TM, C_W + D_W), BF16)]
        body = _prompt_odd_kernel
    n_cast = 0
    if cast_next is not None:
        *srcs, src_idx = cast_next
        n_cast = len(srcs)
        chunk = D_MODEL // (n // TM)
        for src in srcs:
            in_specs.append(pl.BlockSpec((None, chunk, src.shape[2]), lambda i: (src_idx, i, 0)))
            args.append(src)
            outs.append((pl.BlockSpec((None, chunk, src.shape[2]), lambda i: (0, i, 0)),
                         jax.ShapeDtypeStruct((1,) + src.shape[1:], BF16)))
    aliases = {}
    for j, a in enumerate(carry):
        aliases[len(args)] = 1 + j
        in_specs.append(pl.BlockSpec(memory_space=pl.ANY))
        args.append(a)
    return pl.pallas_call(
        functools.partial(body, n_alias=len(carry), slot=layer // 2, n_cast=n_cast),
        grid=(n // TM,),
        in_specs=in_specs,
        out_specs=[o[0] for o in outs],
        out_shape=[o[1] for o in outs],
        scratch_shapes=scratch,
        input_output_aliases=aliases,
        compiler_params=_params("arbitrary"),
        name=f"prompt_layer{layer}",
    )(*args)


N_IN = DEC_SEQ // TM
N_Q = DEC_SEQ // TQ


def _rows(ref, start, size):
    return ref.at[pl.ds(pl.multiple_of(start, size), size)]


def _sample_even_kernel(x_ref, xp_ref, xn_ref, g_ref, mod_ref, w_ref, cw_ref, cb_ref, lg_ref, lb_ref,
                        cos_ref, sin_ref, cost_ref, sint_ref, ck_ref, cv_ref, lam_ref, sg_ref,
                        xr_ref, wo_ref, gp_ref, o_ref,
                        ma_ref, qt_ref, k_ref, vt_ref, bz_ref, sh_ref, acc_ref, ckb_ref, cvt_ref, mb_ref,
                        *, lam_init):
    ph = pl.program_id(1)

    @pl.when(ph == 0)
    def _():
        _prep_diff_cache(ck_ref, cv_ref, ckb_ref, cvt_ref)

    @pl.when(ph < N_IN)
    def _():
        per = TM // SEQ
        _in_even_kernel(x_ref, g_ref, mod_ref, w_ref, cw_ref, cb_ref, lg_ref, lb_ref,
                        xp_ref, xn_ref, cos_ref, sin_ref, cost_ref, sint_ref,
                        _rows(ma_ref, ph * TM, TM), qt_ref.at[pl.ds(ph * per, per)],
                        _rows(k_ref, ph * TM, TM), vt_ref.at[pl.ds(ph * per, per)],
                        _rows(bz_ref, ph * TM, TM), sh_ref, acc_ref, sample=True, pos=ph)

    @pl.when(ph >= N_IN)
    def _():
        t = ph - N_IN
        _diff_sample_body(qt_ref.at[t], k_ref, vt_ref, _rows(bz_ref, t * TQ, TQ), lam_ref, sg_ref,
                          _rows(ma_ref, t * TQ, TQ), xr_ref, wo_ref, gp_ref, mod_ref, o_ref,
                          ckb_ref, cvt_ref, mb_ref, lam_init=lam_init)


def _sample_odd_kernel(x_ref, g_ref, mod_ref, w_ref, qn_ref, kn_ref,
                       cos_ref, sin_ref, cost_ref, sint_ref, cck_ref, ccv_ref, cdk_ref, cdv_ref, sink_ref,
                       xr_ref, wo_ref, gp_ref, o_ref,
                       cqt_ref, ck_ref, cvt_ref, cz_ref, dqt_ref, dk_ref, dvt_ref, dz_ref, m_ref):
    ph = pl.program_id(1)

    @pl.when(ph < N_IN)
    def _():
        per = TM // SEQ
        chunks = TM // LANES
        _in_odd_kernel(x_ref, g_ref, mod_ref, w_ref, qn_ref, kn_ref,
                       cos_ref, sin_ref, cost_ref, sint_ref,
                       cqt_ref.at[pl.ds(ph * per, per)], _rows(ck_ref, ph * TM, TM),
                       cvt_ref.at[pl.ds(ph * chunks, chunks)], _rows(cz_ref, ph * TM, TM),
                       dqt_ref.at[pl.ds(ph * per, per)], _rows(dk_ref, ph * TM, TM),
                       dvt_ref.at[pl.ds(ph * chunks, chunks)], _rows(dz_ref, ph * TM, TM), sample=True)

    @pl.when(ph >= N_IN)
    def _():
        t = ph - N_IN
        _gqa_sample_body(cqt_ref.at[t], ck_ref, cvt_ref, cck_ref, ccv_ref, _rows(cz_ref, t * TQ, TQ),
                         dqt_ref.at[t], dk_ref, dvt_ref, cdk_ref, cdv_ref, _rows(dz_ref, t * TQ, TQ),
                         sink_ref, xr_ref, wo_ref, gp_ref, mod_ref, o_ref, m_ref, t=t)


def _sample_layer(x, layer, g_pre, g_post, mod4, w_in, w_out, w_idx, head, tail, caches, tables,
                  lam_init=None):
    n = x.shape[0]
    even = layer % 2 == 0
    in_tile = lambda b, ph: b * N_IN + jnp.minimum(ph, N_IN - 1)
    q_tile = lambda b, ph: b * N_Q + jnp.maximum(ph - N_IN, 0)
    tab = lambda b, ph: jnp.minimum(ph, N_IN - 1)
    cos, sin, cos_t, sin_t = tables
    x_spec = pl.BlockSpec((TM, D_MODEL), lambda b, ph: (in_tile(b, ph), 0))
    mod_spec = pl.BlockSpec((None, None, 1, 3 * D_MODEL), lambda b, ph: (layer, 1 + b, 0, 0))
    table_specs = [pl.BlockSpec((TM, B_W), lambda b, ph: (tab(b, ph), 0)),
                   pl.BlockSpec((TM, B_W), lambda b, ph: (tab(b, ph), 0)),
                   pl.BlockSpec((B_W, TM), lambda b, ph: (0, tab(b, ph))),
                   pl.BlockSpec((B_W, TM), lambda b, ph: (0, tab(b, ph)))]
    cache_specs = [pl.BlockSpec((None, None) + c.shape[2:], lambda b, ph: (b, layer // 2, 0, 0))
                   for c in caches]
    res_spec = pl.BlockSpec((TQ, D_MODEL), lambda b, ph: (q_tile(b, ph), 0))
    in_specs = [x_spec]
    args = [x]
    if even:
        hb = TM // HALO
        last = n // HALO - 1
        in_specs += [pl.BlockSpec((HALO, D_MODEL), lambda b, ph: (jnp.maximum(in_tile(b, ph) * hb - 1, 0), 0)),
                     pl.BlockSpec((HALO, D_MODEL),
                                  lambda b, ph: (jnp.minimum((in_tile(b, ph) + 1) * hb, last), 0))]
        args += [x, x]
    in_specs += [_pick(g_pre, layer), mod_spec, _pick(w_in, w_idx)] + [_pick(h, layer // 2) for h in head]
    args += [g_pre, mod4, w_in] + list(head)
    in_specs += table_specs + cache_specs + [_pick(t, layer // 2) for t in tail]
    args += [cos, sin, cos_t, sin_t] + list(caches) + list(tail)
    in_specs += [res_spec, _pick(w_out, w_idx), _pick(g_post, layer)]
    args += [x, w_out, g_post]
    seq_wide = lambda w, dt: pltpu.VMEM((DEC_SEQ, w), dt)
    slab = pltpu.VMEM((DEC_SEQ // SEQ, B_W, SEQ), BF16)
    if even:
        scratch = [seq_wide(A_W, BF16), slab, seq_wide(B_W, BF16), slab, seq_wide(B_W, F32),
                   pltpu.VMEM((SUBLANES, SEQ + 3 * SUBLANES, A_W), F32), pltpu.VMEM((SEQ, A_W), F32),
                   pltpu.VMEM((PAST_LEN, B_W), BF16), pltpu.VMEM((B_W, PAST_LEN), BF16),
                   pltpu.VMEM((TQ, B_W), BF16)]
        body = functools.partial(_sample_even_kernel, lam_init=lam_init)
    else:
        chunk = pltpu.VMEM((DEC_SEQ // LANES, KV_W, LANES), BF16)
        scratch = [slab, seq_wide(KV_W, BF16), chunk, seq_wide(C_W, F32),
                   slab, seq_wide(KV_W, BF16), chunk, seq_wide(D_W, F32),
                   pltpu.VMEM((TQ, C_W + D_W), BF16)]
        body = _sample_odd_kernel
    return pl.pallas_call(
        body,
        grid=(DEC_BATCH, N_IN + N_Q),
        in_specs=in_specs,
        out_specs=pl.BlockSpec((TQ, D_MODEL), lambda b, ph: (q_tile(b, ph), 0)),
        out_shape=jax.ShapeDtypeStruct((n, D_MODEL), F32),
        scratch_shapes=scratch,
        compiler_params=_params("arbitrary", "arbitrary"),
        name=f"sample_layer{layer}",
    )(*args)


def _rope_tables():
    nf = DH // 4
    t = jnp.arange(DEC_SEQ)
    row = (t // GRID_W).astype(F32)
    col = (t % GRID_W).astype(F32)
    inv = ROPE_THETA ** (-jnp.arange(nf, dtype=F32) / nf)
    d = jnp.arange(DH)
    axis = d // (2 * nf)
    second = (d % (2 * nf)) // nf
    f = d % nf
    pos = jnp.where(axis[None, :] == 0, row[:, None], col[:, None])
    ang = pos * inv[f][None, :]
    cos = jnp.cos(ang)
    sin = jnp.where(second[None, :] == 0, -jnp.sin(ang), jnp.sin(ang))
    reps = B_W // DH
    cos = jnp.tile(cos, (1, reps))
    sin = jnp.tile(sin, (1, reps))
    return cos, sin, cos.T, sin.T


def kernel(x_prompt, x_sample, cache_b_k, cache_b_v, cache_c_k, cache_c_v, cache_d_k, cache_d_v, c, c_ctx, norm_pre, norm_post, w_mod, b_mod, w_in_even, a_conv_w, a_conv_b, a_ln_g, a_ln_b, b_lambda, b_subln_g, w_out_even, w_in_odd, c_q_norm, c_k_norm, d_sink, w_out_odd):
    n_even = (DEPTH + 1) // 2
    n_odd = DEPTH // 2
    cond8 = jnp.zeros((SUBLANES, D_MODEL), F32).at[0].set(c_ctx).at[1:1 + DEC_BATCH].set(c)
    mod4 = _modulation(cond8, w_mod, b_mod).reshape(DEPTH, SUBLANES, 1, 3 * D_MODEL)
    tables = _rope_tables()

    xp = x_prompt.reshape(BATCH * SEQ, D_MODEL)
    xs = x_sample.reshape(DEC_BATCH * DEC_SEQ, D_MODEL)
    feat = lambda a, w: jnp.moveaxis(a.reshape(a.shape[:3] + (w,)), 2, 3)
    cbk = feat(cache_b_k, B_W)
    cbv = cache_b_v.reshape(DEC_BATCH, n_even, PAST_LEN * H_B, 2 * DH)
    cck = feat(cache_c_k, KV_W)
    ccv = feat(cache_c_v, KV_W)
    cdk = feat(cache_d_k, KV_W)
    cdv = feat(cache_d_v, KV_W)

    g_pre = norm_pre.reshape(DEPTH, 1, D_MODEL)
    g_post = norm_post.reshape(DEPTH, 1, D_MODEL)
    conv = (jnp.zeros((n_even, 4 * SUBLANES, A_W), F32).at[:, :CONV_K].set(a_conv_w),
            a_conv_b.reshape(n_even, 1, A_W), a_ln_g.reshape(n_even, 1, A_W), a_ln_b.reshape(n_even, 1, A_W))
    subln = b_subln_g.reshape(n_even, 1, 2 * DH)
    qkn = (jnp.tile(c_q_norm, (1, C_W // DH)).reshape(n_odd, C_W, 1),
           jnp.tile(c_k_norm, (1, KV_W // DH)).reshape(n_odd, 1, KV_W))
    sink = d_sink.reshape(n_odd, 1, 8)

    weights = (w_in_even[:1].astype(BF16), w_out_even[:1].astype(BF16))
    f32_weights = ((w_in_even, w_out_even), (w_in_odd, w_out_odd))
    new_even, new_odd = (), ()
    for l in range(DEPTH):
        w_in, w_out = weights
        cast_next = f32_weights[(l + 1) % 2] + ((l + 1) // 2,) if l + 1 < DEPTH else None
        if l % 2 == 0:
            lam_init = 0.8 - 0.6 * math.exp(-0.3 * l)
            outs = _prompt_layer(xp, l, g_pre, g_post, mod4, w_in, w_out, 0, conv, (b_lambda, subln),
                                 new_even, lam_init, cast_next)
            xp, new_even, weights = outs[0], outs[1:3], outs[3:]
            xs = _sample_layer(xs, l, g_pre, g_post, mod4, w_in, w_out, 0, conv, (b_lambda, subln),
                               (cbk, cbv), tables, lam_init)
        else:
            outs = _prompt_layer(xp, l, g_pre, g_post, mod4, w_in, w_out, 0, qkn, (sink,), new_odd,
                                 None, cast_next)
            xp, new_odd, weights = outs[0], outs[1:5], outs[5:]
            xs = _sample_layer(xs, l, g_pre, g_post, mod4, w_in, w_out, 0, qkn, (sink,),
                               (cck, ccv, cdk, cdv), tables)

    def token_major(a, heads):
        return jnp.moveaxis(a.reshape(a.shape[:2] + heads + (DH, SEQ)), -1, 2)

    kt, v = new_even
    ckt, cvt, dkt, dvt = new_odd
    return (xp.reshape(BATCH, SEQ, D_MODEL), xs.reshape(DEC_BATCH, DEC_SEQ, D_MODEL),
            token_major(kt, (H_B, 2)), v.reshape(BATCH, n_even, SEQ, H_B, 2 * DH),
            token_major(ckt, (2,)), token_major(cvt, (2,)), token_major(dkt, (2,)), token_major(dvt, (2,)))
```

```python
import functools
import math

import jax
import jax.numpy as jnp
from jax import lax
from jax.experimental import pallas as pl
from jax.experimental.pallas import tpu as pltpu

F32 = jnp.float32
BF16 = jnp.bfloat16

D_MODEL = 1024
BATCH = 16
SEQ = 256
DEPTH = 4
DEC_BATCH = 2
DEC_SEQ = 1024
PAST_LEN = 512
GRID_W = 64
ROPE_THETA = 10000.0
NORM_EPS = 1e-6
DH = 64
A_W = 512
CONV_K = 31
H_B = 4
B_W = 512
C_W = 512
KV_W = 128
D_W = 512
WINDOW = 128
LOG2E = math.log2(math.e)
QK_SCALE = DH ** -0.5 * LOG2E

LANES = 128
SUBLANES = 8
VMEM_LIMIT = 56 * 1024 * 1024

TM = 512
TQ = SEQ
HALO = 16
GQA_AHEAD = (8, 4)
DIFF_AHEAD = (3, 2)
ROW_CHUNK = 64
DEN_ROWS = 16


def _params(*sem):
    return pltpu.CompilerParams(dimension_semantics=sem, vmem_limit_bytes=VMEM_LIMIT)


def _silu(x):
    return x * jax.nn.sigmoid(x)


def _dot(a, b):
    return jnp.dot(a, b, preferred_element_type=F32)


def _pick(stacked, idx):
    if stacked.ndim == 2:
        return _whole(stacked)
    return pl.BlockSpec((None,) + stacked.shape[1:], lambda *_: (idx,) + (0,) * (stacked.ndim - 1))


def _whole(a):
    return pl.BlockSpec(a.shape, lambda *_: (0,) * a.ndim)


def _layer_rows(layer, mod_row, g_ref, mod_ref, gp_ref, *half):
    one = lambda ref, i: ref.at[pl.ds(i, 1)]
    return ((one(g_ref, layer), mod_ref.at[layer, pl.ds(mod_row, 1)], one(gp_ref, layer))
            + tuple(one(r, layer // 2) for r in half))


def _mod_kernel(cond_ref, w_ref, b_ref, o_ref):
    a = _silu(cond_ref[...]).astype(BF16)
    o_ref[...] = _dot(a, w_ref[...].astype(BF16)) + b_ref[pl.ds(pl.program_id(0), 1), :]


def _modulation(cond8, w_mod, b_mod):
    return pl.pallas_call(
        _mod_kernel,
        grid=(DEPTH,),
        in_specs=[
            pl.BlockSpec((SUBLANES, D_MODEL), lambda l: (0, 0)),
            pl.BlockSpec((None, D_MODEL, 3 * D_MODEL), lambda l: (l, 0, 0)),
            _whole(b_mod),
        ],
        out_specs=pl.BlockSpec((None, SUBLANES, 3 * D_MODEL), lambda l: (l, 0, 0)),
        out_shape=jax.ShapeDtypeStruct((DEPTH, SUBLANES, 3 * D_MODEL), F32),
        compiler_params=_params("arbitrary"),
        name="modulation",
    )(cond8, w_mod, b_mod)


def _pre_norm(x_ref, g_ref, mod_ref):
    return _modulate(x_ref[...], g_ref, mod_ref)


def _modulate(x, g_ref, mod_ref):
    ms = jnp.mean(x * x, axis=-1, keepdims=True)
    mod = mod_ref[...]
    sh = mod[:, :D_MODEL]
    sc = mod[:, D_MODEL:2 * D_MODEL]
    h = (x * lax.rsqrt(ms + NORM_EPS)) * (g_ref[...] * (1.0 + sc)) + sh
    return h.astype(BF16)


def _rope(x, cos, sin_signed):
    w = x.shape[-1]
    lane = lax.broadcasted_iota(jnp.int32, (1, w), 1)
    first = (lane % 32) < 16
    partner = jnp.where(first, pltpu.roll(x, w - 16, 1), pltpu.roll(x, 16, 1))
    return x * cos + partner * sin_signed


def _rope_t(x, cos_t, sin_t):
    r = x.shape[0]
    row = lax.broadcasted_iota(jnp.int32, (r, 1), 0)
    first = (row % 32) < 16
    partner = jnp.where(first, pltpu.roll(x, r - 16, 0), pltpu.roll(x, 16, 0))
    return x * cos_t + partner * sin_t


def _store_chunks(ref, xt):
    for c in range(xt.shape[1] // LANES):
        ref[c] = xt[:, c * LANES:(c + 1) * LANES]


def _store_per_seq(ref, xt):
    for s in range(xt.shape[1] // SEQ):
        ref[s] = xt[:, s * SEQ:(s + 1) * SEQ]


def _own_slot(ref, slot, fill=True):
    if len(ref.shape) == 3:
        return ref
    for other in range(ref.shape[1]):
        if fill and other != slot:
            ref[:, other] = jnp.zeros((ref.shape[0],) + tuple(ref.shape[2:]), ref.dtype)
    return ref.at[:, slot]


def _glu(ug):
    return ug[:, :A_W] * jax.nn.sigmoid(ug[:, A_W:])


def _conv_mix(pad, az, cw_ref, cb_ref, lg_ref, lb_ref, sh_ref, acc_ref):
    rows = sh_ref.shape[1]
    for b in range(SUBLANES):
        sh_ref[b] = pad[b:b + rows]
    base = HALO - CONV_K // 2
    for c0 in range(0, A_W, LANES):
        cs = slice(c0, c0 + LANES)
        for r0 in range(0, SEQ, ROW_CHUNK):
            acc = jnp.zeros((ROW_CHUNK, LANES), F32) + cb_ref[:, cs]
            for k in range(CONV_K):
                j = k + base
                s = r0 + (j // SUBLANES) * SUBLANES
                acc = acc + sh_ref[j % SUBLANES, s:s + ROW_CHUNK, cs] * cw_ref[k:k + 1, cs]
            acc_ref[r0:r0 + ROW_CHUNK, cs] = acc
    a = acc_ref[...]
    mu = jnp.mean(a, axis=-1, keepdims=True)
    d = a - mu
    var = jnp.mean(d * d, axis=-1, keepdims=True)
    y = d * lax.rsqrt(var + NORM_EPS) * lg_ref[...] + lb_ref[...]
    return _silu(y) * _silu(az)


def _in_even_kernel(x_ref, g_ref, mod_ref, w_ref, cw_ref, cb_ref, lg_ref, lb_ref, *rest,
                    sample, n_alias=0, slot=0, pos=None):
    if sample:
        (xp_ref, xn_ref, cos_ref, sin_ref, cost_ref, sint_ref,
         ma_ref, qt_ref, k_ref, vt_ref, bz_ref, sh_ref, acc_ref) = rest
    else:
        ma_ref, qt_ref, k_ref, kt_ref, v_ref, vt_ref, bz_ref, sh_ref, acc_ref = rest[n_alias:]
        kt_ref = _own_slot(kt_ref, slot)
        v_ref = _own_slot(v_ref, slot)
    hb = _pre_norm(x_ref, g_ref, mod_ref)
    a = _glu(_dot(hb, w_ref[:, 0:2 * A_W]))
    az = _dot(hb, w_ref[:, 2 * A_W:3 * A_W])
    n_sub = TM // SEQ
    if sample:
        tiles_per_seq = DEC_SEQ // TM
        xh = jnp.concatenate([xp_ref[...], xn_ref[...]], axis=0)
        ah = _glu(_dot(_modulate(xh, g_ref, mod_ref), w_ref[:, 0:2 * A_W]))
        prev = jnp.where(pos != 0, ah[:HALO], 0.0)
        nxt = jnp.where(pos != tiles_per_seq - 1, ah[HALO:], 0.0)
        full = jnp.concatenate([prev, a, nxt], axis=0)
        pads = [full[j * SEQ:(j + 1) * SEQ + 2 * HALO] for j in range(n_sub)]
    else:
        zeros = jnp.zeros((HALO, A_W), F32)
        pads = [jnp.concatenate([zeros, a[j * SEQ:(j + 1) * SEQ], zeros], axis=0) for j in range(n_sub)]
    o = 3 * A_W
    q = _dot(hb, w_ref[:, o:o + B_W])
    k = _dot(hb, w_ref[:, o + B_W:o + 2 * B_W])
    v = _dot(hb, w_ref[:, o + 2 * B_W:o + 3 * B_W])
    bz_ref[...] = _dot(hb, w_ref[:, o + 3 * B_W:o + 4 * B_W])
    qt = q.T
    if sample:
        qt = _rope_t(qt, cost_ref[...], sint_ref[...])
        k = _rope(k, cos_ref[...], sin_ref[...])
    else:
        _store_per_seq(kt_ref, k.T)
        for s in range(TM // SEQ):
            for h in range(H_B):
                v_ref[s, pl.ds(h, SEQ, stride=H_B), :] = v[s * SEQ:(s + 1) * SEQ, h * LANES:(h + 1) * LANES]
    _store_per_seq(qt_ref, (qt * QK_SCALE).astype(BF16))
    k_ref[...] = k.astype(BF16)
    _store_per_seq(vt_ref, v.T.astype(BF16))
    for j, pad in enumerate(pads):
        rs = slice(j * SEQ, (j + 1) * SEQ)
        ma_ref[rs, :] = _conv_mix(pad, az[rs], cw_ref, cb_ref, lg_ref, lb_ref, sh_ref, acc_ref).astype(BF16)


def _group_mean_sq(x):
    width = x.shape[-1]
    xx = x * x
    hi = xx.astype(BF16)
    lo = (xx - hi.astype(F32)).astype(BF16)
    r = lax.broadcasted_iota(jnp.int32, (width, width), 0) // DH
    c = lax.broadcasted_iota(jnp.int32, (width, width), 1) // DH
    g = jnp.where(r == c, 1.0, 0.0).astype(BF16)
    return (_dot(hi, g) + _dot(lo, g)) * (1.0 / DH)


def _head_rms_t(xt, gain_col):
    parts = []
    for j in range(xt.shape[0] // DH):
        blk = xt[j * DH:(j + 1) * DH]
        ms = jnp.mean(blk * blk, axis=0, keepdims=True)
        parts.append(blk * lax.rsqrt(ms + NORM_EPS))
    return jnp.concatenate(parts, axis=0) * gain_col


def _in_odd_kernel(x_ref, g_ref, mod_ref, w_ref, qn_ref, kn_ref, *rest, sample, n_alias=0, slot=0):
    if sample:
        (cos_ref, sin_ref, cost_ref, sint_ref,
         cqt_ref, ck_ref, cvt_ref, cz_ref, dqt_ref, dk_ref, dvt_ref, dz_ref) = rest
    else:
        (cqt_ref, ck_ref, ckt_ref, cvt_ref, cz_ref,
         dqt_ref, dk_ref, dkt_ref, dvt_ref, dz_ref) = rest[n_alias:]
        ckt_ref, cvt_ref, dkt_ref, dvt_ref = [_own_slot(r, slot) for r in (ckt_ref, cvt_ref, dkt_ref, dvt_ref)]
    hb = _pre_norm(x_ref, g_ref, mod_ref)
    y = _dot(hb, w_ref[...])
    o = 0
    cqt = _head_rms_t(y[:, o:o + C_W].T, qn_ref[...])
    o += C_W
    ck = y[:, o:o + KV_W]
    ck = ck * lax.rsqrt(_group_mean_sq(ck) + NORM_EPS) * kn_ref[...]
    o += KV_W
    cvt = y[:, o:o + KV_W].T
    o += KV_W
    cz_ref[...] = y[:, o:o + C_W]
    o += C_W
    dqt = y[:, o:o + D_W].T
    o += D_W
    dk = y[:, o:o + KV_W]
    o += KV_W
    dvt = y[:, o:o + KV_W].T
    o += KV_W
    dz_ref[...] = y[:, o:o + D_W]
    if sample:
        cos_t = cost_ref[...]
        sin_t = sint_ref[...]
        cqt = _rope_t(cqt, cos_t, sin_t)
        dqt = _rope_t(dqt, cos_t, sin_t)
        cos = cos_ref[...][:, :KV_W]
        sin = sin_ref[...][:, :KV_W]
        ck = _rope(ck, cos, sin)
        dk = _rope(dk, cos, sin)
        _store_chunks(cvt_ref, cvt.astype(BF16))
        _store_chunks(dvt_ref, dvt.astype(BF16))
    else:
        _store_per_seq(ckt_ref, ck.T)
        _store_per_seq(dkt_ref, dk.T)
        _store_per_seq(cvt_ref, cvt)
        _store_per_seq(dvt_ref, dvt)
    _store_per_seq(cqt_ref, (cqt * QK_SCALE).astype(BF16))
    _store_per_seq(dqt_ref, (dqt * QK_SCALE).astype(BF16))
    ck_ref[...] = ck.astype(BF16)
    dk_ref[...] = dk.astype(BF16)


def _exp_terms(segs, extra=None):
    m = None
    for s in segs:
        mi = jnp.max(s, axis=0, keepdims=True)
        m = mi if m is None else jnp.maximum(m, mi)
    if extra is not None:
        extra = extra * LOG2E
        m = jnp.maximum(m, extra)
    es = [jnp.exp2(s - m) for s in segs]
    return es, (None if extra is None else jnp.exp2(extra - m))


def _softmax_t(segs):
    es, _ = _exp_terms(segs)
    den = None
    for e in es:
        di = jnp.sum(e, axis=0, keepdims=True)
        den = di if den is None else den + di
    return es, den


def _keep_rows(xt, lo, hi):
    zeros = lambda r: jnp.zeros((r, xt.shape[1]), xt.dtype)
    parts = []
    if lo > 0:
        parts.append(zeros(lo))
    parts.append(xt[lo:hi])
    if hi < xt.shape[0]:
        parts.append(zeros(xt.shape[0] - hi))
    return jnp.concatenate(parts, axis=0)


def _pipelined(n, scores, finish, ahead):
    ready = [scores(j) for j in range(min(ahead, n))]
    for j in range(n):
        if j + ahead < n:
            ready.append(scores(j + ahead))
        finish(j, ready.pop(0))


def _diff_attn(qt_ref, kvs, z_ref, lam_ref, g_ref, o_ref, *, lam_init):
    lv = lam_ref[...]
    lam = (jnp.exp(jnp.sum(lv[0:1] * lv[1:2], axis=-1, keepdims=True))
           - jnp.exp(jnp.sum(lv[2:3] * lv[3:4], axis=-1, keepdims=True)) + lam_init)

    def scores(h):
        cs = slice(h * LANES, (h + 1) * LANES)
        qt = qt_ref[cs, :]
        ks = [get_k(cs) for get_k, _ in kvs]
        return [[_dot(kk, _keep_rows(qt, c * DH, (c + 1) * DH)) for kk in ks] for c in range(2)]

    def finish(h, ss):
        cs = slice(h * LANES, (h + 1) * LANES)
        es0, den0 = _softmax_t(ss[0])
        es1, den1 = _softmax_t(ss[1])
        r0 = 1.0 / den0
        r1 = lam / den1
        ot = None
        for e0, e1, (_, get_vt) in zip(es0, es1, kvs):
            w = e0 * r0 - e1 * r1
            oi = _dot(get_vt(cs), w.astype(BF16))
            ot = oi if ot is None else ot + oi
        ms = jnp.mean(ot * ot, axis=0, keepdims=True)
        o = (ot * lax.rsqrt(ms + NORM_EPS)).T
        o = (o * g_ref[...]) * (1.0 - lam_init)
        o_ref[:, cs] = (o * _silu(z_ref[:, cs])).astype(o_ref.dtype)

    _pipelined(H_B, scores, finish, DIFF_AHEAD[len(kvs) - 1])


def _post_residual(o, x_ref, g_ref, mod_ref, o_ref):
    ms = jnp.mean(o * o, axis=-1, keepdims=True)
    r = o * lax.rsqrt(ms + NORM_EPS) * g_ref[...]
    gate = mod_ref[...][:, 2 * D_MODEL:]
    o_ref[...] = x_ref[...] + gate * r


def _diff_prompt_kernel(qt_ref, k_ref, vt_ref, z_ref, lam_ref, sg_ref, ma_ref, x_ref, w_ref, g_ref,
                        mod_ref, o_ref, mb_ref, *, lam_init):
    oa = _dot(ma_ref[...], w_ref[0:A_W, :])
    for s in range(qt_ref.shape[0]):
        rs = pl.ds(s * SEQ, SEQ)
        kv = (lambda cs, s=s: k_ref[s * SEQ:(s + 1) * SEQ, cs], lambda cs, s=s: vt_ref[s, cs, :])
        _diff_attn(qt_ref.at[s], [kv], z_ref.at[rs], lam_ref, sg_ref, mb_ref.at[rs], lam_init=lam_init)
    _post_residual(oa + _dot(mb_ref[...], w_ref[A_W:, :]), x_ref, g_ref, mod_ref, o_ref)


def _prep_diff_cache(ck_ref, cv_ref, ckb_ref, cvt_ref):
    ckb_ref[...] = ck_ref[...].T.astype(BF16)
    for h in range(H_B):
        cs = slice(h * LANES, (h + 1) * LANES)
        cvt_ref[cs, :] = cv_ref[pl.ds(h, PAST_LEN, stride=H_B), :].T.astype(BF16)


def _diff_sample_body(qt_ref, k_ref, vt_ref, z_ref, lam_ref, sg_ref, ma_ref, x_ref, w_ref, g_ref,
                      mod_ref, o_ref, ckb_ref, cvt_ref, mb_ref, *, lam_init):
    ctx = (lambda cs: ckb_ref[:, cs], lambda cs: cvt_ref[cs, :])
    loc = (lambda cs: k_ref[:, cs],
           lambda cs: jnp.concatenate([vt_ref[c, cs, :] for c in range(vt_ref.shape[0])], axis=1))
    oa = _dot(ma_ref[...], w_ref[0:A_W, :])
    _diff_attn(qt_ref, [ctx, loc], z_ref, lam_ref, sg_ref, mb_ref, lam_init=lam_init)
    _post_residual(oa + _dot(mb_ref[...], w_ref[A_W:, :]), x_ref, g_ref, mod_ref, o_ref)


def _gqa(qt_ref, segs, z_ref, o_ref, sink_ref=None):
    halves = []

    def scores(j):
        n = j // 4
        qj = qt_ref[j * DH:(j + 1) * DH, :]
        zero = jnp.zeros_like(qj)
        qz = jnp.concatenate([qj, zero] if n == 0 else [zero, qj], axis=0)
        return [_dot(k, qz) for k, _, _ in segs]

    def finish(j, ss):
        n = j // 4
        ss = [s if valid is None else jnp.where(valid, s, -jnp.inf)
              for s, (_, _, valid) in zip(ss, segs)]
        extra = None if sink_ref is None else sink_ref[:, j:j + 1]
        es, den = _exp_terms(ss, extra)
        ot = None
        for e, (_, vt, _) in zip(es, segs):
            ones = jnp.ones((DEN_ROWS, vt.shape[1]), BF16)
            vt1 = jnp.concatenate([vt[n * DH:(n + 1) * DH], ones], axis=0)
            oi = _dot(vt1, e.astype(BF16))
            ot = oi if ot is None else ot + oi
        den = ot[DH:DH + 1] if den is None else den + ot[DH:DH + 1]
        halves.append(ot[:DH] * (1.0 / den))
        if j % 2 == 1:
            cs = slice((j // 2) * LANES, (j // 2 + 1) * LANES)
            o_pair = jnp.concatenate(halves[-2:], axis=0).T
            o_ref[:, cs] = (o_pair * _silu(z_ref[:, cs])).astype(o_ref.dtype)

    _pipelined(2 * 4, scores, finish, GQA_AHEAD[len(segs) - 1])


def _gqa_prompt_kernel(cqt_ref, ck_ref, cvt_ref, cz_ref, dqt_ref, dk_ref, dvt_ref, dz_ref, sink_ref,
                       x_ref, w_ref, g_ref, mod_ref, o_ref, m_ref):
    for s in range(cqt_ref.shape[0]):
        rows = slice(s * SEQ, (s + 1) * SEQ)
        rs = pl.ds(s * SEQ, SEQ)
        seg = lambda k_ref, vt_ref: (k_ref[rows, :], vt_ref[s].astype(BF16), None)
        _gqa(cqt_ref.at[s], [seg(ck_ref, cvt_ref)], cz_ref.at[rs], m_ref.at[rs, pl.ds(0, C_W)])
        _gqa(dqt_ref.at[s], [seg(dk_ref, dvt_ref)], dz_ref.at[rs], m_ref.at[rs, pl.ds(C_W, D_W)], sink_ref)
    _post_residual(_dot(m_ref[...], w_ref[...]), x_ref, g_ref, mod_ref, o_ref)


def _gqa_sample_body(cqt_ref, ck_ref, cvt_ref, cck_ref, ccv_ref, cz_ref,
                     dqt_ref, dk_ref, dvt_ref, cdk_ref, cdv_ref, dz_ref, sink_ref,
                     x_ref, w_ref, g_ref, mod_ref, o_ref, m_ref, *, t):
    oc_ref = m_ref.at[:, pl.ds(0, C_W)]
    od_ref = m_ref.at[:, pl.ds(C_W, D_W)]
    ctx = lambda kt_ref, vt_ref: (kt_ref[...].T.astype(BF16), vt_ref[...].astype(BF16), None)
    n_chunks = DEC_SEQ // LANES
    cvt = jnp.concatenate([cvt_ref[c] for c in range(n_chunks)], axis=1)
    _gqa(cqt_ref, [ctx(cck_ref, ccv_ref), (ck_ref[...], cvt, None)], cz_ref, oc_ref)
    span = 2 * TQ
    t0 = t * TQ
    ws = pl.multiple_of(jnp.clip(t0 - WINDOW, 0, DEC_SEQ - span), WINDOW)
    kpos = ws + lax.broadcasted_iota(jnp.int32, (span, TQ), 0)
    qpos = t0 + lax.broadcasted_iota(jnp.int32, (span, TQ), 1)
    valid = jnp.abs(qpos - kpos) <= WINDOW
    c0 = ws // LANES
    dvt = jnp.concatenate([dvt_ref[c0 + c] for c in range(span // LANES)], axis=1)
    _gqa(dqt_ref, [ctx(cdk_ref, cdv_ref), (dk_ref[pl.ds(ws, span), :], dvt, valid)],
         dz_ref, od_ref, sink_ref)
    _post_residual(_dot(m_ref[...], w_ref[...]), x_ref, g_ref, mod_ref, o_ref)


def _cast_next_weights(rest, n_cast, n_alias, n_out):
    srcs = rest[:n_cast]
    rest = rest[n_cast + n_alias:]
    dsts = rest[n_out:n_out + n_cast]
    for src, dst in zip(srcs, dsts):
        dst[...] = src[...].astype(BF16)
    return rest[:n_out] + rest[n_out + n_cast:]


def _prompt_even_kernel(x_ref, g_ref, mod_ref, w_ref, cw_ref, cb_ref, lg_ref, lb_ref, lam_ref, sg_ref,
                        wo_ref, gp_ref, *rest, lam_init, layer, n_alias, slot, n_cast):
    g_ref, mod_ref, gp_ref, cb_ref, lg_ref, lb_ref, sg_ref = _layer_rows(
        layer, 0, g_ref, mod_ref, gp_ref, cb_ref, lg_ref, lb_ref, sg_ref)
    rest = _cast_next_weights(rest, n_cast, n_alias, n_out=3)
    (o_ref, kt_ref, v_ref,
     ma_ref, qt_ref, k_ref, vt_ref, bz_ref, sh_ref, acc_ref, mb_ref) = rest
    _in_even_kernel(x_ref, g_ref, mod_ref, w_ref, cw_ref, cb_ref, lg_ref, lb_ref,
                    ma_ref, qt_ref, k_ref, kt_ref, v_ref, vt_ref, bz_ref, sh_ref, acc_ref,
                    sample=False, slot=slot)
    _diff_prompt_kernel(qt_ref, k_ref, vt_ref, bz_ref, lam_ref, sg_ref, ma_ref, x_ref, wo_ref, gp_ref,
                        mod_ref, o_ref, mb_ref, lam_init=lam_init)


def _prompt_odd_kernel(x_ref, g_ref, mod_ref, w_ref, qn_ref, kn_ref, sink_ref, wo_ref, gp_ref, *rest,
                       layer, n_alias, slot, n_cast):
    g_ref, mod_ref, gp_ref, kn_ref, sink_ref = _layer_rows(layer, 0, g_ref, mod_ref, gp_ref, kn_ref, sink_ref)
    rest = _cast_next_weights(rest, n_cast, n_alias, n_out=5)
    (o_ref, ckt_ref, cvt_ref, dkt_ref, dvt_ref,
     cqt_ref, ck_ref, cz_ref, dqt_ref, dk_ref, dz_ref, m_ref) = rest
    _in_odd_kernel(x_ref, g_ref, mod_ref, w_ref, qn_ref, kn_ref,
                   cqt_ref, ck_ref, ckt_ref, cvt_ref, cz_ref, dqt_ref, dk_ref, dkt_ref, dvt_ref, dz_ref,
                   sample=False, slot=slot)
    _gqa_prompt_kernel(cqt_ref, ck_ref, _own_slot(cvt_ref, slot, fill=False), cz_ref,
                       dqt_ref, dk_ref, _own_slot(dvt_ref, slot, fill=False), dz_ref, sink_ref,
                       x_ref, wo_ref, gp_ref, mod_ref, o_ref, m_ref)


def _prompt_layer(x, layer, g_pre, g_post, mod4, w_in, w_out, w_idx, head, tail, carry, lam_init=None,
                  cast_next=None):
    n = x.shape[0]
    even = layer % 2 == 0
    row = lambda i: (i, 0)
    n_seq = TM // SEQ
    params = (g_pre, w_in) + tuple(head) + tuple(tail) + (w_out, g_post)
    idx = (layer, w_idx) + (layer // 2,) * (len(head) + len(tail)) + (w_idx, layer)
    specs = [_pick(p, i) for p, i in zip(params, idx)]
    in_specs = [pl.BlockSpec((TM, D_MODEL), row), specs[0],
                _whole(mod4)] + specs[1:]
    args = [x, g_pre, mod4, w_in] + list(head) + list(tail) + [w_out, g_post]
    n_layers = (DEPTH + 1 - layer % 2) // 2
    if carry:
        cache_spec = lambda r, c: pl.BlockSpec((n_seq, None, r, c), lambda i: (i, layer // 2, 0, 0))
    else:
        cache_spec = lambda r, c: pl.BlockSpec((n_seq, n_layers, r, c), lambda i: (i, 0, 0, 0))
    cache = lambda r, c: (cache_spec(r, c), jax.ShapeDtypeStruct((n // SEQ, n_layers, r, c), F32))
    outs = [(pl.BlockSpec((TM, D_MODEL), row), jax.ShapeDtypeStruct((n, D_MODEL), F32))]
    wide = pltpu.VMEM((TM, B_W), BF16)
    slab = pltpu.VMEM((n_seq, B_W, SEQ), BF16)
    if even:
        outs += [cache(B_W, SEQ), cache(SEQ * H_B, LANES)]
        scratch = [wide, slab, wide, slab, pltpu.VMEM((TM, B_W), F32),
                   pltpu.VMEM((SUBLANES, SEQ + 3 * SUBLANES, A_W), F32), pltpu.VMEM((SEQ, A_W), F32), wide]
        body = functools.partial(_prompt_even_kernel, lam_init=lam_init)
    else:
        outs += [cache(KV_W, SEQ)] * 4
        narrow = pltpu.VMEM((TM, KV_W), BF16)
        gate = pltpu.VMEM((TM, C_W), F32)
        scratch = [slab, narrow, gate, slab, narrow, gate, pltpu.VMEM((TM, C_W + D_W), BF16)]
        body = _prompt_odd_kernel
    n_cast = 0
    if cast_next is not None:
        *srcs, src_idx = cast_next
        n_cast = len(srcs)
        chunk = D_MODEL // (n // TM)
        for src in srcs:
            in_specs.append(pl.BlockSpec((None, chunk, src.shape[2]), lambda i: (src_idx, i, 0)))
            args.append(src)
            outs.append((pl.BlockSpec((None, chunk, src.shape[2]), lambda i: (0, i, 0)),
                         jax.ShapeDtypeStruct((1,) + src.shape[1:], BF16)))
    aliases = {}
    for j, a in enumerate(carry):
        aliases[len(args)] = 1 + j
        in_specs.append(pl.BlockSpec(memory_space=pl.ANY))
        args.append(a)
    return pl.pallas_call(
        functools.partial(body, layer=layer, n_alias=len(carry), slot=layer // 2, n_cast=n_cast),
        grid=(n // TM,),
        in_specs=in_specs,
        out_specs=[o[0] for o in outs],
        out_shape=[o[1] for o in outs],
        scratch_shapes=scratch,
        input_output_aliases=aliases,
        compiler_params=_params("arbitrary"),
        name=f"prompt_layer{layer}",
    )(*args)


N_IN = DEC_SEQ // TM
N_Q = DEC_SEQ // TQ


def _rows(ref, start, size):
    return ref.at[pl.ds(pl.multiple_of(start, size), size)]


def _sample_even_kernel(x_ref, xp_ref, xn_ref, g_ref, mod_ref, w_ref, cw_ref, cb_ref, lg_ref, lb_ref,
                        cos_ref, sin_ref, cost_ref, sint_ref, ck_ref, cv_ref, lam_ref, sg_ref,
                        xr_ref, wo_ref, gp_ref, o_ref,
                        ma_ref, qt_ref, k_ref, vt_ref, bz_ref, sh_ref, acc_ref, ckb_ref, cvt_ref, mb_ref,
                        *, lam_init, layer):
    g_ref, mod_ref, gp_ref, cb_ref, lg_ref, lb_ref, sg_ref = _layer_rows(
        layer, 1 + pl.program_id(0), g_ref, mod_ref, gp_ref, cb_ref, lg_ref, lb_ref, sg_ref)
    ph = pl.program_id(1)

    @pl.when(ph == 0)
    def _():
        _prep_diff_cache(ck_ref, cv_ref, ckb_ref, cvt_ref)

    @pl.when(ph < N_IN)
    def _():
        per = TM // SEQ
        _in_even_kernel(x_ref, g_ref, mod_ref, w_ref, cw_ref, cb_ref, lg_ref, lb_ref,
                        xp_ref, xn_ref, cos_ref, sin_ref, cost_ref, sint_ref,
                        _rows(ma_ref, ph * TM, TM), qt_ref.at[pl.ds(ph * per, per)],
                        _rows(k_ref, ph * TM, TM), vt_ref.at[pl.ds(ph * per, per)],
                        _rows(bz_ref, ph * TM, TM), sh_ref, acc_ref, sample=True, pos=ph)

    @pl.when(ph >= N_IN)
    def _():
        t = ph - N_IN
        _diff_sample_body(qt_ref.at[t], k_ref, vt_ref, _rows(bz_ref, t * TQ, TQ), lam_ref, sg_ref,
                          _rows(ma_ref, t * TQ, TQ), xr_ref, wo_ref, gp_ref, mod_ref, o_ref,
                          ckb_ref, cvt_ref, mb_ref, lam_init=lam_init)


def _sample_odd_kernel(x_ref, g_ref, mod_ref, w_ref, qn_ref, kn_ref,
                       cos_ref, sin_ref, cost_ref, sint_ref, cck_ref, ccv_ref, cdk_ref, cdv_ref, sink_ref,
                       xr_ref, wo_ref, gp_ref, o_ref,
                       cqt_ref, ck_ref, cvt_ref, cz_ref, dqt_ref, dk_ref, dvt_ref, dz_ref, m_ref, *, layer):
    g_ref, mod_ref, gp_ref, kn_ref, sink_ref = _layer_rows(
        layer, 1 + pl.program_id(0), g_ref, mod_ref, gp_ref, kn_ref, sink_ref)
    ph = pl.program_id(1)

    @pl.when(ph < N_IN)
    def _():
        per = TM // SEQ
        chunks = TM // LANES
        _in_odd_kernel(x_ref, g_ref, mod_ref, w_ref, qn_ref, kn_ref,
                       cos_ref, sin_ref, cost_ref, sint_ref,
                       cqt_ref.at[pl.ds(ph * per, per)], _rows(ck_ref, ph * TM, TM),
                       cvt_ref.at[pl.ds(ph * chunks, chunks)], _rows(cz_ref, ph * TM, TM),
                       dqt_ref.at[pl.ds(ph * per, per)], _rows(dk_ref, ph * TM, TM),
                       dvt_ref.at[pl.ds(ph * chunks, chunks)], _rows(dz_ref, ph * TM, TM), sample=True)

    @pl.when(ph >= N_IN)
    def _():
        t = ph - N_IN
        _gqa_sample_body(cqt_ref.at[t], ck_ref, cvt_ref, cck_ref, ccv_ref, _rows(cz_ref, t * TQ, TQ),
                         dqt_ref.at[t], dk_ref, dvt_ref, cdk_ref, cdv_ref, _rows(dz_ref, t * TQ, TQ),
                         sink_ref, xr_ref, wo_ref, gp_ref, mod_ref, o_ref, m_ref, t=t)


def _sample_layer(x, layer, g_pre, g_post, mod4, w_in, w_out, w_idx, head, tail, caches, tables,
                  lam_init=None):
    n = x.shape[0]
    even = layer % 2 == 0
    in_tile = lambda b, ph: b * N_IN + jnp.minimum(ph, N_IN - 1)
    q_tile = lambda b, ph: b * N_Q + jnp.maximum(ph - N_IN, 0)
    tab = lambda b, ph: jnp.minimum(ph, N_IN - 1)
    cos, sin, cos_t, sin_t = tables
    x_spec = pl.BlockSpec((TM, D_MODEL), lambda b, ph: (in_tile(b, ph), 0))
    mod_spec = _whole(mod4)
    table_specs = [pl.BlockSpec((TM, B_W), lambda b, ph: (tab(b, ph), 0)),
                   pl.BlockSpec((TM, B_W), lambda b, ph: (tab(b, ph), 0)),
                   pl.BlockSpec((B_W, TM), lambda b, ph: (0, tab(b, ph))),
                   pl.BlockSpec((B_W, TM), lambda b, ph: (0, tab(b, ph)))]
    cache_specs = [pl.BlockSpec((None, None) + c.shape[2:], lambda b, ph: (b, layer // 2, 0, 0))
                   for c in caches]
    res_spec = pl.BlockSpec((TQ, D_MODEL), lambda b, ph: (q_tile(b, ph), 0))
    in_specs = [x_spec]
    args = [x]
    if even:
        hb = TM // HALO
        last = n // HALO - 1
        in_specs += [pl.BlockSpec((HALO, D_MODEL), lambda b, ph: (jnp.maximum(in_tile(b, ph) * hb - 1, 0), 0)),
                     pl.BlockSpec((HALO, D_MODEL),
                                  lambda b, ph: (jnp.minimum((in_tile(b, ph) + 1) * hb, last), 0))]
        args += [x, x]
    in_specs += [_pick(g_pre, layer), mod_spec, _pick(w_in, w_idx)] + [_pick(h, layer // 2) for h in head]
    args += [g_pre, mod4, w_in] + list(head)
    in_specs += table_specs + cache_specs + [_pick(t, layer // 2) for t in tail]
    args += [cos, sin, cos_t, sin_t] + list(caches) + list(tail)
    in_specs += [res_spec, _pick(w_out, w_idx), _pick(g_post, layer)]
    args += [x, w_out, g_post]
    seq_wide = lambda w, dt: pltpu.VMEM((DEC_SEQ, w), dt)
    slab = pltpu.VMEM((DEC_SEQ // SEQ, B_W, SEQ), BF16)
    if even:
        scratch = [seq_wide(A_W, BF16), slab, seq_wide(B_W, BF16), slab, seq_wide(B_W, F32),
                   pltpu.VMEM((SUBLANES, SEQ + 3 * SUBLANES, A_W), F32), pltpu.VMEM((SEQ, A_W), F32),
                   pltpu.VMEM((PAST_LEN, B_W), BF16), pltpu.VMEM((B_W, PAST_LEN), BF16),
                   pltpu.VMEM((TQ, B_W), BF16)]
        body = functools.partial(_sample_even_kernel, lam_init=lam_init, layer=layer)
    else:
        chunk = pltpu.VMEM((DEC_SEQ // LANES, KV_W, LANES), BF16)
        scratch = [slab, seq_wide(KV_W, BF16), chunk, seq_wide(C_W, F32),
                   slab, seq_wide(KV_W, BF16), chunk, seq_wide(D_W, F32),
                   pltpu.VMEM((TQ, C_W + D_W), BF16)]
        body = functools.partial(_sample_odd_kernel, layer=layer)
    return pl.pallas_call(
        body,
        grid=(DEC_BATCH, N_IN + N_Q),
        in_specs=in_specs,
        out_specs=pl.BlockSpec((TQ, D_MODEL), lambda b, ph: (q_tile(b, ph), 0)),
        out_shape=jax.ShapeDtypeStruct((n, D_MODEL), F32),
        scratch_shapes=scratch,
        compiler_params=_params("arbitrary", "arbitrary"),
        name=f"sample_layer{layer}",
    )(*args)


def _rope_tables():
    nf = DH // 4
    t = jnp.arange(DEC_SEQ)
    row = (t // GRID_W).astype(F32)
    col = (t % GRID_W).astype(F32)
    inv = ROPE_THETA ** (-jnp.arange(nf, dtype=F32) / nf)
    d = jnp.arange(DH)
    axis = d // (2 * nf)
    second = (d % (2 * nf)) // nf
    f = d % nf
    pos = jnp.where(axis[None, :] == 0, row[:, None], col[:, None])
    ang = pos * inv[f][None, :]
    cos = jnp.cos(ang)
    sin = jnp.where(second[None, :] == 0, -jnp.sin(ang), jnp.sin(ang))
    reps = B_W // DH
    cos = jnp.tile(cos, (1, reps))
    sin = jnp.tile(sin, (1, reps))
    return cos, sin, cos.T, sin.T


def kernel(x_prompt, x_sample, cache_b_k, cache_b_v, cache_c_k, cache_c_v, cache_d_k, cache_d_v, c, c_ctx, norm_pre, norm_post, w_mod, b_mod, w_in_even, a_conv_w, a_conv_b, a_ln_g, a_ln_b, b_lambda, b_subln_g, w_out_even, w_in_odd, c_q_norm, c_k_norm, d_sink, w_out_odd):
    n_even = (DEPTH + 1) // 2
    n_odd = DEPTH // 2
    cond8 = jnp.zeros((SUBLANES, D_MODEL), F32).at[0].set(c_ctx).at[1:1 + DEC_BATCH].set(c)
    mod4 = _modulation(cond8, w_mod, b_mod)
    tables = _rope_tables()

    xp = x_prompt.reshape(BATCH * SEQ, D_MODEL)
    xs = x_sample.reshape(DEC_BATCH * DEC_SEQ, D_MODEL)
    feat = lambda a, w: jnp.moveaxis(a.reshape(a.shape[:3] + (w,)), 2, 3)
    cbk = feat(cache_b_k, B_W)
    cbv = cache_b_v.reshape(DEC_BATCH, n_even, PAST_LEN * H_B, 2 * DH)
    cck = feat(cache_c_k, KV_W)
    ccv = feat(cache_c_v, KV_W)
    cdk = feat(cache_d_k, KV_W)
    cdv = feat(cache_d_v, KV_W)

    g_pre, g_post = norm_pre, norm_post
    conv = (jnp.zeros((n_even, 4 * SUBLANES, A_W), F32).at[:, :CONV_K].set(a_conv_w),
            a_conv_b, a_ln_g, a_ln_b)
    subln = b_subln_g
    qkn = (jnp.tile(c_q_norm, (1, C_W // DH)).reshape(n_odd, C_W, 1), jnp.tile(c_k_norm, (1, KV_W // DH)))
    sink = d_sink

    weights = (w_in_even[:1].astype(BF16), w_out_even[:1].astype(BF16))
    f32_weights = ((w_in_even, w_out_even), (w_in_odd, w_out_odd))
    new_even, new_odd = (), ()
    for l in range(DEPTH):
        w_in, w_out = weights
        cast_next = f32_weights[(l + 1) % 2] + ((l + 1) // 2,) if l + 1 < DEPTH else None
        if l % 2 == 0:
            lam_init = 0.8 - 0.6 * math.exp(-0.3 * l)
            outs = _prompt_layer(xp, l, g_pre, g_post, mod4, w_in, w_out, 0, conv, (b_lambda, subln),
                                 new_even, lam_init, cast_next)
            xp, new_even, weights = outs[0], outs[1:3], outs[3:]
            xs = _sample_layer(xs, l, g_pre, g_post, mod4, w_in, w_out, 0, conv, (b_lambda, subln),
                               (cbk, cbv), tables, lam_init)
        else:
            outs = _prompt_layer(xp, l, g_pre, g_post, mod4, w_in, w_out, 0, qkn, (sink,), new_odd,
                                 None, cast_next)
            xp, new_odd, weights = outs[0], outs[1:5], outs[5:]
            xs = _sample_layer(xs, l, g_pre, g_post, mod4, w_in, w_out, 0, qkn, (sink,),
                               (cck, ccv, cdk, cdv), tables)

    def token_major(a, heads):
        return jnp.moveaxis(a.reshape(a.shape[:2] + heads + (DH, SEQ)), -1, 2)

    kt, v = new_even
    ckt, cvt, dkt, dvt = new_odd
    return (xp.reshape(BATCH, SEQ, D_MODEL), xs.reshape(DEC_BATCH, DEC_SEQ, D_MODEL),
            token_major(kt, (H_B, 2)), v.reshape(BATCH, n_even, SEQ, H_B, 2 * DH),
            token_major(ckt, (2,)), token_major(cvt, (2,)), token_major(dkt, (2,)), token_major(dvt, (2,)))
```

```python
import functools
import math

import jax
import jax.numpy as jnp
from jax import lax
from jax.experimental import pallas as pl
from jax.experimental.pallas import tpu as pltpu

F32 = jnp.float32
BF16 = jnp.bfloat16

D_MODEL = 1024
BATCH = 16
SEQ = 256
DEPTH = 4
DEC_BATCH = 2
DEC_SEQ = 1024
PAST_LEN = 512
GRID_W = 64
ROPE_THETA = 10000.0
NORM_EPS = 1e-6
DH = 64
A_W = 512
CONV_K = 31
H_B = 4
B_W = 512
C_W = 512
KV_W = 128
D_W = 512
WINDOW = 128
LOG2E = math.log2(math.e)
QK_SCALE = DH ** -0.5 * LOG2E

LANES = 128
SUBLANES = 8
VMEM_LIMIT = 56 * 1024 * 1024

TM = 512
TQ = SEQ
HALO = 16
GQA_AHEAD = (8, 3)
DIFF_AHEAD = (3, 2)
ROW_CHUNK = 64
DEN_ROWS = 16


def _params(*sem):
    return pltpu.CompilerParams(dimension_semantics=sem, vmem_limit_bytes=VMEM_LIMIT)


def _silu(x):
    return x * jax.nn.sigmoid(x)


def _dot(a, b):
    return jnp.dot(a, b, preferred_element_type=F32)


def _pick(stacked, idx):
    if stacked.ndim == 2:
        return _whole(stacked)
    return pl.BlockSpec((None,) + stacked.shape[1:], lambda *_: (idx,) + (0,) * (stacked.ndim - 1))


def _whole(a):
    return pl.BlockSpec(a.shape, lambda *_: (0,) * a.ndim)


def _layer_rows(layer, mod_row, g_ref, mod_ref, gp_ref, *half):
    one = lambda ref, i: ref.at[pl.ds(i, 1)]
    return ((one(g_ref, layer), one(mod_ref, mod_row), one(gp_ref, layer))
            + tuple(one(r, layer // 2) for r in half))


def _mod_kernel(cond_ref, w_ref, b_ref, o_ref):
    a = _silu(cond_ref[...]).astype(BF16)
    o_ref[...] = _dot(a, w_ref[...].astype(BF16)) + b_ref[0:1, :]


def _modulation_first(cond8, w_mod, b_mod):
    return pl.pallas_call(
        _mod_kernel,
        grid=(3,),
        in_specs=[
            pl.BlockSpec((SUBLANES, D_MODEL), lambda j: (0, 0)),
            pl.BlockSpec((None, D_MODEL, D_MODEL), lambda j: (0, 0, j)),
            pl.BlockSpec((DEPTH, D_MODEL), lambda j: (0, j)),
        ],
        out_specs=pl.BlockSpec((SUBLANES, D_MODEL), lambda j: (0, j)),
        out_shape=jax.ShapeDtypeStruct((SUBLANES, 3 * D_MODEL), F32),
        compiler_params=_params("arbitrary"),
        name="modulation",
    )(cond8, w_mod, b_mod)


def _pre_norm(x_ref, g_ref, mod_ref):
    return _modulate(x_ref[...], g_ref, mod_ref)


def _modulate(x, g_ref, mod_ref):
    ms = jnp.mean(x * x, axis=-1, keepdims=True)
    mod = mod_ref[...]
    sh = mod[:, :D_MODEL]
    sc = mod[:, D_MODEL:2 * D_MODEL]
    h = (x * lax.rsqrt(ms + NORM_EPS)) * (g_ref[...] * (1.0 + sc)) + sh
    return h.astype(BF16)


def _rope(x, cos, sin_signed):
    w = x.shape[-1]
    lane = lax.broadcasted_iota(jnp.int32, (1, w), 1)
    first = (lane % 32) < 16
    partner = jnp.where(first, pltpu.roll(x, w - 16, 1), pltpu.roll(x, 16, 1))
    return x * cos + partner * sin_signed


def _rope_t(x, cos_t, sin_t):
    r = x.shape[0]
    row = lax.broadcasted_iota(jnp.int32, (r, 1), 0)
    first = (row % 32) < 16
    partner = jnp.where(first, pltpu.roll(x, r - 16, 0), pltpu.roll(x, 16, 0))
    return x * cos_t + partner * sin_t


def _store_chunks(ref, xt):
    for c in range(xt.shape[1] // LANES):
        ref[c] = xt[:, c * LANES:(c + 1) * LANES]


def _store_per_seq(ref, xt):
    for s in range(xt.shape[1] // SEQ):
        ref[s] = xt[:, s * SEQ:(s + 1) * SEQ]


def _own_slot(ref, slot, fill=True):
    if len(ref.shape) == 3:
        return ref
    for other in range(ref.shape[1]):
        if fill and other != slot:
            ref[:, other] = jnp.zeros((ref.shape[0],) + tuple(ref.shape[2:]), ref.dtype)
    return ref.at[:, slot]


def _glu(ug):
    return ug[:, :A_W] * jax.nn.sigmoid(ug[:, A_W:])


def _conv_mix(pad, az, cw_ref, cb_ref, lg_ref, lb_ref, sh_ref, acc_ref):
    rows = sh_ref.shape[1]
    for b in range(SUBLANES):
        sh_ref[b] = pad[b:b + rows]
    base = HALO - CONV_K // 2
    for c0 in range(0, A_W, LANES):
        cs = slice(c0, c0 + LANES)
        for r0 in range(0, SEQ, ROW_CHUNK):
            acc = jnp.zeros((ROW_CHUNK, LANES), F32) + cb_ref[:, cs]
            for k in range(CONV_K):
                j = k + base
                s = r0 + (j // SUBLANES) * SUBLANES
                acc = acc + sh_ref[j % SUBLANES, s:s + ROW_CHUNK, cs] * cw_ref[k:k + 1, cs]
            acc_ref[r0:r0 + ROW_CHUNK, cs] = acc
    a = acc_ref[...]
    mu = jnp.mean(a, axis=-1, keepdims=True)
    d = a - mu
    var = jnp.mean(d * d, axis=-1, keepdims=True)
    y = d * lax.rsqrt(var + NORM_EPS) * lg_ref[...] + lb_ref[...]
    return _silu(y) * _silu(az)


def _in_even_kernel(x_ref, g_ref, mod_ref, w_ref, cw_ref, cb_ref, lg_ref, lb_ref, *rest,
                    sample, n_alias=0, slot=0, pos=None):
    if sample:
        (xp_ref, xn_ref, cos_ref, sin_ref, cost_ref, sint_ref,
         ma_ref, qt_ref, k_ref, vt_ref, bz_ref, sh_ref, acc_ref) = rest
    else:
        ma_ref, qt_ref, k_ref, kt_ref, v_ref, vt_ref, bz_ref, sh_ref, acc_ref = rest[n_alias:]
        kt_ref = _own_slot(kt_ref, slot)
        v_ref = _own_slot(v_ref, slot)
    hb = _pre_norm(x_ref, g_ref, mod_ref)
    a = _glu(_dot(hb, w_ref[:, 0:2 * A_W]))
    az = _dot(hb, w_ref[:, 2 * A_W:3 * A_W])
    n_sub = TM // SEQ
    if sample:
        tiles_per_seq = DEC_SEQ // TM
        xh = jnp.concatenate([xp_ref[...], xn_ref[...]], axis=0)
        ah = _glu(_dot(_modulate(xh, g_ref, mod_ref), w_ref[:, 0:2 * A_W]))
        prev = jnp.where(pos != 0, ah[:HALO], 0.0)
        nxt = jnp.where(pos != tiles_per_seq - 1, ah[HALO:], 0.0)
        full = jnp.concatenate([prev, a, nxt], axis=0)
        pads = [full[j * SEQ:(j + 1) * SEQ + 2 * HALO] for j in range(n_sub)]
    else:
        zeros = jnp.zeros((HALO, A_W), F32)
        pads = [jnp.concatenate([zeros, a[j * SEQ:(j + 1) * SEQ], zeros], axis=0) for j in range(n_sub)]
    o = 3 * A_W
    q = _dot(hb, w_ref[:, o:o + B_W])
    k = _dot(hb, w_ref[:, o + B_W:o + 2 * B_W])
    v = _dot(hb, w_ref[:, o + 2 * B_W:o + 3 * B_W])
    bz_ref[...] = _dot(hb, w_ref[:, o + 3 * B_W:o + 4 * B_W])
    qt = q.T
    if sample:
        qt = _rope_t(qt, cost_ref[...], sint_ref[...])
        k = _rope(k, cos_ref[...], sin_ref[...])
    else:
        _store_per_seq(kt_ref, k.T)
        for s in range(TM // SEQ):
            for h in range(H_B):
                v_ref[s, pl.ds(h, SEQ, stride=H_B), :] = v[s * SEQ:(s + 1) * SEQ, h * LANES:(h + 1) * LANES]
    _store_per_seq(qt_ref, (qt * QK_SCALE).astype(BF16))
    k_ref[...] = k.astype(BF16)
    _store_per_seq(vt_ref, v.T.astype(BF16))
    for j, pad in enumerate(pads):
        rs = slice(j * SEQ, (j + 1) * SEQ)
        ma_ref[rs, :] = _conv_mix(pad, az[rs], cw_ref, cb_ref, lg_ref, lb_ref, sh_ref, acc_ref).astype(BF16)


def _group_mean_sq(x):
    width = x.shape[-1]
    xx = x * x
    hi = xx.astype(BF16)
    lo = (xx - hi.astype(F32)).astype(BF16)
    r = lax.broadcasted_iota(jnp.int32, (width, width), 0) // DH
    c = lax.broadcasted_iota(jnp.int32, (width, width), 1) // DH
    g = jnp.where(r == c, 1.0, 0.0).astype(BF16)
    return (_dot(hi, g) + _dot(lo, g)) * (1.0 / DH)


def _head_rms_t(xt, gain_col):
    parts = []
    for j in range(xt.shape[0] // DH):
        blk = xt[j * DH:(j + 1) * DH]
        ms = jnp.mean(blk * blk, axis=0, keepdims=True)
        parts.append(blk * lax.rsqrt(ms + NORM_EPS))
    return jnp.concatenate(parts, axis=0) * gain_col


def _in_odd_kernel(x_ref, g_ref, mod_ref, w_ref, qn_ref, kn_ref, *rest, sample, n_alias=0, slot=0):
    if sample:
        (cos_ref, sin_ref, cost_ref, sint_ref,
         cqt_ref, ck_ref, cvt_ref, cz_ref, dqt_ref, dk_ref, dvt_ref, dz_ref) = rest
    else:
        (cqt_ref, ck_ref, ckt_ref, cvt_ref, cz_ref,
         dqt_ref, dk_ref, dkt_ref, dvt_ref, dz_ref) = rest[n_alias:]
        ckt_ref, cvt_ref, dkt_ref, dvt_ref = [_own_slot(r, slot) for r in (ckt_ref, cvt_ref, dkt_ref, dvt_ref)]
    hb = _pre_norm(x_ref, g_ref, mod_ref)
    y = _dot(hb, w_ref[...])
    o = 0
    cqt = _head_rms_t(y[:, o:o + C_W].T, qn_ref[...])
    o += C_W
    ck = y[:, o:o + KV_W]
    ck = ck * lax.rsqrt(_group_mean_sq(ck) + NORM_EPS) * kn_ref[...]
    o += KV_W
    cvt = y[:, o:o + KV_W].T
    o += KV_W
    cz_ref[...] = y[:, o:o + C_W]
    o += C_W
    dqt = y[:, o:o + D_W].T
    o += D_W
    dk = y[:, o:o + KV_W]
    o += KV_W
    dvt = y[:, o:o + KV_W].T
    o += KV_W
    dz_ref[...] = y[:, o:o + D_W]
    if sample:
        cos_t = cost_ref[...]
        sin_t = sint_ref[...]
        cqt = _rope_t(cqt, cos_t, sin_t)
        dqt = _rope_t(dqt, cos_t, sin_t)
        cos = cos_ref[...][:, :KV_W]
        sin = sin_ref[...][:, :KV_W]
        ck = _rope(ck, cos, sin)
        dk = _rope(dk, cos, sin)
        _store_chunks(cvt_ref, cvt.astype(BF16))
        _store_chunks(dvt_ref, dvt.astype(BF16))
    else:
        _store_per_seq(ckt_ref, ck.T)
        _store_per_seq(dkt_ref, dk.T)
        _store_per_seq(cvt_ref, cvt)
        _store_per_seq(dvt_ref, dvt)
    _store_per_seq(cqt_ref, (cqt * QK_SCALE).astype(BF16))
    _store_per_seq(dqt_ref, (dqt * QK_SCALE).astype(BF16))
    ck_ref[...] = ck.astype(BF16)
    dk_ref[...] = dk.astype(BF16)


def _exp_terms(segs, extra=None):
    m = None
    for s in segs:
        mi = jnp.max(s, axis=0, keepdims=True)
        m = mi if m is None else jnp.maximum(m, mi)
    if extra is not None:
        extra = extra * LOG2E
        m = jnp.maximum(m, extra)
    es = [jnp.exp2(s - m) for s in segs]
    return es, (None if extra is None else jnp.exp2(extra - m))


def _softmax_t(segs):
    es, _ = _exp_terms(segs)
    den = None
    for e in es:
        di = jnp.sum(e, axis=0, keepdims=True)
        den = di if den is None else den + di
    return es, den


def _keep_rows(xt, lo, hi):
    zeros = lambda r: jnp.zeros((r, xt.shape[1]), xt.dtype)
    parts = []
    if lo > 0:
        parts.append(zeros(lo))
    parts.append(xt[lo:hi])
    if hi < xt.shape[0]:
        parts.append(zeros(xt.shape[0] - hi))
    return jnp.concatenate(parts, axis=0)


def _pipelined(n, scores, finish, ahead):
    ready = [scores(j) for j in range(min(ahead, n))]
    for j in range(n):
        if j + ahead < n:
            ready.append(scores(j + ahead))
        finish(j, ready.pop(0))


def _diff_attn(qt_ref, kvs, z_ref, lam_ref, g_ref, o_ref, *, lam_init):
    lv = lam_ref[...]
    lam = (jnp.exp(jnp.sum(lv[0:1] * lv[1:2], axis=-1, keepdims=True))
           - jnp.exp(jnp.sum(lv[2:3] * lv[3:4], axis=-1, keepdims=True)) + lam_init)

    def scores(h):
        cs = slice(h * LANES, (h + 1) * LANES)
        qt = qt_ref[cs, :]
        ks = [get_k(cs) for get_k, _ in kvs]
        return [[_dot(kk, _keep_rows(qt, c * DH, (c + 1) * DH)) for kk in ks] for c in range(2)]

    def finish(h, ss):
        cs = slice(h * LANES, (h + 1) * LANES)
        es0, den0 = _softmax_t(ss[0])
        es1, den1 = _softmax_t(ss[1])
        r0 = 1.0 / den0
        r1 = lam / den1
        ot = None
        for e0, e1, (_, get_vt) in zip(es0, es1, kvs):
            w = e0 * r0 - e1 * r1
            oi = _dot(get_vt(cs), w.astype(BF16))
            ot = oi if ot is None else ot + oi
        ms = jnp.mean(ot * ot, axis=0, keepdims=True)
        o = (ot * lax.rsqrt(ms + NORM_EPS)).T
        o = (o * g_ref[...]) * (1.0 - lam_init)
        o_ref[:, cs] = (o * _silu(z_ref[:, cs])).astype(o_ref.dtype)

    _pipelined(H_B, scores, finish, DIFF_AHEAD[len(kvs) - 1])


def _post_residual(o, x_ref, g_ref, mod_ref, o_ref):
    ms = jnp.mean(o * o, axis=-1, keepdims=True)
    r = o * lax.rsqrt(ms + NORM_EPS) * g_ref[...]
    gate = mod_ref[...][:, 2 * D_MODEL:]
    o_ref[...] = x_ref[...] + gate * r


def _diff_prompt_kernel(qt_ref, k_ref, vt_ref, z_ref, lam_ref, sg_ref, ma_ref, x_ref, w_ref, g_ref,
                        mod_ref, o_ref, mb_ref, *, lam_init):
    oa = _dot(ma_ref[...], w_ref[0:A_W, :])
    for s in range(qt_ref.shape[0]):
        rs = pl.ds(s * SEQ, SEQ)
        kv = (lambda cs, s=s: k_ref[s * SEQ:(s + 1) * SEQ, cs], lambda cs, s=s: vt_ref[s, cs, :])
        _diff_attn(qt_ref.at[s], [kv], z_ref.at[rs], lam_ref, sg_ref, mb_ref.at[rs], lam_init=lam_init)
    _post_residual(oa + _dot(mb_ref[...], w_ref[A_W:, :]), x_ref, g_ref, mod_ref, o_ref)


def _prep_diff_cache(ck_ref, cv_ref, ckb_ref, cvt_ref):
    ckb_ref[...] = ck_ref[...].T.astype(BF16)
    for h in range(H_B):
        cs = slice(h * LANES, (h + 1) * LANES)
        cvt_ref[cs, :] = cv_ref[pl.ds(h, PAST_LEN, stride=H_B), :].T.astype(BF16)


def _diff_sample_body(qt_ref, k_ref, vt_ref, z_ref, lam_ref, sg_ref, ma_ref, x_ref, w_ref, g_ref,
                      mod_ref, o_ref, ckb_ref, cvt_ref, mb_ref, *, lam_init):
    ctx = (lambda cs: ckb_ref[:, cs], lambda cs: cvt_ref[cs, :])
    loc = (lambda cs: k_ref[:, cs],
           lambda cs: jnp.concatenate([vt_ref[c, cs, :] for c in range(vt_ref.shape[0])], axis=1))
    oa = _dot(ma_ref[...], w_ref[0:A_W, :])
    _diff_attn(qt_ref, [ctx, loc], z_ref, lam_ref, sg_ref, mb_ref, lam_init=lam_init)
    _post_residual(oa + _dot(mb_ref[...], w_ref[A_W:, :]), x_ref, g_ref, mod_ref, o_ref)


def _gqa(qt_ref, segs, z_ref, o_ref, sink_ref=None):
    halves = []

    def scores(j):
        n = j // 4
        qj = qt_ref[j * DH:(j + 1) * DH, :]
        zero = jnp.zeros_like(qj)
        qz = jnp.concatenate([qj, zero] if n == 0 else [zero, qj], axis=0)
        return [_dot(k, qz) for k, _, _ in segs]

    def finish(j, ss):
        n = j // 4
        ss = [s if valid is None else jnp.where(valid, s, -jnp.inf)
              for s, (_, _, valid) in zip(ss, segs)]
        extra = None if sink_ref is None else sink_ref[:, j:j + 1]
        es, den = _exp_terms(ss, extra)
        ot = None
        for e, (_, vt, _) in zip(es, segs):
            ones = jnp.ones((DEN_ROWS, vt.shape[1]), BF16)
            vt1 = jnp.concatenate([vt[n * DH:(n + 1) * DH], ones], axis=0)
            oi = _dot(vt1, e.astype(BF16))
            ot = oi if ot is None else ot + oi
        den = ot[DH:DH + 1] if den is None else den + ot[DH:DH + 1]
        halves.append(ot[:DH] * (1.0 / den))
        if j % 2 == 1:
            cs = slice((j // 2) * LANES, (j // 2 + 1) * LANES)
            o_pair = jnp.concatenate(halves[-2:], axis=0).T
            o_ref[:, cs] = (o_pair * _silu(z_ref[:, cs])).astype(o_ref.dtype)

    _pipelined(2 * 4, scores, finish, GQA_AHEAD[len(segs) - 1])


def _gqa_prompt_kernel(cqt_ref, ck_ref, cvt_ref, cz_ref, dqt_ref, dk_ref, dvt_ref, dz_ref, sink_ref,
                       x_ref, w_ref, g_ref, mod_ref, o_ref, m_ref):
    for s in range(cqt_ref.shape[0]):
        rows = slice(s * SEQ, (s + 1) * SEQ)
        rs = pl.ds(s * SEQ, SEQ)
        seg = lambda k_ref, vt_ref: (k_ref[rows, :], vt_ref[s].astype(BF16), None)
        _gqa(cqt_ref.at[s], [seg(ck_ref, cvt_ref)], cz_ref.at[rs], m_ref.at[rs, pl.ds(0, C_W)])
        _gqa(dqt_ref.at[s], [seg(dk_ref, dvt_ref)], dz_ref.at[rs], m_ref.at[rs, pl.ds(C_W, D_W)], sink_ref)
    _post_residual(_dot(m_ref[...], w_ref[...]), x_ref, g_ref, mod_ref, o_ref)


def _gqa_sample_body(cqt_ref, ck_ref, cvt_ref, cck_ref, ccv_ref, cz_ref,
                     dqt_ref, dk_ref, dvt_ref, cdk_ref, cdv_ref, dz_ref, sink_ref,
                     x_ref, w_ref, g_ref, mod_ref, o_ref, m_ref, *, t):
    oc_ref = m_ref.at[:, pl.ds(0, C_W)]
    od_ref = m_ref.at[:, pl.ds(C_W, D_W)]
    ctx = lambda kt_ref, vt_ref: (kt_ref[...].T.astype(BF16), vt_ref[...].astype(BF16), None)
    n_chunks = DEC_SEQ // LANES
    cvt = jnp.concatenate([cvt_ref[c] for c in range(n_chunks)], axis=1)
    _gqa(cqt_ref, [ctx(cck_ref, ccv_ref), (ck_ref[...], cvt, None)], cz_ref, oc_ref)
    span = 2 * TQ
    t0 = t * TQ
    ws = pl.multiple_of(jnp.clip(t0 - WINDOW, 0, DEC_SEQ - span), WINDOW)
    kpos = ws + lax.broadcasted_iota(jnp.int32, (span, TQ), 0)
    qpos = t0 + lax.broadcasted_iota(jnp.int32, (span, TQ), 1)
    valid = jnp.abs(qpos - kpos) <= WINDOW
    c0 = ws // LANES
    dvt = jnp.concatenate([dvt_ref[c0 + c] for c in range(span // LANES)], axis=1)
    _gqa(dqt_ref, [ctx(cdk_ref, cdv_ref), (dk_ref[pl.ds(ws, span), :], dvt, valid)],
         dz_ref, od_ref, sink_ref)
    _post_residual(_dot(m_ref[...], w_ref[...]), x_ref, g_ref, mod_ref, o_ref)


N_PREP_IN = 5
N_PREP_OUT = 3


def _prep_next_layer(rest, prep, n_alias, n_out, next_layer):
    if not prep:
        return rest[n_alias:]
    w_in_src, w_out_src, cond_ref, wm_ref, bm_ref = rest[:N_PREP_IN]
    rest = rest[N_PREP_IN + n_alias:]
    w_in_dst, w_out_dst, mod_dst = rest[n_out:n_out + N_PREP_OUT]
    w_in_dst[...] = w_in_src[...].astype(BF16)
    w_out_dst[...] = w_out_src[...].astype(BF16)
    part = _dot(_silu(cond_ref[...]).astype(BF16), wm_ref[...].astype(BF16))
    first = pl.program_id(0) == 0

    @pl.when(first)
    def _():
        mod_dst[...] = part + bm_ref[next_layer:next_layer + 1, :]

    @pl.when(jnp.logical_not(first))
    def _():
        mod_dst[...] += part

    return rest[:n_out] + rest[n_out + N_PREP_OUT:]


def _prompt_even_kernel(x_ref, g_ref, mod_ref, w_ref, cw_ref, cb_ref, lg_ref, lb_ref, lam_ref, sg_ref,
                        wo_ref, gp_ref, *rest, lam_init, layer, n_alias, slot, prep):
    g_ref, mod_ref, gp_ref, cb_ref, lg_ref, lb_ref, sg_ref = _layer_rows(
        layer, 0, g_ref, mod_ref, gp_ref, cb_ref, lg_ref, lb_ref, sg_ref)
    rest = _prep_next_layer(rest, prep, n_alias, 3, layer + 1)
    (o_ref, kt_ref, v_ref,
     ma_ref, qt_ref, k_ref, vt_ref, bz_ref, sh_ref, acc_ref, mb_ref) = rest
    _in_even_kernel(x_ref, g_ref, mod_ref, w_ref, cw_ref, cb_ref, lg_ref, lb_ref,
                    ma_ref, qt_ref, k_ref, kt_ref, v_ref, vt_ref, bz_ref, sh_ref, acc_ref,
                    sample=False, slot=slot)
    _diff_prompt_kernel(qt_ref, k_ref, vt_ref, bz_ref, lam_ref, sg_ref, ma_ref, x_ref, wo_ref, gp_ref,
                        mod_ref, o_ref, mb_ref, lam_init=lam_init)


def _prompt_odd_kernel(x_ref, g_ref, mod_ref, w_ref, qn_ref, kn_ref, sink_ref, wo_ref, gp_ref, *rest,
                       layer, n_alias, slot, prep):
    g_ref, mod_ref, gp_ref, kn_ref, sink_ref = _layer_rows(layer, 0, g_ref, mod_ref, gp_ref, kn_ref, sink_ref)
    rest = _prep_next_layer(rest, prep, n_alias, 5, layer + 1)
    (o_ref, ckt_ref, cvt_ref, dkt_ref, dvt_ref,
     cqt_ref, ck_ref, cz_ref, dqt_ref, dk_ref, dz_ref, m_ref) = rest
    _in_odd_kernel(x_ref, g_ref, mod_ref, w_ref, qn_ref, kn_ref,
                   cqt_ref, ck_ref, ckt_ref, cvt_ref, cz_ref, dqt_ref, dk_ref, dkt_ref, dvt_ref, dz_ref,
                   sample=False, slot=slot)
    _gqa_prompt_kernel(cqt_ref, ck_ref, _own_slot(cvt_ref, slot, fill=False), cz_ref,
                       dqt_ref, dk_ref, _own_slot(dvt_ref, slot, fill=False), dz_ref, sink_ref,
                       x_ref, wo_ref, gp_ref, mod_ref, o_ref, m_ref)


def _prompt_layer(x, layer, g_pre, g_post, mod4, w_in, w_out, w_idx, head, tail, carry, lam_init=None,
                  prep_next=None):
    n = x.shape[0]
    even = layer % 2 == 0
    row = lambda i: (i, 0)
    n_seq = TM // SEQ
    params = (g_pre, w_in) + tuple(head) + tuple(tail) + (w_out, g_post)
    idx = (layer, w_idx) + (layer // 2,) * (len(head) + len(tail)) + (w_idx, layer)
    specs = [_pick(p, i) for p, i in zip(params, idx)]
    in_specs = [pl.BlockSpec((TM, D_MODEL), row), specs[0],
                _whole(mod4)] + specs[1:]
    args = [x, g_pre, mod4, w_in] + list(head) + list(tail) + [w_out, g_post]
    n_layers = (DEPTH + 1 - layer % 2) // 2
    if carry:
        cache_spec = lambda r, c: pl.BlockSpec((n_seq, None, r, c), lambda i: (i, layer // 2, 0, 0))
    else:
        cache_spec = lambda r, c: pl.BlockSpec((n_seq, n_layers, r, c), lambda i: (i, 0, 0, 0))
    cache = lambda r, c: (cache_spec(r, c), jax.ShapeDtypeStruct((n // SEQ, n_layers, r, c), F32))
    outs = [(pl.BlockSpec((TM, D_MODEL), row), jax.ShapeDtypeStruct((n, D_MODEL), F32))]
    wide = pltpu.VMEM((TM, B_W), BF16)
    slab = pltpu.VMEM((n_seq, B_W, SEQ), BF16)
    if even:
        outs += [cache(B_W, SEQ), cache(SEQ * H_B, LANES)]
        scratch = [wide, slab, wide, slab, pltpu.VMEM((TM, B_W), F32),
                   pltpu.VMEM((SUBLANES, SEQ + 3 * SUBLANES, A_W), F32), pltpu.VMEM((SEQ, A_W), F32), wide]
        body = functools.partial(_prompt_even_kernel, lam_init=lam_init)
    else:
        outs += [cache(KV_W, SEQ)] * 4
        narrow = pltpu.VMEM((TM, KV_W), BF16)
        gate = pltpu.VMEM((TM, C_W), F32)
        scratch = [slab, narrow, gate, slab, narrow, gate, pltpu.VMEM((TM, C_W + D_W), BF16)]
        body = _prompt_odd_kernel
    if prep_next is not None:
        w_in_f32, w_out_f32, src_idx, cond_chunks, w_mod, b_mod = prep_next
        chunk = D_MODEL // (n // TM)
        assert cond_chunks.shape == (n // TM, SUBLANES, chunk)
        for src in (w_in_f32, w_out_f32):
            in_specs.append(pl.BlockSpec((None, chunk, src.shape[2]), lambda i: (src_idx, i, 0)))
            args.append(src)
            outs.append((pl.BlockSpec((None, chunk, src.shape[2]), lambda i: (0, i, 0)),
                         jax.ShapeDtypeStruct((1,) + src.shape[1:], BF16)))
        in_specs += [pl.BlockSpec((None, SUBLANES, chunk), lambda i: (i, 0, 0)),
                     pl.BlockSpec((None, chunk, 3 * D_MODEL), lambda i: (layer + 1, i, 0)),
                     _whole(b_mod)]
        args += [cond_chunks, w_mod, b_mod]
        outs.append((pl.BlockSpec((SUBLANES, 3 * D_MODEL), lambda i: (0, 0)),
                     jax.ShapeDtypeStruct((SUBLANES, 3 * D_MODEL), F32)))
    aliases = {}
    for j, a in enumerate(carry):
        aliases[len(args)] = 1 + j
        in_specs.append(pl.BlockSpec(memory_space=pl.ANY))
        args.append(a)
    return pl.pallas_call(
        functools.partial(body, layer=layer, n_alias=len(carry), slot=layer // 2,
                          prep=prep_next is not None),
        grid=(n // TM,),
        in_specs=in_specs,
        out_specs=[o[0] for o in outs],
        out_shape=[o[1] for o in outs],
        scratch_shapes=scratch,
        input_output_aliases=aliases,
        compiler_params=_params("arbitrary"),
        name=f"prompt_layer{layer}",
    )(*args)


N_IN = DEC_SEQ // TM
N_Q = DEC_SEQ // TQ


def _rows(ref, start, size):
    return ref.at[pl.ds(pl.multiple_of(start, size), size)]


def _sample_even_kernel(x_ref, xp_ref, xn_ref, g_ref, mod_ref, w_ref, cw_ref, cb_ref, lg_ref, lb_ref,
                        cos_ref, sin_ref, cost_ref, sint_ref, ck_ref, cv_ref, lam_ref, sg_ref,
                        xr_ref, wo_ref, gp_ref, o_ref,
                        ma_ref, qt_ref, k_ref, vt_ref, bz_ref, sh_ref, acc_ref, ckb_ref, cvt_ref, mb_ref,
                        *, lam_init, layer):
    g_ref, mod_ref, gp_ref, cb_ref, lg_ref, lb_ref, sg_ref = _layer_rows(
        layer, 1 + pl.program_id(0), g_ref, mod_ref, gp_ref, cb_ref, lg_ref, lb_ref, sg_ref)
    ph = pl.program_id(1)

    @pl.when(ph == 0)
    def _():
        _prep_diff_cache(ck_ref, cv_ref, ckb_ref, cvt_ref)

    @pl.when(ph < N_IN)
    def _():
        per = TM // SEQ
        _in_even_kernel(x_ref, g_ref, mod_ref, w_ref, cw_ref, cb_ref, lg_ref, lb_ref,
                        xp_ref, xn_ref, cos_ref, sin_ref, cost_ref, sint_ref,
                        _rows(ma_ref, ph * TM, TM), qt_ref.at[pl.ds(ph * per, per)],
                        _rows(k_ref, ph * TM, TM), vt_ref.at[pl.ds(ph * per, per)],
                        _rows(bz_ref, ph * TM, TM), sh_ref, acc_ref, sample=True, pos=ph)

    @pl.when(ph >= N_IN)
    def _():
        t = ph - N_IN
        _diff_sample_body(qt_ref.at[t], k_ref, vt_ref, _rows(bz_ref, t * TQ, TQ), lam_ref, sg_ref,
                          _rows(ma_ref, t * TQ, TQ), xr_ref, wo_ref, gp_ref, mod_ref, o_ref,
                          ckb_ref, cvt_ref, mb_ref, lam_init=lam_init)


def _sample_odd_kernel(x_ref, g_ref, mod_ref, w_ref, qn_ref, kn_ref,
                       cos_ref, sin_ref, cost_ref, sint_ref, cck_ref, ccv_ref, cdk_ref, cdv_ref, sink_ref,
                       xr_ref, wo_ref, gp_ref, o_ref,
                       cqt_ref, ck_ref, cvt_ref, cz_ref, dqt_ref, dk_ref, dvt_ref, dz_ref, m_ref, *, layer):
    g_ref, mod_ref, gp_ref, kn_ref, sink_ref = _layer_rows(
        layer, 1 + pl.program_id(0), g_ref, mod_ref, gp_ref, kn_ref, sink_ref)
    ph = pl.program_id(1)

    @pl.when(ph < N_IN)
    def _():
        per = TM // SEQ
        chunks = TM // LANES
        _in_odd_kernel(x_ref, g_ref, mod_ref, w_ref, qn_ref, kn_ref,
                       cos_ref, sin_ref, cost_ref, sint_ref,
                       cqt_ref.at[pl.ds(ph * per, per)], _rows(ck_ref, ph * TM, TM),
                       cvt_ref.at[pl.ds(ph * chunks, chunks)], _rows(cz_ref, ph * TM, TM),
                       dqt_ref.at[pl.ds(ph * per, per)], _rows(dk_ref, ph * TM, TM),
                       dvt_ref.at[pl.ds(ph * chunks, chunks)], _rows(dz_ref, ph * TM, TM), sample=True)

    @pl.when(ph >= N_IN)
    def _():
        t = ph - N_IN
        _gqa_sample_body(cqt_ref.at[t], ck_ref, cvt_ref, cck_ref, ccv_ref, _rows(cz_ref, t * TQ, TQ),
                         dqt_ref.at[t], dk_ref, dvt_ref, cdk_ref, cdv_ref, _rows(dz_ref, t * TQ, TQ),
                         sink_ref, xr_ref, wo_ref, gp_ref, mod_ref, o_ref, m_ref, t=t)


def _sample_layer(x, layer, g_pre, g_post, mod4, w_in, w_out, w_idx, head, tail, caches, tables,
                  lam_init=None):
    n = x.shape[0]
    even = layer % 2 == 0
    in_tile = lambda b, ph: b * N_IN + jnp.minimum(ph, N_IN - 1)
    q_tile = lambda b, ph: b * N_Q + jnp.maximum(ph - N_IN, 0)
    tab = lambda b, ph: jnp.minimum(ph, N_IN - 1)
    cos, sin, cos_t, sin_t = tables
    x_spec = pl.BlockSpec((TM, D_MODEL), lambda b, ph: (in_tile(b, ph), 0))
    mod_spec = _whole(mod4)
    table_specs = [pl.BlockSpec((TM, B_W), lambda b, ph: (tab(b, ph), 0)),
                   pl.BlockSpec((TM, B_W), lambda b, ph: (tab(b, ph), 0)),
                   pl.BlockSpec((B_W, TM), lambda b, ph: (0, tab(b, ph))),
                   pl.BlockSpec((B_W, TM), lambda b, ph: (0, tab(b, ph)))]
    cache_specs = [pl.BlockSpec((None, None) + c.shape[2:], lambda b, ph: (b, layer // 2, 0, 0))
                   for c in caches]
    res_spec = pl.BlockSpec((TQ, D_MODEL), lambda b, ph: (q_tile(b, ph), 0))
    in_specs = [x_spec]
    args = [x]
    if even:
        hb = TM // HALO
        last = n // HALO - 1
        in_specs += [pl.BlockSpec((HALO, D_MODEL), lambda b, ph: (jnp.maximum(in_tile(b, ph) * hb - 1, 0), 0)),
                     pl.BlockSpec((HALO, D_MODEL),
                                  lambda b, ph: (jnp.minimum((in_tile(b, ph) + 1) * hb, last), 0))]
        args += [x, x]
    in_specs += [_pick(g_pre, layer), mod_spec, _pick(w_in, w_idx)] + [_pick(h, layer // 2) for h in head]
    args += [g_pre, mod4, w_in] + list(head)
    in_specs += table_specs + cache_specs + [_pick(t, layer // 2) for t in tail]
    args += [cos, sin, cos_t, sin_t] + list(caches) + list(tail)
    in_specs += [res_spec, _pick(w_out, w_idx), _pick(g_post, layer)]
    args += [x, w_out, g_post]
    seq_wide = lambda w, dt: pltpu.VMEM((DEC_SEQ, w), dt)
    slab = pltpu.VMEM((DEC_SEQ // SEQ, B_W, SEQ), BF16)
    if even:
        scratch = [seq_wide(A_W, BF16), slab, seq_wide(B_W, BF16), slab, seq_wide(B_W, F32),
                   pltpu.VMEM((SUBLANES, SEQ + 3 * SUBLANES, A_W), F32), pltpu.VMEM((SEQ, A_W), F32),
                   pltpu.VMEM((PAST_LEN, B_W), BF16), pltpu.VMEM((B_W, PAST_LEN), BF16),
                   pltpu.VMEM((TQ, B_W), BF16)]
        body = functools.partial(_sample_even_kernel, lam_init=lam_init, layer=layer)
    else:
        chunk = pltpu.VMEM((DEC_SEQ // LANES, KV_W, LANES), BF16)
        scratch = [slab, seq_wide(KV_W, BF16), chunk, seq_wide(C_W, F32),
                   slab, seq_wide(KV_W, BF16), chunk, seq_wide(D_W, F32),
                   pltpu.VMEM((TQ, C_W + D_W), BF16)]
        body = functools.partial(_sample_odd_kernel, layer=layer)
    return pl.pallas_call(
        body,
        grid=(DEC_BATCH, N_IN + N_Q),
        in_specs=in_specs,
        out_specs=pl.BlockSpec((TQ, D_MODEL), lambda b, ph: (q_tile(b, ph), 0)),
        out_shape=jax.ShapeDtypeStruct((n, D_MODEL), F32),
        scratch_shapes=scratch,
        compiler_params=_params("arbitrary", "arbitrary"),
        name=f"sample_layer{layer}",
    )(*args)


def _rope_tables():
    nf = DH // 4
    t = jnp.arange(DEC_SEQ)
    row = (t // GRID_W).astype(F32)
    col = (t % GRID_W).astype(F32)
    inv = ROPE_THETA ** (-jnp.arange(nf, dtype=F32) / nf)
    d = jnp.arange(DH)
    axis = d // (2 * nf)
    second = (d % (2 * nf)) // nf
    f = d % nf
    pos = jnp.where(axis[None, :] == 0, row[:, None], col[:, None])
    ang = pos * inv[f][None, :]
    cos = jnp.cos(ang)
    sin = jnp.where(second[None, :] == 0, -jnp.sin(ang), jnp.sin(ang))
    reps = B_W // DH
    cos = jnp.tile(cos, (1, reps))
    sin = jnp.tile(sin, (1, reps))
    return cos, sin, cos.T, sin.T


def kernel(x_prompt, x_sample, cache_b_k, cache_b_v, cache_c_k, cache_c_v, cache_d_k, cache_d_v, c, c_ctx, norm_pre, norm_post, w_mod, b_mod, w_in_even, a_conv_w, a_conv_b, a_ln_g, a_ln_b, b_lambda, b_subln_g, w_out_even, w_in_odd, c_q_norm, c_k_norm, d_sink, w_out_odd):
    n_even = (DEPTH + 1) // 2
    n_odd = DEPTH // 2
    cond8 = jnp.zeros((SUBLANES, D_MODEL), F32).at[0].set(c_ctx).at[1:1 + DEC_BATCH].set(c)
    mod = _modulation_first(cond8, w_mod, b_mod)
    n_chunk = BATCH * SEQ // TM
    cond_chunks = jnp.moveaxis(cond8.reshape(SUBLANES, n_chunk, D_MODEL // n_chunk), 1, 0)
    tables = _rope_tables()

    xp = x_prompt.reshape(BATCH * SEQ, D_MODEL)
    xs = x_sample.reshape(DEC_BATCH * DEC_SEQ, D_MODEL)
    feat = lambda a, w: jnp.moveaxis(a.reshape(a.shape[:3] + (w,)), 2, 3)
    cbk = feat(cache_b_k, B_W)
    cbv = cache_b_v.reshape(DEC_BATCH, n_even, PAST_LEN * H_B, 2 * DH)
    cck = feat(cache_c_k, KV_W)
    ccv = feat(cache_c_v, KV_W)
    cdk = feat(cache_d_k, KV_W)
    cdv = feat(cache_d_v, KV_W)

    g_pre, g_post = norm_pre, norm_post
    conv = (jnp.zeros((n_even, 4 * SUBLANES, A_W), F32).at[:, :CONV_K].set(a_conv_w),
            a_conv_b, a_ln_g, a_ln_b)
    subln = b_subln_g
    qkn = (jnp.tile(c_q_norm, (1, C_W // DH)).reshape(n_odd, C_W, 1), jnp.tile(c_k_norm, (1, KV_W // DH)))
    sink = d_sink

    weights = (w_in_even[:1].astype(BF16), w_out_even[:1].astype(BF16))
    f32_weights = ((w_in_even, w_out_even), (w_in_odd, w_out_odd))
    new_even, new_odd = (), ()
    for l in range(DEPTH):
        w_in, w_out = weights
        mod4 = mod
        prep_next = (f32_weights[(l + 1) % 2] + ((l + 1) // 2, cond_chunks, w_mod, b_mod)
                     if l + 1 < DEPTH else None)
        if l % 2 == 0:
            lam_init = 0.8 - 0.6 * math.exp(-0.3 * l)
            outs = _prompt_layer(xp, l, g_pre, g_post, mod4, w_in, w_out, 0, conv, (b_lambda, subln),
                                 new_even, lam_init, prep_next)
            xp, new_even, (*weights, mod) = outs[0], outs[1:3], outs[3:]
            xs = _sample_layer(xs, l, g_pre, g_post, mod4, w_in, w_out, 0, conv, (b_lambda, subln),
                               (cbk, cbv), tables, lam_init)
        else:
            outs = _prompt_layer(xp, l, g_pre, g_post, mod4, w_in, w_out, 0, qkn, (sink,), new_odd,
                                 None, prep_next)
            xp, new_odd, (*weights, mod) = outs[0], outs[1:5], (outs[5:] if prep_next else (None, None, None))
            xs = _sample_layer(xs, l, g_pre, g_post, mod4, w_in, w_out, 0, qkn, (sink,),
                               (cck, ccv, cdk, cdv), tables)

    def token_major(a, heads):
        return jnp.moveaxis(a.reshape(a.shape[:2] + heads + (DH, SEQ)), -1, 2)

    kt, v = new_even
    ckt, cvt, dkt, dvt = new_odd
    return (xp.reshape(BATCH, SEQ, D_MODEL), xs.reshape(DEC_BATCH, DEC_SEQ, D_MODEL),
            token_major(kt, (H_B, 2)), v.reshape(BATCH, n_even, SEQ, H_B, 2 * DH),
            token_major(ckt, (2,)), token_major(cvt, (2,)), token_major(dkt, (2,)), token_major(dvt, (2,)))
```

```python
import functools
import math

import jax
import jax.numpy as jnp
from jax import lax
from jax.experimental import pallas as pl
from jax.experimental.pallas import tpu as pltpu

F32 = jnp.float32
BF16 = jnp.bfloat16

D_MODEL = 1024
BATCH = 16
SEQ = 256
DEPTH = 4
DEC_BATCH = 2
DEC_SEQ = 1024
PAST_LEN = 512
GRID_W = 64
ROPE_THETA = 10000.0
NORM_EPS = 1e-6
DH = 64
A_W = 512
CONV_K = 31
H_B = 4
B_W = 512
C_W = 512
KV_W = 128
D_W = 512
WINDOW = 128
LOG2E = math.log2(math.e)
QK_SCALE = DH ** -0.5 * LOG2E

LANES = 128
SUBLANES = 8
VMEM_LIMIT = 56 * 1024 * 1024

TM = 512
TQ = SEQ
HALO = 16
GQA_AHEAD = (8, 3)
DIFF_AHEAD = (3, 2)
ROW_CHUNK = 64
DEN_ROWS = 16


def _params(*sem):
    return pltpu.CompilerParams(dimension_semantics=sem, vmem_limit_bytes=VMEM_LIMIT)


def _silu(x):
    return x * jax.nn.sigmoid(x)


def _dot(a, b):
    return jnp.dot(a, b, preferred_element_type=F32)


def _pick(stacked, idx):
    if stacked.ndim == 2:
        return _whole(stacked)
    return pl.BlockSpec((None,) + stacked.shape[1:], lambda *_: (idx,) + (0,) * (stacked.ndim - 1))


def _whole(a):
    return pl.BlockSpec(a.shape, lambda *_: (0,) * a.ndim)


def _layer_rows(layer, mod_row, g_ref, mod_ref, gp_ref, *half):
    one = lambda ref, i: ref.at[pl.ds(i, 1)]
    return ((one(g_ref, layer), one(mod_ref, mod_row), one(gp_ref, layer))
            + tuple(one(r, layer // 2) for r in half))


def _mod_kernel(cond_ref, w_ref, b_ref, o_ref):
    a = _silu(cond_ref[...]).astype(BF16)
    o_ref[...] = _dot(a, w_ref[...].astype(BF16)) + b_ref[0:1, :]


def _modulation_first(cond8, w_mod, b_mod):
    return pl.pallas_call(
        _mod_kernel,
        grid=(3,),
        in_specs=[
            pl.BlockSpec((SUBLANES, D_MODEL), lambda j: (0, 0)),
            pl.BlockSpec((None, D_MODEL, D_MODEL), lambda j: (0, 0, j)),
            pl.BlockSpec((DEPTH, D_MODEL), lambda j: (0, j)),
        ],
        out_specs=pl.BlockSpec((SUBLANES, D_MODEL), lambda j: (0, j)),
        out_shape=jax.ShapeDtypeStruct((SUBLANES, 3 * D_MODEL), F32),
        compiler_params=_params("arbitrary"),
        name="modulation",
    )(cond8, w_mod, b_mod)


def _pre_norm(x_ref, g_ref, mod_ref):
    return _modulate(x_ref[...], g_ref, mod_ref)


def _modulate(x, g_ref, mod_ref):
    ms = jnp.mean(x * x, axis=-1, keepdims=True)
    mod = mod_ref[...]
    sh = mod[:, :D_MODEL]
    sc = mod[:, D_MODEL:2 * D_MODEL]
    h = (x * lax.rsqrt(ms + NORM_EPS)) * (g_ref[...] * (1.0 + sc)) + sh
    return h.astype(BF16)


def _rope(x, cos, sin_signed):
    w = x.shape[-1]
    lane = lax.broadcasted_iota(jnp.int32, (1, w), 1)
    first = (lane % 32) < 16
    partner = jnp.where(first, pltpu.roll(x, w - 16, 1), pltpu.roll(x, 16, 1))
    return x * cos + partner * sin_signed


def _rope_t(x, cos_t, sin_t):
    r = x.shape[0]
    row = lax.broadcasted_iota(jnp.int32, (r, 1), 0)
    first = (row % 32) < 16
    partner = jnp.where(first, pltpu.roll(x, r - 16, 0), pltpu.roll(x, 16, 0))
    return x * cos_t + partner * sin_t


def _store_chunks(ref, xt):
    for c in range(xt.shape[1] // LANES):
        ref[c] = xt[:, c * LANES:(c + 1) * LANES]


def _store_per_seq(ref, xt):
    for s in range(xt.shape[1] // SEQ):
        ref[s] = xt[:, s * SEQ:(s + 1) * SEQ]


def _own_slot(ref, slot, fill=True):
    if len(ref.shape) == 3:
        return ref
    for other in range(ref.shape[1]):
        if fill and other != slot:
            ref[:, other] = jnp.zeros((ref.shape[0],) + tuple(ref.shape[2:]), ref.dtype)
    return ref.at[:, slot]


def _glu(ug):
    return ug[:, :A_W] * jax.nn.sigmoid(ug[:, A_W:])


def _conv_mix(pad, az, cw_ref, cb_ref, lg_ref, lb_ref, sh_ref, acc_ref):
    rows = sh_ref.shape[1]
    for b in range(SUBLANES):
        sh_ref[b] = pad[b:b + rows]
    base = HALO - CONV_K // 2
    for c0 in range(0, A_W, LANES):
        cs = slice(c0, c0 + LANES)
        for r0 in range(0, SEQ, ROW_CHUNK):
            acc = jnp.zeros((ROW_CHUNK, LANES), F32) + cb_ref[:, cs]
            for k in range(CONV_K):
                j = k + base
                s = r0 + (j // SUBLANES) * SUBLANES
                acc = acc + sh_ref[j % SUBLANES, s:s + ROW_CHUNK, cs] * cw_ref[k:k + 1, cs]
            acc_ref[r0:r0 + ROW_CHUNK, cs] = acc
    a = acc_ref[...]
    mu = jnp.mean(a, axis=-1, keepdims=True)
    d = a - mu
    var = jnp.mean(d * d, axis=-1, keepdims=True)
    y = d * lax.rsqrt(var + NORM_EPS) * lg_ref[...] + lb_ref[...]
    return _silu(y) * _silu(az)


def _in_even_kernel(x_ref, g_ref, mod_ref, w_ref, cw_ref, cb_ref, lg_ref, lb_ref, *rest,
                    sample, n_alias=0, slot=0, pos=None):
    if sample:
        (xp_ref, xn_ref, cos_ref, sin_ref, cost_ref, sint_ref,
         ma_ref, qt_ref, k_ref, vt_ref, bz_ref, sh_ref, acc_ref) = rest
    else:
        ma_ref, qt_ref, k_ref, kt_ref, v_ref, vt_ref, bz_ref, sh_ref, acc_ref = rest[n_alias:]
        kt_ref = _own_slot(kt_ref, slot)
        v_ref = _own_slot(v_ref, slot)
    hb = _pre_norm(x_ref, g_ref, mod_ref)
    a = _glu(_dot(hb, w_ref[:, 0:2 * A_W]))
    az = _dot(hb, w_ref[:, 2 * A_W:3 * A_W])
    n_sub = TM // SEQ
    if sample:
        tiles_per_seq = DEC_SEQ // TM
        xh = jnp.concatenate([xp_ref[...], xn_ref[...]], axis=0)
        ah = _glu(_dot(_modulate(xh, g_ref, mod_ref), w_ref[:, 0:2 * A_W]))
        prev = jnp.where(pos != 0, ah[:HALO], 0.0)
        nxt = jnp.where(pos != tiles_per_seq - 1, ah[HALO:], 0.0)
        full = jnp.concatenate([prev, a, nxt], axis=0)
        pads = [full[j * SEQ:(j + 1) * SEQ + 2 * HALO] for j in range(n_sub)]
    else:
        zeros = jnp.zeros((HALO, A_W), F32)
        pads = [jnp.concatenate([zeros, a[j * SEQ:(j + 1) * SEQ], zeros], axis=0) for j in range(n_sub)]
    o = 3 * A_W
    q = _dot(hb, w_ref[:, o:o + B_W])
    k = _dot(hb, w_ref[:, o + B_W:o + 2 * B_W])
    v = _dot(hb, w_ref[:, o + 2 * B_W:o + 3 * B_W])
    bz_ref[...] = _dot(hb, w_ref[:, o + 3 * B_W:o + 4 * B_W])
    qt = q.T
    if sample:
        qt = _rope_t(qt, cost_ref[...], sint_ref[...])
        k = _rope(k, cos_ref[...], sin_ref[...])
    else:
        _store_per_seq(kt_ref, k.T)
        for s in range(TM // SEQ):
            for h in range(H_B):
                v_ref[s, pl.ds(h, SEQ, stride=H_B), :] = v[s * SEQ:(s + 1) * SEQ, h * LANES:(h + 1) * LANES]
    _store_per_seq(qt_ref, (qt * QK_SCALE).astype(BF16))
    k_ref[...] = k.astype(BF16)
    _store_per_seq(vt_ref, v.T.astype(BF16))
    for j, pad in enumerate(pads):
        rs = slice(j * SEQ, (j + 1) * SEQ)
        ma_ref[rs, :] = _conv_mix(pad, az[rs], cw_ref, cb_ref, lg_ref, lb_ref, sh_ref, acc_ref).astype(BF16)


def _group_mean_sq(x):
    width = x.shape[-1]
    xx = x * x
    hi = xx.astype(BF16)
    lo = (xx - hi.astype(F32)).astype(BF16)
    r = lax.broadcasted_iota(jnp.int32, (width, width), 0) // DH
    c = lax.broadcasted_iota(jnp.int32, (width, width), 1) // DH
    g = jnp.where(r == c, 1.0, 0.0).astype(BF16)
    return (_dot(hi, g) + _dot(lo, g)) * (1.0 / DH)


def _head_rms_t(xt, gain_col):
    parts = []
    for j in range(xt.shape[0] // DH):
        blk = xt[j * DH:(j + 1) * DH]
        ms = jnp.mean(blk * blk, axis=0, keepdims=True)
        parts.append(blk * lax.rsqrt(ms + NORM_EPS))
    return jnp.concatenate(parts, axis=0) * gain_col


def _in_odd_kernel(x_ref, g_ref, mod_ref, w_ref, qn_ref, kn_ref, *rest, sample, n_alias=0, slot=0):
    if sample:
        (cos_ref, sin_ref, cost_ref, sint_ref,
         cqt_ref, ck_ref, cvt_ref, cz_ref, dqt_ref, dk_ref, dvt_ref, dz_ref) = rest
    else:
        (cqt_ref, ck_ref, ckt_ref, cvt_ref, cz_ref,
         dqt_ref, dk_ref, dkt_ref, dvt_ref, dz_ref) = rest[n_alias:]
        ckt_ref, cvt_ref, dkt_ref, dvt_ref = [_own_slot(r, slot) for r in (ckt_ref, cvt_ref, dkt_ref, dvt_ref)]
    hb = _pre_norm(x_ref, g_ref, mod_ref)
    y = _dot(hb, w_ref[...])
    o = 0
    cqt = _head_rms_t(y[:, o:o + C_W].T, qn_ref[...])
    o += C_W
    ck = y[:, o:o + KV_W]
    ck = ck * lax.rsqrt(_group_mean_sq(ck) + NORM_EPS) * kn_ref[...]
    o += KV_W
    cvt = y[:, o:o + KV_W].T
    o += KV_W
    cz_ref[...] = y[:, o:o + C_W]
    o += C_W
    dqt = y[:, o:o + D_W].T
    o += D_W
    dk = y[:, o:o + KV_W]
    o += KV_W
    dvt = y[:, o:o + KV_W].T
    o += KV_W
    dz_ref[...] = y[:, o:o + D_W]
    if sample:
        cos_t = cost_ref[...]
        sin_t = sint_ref[...]
        cqt = _rope_t(cqt, cos_t, sin_t)
        dqt = _rope_t(dqt, cos_t, sin_t)
        cos = cos_ref[...][:, :KV_W]
        sin = sin_ref[...][:, :KV_W]
        ck = _rope(ck, cos, sin)
        dk = _rope(dk, cos, sin)
        _store_chunks(cvt_ref, cvt.astype(BF16))
        _store_chunks(dvt_ref, dvt.astype(BF16))
    else:
        _store_per_seq(ckt_ref, ck.T)
        _store_per_seq(dkt_ref, dk.T)
        _store_per_seq(cvt_ref, cvt)
        _store_per_seq(dvt_ref, dvt)
    _store_per_seq(cqt_ref, (cqt * QK_SCALE).astype(BF16))
    _store_per_seq(dqt_ref, (dqt * QK_SCALE).astype(BF16))
    ck_ref[...] = ck.astype(BF16)
    dk_ref[...] = dk.astype(BF16)


def _exp_terms(segs, extra=None):
    m = None
    for s in segs:
        mi = jnp.max(s, axis=0, keepdims=True)
        m = mi if m is None else jnp.maximum(m, mi)
    if extra is not None:
        extra = extra * LOG2E
        m = jnp.maximum(m, extra)
    es = [jnp.exp2(s - m) for s in segs]
    return es, (None if extra is None else jnp.exp2(extra - m))


def _softmax_t(segs):
    es, _ = _exp_terms(segs)
    den = None
    for e in es:
        di = jnp.sum(e, axis=0, keepdims=True)
        den = di if den is None else den + di
    return es, den


def _keep_rows(xt, lo, hi):
    zeros = lambda r: jnp.zeros((r, xt.shape[1]), xt.dtype)
    parts = []
    if lo > 0:
        parts.append(zeros(lo))
    parts.append(xt[lo:hi])
    if hi < xt.shape[0]:
        parts.append(zeros(xt.shape[0] - hi))
    return jnp.concatenate(parts, axis=0)


def _pipelined(n, scores, finish, ahead):
    ready = [scores(j) for j in range(min(ahead, n))]
    for j in range(n):
        if j + ahead < n:
            ready.append(scores(j + ahead))
        finish(j, ready.pop(0))


def _diff_attn(qt_ref, kvs, z_ref, lam_ref, g_ref, o_ref, *, lam_init):
    lv = lam_ref[...]
    lam = (jnp.exp(jnp.sum(lv[0:1] * lv[1:2], axis=-1, keepdims=True))
           - jnp.exp(jnp.sum(lv[2:3] * lv[3:4], axis=-1, keepdims=True)) + lam_init)

    def scores(h):
        cs = slice(h * LANES, (h + 1) * LANES)
        qt = qt_ref[cs, :]
        ks = [get_k(cs) for get_k, _ in kvs]
        return [[_dot(kk, _keep_rows(qt, c * DH, (c + 1) * DH)) for kk in ks] for c in range(2)]

    def finish(h, ss):
        cs = slice(h * LANES, (h + 1) * LANES)
        es0, den0 = _softmax_t(ss[0])
        es1, den1 = _softmax_t(ss[1])
        r0 = 1.0 / den0
        r1 = lam / den1
        ot = None
        for e0, e1, (_, get_vt) in zip(es0, es1, kvs):
            w = e0 * r0 - e1 * r1
            oi = _dot(get_vt(cs), w.astype(BF16))
            ot = oi if ot is None else ot + oi
        ms = jnp.mean(ot * ot, axis=0, keepdims=True)
        o = (ot * lax.rsqrt(ms + NORM_EPS)).T
        o = (o * g_ref[...]) * (1.0 - lam_init)
        o_ref[:, cs] = (o * _silu(z_ref[:, cs])).astype(o_ref.dtype)

    _pipelined(H_B, scores, finish, DIFF_AHEAD[len(kvs) - 1])


def _post_residual(o, x_ref, g_ref, mod_ref, o_ref):
    ms = jnp.mean(o * o, axis=-1, keepdims=True)
    r = o * lax.rsqrt(ms + NORM_EPS) * g_ref[...]
    gate = mod_ref[...][:, 2 * D_MODEL:]
    o_ref[...] = x_ref[...] + gate * r


def _diff_prompt_kernel(qt_ref, k_ref, vt_ref, z_ref, lam_ref, sg_ref, ma_ref, x_ref, w_ref, g_ref,
                        mod_ref, o_ref, mb_ref, *, lam_init):
    oa = _dot(ma_ref[...], w_ref[0:A_W, :])
    for s in range(qt_ref.shape[0]):
        rs = pl.ds(s * SEQ, SEQ)
        kv = (lambda cs, s=s: k_ref[s * SEQ:(s + 1) * SEQ, cs], lambda cs, s=s: vt_ref[s, cs, :])
        _diff_attn(qt_ref.at[s], [kv], z_ref.at[rs], lam_ref, sg_ref, mb_ref.at[rs], lam_init=lam_init)
    _post_residual(oa + _dot(mb_ref[...], w_ref[A_W:, :]), x_ref, g_ref, mod_ref, o_ref)


def _prep_diff_cache(ck_ref, cv_ref, ckb_ref, cvt_ref):
    ckb_ref[...] = ck_ref[...].T.astype(BF16)
    for h in range(H_B):
        cs = slice(h * LANES, (h + 1) * LANES)
        cvt_ref[cs, :] = cv_ref[pl.ds(h, PAST_LEN, stride=H_B), :].T.astype(BF16)


def _diff_sample_body(qt_ref, k_ref, vt_ref, z_ref, lam_ref, sg_ref, ma_ref, x_ref, w_ref, g_ref,
                      mod_ref, o_ref, ckb_ref, cvt_ref, mb_ref, *, lam_init):
    ctx = (lambda cs: ckb_ref[:, cs], lambda cs: cvt_ref[cs, :])
    loc = (lambda cs: k_ref[:, cs],
           lambda cs: jnp.concatenate([vt_ref[c, cs, :] for c in range(vt_ref.shape[0])], axis=1))
    oa = _dot(ma_ref[...], w_ref[0:A_W, :])
    _diff_attn(qt_ref, [ctx, loc], z_ref, lam_ref, sg_ref, mb_ref, lam_init=lam_init)
    _post_residual(oa + _dot(mb_ref[...], w_ref[A_W:, :]), x_ref, g_ref, mod_ref, o_ref)


def _gqa(qt_ref, segs, z_ref, o_ref, sink_ref=None):
    halves = []

    def scores(j):
        n = j // 4
        qj = qt_ref[j * DH:(j + 1) * DH, :]
        zero = jnp.zeros_like(qj)
        qz = jnp.concatenate([qj, zero] if n == 0 else [zero, qj], axis=0)
        return [_dot(k, qz) for k, _, _ in segs]

    def finish(j, ss):
        n = j // 4
        ss = [s if valid is None else jnp.where(valid, s, -jnp.inf)
              for s, (_, _, valid) in zip(ss, segs)]
        extra = None if sink_ref is None else sink_ref[:, j:j + 1]
        es, den = _exp_terms(ss, extra)
        ot = None
        for e, (_, vt, _) in zip(es, segs):
            ones = jnp.ones((DEN_ROWS, vt.shape[1]), BF16)
            vt1 = jnp.concatenate([vt[n * DH:(n + 1) * DH], ones], axis=0)
            oi = _dot(vt1, e.astype(BF16))
            ot = oi if ot is None else ot + oi
        den = ot[DH:DH + 1] if den is None else den + ot[DH:DH + 1]
        halves.append(ot[:DH] * (1.0 / den))
        if j % 2 == 1:
            cs = slice((j // 2) * LANES, (j // 2 + 1) * LANES)
            o_pair = jnp.concatenate(halves[-2:], axis=0).T
            o_ref[:, cs] = (o_pair * _silu(z_ref[:, cs])).astype(o_ref.dtype)

    _pipelined(2 * 4, scores, finish, GQA_AHEAD[len(segs) - 1])


def _gqa_prompt_kernel(cqt_ref, ck_ref, cvt_ref, cz_ref, dqt_ref, dk_ref, dvt_ref, dz_ref, sink_ref,
                       x_ref, w_ref, g_ref, mod_ref, o_ref, m_ref):
    for s in range(cqt_ref.shape[0]):
        rows = slice(s * SEQ, (s + 1) * SEQ)
        rs = pl.ds(s * SEQ, SEQ)
        seg = lambda k_ref, vt_ref: (k_ref[rows, :], vt_ref[s].astype(BF16), None)
        _gqa(cqt_ref.at[s], [seg(ck_ref, cvt_ref)], cz_ref.at[rs], m_ref.at[rs, pl.ds(0, C_W)])
        _gqa(dqt_ref.at[s], [seg(dk_ref, dvt_ref)], dz_ref.at[rs], m_ref.at[rs, pl.ds(C_W, D_W)], sink_ref)
    _post_residual(_dot(m_ref[...], w_ref[...]), x_ref, g_ref, mod_ref, o_ref)


def _gqa_sample_body(cqt_ref, ck_ref, cvt_ref, cck_ref, ccv_ref, cz_ref,
                     dqt_ref, dk_ref, dvt_ref, cdk_ref, cdv_ref, dz_ref, sink_ref,
                     x_ref, w_ref, g_ref, mod_ref, o_ref, m_ref, *, t):
    oc_ref = m_ref.at[:, pl.ds(0, C_W)]
    od_ref = m_ref.at[:, pl.ds(C_W, D_W)]
    ctx = lambda kt_ref, vt_ref: (kt_ref[...].T.astype(BF16), vt_ref[...].astype(BF16), None)
    n_chunks = DEC_SEQ // LANES
    cvt = jnp.concatenate([cvt_ref[c] for c in range(n_chunks)], axis=1)
    _gqa(cqt_ref, [ctx(cck_ref, ccv_ref), (ck_ref[...], cvt, None)], cz_ref, oc_ref)
    span = 2 * TQ
    t0 = t * TQ
    ws = pl.multiple_of(jnp.clip(t0 - WINDOW, 0, DEC_SEQ - span), WINDOW)
    kpos = ws + lax.broadcasted_iota(jnp.int32, (span, TQ), 0)
    qpos = t0 + lax.broadcasted_iota(jnp.int32, (span, TQ), 1)
    valid = jnp.abs(qpos - kpos) <= WINDOW
    c0 = ws // LANES
    dvt = jnp.concatenate([dvt_ref[c0 + c] for c in range(span // LANES)], axis=1)
    _gqa(dqt_ref, [ctx(cdk_ref, cdv_ref), (dk_ref[pl.ds(ws, span), :], dvt, valid)],
         dz_ref, od_ref, sink_ref)
    _post_residual(_dot(m_ref[...], w_ref[...]), x_ref, g_ref, mod_ref, o_ref)


N_PREP_IN = 5
N_PREP_OUT = 3


def _prep_next_layer(rest, prep, n_alias, n_out, next_layer):
    if not prep:
        return rest[n_alias:]
    w_in_src, w_out_src, cond_ref, wm_ref, bm_ref = rest[:N_PREP_IN]
    rest = rest[N_PREP_IN + n_alias:]
    w_in_dst, w_out_dst, mod_dst = rest[n_out:n_out + N_PREP_OUT]
    w_in_dst[...] = w_in_src[...].astype(BF16)
    w_out_dst[...] = w_out_src[...].astype(BF16)
    part = _dot(_silu(cond_ref[...]).astype(BF16), wm_ref[...].astype(BF16))
    first = pl.program_id(0) == 0

    @pl.when(first)
    def _():
        mod_dst[...] = part + bm_ref[next_layer:next_layer + 1, :]

    @pl.when(jnp.logical_not(first))
    def _():
        mod_dst[...] += part

    return rest[:n_out] + rest[n_out + N_PREP_OUT:]


def _prompt_even_kernel(x_ref, g_ref, mod_ref, w_ref, cw_ref, cb_ref, lg_ref, lb_ref, lam_ref, sg_ref,
                        wo_ref, gp_ref, *rest, lam_init, layer, n_alias, slot, prep):
    g_ref, mod_ref, gp_ref, cb_ref, lg_ref, lb_ref, sg_ref = _layer_rows(
        layer, 0, g_ref, mod_ref, gp_ref, cb_ref, lg_ref, lb_ref, sg_ref)
    rest = _prep_next_layer(rest, prep, n_alias, 3, layer + 1)
    (o_ref, kt_ref, v_ref,
     ma_ref, qt_ref, k_ref, vt_ref, bz_ref, sh_ref, acc_ref, mb_ref) = rest
    _in_even_kernel(x_ref, g_ref, mod_ref, w_ref, cw_ref, cb_ref, lg_ref, lb_ref,
                    ma_ref, qt_ref, k_ref, kt_ref, v_ref, vt_ref, bz_ref, sh_ref, acc_ref,
                    sample=False, slot=slot)
    _diff_prompt_kernel(qt_ref, k_ref, vt_ref, bz_ref, lam_ref, sg_ref, ma_ref, x_ref, wo_ref, gp_ref,
                        mod_ref, o_ref, mb_ref, lam_init=lam_init)


def _prompt_odd_kernel(x_ref, g_ref, mod_ref, w_ref, qn_ref, kn_ref, sink_ref, wo_ref, gp_ref, *rest,
                       layer, n_alias, slot, prep):
    g_ref, mod_ref, gp_ref, kn_ref, sink_ref = _layer_rows(layer, 0, g_ref, mod_ref, gp_ref, kn_ref, sink_ref)
    rest = _prep_next_layer(rest, prep, n_alias, 5, layer + 1)
    (o_ref, ckt_ref, cvt_ref, dkt_ref, dvt_ref,
     cqt_ref, ck_ref, cz_ref, dqt_ref, dk_ref, dz_ref, m_ref) = rest
    _in_odd_kernel(x_ref, g_ref, mod_ref, w_ref, qn_ref, kn_ref,
                   cqt_ref, ck_ref, ckt_ref, cvt_ref, cz_ref, dqt_ref, dk_ref, dkt_ref, dvt_ref, dz_ref,
                   sample=False, slot=slot)
    _gqa_prompt_kernel(cqt_ref, ck_ref, _own_slot(cvt_ref, slot, fill=False), cz_ref,
                       dqt_ref, dk_ref, _own_slot(dvt_ref, slot, fill=False), dz_ref, sink_ref,
                       x_ref, wo_ref, gp_ref, mod_ref, o_ref, m_ref)


def _prompt_layer(x, layer, g_pre, g_post, mod, w_in, w_out, w_idx, head, tail, carry, lam_init=None,
                  prep_next=None):
    n = x.shape[0]
    even = layer % 2 == 0
    row = lambda i: (i, 0)
    n_seq = TM // SEQ
    params = (g_pre, w_in) + tuple(head) + tuple(tail) + (w_out, g_post)
    idx = (layer, w_idx) + (layer // 2,) * (len(head) + len(tail)) + (w_idx, layer)
    specs = [_pick(p, i) for p, i in zip(params, idx)]
    in_specs = [pl.BlockSpec((TM, D_MODEL), row), specs[0],
                _whole(mod)] + specs[1:]
    args = [x, g_pre, mod, w_in] + list(head) + list(tail) + [w_out, g_post]
    n_layers = (DEPTH + 1 - layer % 2) // 2
    if carry:
        cache_spec = lambda r, c: pl.BlockSpec((n_seq, None, r, c), lambda i: (i, layer // 2, 0, 0))
    else:
        cache_spec = lambda r, c: pl.BlockSpec((n_seq, n_layers, r, c), lambda i: (i, 0, 0, 0))
    cache = lambda r, c: (cache_spec(r, c), jax.ShapeDtypeStruct((n // SEQ, n_layers, r, c), F32))
    outs = [(pl.BlockSpec((TM, D_MODEL), row), jax.ShapeDtypeStruct((n, D_MODEL), F32))]
    wide = pltpu.VMEM((TM, B_W), BF16)
    slab = pltpu.VMEM((n_seq, B_W, SEQ), BF16)
    if even:
        outs += [cache(B_W, SEQ), cache(SEQ * H_B, LANES)]
        scratch = [wide, slab, wide, slab, pltpu.VMEM((TM, B_W), F32),
                   pltpu.VMEM((SUBLANES, SEQ + 3 * SUBLANES, A_W), F32), pltpu.VMEM((SEQ, A_W), F32), wide]
        body = functools.partial(_prompt_even_kernel, lam_init=lam_init)
    else:
        outs += [cache(KV_W, SEQ)] * 4
        narrow = pltpu.VMEM((TM, KV_W), BF16)
        gate = pltpu.VMEM((TM, C_W), F32)
        scratch = [slab, narrow, gate, slab, narrow, gate, pltpu.VMEM((TM, C_W + D_W), BF16)]
        body = _prompt_odd_kernel
    if prep_next is not None:
        w_in_f32, w_out_f32, src_idx, cond_chunks, w_mod, b_mod = prep_next
        chunk = D_MODEL // (n // TM)
        assert cond_chunks.shape == (n // TM, SUBLANES, chunk)
        for src in (w_in_f32, w_out_f32):
            in_specs.append(pl.BlockSpec((None, chunk, src.shape[2]), lambda i: (src_idx, i, 0)))
            args.append(src)
            outs.append((pl.BlockSpec((None, chunk, src.shape[2]), lambda i: (0, i, 0)),
                         jax.ShapeDtypeStruct((1,) + src.shape[1:], BF16)))
        in_specs += [pl.BlockSpec((None, SUBLANES, chunk), lambda i: (i, 0, 0)),
                     pl.BlockSpec((None, chunk, 3 * D_MODEL), lambda i: (layer + 1, i, 0)),
                     _whole(b_mod)]
        args += [cond_chunks, w_mod, b_mod]
        outs.append((pl.BlockSpec((SUBLANES, 3 * D_MODEL), lambda i: (0, 0)),
                     jax.ShapeDtypeStruct((SUBLANES, 3 * D_MODEL), F32)))
    aliases = {}
    for j, a in enumerate(carry):
        aliases[len(args)] = 1 + j
        in_specs.append(pl.BlockSpec(memory_space=pl.ANY))
        args.append(a)
    return pl.pallas_call(
        functools.partial(body, layer=layer, n_alias=len(carry), slot=layer // 2,
                          prep=prep_next is not None),
        grid=(n // TM,),
        in_specs=in_specs,
        out_specs=[o[0] for o in outs],
        out_shape=[o[1] for o in outs],
        scratch_shapes=scratch,
        input_output_aliases=aliases,
        compiler_params=_params("arbitrary"),
        name=f"prompt_layer{layer}",
    )(*args)


N_IN = DEC_SEQ // TM
N_Q = DEC_SEQ // TQ
FIRST_Q = N_IN - 1
N_PHASE = N_IN + N_Q - 1


def _rows(ref, start, size):
    if isinstance(start, int):
        return ref.at[pl.ds(start, size)]
    return ref.at[pl.ds(pl.multiple_of(start, size), size)]


def _run_phases(ph, project, attend):
    if FIRST_Q > 0:
        @pl.when(ph < FIRST_Q)
        def _():
            project(ph)

    @pl.when(ph == FIRST_Q)
    def _():
        project(FIRST_Q)
        attend(0)

    @pl.when(ph > FIRST_Q)
    def _():
        attend(ph - FIRST_Q)


def _sample_even_kernel(x_ref, xp_ref, xn_ref, g_ref, mod_ref, w_ref, cw_ref, cb_ref, lg_ref, lb_ref,
                        cos_ref, sin_ref, cost_ref, sint_ref, ck_ref, cv_ref, lam_ref, sg_ref,
                        xr_ref, wo_ref, gp_ref, o_ref,
                        ma_ref, qt_ref, k_ref, vt_ref, bz_ref, sh_ref, acc_ref, ckb_ref, cvt_ref, mb_ref,
                        *, lam_init, layer):
    g_ref, mod_ref, gp_ref, cb_ref, lg_ref, lb_ref, sg_ref = _layer_rows(
        layer, 1 + pl.program_id(0), g_ref, mod_ref, gp_ref, cb_ref, lg_ref, lb_ref, sg_ref)
    ph = pl.program_id(1)
    per = TM // SEQ

    def project(p):
        _in_even_kernel(x_ref, g_ref, mod_ref, w_ref, cw_ref, cb_ref, lg_ref, lb_ref,
                        xp_ref, xn_ref, cos_ref, sin_ref, cost_ref, sint_ref,
                        _rows(ma_ref, p * TM, TM), qt_ref.at[pl.ds(p * per, per)],
                        _rows(k_ref, p * TM, TM), vt_ref.at[pl.ds(p * per, per)],
                        _rows(bz_ref, p * TM, TM), sh_ref, acc_ref, sample=True, pos=p)

    def attend(t):
        _diff_sample_body(qt_ref.at[t], k_ref, vt_ref, _rows(bz_ref, t * TQ, TQ), lam_ref, sg_ref,
                          _rows(ma_ref, t * TQ, TQ), xr_ref, wo_ref, gp_ref, mod_ref, o_ref,
                          ckb_ref, cvt_ref, mb_ref, lam_init=lam_init)

    @pl.when(ph == 0)
    def _():
        _prep_diff_cache(ck_ref, cv_ref, ckb_ref, cvt_ref)

    _run_phases(ph, project, attend)


def _sample_odd_kernel(x_ref, g_ref, mod_ref, w_ref, qn_ref, kn_ref,
                       cos_ref, sin_ref, cost_ref, sint_ref, cck_ref, ccv_ref, cdk_ref, cdv_ref, sink_ref,
                       xr_ref, wo_ref, gp_ref, o_ref,
                       cqt_ref, ck_ref, cvt_ref, cz_ref, dqt_ref, dk_ref, dvt_ref, dz_ref, m_ref, *, layer):
    g_ref, mod_ref, gp_ref, kn_ref, sink_ref = _layer_rows(
        layer, 1 + pl.program_id(0), g_ref, mod_ref, gp_ref, kn_ref, sink_ref)
    ph = pl.program_id(1)
    per = TM // SEQ
    chunks = TM // LANES

    def project(p):
        _in_odd_kernel(x_ref, g_ref, mod_ref, w_ref, qn_ref, kn_ref,
                       cos_ref, sin_ref, cost_ref, sint_ref,
                       cqt_ref.at[pl.ds(p * per, per)], _rows(ck_ref, p * TM, TM),
                       cvt_ref.at[pl.ds(p * chunks, chunks)], _rows(cz_ref, p * TM, TM),
                       dqt_ref.at[pl.ds(p * per, per)], _rows(dk_ref, p * TM, TM),
                       dvt_ref.at[pl.ds(p * chunks, chunks)], _rows(dz_ref, p * TM, TM), sample=True)

    def attend(t):
        _gqa_sample_body(cqt_ref.at[t], ck_ref, cvt_ref, cck_ref, ccv_ref, _rows(cz_ref, t * TQ, TQ),
                         dqt_ref.at[t], dk_ref, dvt_ref, cdk_ref, cdv_ref, _rows(dz_ref, t * TQ, TQ),
                         sink_ref, xr_ref, wo_ref, gp_ref, mod_ref, o_ref, m_ref, t=t)

    _run_phases(ph, project, attend)


def _sample_layer(x, layer, g_pre, g_post, mod, w_in, w_out, w_idx, head, tail, caches, tables,
                  lam_init=None):
    n = x.shape[0]
    even = layer % 2 == 0
    in_tile = lambda b, ph: b * N_IN + jnp.minimum(ph, N_IN - 1)
    q_tile = lambda b, ph: b * N_Q + jnp.maximum(ph - FIRST_Q, 0)
    tab = lambda b, ph: jnp.minimum(ph, N_IN - 1)
    cos, sin, cos_t, sin_t = tables
    x_spec = pl.BlockSpec((TM, D_MODEL), lambda b, ph: (in_tile(b, ph), 0))
    mod_spec = _whole(mod)
    table_specs = [pl.BlockSpec((TM, B_W), lambda b, ph: (tab(b, ph), 0)),
                   pl.BlockSpec((TM, B_W), lambda b, ph: (tab(b, ph), 0)),
                   pl.BlockSpec((B_W, TM), lambda b, ph: (0, tab(b, ph))),
                   pl.BlockSpec((B_W, TM), lambda b, ph: (0, tab(b, ph)))]
    cache_specs = [pl.BlockSpec((None, None) + c.shape[2:], lambda b, ph: (b, layer // 2, 0, 0))
                   for c in caches]
    res_spec = pl.BlockSpec((TQ, D_MODEL), lambda b, ph: (q_tile(b, ph), 0))
    in_specs = [x_spec]
    args = [x]
    if even:
        hb = TM // HALO
        last = n // HALO - 1
        in_specs += [pl.BlockSpec((HALO, D_MODEL), lambda b, ph: (jnp.maximum(in_tile(b, ph) * hb - 1, 0), 0)),
                     pl.BlockSpec((HALO, D_MODEL),
                                  lambda b, ph: (jnp.minimum((in_tile(b, ph) + 1) * hb, last), 0))]
        args += [x, x]
    in_specs += [_pick(g_pre, layer), mod_spec, _pick(w_in, w_idx)] + [_pick(h, layer // 2) for h in head]
    args += [g_pre, mod, w_in] + list(head)
    in_specs += table_specs + cache_specs + [_pick(t, layer // 2) for t in tail]
    args += [cos, sin, cos_t, sin_t] + list(caches) + list(tail)
    in_specs += [res_spec, _pick(w_out, w_idx), _pick(g_post, layer)]
    args += [x, w_out, g_post]
    seq_wide = lambda w, dt: pltpu.VMEM((DEC_SEQ, w), dt)
    slab = pltpu.VMEM((DEC_SEQ // SEQ, B_W, SEQ), BF16)
    if even:
        scratch = [seq_wide(A_W, BF16), slab, seq_wide(B_W, BF16), slab, seq_wide(B_W, F32),
                   pltpu.VMEM((SUBLANES, SEQ + 3 * SUBLANES, A_W), F32), pltpu.VMEM((SEQ, A_W), F32),
                   pltpu.VMEM((PAST_LEN, B_W), BF16), pltpu.VMEM((B_W, PAST_LEN), BF16),
                   pltpu.VMEM((TQ, B_W), BF16)]
        body = functools.partial(_sample_even_kernel, lam_init=lam_init, layer=layer)
    else:
        chunk = pltpu.VMEM((DEC_SEQ // LANES, KV_W, LANES), BF16)
        scratch = [slab, seq_wide(KV_W, BF16), chunk, seq_wide(C_W, F32),
                   slab, seq_wide(KV_W, BF16), chunk, seq_wide(D_W, F32),
                   pltpu.VMEM((TQ, C_W + D_W), BF16)]
        body = functools.partial(_sample_odd_kernel, layer=layer)
    return pl.pallas_call(
        body,
        grid=(DEC_BATCH, N_PHASE),
        in_specs=in_specs,
        out_specs=pl.BlockSpec((TQ, D_MODEL), lambda b, ph: (q_tile(b, ph), 0)),
        out_shape=jax.ShapeDtypeStruct((n, D_MODEL), F32),
        scratch_shapes=scratch,
        compiler_params=_params("arbitrary", "arbitrary"),
        name=f"sample_layer{layer}",
    )(*args)


def _rope_tables():
    nf = DH // 4
    t = jnp.arange(DEC_SEQ)
    row = (t // GRID_W).astype(F32)
    col = (t % GRID_W).astype(F32)
    inv = ROPE_THETA ** (-jnp.arange(nf, dtype=F32) / nf)
    d = jnp.arange(DH)
    axis = d // (2 * nf)
    second = (d % (2 * nf)) // nf
    f = d % nf
    pos = jnp.where(axis[None, :] == 0, row[:, None], col[:, None])
    ang = pos * inv[f][None, :]
    cos = jnp.cos(ang)
    sin = jnp.where(second[None, :] == 0, -jnp.sin(ang), jnp.sin(ang))
    reps = B_W // DH
    cos = jnp.tile(cos, (1, reps))
    sin = jnp.tile(sin, (1, reps))
    return cos, sin, cos.T, sin.T


def kernel(x_prompt, x_sample, cache_b_k, cache_b_v, cache_c_k, cache_c_v, cache_d_k, cache_d_v, c, c_ctx, norm_pre, norm_post, w_mod, b_mod, w_in_even, a_conv_w, a_conv_b, a_ln_g, a_ln_b, b_lambda, b_subln_g, w_out_even, w_in_odd, c_q_norm, c_k_norm, d_sink, w_out_odd):
    n_even = (DEPTH + 1) // 2
    n_odd = DEPTH // 2
    cond8 = jnp.zeros((SUBLANES, D_MODEL), F32).at[0].set(c_ctx).at[1:1 + DEC_BATCH].set(c)
    mod = _modulation_first(cond8, w_mod, b_mod)
    n_chunk = BATCH * SEQ // TM
    cond_chunks = jnp.moveaxis(cond8.reshape(SUBLANES, n_chunk, D_MODEL // n_chunk), 1, 0)
    tables = _rope_tables()

    xp = x_prompt.reshape(BATCH * SEQ, D_MODEL)
    xs = x_sample.reshape(DEC_BATCH * DEC_SEQ, D_MODEL)
    feat = lambda a, w: jnp.moveaxis(a.reshape(a.shape[:3] + (w,)), 2, 3)
    cbk = feat(cache_b_k, B_W)
    cbv = cache_b_v.reshape(DEC_BATCH, n_even, PAST_LEN * H_B, 2 * DH)
    cck = feat(cache_c_k, KV_W)
    ccv = feat(cache_c_v, KV_W)
    cdk = feat(cache_d_k, KV_W)
    cdv = feat(cache_d_v, KV_W)

    g_pre, g_post = norm_pre, norm_post
    conv = (jnp.zeros((n_even, 4 * SUBLANES, A_W), F32).at[:, :CONV_K].set(a_conv_w),
            a_conv_b, a_ln_g, a_ln_b)
    subln = b_subln_g
    qkn = (jnp.tile(c_q_norm, (1, C_W // DH)).reshape(n_odd, C_W, 1), jnp.tile(c_k_norm, (1, KV_W // DH)))
    sink = d_sink

    weights = (w_in_even[:1].astype(BF16), w_out_even[:1].astype(BF16))
    f32_weights = ((w_in_even, w_out_even), (w_in_odd, w_out_odd))
    new_even, new_odd = (), ()
    for l in range(DEPTH):
        w_in, w_out = weights
        mod_l = mod
        prep_next = (f32_weights[(l + 1) % 2] + ((l + 1) // 2, cond_chunks, w_mod, b_mod)
                     if l + 1 < DEPTH else None)
        if l % 2 == 0:
            lam_init = 0.8 - 0.6 * math.exp(-0.3 * l)
            outs = _prompt_layer(xp, l, g_pre, g_post, mod_l, w_in, w_out, 0, conv, (b_lambda, subln),
                                 new_even, lam_init, prep_next)
            xp, new_even, (*weights, mod) = outs[0], outs[1:3], outs[3:]
            xs = _sample_layer(xs, l, g_pre, g_post, mod_l, w_in, w_out, 0, conv, (b_lambda, subln),
                               (cbk, cbv), tables, lam_init)
        else:
            outs = _prompt_layer(xp, l, g_pre, g_post, mod_l, w_in, w_out, 0, qkn, (sink,), new_odd,
                                 None, prep_next)
            xp, new_odd, (*weights, mod) = outs[0], outs[1:5], (outs[5:] if prep_next else (None, None, None))
            xs = _sample_layer(xs, l, g_pre, g_post, mod_l, w_in, w_out, 0, qkn, (sink,),
                               (cck, ccv, cdk, cdv), tables)

    def token_major(a, heads):
        return jnp.moveaxis(a.reshape(a.shape[:2] + heads + (DH, SEQ)), -1, 2)

    kt, v = new_even
    ckt, cvt, dkt, dvt = new_odd
    return (xp.reshape(BATCH, SEQ, D_MODEL), xs.reshape(DEC_BATCH, DEC_SEQ, D_MODEL),
            token_major(kt, (H_B, 2)), v.reshape(BATCH, n_even, SEQ, H_B, 2 * DH),
            token_major(ckt, (2,)), token_major(cvt, (2,)), token_major(dkt, (2,)), token_major(dvt, (2,)))
```

```python
import functools
import math

import jax
import jax.numpy as jnp
from jax import lax
from jax.experimental import pallas as pl
from jax.experimental.pallas import tpu as pltpu

F32 = jnp.float32
BF16 = jnp.bfloat16

D_MODEL = 1024
BATCH = 16
SEQ = 256
DEPTH = 4
DEC_BATCH = 2
DEC_SEQ = 1024
PAST_LEN = 512
GRID_W = 64
ROPE_THETA = 10000.0
NORM_EPS = 1e-6
DH = 64
A_W = 512
CONV_K = 31
H_B = 4
B_W = 512
C_W = 512
KV_W = 128
D_W = 512
WINDOW = 128
LOG2E = math.log2(math.e)
QK_SCALE = DH ** -0.5 * LOG2E

LANES = 128
SUBLANES = 8
VMEM_LIMIT = 56 * 1024 * 1024

TM = 512
TQ = SEQ
HALO = 16
GQA_AHEAD = (8, 3)
DIFF_AHEAD = (3, 2)
ROW_CHUNK = 64
DEN_ROWS = 16


def _params(*sem):
    return pltpu.CompilerParams(dimension_semantics=sem, vmem_limit_bytes=VMEM_LIMIT)


def _silu(x):
    return x * jax.nn.sigmoid(x)


def _dot(a, b):
    return jnp.dot(a, b, preferred_element_type=F32)


def _pick(stacked, idx):
    if stacked.ndim == 2:
        return _whole(stacked)
    return pl.BlockSpec((None,) + stacked.shape[1:], lambda *_: (idx,) + (0,) * (stacked.ndim - 1))


def _whole(a):
    return pl.BlockSpec(a.shape, lambda *_: (0,) * a.ndim)


def _layer_rows(layer, mod_row, g_ref, mod_ref, gp_ref, *half):
    one = lambda ref, i: ref.at[pl.ds(i, 1)]
    return ((one(g_ref, layer), one(mod_ref, mod_row), one(gp_ref, layer))
            + tuple(one(r, layer // 2) for r in half))


def _pre_norm(x_ref, g_ref, mod_ref):
    return _modulate(x_ref[...], g_ref, mod_ref)


def _modulate(x, g_ref, mod_ref):
    ms = jnp.mean(x * x, axis=-1, keepdims=True)
    mod = mod_ref[...]
    sh = mod[:, :D_MODEL]
    sc = mod[:, D_MODEL:2 * D_MODEL]
    h = (x * lax.rsqrt(ms + NORM_EPS)) * (g_ref[...] * (1.0 + sc)) + sh
    return h.astype(BF16)


def _rope(x, cos, sin_signed):
    w = x.shape[-1]
    lane = lax.broadcasted_iota(jnp.int32, (1, w), 1)
    first = (lane % 32) < 16
    partner = jnp.where(first, pltpu.roll(x, w - 16, 1), pltpu.roll(x, 16, 1))
    return x * cos + partner * sin_signed


def _rope_t(x, cos_t, sin_t):
    r = x.shape[0]
    row = lax.broadcasted_iota(jnp.int32, (r, 1), 0)
    first = (row % 32) < 16
    partner = jnp.where(first, pltpu.roll(x, r - 16, 0), pltpu.roll(x, 16, 0))
    return x * cos_t + partner * sin_t


def _store_chunks(ref, xt):
    for c in range(xt.shape[1] // LANES):
        ref[c] = xt[:, c * LANES:(c + 1) * LANES]


def _store_per_seq(ref, xt):
    for s in range(xt.shape[1] // SEQ):
        ref[s] = xt[:, s * SEQ:(s + 1) * SEQ]


def _own_slot(ref, slot, fill=True):
    if len(ref.shape) == 3:
        return ref
    for other in range(ref.shape[1]):
        if fill and other != slot:
            ref[:, other] = jnp.zeros((ref.shape[0],) + tuple(ref.shape[2:]), ref.dtype)
    return ref.at[:, slot]


def _glu(ug):
    return ug[:, :A_W] * jax.nn.sigmoid(ug[:, A_W:])


def _conv_mix(pad, az, cw_ref, cb_ref, lg_ref, lb_ref, sh_ref, acc_ref):
    rows = sh_ref.shape[1]
    for b in range(SUBLANES):
        sh_ref[b] = pad[b:b + rows]
    base = HALO - CONV_K // 2
    for c0 in range(0, A_W, LANES):
        cs = slice(c0, c0 + LANES)
        for r0 in range(0, SEQ, ROW_CHUNK):
            acc = jnp.zeros((ROW_CHUNK, LANES), F32) + cb_ref[:, cs]
            for k in range(CONV_K):
                j = k + base
                s = r0 + (j // SUBLANES) * SUBLANES
                acc = acc + sh_ref[j % SUBLANES, s:s + ROW_CHUNK, cs] * cw_ref[k:k + 1, cs]
            acc_ref[r0:r0 + ROW_CHUNK, cs] = acc
    a = acc_ref[...]
    mu = jnp.mean(a, axis=-1, keepdims=True)
    d = a - mu
    var = jnp.mean(d * d, axis=-1, keepdims=True)
    y = d * lax.rsqrt(var + NORM_EPS) * lg_ref[...] + lb_ref[...]
    return _silu(y) * _silu(az)


def _in_even_kernel(x_ref, g_ref, mod_ref, w_ref, cw_ref, cb_ref, lg_ref, lb_ref, *rest,
                    sample, n_alias=0, slot=0, pos=None):
    if sample:
        (xp_ref, xn_ref, cos_ref, sin_ref, cost_ref, sint_ref,
         ma_ref, qt_ref, k_ref, vt_ref, bz_ref, sh_ref, acc_ref) = rest
    else:
        ma_ref, qt_ref, k_ref, kt_ref, v_ref, vt_ref, bz_ref, sh_ref, acc_ref = rest[n_alias:]
        kt_ref = _own_slot(kt_ref, slot)
        v_ref = _own_slot(v_ref, slot)
    hb = _pre_norm(x_ref, g_ref, mod_ref)
    a = _glu(_dot(hb, w_ref[:, 0:2 * A_W]))
    az = _dot(hb, w_ref[:, 2 * A_W:3 * A_W])
    n_sub = TM // SEQ
    if sample:
        tiles_per_seq = DEC_SEQ // TM
        xh = jnp.concatenate([xp_ref[...], xn_ref[...]], axis=0)
        ah = _glu(_dot(_modulate(xh, g_ref, mod_ref), w_ref[:, 0:2 * A_W]))
        prev = jnp.where(pos != 0, ah[:HALO], 0.0)
        nxt = jnp.where(pos != tiles_per_seq - 1, ah[HALO:], 0.0)
        full = jnp.concatenate([prev, a, nxt], axis=0)
        pads = [full[j * SEQ:(j + 1) * SEQ + 2 * HALO] for j in range(n_sub)]
    else:
        zeros = jnp.zeros((HALO, A_W), F32)
        pads = [jnp.concatenate([zeros, a[j * SEQ:(j + 1) * SEQ], zeros], axis=0) for j in range(n_sub)]
    o = 3 * A_W
    q = _dot(hb, w_ref[:, o:o + B_W])
    k = _dot(hb, w_ref[:, o + B_W:o + 2 * B_W])
    v = _dot(hb, w_ref[:, o + 2 * B_W:o + 3 * B_W])
    bz_ref[...] = _dot(hb, w_ref[:, o + 3 * B_W:o + 4 * B_W])
    qt = q.T
    if sample:
        qt = _rope_t(qt, cost_ref[...], sint_ref[...])
        k = _rope(k, cos_ref[...], sin_ref[...])
    else:
        _store_per_seq(kt_ref, k.T)
        for s in range(TM // SEQ):
            for h in range(H_B):
                v_ref[s, pl.ds(h, SEQ, stride=H_B), :] = v[s * SEQ:(s + 1) * SEQ, h * LANES:(h + 1) * LANES]
    _store_per_seq(qt_ref, (qt * QK_SCALE).astype(BF16))
    k_ref[...] = k.astype(BF16)
    _store_per_seq(vt_ref, v.T.astype(BF16))
    for j, pad in enumerate(pads):
        rs = slice(j * SEQ, (j + 1) * SEQ)
        ma_ref[rs, :] = _conv_mix(pad, az[rs], cw_ref, cb_ref, lg_ref, lb_ref, sh_ref, acc_ref).astype(BF16)


def _group_mean_sq(x):
    width = x.shape[-1]
    xx = x * x
    hi = xx.astype(BF16)
    lo = (xx - hi.astype(F32)).astype(BF16)
    r = lax.broadcasted_iota(jnp.int32, (width, width), 0) // DH
    c = lax.broadcasted_iota(jnp.int32, (width, width), 1) // DH
    g = jnp.where(r == c, 1.0, 0.0).astype(BF16)
    return (_dot(hi, g) + _dot(lo, g)) * (1.0 / DH)


def _head_rms_t(xt, gain_col):
    parts = []
    for j in range(xt.shape[0] // DH):
        blk = xt[j * DH:(j + 1) * DH]
        ms = jnp.mean(blk * blk, axis=0, keepdims=True)
        parts.append(blk * lax.rsqrt(ms + NORM_EPS))
    return jnp.concatenate(parts, axis=0) * gain_col


def _in_odd_kernel(x_ref, g_ref, mod_ref, w_ref, qn_ref, kn_ref, *rest, sample, n_alias=0, slot=0):
    if sample:
        (cos_ref, sin_ref, cost_ref, sint_ref,
         cqt_ref, ck_ref, cvt_ref, cz_ref, dqt_ref, dk_ref, dvt_ref, dz_ref) = rest
    else:
        (cqt_ref, ck_ref, ckt_ref, cvt_ref, cz_ref,
         dqt_ref, dk_ref, dkt_ref, dvt_ref, dz_ref) = rest[n_alias:]
        ckt_ref, cvt_ref, dkt_ref, dvt_ref = [_own_slot(r, slot) for r in (ckt_ref, cvt_ref, dkt_ref, dvt_ref)]
    hb = _pre_norm(x_ref, g_ref, mod_ref)
    y = _dot(hb, w_ref[...])
    o = 0
    cqt = _head_rms_t(y[:, o:o + C_W].T, qn_ref[...])
    o += C_W
    ck = y[:, o:o + KV_W]
    ck = ck * lax.rsqrt(_group_mean_sq(ck) + NORM_EPS) * kn_ref[...]
    o += KV_W
    cvt = y[:, o:o + KV_W].T
    o += KV_W
    cz_ref[...] = y[:, o:o + C_W]
    o += C_W
    dqt = y[:, o:o + D_W].T
    o += D_W
    dk = y[:, o:o + KV_W]
    o += KV_W
    dvt = y[:, o:o + KV_W].T
    o += KV_W
    dz_ref[...] = y[:, o:o + D_W]
    if sample:
        cos_t = cost_ref[...]
        sin_t = sint_ref[...]
        cqt = _rope_t(cqt, cos_t, sin_t)
        dqt = _rope_t(dqt, cos_t, sin_t)
        cos = cos_ref[...][:, :KV_W]
        sin = sin_ref[...][:, :KV_W]
        ck = _rope(ck, cos, sin)
        dk = _rope(dk, cos, sin)
        _store_chunks(cvt_ref, cvt.astype(BF16))
        _store_chunks(dvt_ref, dvt.astype(BF16))
    else:
        _store_per_seq(ckt_ref, ck.T)
        _store_per_seq(dkt_ref, dk.T)
        _store_per_seq(cvt_ref, cvt)
        _store_per_seq(dvt_ref, dvt)
    _store_per_seq(cqt_ref, (cqt * QK_SCALE).astype(BF16))
    _store_per_seq(dqt_ref, (dqt * QK_SCALE).astype(BF16))
    ck_ref[...] = ck.astype(BF16)
    dk_ref[...] = dk.astype(BF16)


def _exp_terms(segs, extra=None):
    m = None
    for s in segs:
        mi = jnp.max(s, axis=0, keepdims=True)
        m = mi if m is None else jnp.maximum(m, mi)
    if extra is not None:
        extra = extra * LOG2E
        m = jnp.maximum(m, extra)
    es = [jnp.exp2(s - m) for s in segs]
    return es, (None if extra is None else jnp.exp2(extra - m))


def _softmax_t(segs):
    es, _ = _exp_terms(segs)
    den = None
    for e in es:
        di = jnp.sum(e, axis=0, keepdims=True)
        den = di if den is None else den + di
    return es, den


def _keep_rows(xt, lo, hi):
    zeros = lambda r: jnp.zeros((r, xt.shape[1]), xt.dtype)
    parts = []
    if lo > 0:
        parts.append(zeros(lo))
    parts.append(xt[lo:hi])
    if hi < xt.shape[0]:
        parts.append(zeros(xt.shape[0] - hi))
    return jnp.concatenate(parts, axis=0)


def _pipelined(n, scores, finish, ahead):
    ready = [scores(j) for j in range(min(ahead, n))]
    for j in range(n):
        if j + ahead < n:
            ready.append(scores(j + ahead))
        finish(j, ready.pop(0))


def _diff_attn(qt_ref, kvs, z_ref, lam_ref, g_ref, o_ref, *, lam_init):
    lv = lam_ref[...]
    lam = (jnp.exp(jnp.sum(lv[0:1] * lv[1:2], axis=-1, keepdims=True))
           - jnp.exp(jnp.sum(lv[2:3] * lv[3:4], axis=-1, keepdims=True)) + lam_init)

    def scores(h):
        cs = slice(h * LANES, (h + 1) * LANES)
        qt = qt_ref[cs, :]
        ks = [get_k(cs) for get_k, _ in kvs]
        return [[_dot(kk, _keep_rows(qt, c * DH, (c + 1) * DH)) for kk in ks] for c in range(2)]

    def finish(h, ss):
        cs = slice(h * LANES, (h + 1) * LANES)
        es0, den0 = _softmax_t(ss[0])
        es1, den1 = _softmax_t(ss[1])
        r0 = 1.0 / den0
        r1 = lam / den1
        ot = None
        for e0, e1, (_, get_vt) in zip(es0, es1, kvs):
            w = e0 * r0 - e1 * r1
            oi = _dot(get_vt(cs), w.astype(BF16))
            ot = oi if ot is None else ot + oi
        ms = jnp.mean(ot * ot, axis=0, keepdims=True)
        o = (ot * lax.rsqrt(ms + NORM_EPS)).T
        o = (o * g_ref[...]) * (1.0 - lam_init)
        o_ref[:, cs] = (o * _silu(z_ref[:, cs])).astype(o_ref.dtype)

    _pipelined(H_B, scores, finish, DIFF_AHEAD[len(kvs) - 1])


def _post_residual(o, x_ref, g_ref, mod_ref, o_ref):
    ms = jnp.mean(o * o, axis=-1, keepdims=True)
    r = o * lax.rsqrt(ms + NORM_EPS) * g_ref[...]
    gate = mod_ref[...][:, 2 * D_MODEL:]
    o_ref[...] = x_ref[...] + gate * r


def _diff_prompt_kernel(qt_ref, k_ref, vt_ref, z_ref, lam_ref, sg_ref, ma_ref, x_ref, w_ref, g_ref,
                        mod_ref, o_ref, mb_ref, *, lam_init):
    oa = _dot(ma_ref[...], w_ref[0:A_W, :])
    for s in range(qt_ref.shape[0]):
        rs = pl.ds(s * SEQ, SEQ)
        kv = (lambda cs, s=s: k_ref[s * SEQ:(s + 1) * SEQ, cs], lambda cs, s=s: vt_ref[s, cs, :])
        _diff_attn(qt_ref.at[s], [kv], z_ref.at[rs], lam_ref, sg_ref, mb_ref.at[rs], lam_init=lam_init)
    _post_residual(oa + _dot(mb_ref[...], w_ref[A_W:, :]), x_ref, g_ref, mod_ref, o_ref)


def _prep_diff_cache(ck_ref, cv_ref, ckb_ref, cvt_ref):
    ckb_ref[...] = ck_ref[...].T.astype(BF16)
    for h in range(H_B):
        cs = slice(h * LANES, (h + 1) * LANES)
        cvt_ref[cs, :] = cv_ref[pl.ds(h, PAST_LEN, stride=H_B), :].T.astype(BF16)


def _diff_sample_body(qt_ref, k_ref, vt_ref, z_ref, lam_ref, sg_ref, ma_ref, x_ref, w_ref, g_ref,
                      mod_ref, o_ref, ckb_ref, cvt_ref, mb_ref, *, lam_init):
    ctx = (lambda cs: ckb_ref[:, cs], lambda cs: cvt_ref[cs, :])
    loc = (lambda cs: k_ref[:, cs],
           lambda cs: jnp.concatenate([vt_ref[c, cs, :] for c in range(vt_ref.shape[0])], axis=1))
    oa = _dot(ma_ref[...], w_ref[0:A_W, :])
    _diff_attn(qt_ref, [ctx, loc], z_ref, lam_ref, sg_ref, mb_ref, lam_init=lam_init)
    _post_residual(oa + _dot(mb_ref[...], w_ref[A_W:, :]), x_ref, g_ref, mod_ref, o_ref)


def _gqa(qt_ref, segs, z_ref, o_ref, sink_ref=None):
    halves = []

    def scores(j):
        n = j // 4
        qj = qt_ref[j * DH:(j + 1) * DH, :]
        zero = jnp.zeros_like(qj)
        qz = jnp.concatenate([qj, zero] if n == 0 else [zero, qj], axis=0)
        return [_dot(k, qz) for k, _, _ in segs]

    def finish(j, ss):
        n = j // 4
        ss = [s if valid is None else jnp.where(valid, s, -jnp.inf)
              for s, (_, _, valid) in zip(ss, segs)]
        extra = None if sink_ref is None else sink_ref[:, j:j + 1]
        es, den = _exp_terms(ss, extra)
        ot = None
        for e, (_, vt, _) in zip(es, segs):
            ones = jnp.ones((DEN_ROWS, vt.shape[1]), BF16)
            vt1 = jnp.concatenate([vt[n * DH:(n + 1) * DH], ones], axis=0)
            oi = _dot(vt1, e.astype(BF16))
            ot = oi if ot is None else ot + oi
        den = ot[DH:DH + 1] if den is None else den + ot[DH:DH + 1]
        halves.append(ot[:DH] * (1.0 / den))
        if j % 2 == 1:
            cs = slice((j // 2) * LANES, (j // 2 + 1) * LANES)
            o_pair = jnp.concatenate(halves[-2:], axis=0).T
            o_ref[:, cs] = (o_pair * _silu(z_ref[:, cs])).astype(o_ref.dtype)

    _pipelined(2 * 4, scores, finish, GQA_AHEAD[len(segs) - 1])


def _gqa_prompt_kernel(cqt_ref, ck_ref, cvt_ref, cz_ref, dqt_ref, dk_ref, dvt_ref, dz_ref, sink_ref,
                       x_ref, w_ref, g_ref, mod_ref, o_ref, m_ref):
    for s in range(cqt_ref.shape[0]):
        rows = slice(s * SEQ, (s + 1) * SEQ)
        rs = pl.ds(s * SEQ, SEQ)
        seg = lambda k_ref, vt_ref: (k_ref[rows, :], vt_ref[s].astype(BF16), None)
        _gqa(cqt_ref.at[s], [seg(ck_ref, cvt_ref)], cz_ref.at[rs], m_ref.at[rs, pl.ds(0, C_W)])
        _gqa(dqt_ref.at[s], [seg(dk_ref, dvt_ref)], dz_ref.at[rs], m_ref.at[rs, pl.ds(C_W, D_W)], sink_ref)
    _post_residual(_dot(m_ref[...], w_ref[...]), x_ref, g_ref, mod_ref, o_ref)


def _gqa_sample_body(cqt_ref, ck_ref, cvt_ref, cck_ref, ccv_ref, cz_ref,
                     dqt_ref, dk_ref, dvt_ref, cdk_ref, cdv_ref, dz_ref, sink_ref,
                     x_ref, w_ref, g_ref, mod_ref, o_ref, m_ref, *, t):
    oc_ref = m_ref.at[:, pl.ds(0, C_W)]
    od_ref = m_ref.at[:, pl.ds(C_W, D_W)]
    ctx = lambda kt_ref, vt_ref: (kt_ref[...].T.astype(BF16), vt_ref[...].astype(BF16), None)
    n_chunks = DEC_SEQ // LANES
    cvt = jnp.concatenate([cvt_ref[c] for c in range(n_chunks)], axis=1)
    _gqa(cqt_ref, [ctx(cck_ref, ccv_ref), (ck_ref[...], cvt, None)], cz_ref, oc_ref)
    span = 2 * TQ
    t0 = t * TQ
    ws = pl.multiple_of(jnp.clip(t0 - WINDOW, 0, DEC_SEQ - span), WINDOW)
    kpos = ws + lax.broadcasted_iota(jnp.int32, (span, TQ), 0)
    qpos = t0 + lax.broadcasted_iota(jnp.int32, (span, TQ), 1)
    valid = jnp.abs(qpos - kpos) <= WINDOW
    c0 = ws // LANES
    dvt = jnp.concatenate([dvt_ref[c0 + c] for c in range(span // LANES)], axis=1)
    _gqa(dqt_ref, [ctx(cdk_ref, cdv_ref), (dk_ref[pl.ds(ws, span), :], dvt, valid)],
         dz_ref, od_ref, sink_ref)
    _post_residual(_dot(m_ref[...], w_ref[...]), x_ref, g_ref, mod_ref, o_ref)


N_PREP_IN = 5
N_PREP_OUT = 3


def _prep_next_layer(rest, prep, n_alias, n_out, next_layer):
    if not prep:
        return rest[n_alias:]
    w_in_src, w_out_src, cond_ref, wm_ref, bm_ref = rest[:N_PREP_IN]
    rest = rest[N_PREP_IN + n_alias:]
    w_in_dst, w_out_dst, mod_dst = rest[n_out:n_out + N_PREP_OUT]
    w_in_dst[...] = w_in_src[...].astype(BF16)
    w_out_dst[...] = w_out_src[...].astype(BF16)
    part = _dot(_silu(cond_ref[...]).astype(BF16), wm_ref[...].astype(BF16))
    first = pl.program_id(0) == 0

    @pl.when(first)
    def _():
        mod_dst[...] = part + bm_ref[next_layer:next_layer + 1, :]

    @pl.when(jnp.logical_not(first))
    def _():
        mod_dst[...] += part

    return rest[:n_out] + rest[n_out + N_PREP_OUT:]


def _prep_specs(prep, layer, n_steps):
    w_in_f32, w_out_f32, src_idx, cond_chunks, w_mod, b_mod = prep
    chunk = D_MODEL // n_steps
    assert cond_chunks.shape == (n_steps, SUBLANES, chunk)
    in_specs, args, outs = [], [], []
    for src in (w_in_f32, w_out_f32):
        in_specs.append(pl.BlockSpec((None, chunk, src.shape[2]), lambda i: (src_idx, i, 0)))
        args.append(src)
        outs.append((pl.BlockSpec((None, chunk, src.shape[2]), lambda i: (0, i, 0)),
                     jax.ShapeDtypeStruct((1,) + src.shape[1:], BF16)))
    in_specs += [pl.BlockSpec((None, SUBLANES, chunk), lambda i: (i, 0, 0)),
                 pl.BlockSpec((None, chunk, 3 * D_MODEL), lambda i: (layer, i, 0)),
                 _whole(b_mod)]
    args += [cond_chunks, w_mod, b_mod]
    outs.append((pl.BlockSpec((SUBLANES, 3 * D_MODEL), lambda i: (0, 0)),
                 jax.ShapeDtypeStruct((SUBLANES, 3 * D_MODEL), F32)))
    return in_specs, args, outs


def _prep_first_kernel(*refs):
    _prep_next_layer(refs, True, 0, 0, 0)


def _prep_first_layer(prep):
    n_steps = prep[3].shape[0]
    in_specs, args, outs = _prep_specs(prep, 0, n_steps)
    return pl.pallas_call(
        _prep_first_kernel,
        grid=(n_steps,),
        in_specs=in_specs,
        out_specs=[o[0] for o in outs],
        out_shape=[o[1] for o in outs],
        compiler_params=_params("arbitrary"),
        name="prep_layer0",
    )(*args)


def _prompt_even_kernel(x_ref, g_ref, mod_ref, w_ref, cw_ref, cb_ref, lg_ref, lb_ref, lam_ref, sg_ref,
                        wo_ref, gp_ref, *rest, lam_init, layer, n_alias, slot, prep):
    g_ref, mod_ref, gp_ref, cb_ref, lg_ref, lb_ref, sg_ref = _layer_rows(
        layer, 0, g_ref, mod_ref, gp_ref, cb_ref, lg_ref, lb_ref, sg_ref)
    rest = _prep_next_layer(rest, prep, n_alias, 3, layer + 1)
    (o_ref, kt_ref, v_ref,
     ma_ref, qt_ref, k_ref, vt_ref, bz_ref, sh_ref, acc_ref, mb_ref) = rest
    _in_even_kernel(x_ref, g_ref, mod_ref, w_ref, cw_ref, cb_ref, lg_ref, lb_ref,
                    ma_ref, qt_ref, k_ref, kt_ref, v_ref, vt_ref, bz_ref, sh_ref, acc_ref,
                    sample=False, slot=slot)
    _diff_prompt_kernel(qt_ref, k_ref, vt_ref, bz_ref, lam_ref, sg_ref, ma_ref, x_ref, wo_ref, gp_ref,
                        mod_ref, o_ref, mb_ref, lam_init=lam_init)


def _prompt_odd_kernel(x_ref, g_ref, mod_ref, w_ref, qn_ref, kn_ref, sink_ref, wo_ref, gp_ref, *rest,
                       layer, n_alias, slot, prep):
    g_ref, mod_ref, gp_ref, kn_ref, sink_ref = _layer_rows(layer, 0, g_ref, mod_ref, gp_ref, kn_ref, sink_ref)
    rest = _prep_next_layer(rest, prep, n_alias, 5, layer + 1)
    (o_ref, ckt_ref, cvt_ref, dkt_ref, dvt_ref,
     cqt_ref, ck_ref, cz_ref, dqt_ref, dk_ref, dz_ref, m_ref) = rest
    _in_odd_kernel(x_ref, g_ref, mod_ref, w_ref, qn_ref, kn_ref,
                   cqt_ref, ck_ref, ckt_ref, cvt_ref, cz_ref, dqt_ref, dk_ref, dkt_ref, dvt_ref, dz_ref,
                   sample=False, slot=slot)
    _gqa_prompt_kernel(cqt_ref, ck_ref, _own_slot(cvt_ref, slot, fill=False), cz_ref,
                       dqt_ref, dk_ref, _own_slot(dvt_ref, slot, fill=False), dz_ref, sink_ref,
                       x_ref, wo_ref, gp_ref, mod_ref, o_ref, m_ref)


def _prompt_layer(x, layer, g_pre, g_post, mod, w_in, w_out, w_idx, head, tail, carry, lam_init=None,
                  prep_next=None):
    n = x.shape[0]
    even = layer % 2 == 0
    row = lambda i: (i, 0)
    n_seq = TM // SEQ
    params = (g_pre, w_in) + tuple(head) + tuple(tail) + (w_out, g_post)
    idx = (layer, w_idx) + (layer // 2,) * (len(head) + len(tail)) + (w_idx, layer)
    specs = [_pick(p, i) for p, i in zip(params, idx)]
    in_specs = [pl.BlockSpec((TM, D_MODEL), row), specs[0],
                _whole(mod)] + specs[1:]
    args = [x, g_pre, mod, w_in] + list(head) + list(tail) + [w_out, g_post]
    n_layers = (DEPTH + 1 - layer % 2) // 2
    if carry:
        cache_spec = lambda r, c: pl.BlockSpec((n_seq, None, r, c), lambda i: (i, layer // 2, 0, 0))
    else:
        cache_spec = lambda r, c: pl.BlockSpec((n_seq, n_layers, r, c), lambda i: (i, 0, 0, 0))
    cache = lambda r, c: (cache_spec(r, c), jax.ShapeDtypeStruct((n // SEQ, n_layers, r, c), F32))
    outs = [(pl.BlockSpec((TM, D_MODEL), row), jax.ShapeDtypeStruct((n, D_MODEL), F32))]
    wide = pltpu.VMEM((TM, B_W), BF16)
    slab = pltpu.VMEM((n_seq, B_W, SEQ), BF16)
    if even:
        outs += [cache(B_W, SEQ), cache(SEQ * H_B, LANES)]
        scratch = [wide, slab, wide, slab, pltpu.VMEM((TM, B_W), F32),
                   pltpu.VMEM((SUBLANES, SEQ + 3 * SUBLANES, A_W), F32), pltpu.VMEM((SEQ, A_W), F32), wide]
        body = functools.partial(_prompt_even_kernel, lam_init=lam_init)
    else:
        outs += [cache(KV_W, SEQ)] * 4
        narrow = pltpu.VMEM((TM, KV_W), BF16)
        gate = pltpu.VMEM((TM, C_W), F32)
        scratch = [slab, narrow, gate, slab, narrow, gate, pltpu.VMEM((TM, C_W + D_W), BF16)]
        body = _prompt_odd_kernel
    if prep_next is not None:
        prep_in, prep_args, prep_outs = _prep_specs(prep_next, layer + 1, n // TM)
        in_specs += prep_in
        args += prep_args
        outs += prep_outs
    aliases = {}
    for j, a in enumerate(carry):
        aliases[len(args)] = 1 + j
        in_specs.append(pl.BlockSpec(memory_space=pl.ANY))
        args.append(a)
    return pl.pallas_call(
        functools.partial(body, layer=layer, n_alias=len(carry), slot=layer // 2,
                          prep=prep_next is not None),
        grid=(n // TM,),
        in_specs=in_specs,
        out_specs=[o[0] for o in outs],
        out_shape=[o[1] for o in outs],
        scratch_shapes=scratch,
        input_output_aliases=aliases,
        compiler_params=_params("arbitrary"),
        name=f"prompt_layer{layer}",
    )(*args)


N_IN = DEC_SEQ // TM
N_Q = DEC_SEQ // TQ
FIRST_Q = N_IN - 1
N_PHASE = N_IN + N_Q - 1


def _rows(ref, start, size):
    if isinstance(start, int):
        return ref.at[pl.ds(start, size)]
    return ref.at[pl.ds(pl.multiple_of(start, size), size)]


def _run_phases(ph, project, attend):
    if FIRST_Q > 0:
        @pl.when(ph < FIRST_Q)
        def _():
            project(ph)

    @pl.when(ph == FIRST_Q)
    def _():
        project(FIRST_Q)
        attend(0)

    @pl.when(ph > FIRST_Q)
    def _():
        attend(ph - FIRST_Q)


def _sample_even_kernel(x_ref, xp_ref, xn_ref, g_ref, mod_ref, w_ref, cw_ref, cb_ref, lg_ref, lb_ref,
                        cos_ref, sin_ref, cost_ref, sint_ref, ck_ref, cv_ref, lam_ref, sg_ref,
                        xr_ref, wo_ref, gp_ref, o_ref,
                        ma_ref, qt_ref, k_ref, vt_ref, bz_ref, sh_ref, acc_ref, ckb_ref, cvt_ref, mb_ref,
                        *, lam_init, layer):
    g_ref, mod_ref, gp_ref, cb_ref, lg_ref, lb_ref, sg_ref = _layer_rows(
        layer, 1 + pl.program_id(0), g_ref, mod_ref, gp_ref, cb_ref, lg_ref, lb_ref, sg_ref)
    ph = pl.program_id(1)
    per = TM // SEQ

    def project(p):
        _in_even_kernel(x_ref, g_ref, mod_ref, w_ref, cw_ref, cb_ref, lg_ref, lb_ref,
                        xp_ref, xn_ref, cos_ref, sin_ref, cost_ref, sint_ref,
                        _rows(ma_ref, p * TM, TM), qt_ref.at[pl.ds(p * per, per)],
                        _rows(k_ref, p * TM, TM), vt_ref.at[pl.ds(p * per, per)],
                        _rows(bz_ref, p * TM, TM), sh_ref, acc_ref, sample=True, pos=p)

    def attend(t):
        _diff_sample_body(qt_ref.at[t], k_ref, vt_ref, _rows(bz_ref, t * TQ, TQ), lam_ref, sg_ref,
                          _rows(ma_ref, t * TQ, TQ), xr_ref, wo_ref, gp_ref, mod_ref, o_ref,
                          ckb_ref, cvt_ref, mb_ref, lam_init=lam_init)

    @pl.when(ph == 0)
    def _():
        _prep_diff_cache(ck_ref, cv_ref, ckb_ref, cvt_ref)

    _run_phases(ph, project, attend)


def _sample_odd_kernel(x_ref, g_ref, mod_ref, w_ref, qn_ref, kn_ref,
                       cos_ref, sin_ref, cost_ref, sint_ref, cck_ref, ccv_ref, cdk_ref, cdv_ref, sink_ref,
                       xr_ref, wo_ref, gp_ref, o_ref,
                       cqt_ref, ck_ref, cvt_ref, cz_ref, dqt_ref, dk_ref, dvt_ref, dz_ref, m_ref, *, layer):
    g_ref, mod_ref, gp_ref, kn_ref, sink_ref = _layer_rows(
        layer, 1 + pl.program_id(0), g_ref, mod_ref, gp_ref, kn_ref, sink_ref)
    ph = pl.program_id(1)
    per = TM // SEQ
    chunks = TM // LANES

    def project(p):
        _in_odd_kernel(x_ref, g_ref, mod_ref, w_ref, qn_ref, kn_ref,
                       cos_ref, sin_ref, cost_ref, sint_ref,
                       cqt_ref.at[pl.ds(p * per, per)], _rows(ck_ref, p * TM, TM),
                       cvt_ref.at[pl.ds(p * chunks, chunks)], _rows(cz_ref, p * TM, TM),
                       dqt_ref.at[pl.ds(p * per, per)], _rows(dk_ref, p * TM, TM),
                       dvt_ref.at[pl.ds(p * chunks, chunks)], _rows(dz_ref, p * TM, TM), sample=True)

    def attend(t):
        _gqa_sample_body(cqt_ref.at[t], ck_ref, cvt_ref, cck_ref, ccv_ref, _rows(cz_ref, t * TQ, TQ),
                         dqt_ref.at[t], dk_ref, dvt_ref, cdk_ref, cdv_ref, _rows(dz_ref, t * TQ, TQ),
                         sink_ref, xr_ref, wo_ref, gp_ref, mod_ref, o_ref, m_ref, t=t)

    _run_phases(ph, project, attend)


def _sample_layer(x, layer, g_pre, g_post, mod, w_in, w_out, w_idx, head, tail, caches, tables,
                  lam_init=None):
    n = x.shape[0]
    even = layer % 2 == 0
    in_tile = lambda b, ph: b * N_IN + jnp.minimum(ph, N_IN - 1)
    q_tile = lambda b, ph: b * N_Q + jnp.maximum(ph - FIRST_Q, 0)
    tab = lambda b, ph: jnp.minimum(ph, N_IN - 1)
    cos, sin, cos_t, sin_t = tables
    x_spec = pl.BlockSpec((TM, D_MODEL), lambda b, ph: (in_tile(b, ph), 0))
    mod_spec = _whole(mod)
    table_specs = [pl.BlockSpec((TM, B_W), lambda b, ph: (tab(b, ph), 0)),
                   pl.BlockSpec((TM, B_W), lambda b, ph: (tab(b, ph), 0)),
                   pl.BlockSpec((B_W, TM), lambda b, ph: (0, tab(b, ph))),
                   pl.BlockSpec((B_W, TM), lambda b, ph: (0, tab(b, ph)))]
    cache_specs = [pl.BlockSpec((None, None) + c.shape[2:], lambda b, ph: (b, layer // 2, 0, 0))
                   for c in caches]
    res_spec = pl.BlockSpec((TQ, D_MODEL), lambda b, ph: (q_tile(b, ph), 0))
    in_specs = [x_spec]
    args = [x]
    if even:
        hb = TM // HALO
        last = n // HALO - 1
        in_specs += [pl.BlockSpec((HALO, D_MODEL), lambda b, ph: (jnp.maximum(in_tile(b, ph) * hb - 1, 0), 0)),
                     pl.BlockSpec((HALO, D_MODEL),
                                  lambda b, ph: (jnp.minimum((in_tile(b, ph) + 1) * hb, last), 0))]
        args += [x, x]
    in_specs += [_pick(g_pre, layer), mod_spec, _pick(w_in, w_idx)] + [_pick(h, layer // 2) for h in head]
    args += [g_pre, mod, w_in] + list(head)
    in_specs += table_specs + cache_specs + [_pick(t, layer // 2) for t in tail]
    args += [cos, sin, cos_t, sin_t] + list(caches) + list(tail)
    in_specs += [res_spec, _pick(w_out, w_idx), _pick(g_post, layer)]
    args += [x, w_out, g_post]
    seq_wide = lambda w, dt: pltpu.VMEM((DEC_SEQ, w), dt)
    slab = pltpu.VMEM((DEC_SEQ // SEQ, B_W, SEQ), BF16)
    if even:
        scratch = [seq_wide(A_W, BF16), slab, seq_wide(B_W, BF16), slab, seq_wide(B_W, F32),
                   pltpu.VMEM((SUBLANES, SEQ + 3 * SUBLANES, A_W), F32), pltpu.VMEM((SEQ, A_W), F32),
                   pltpu.VMEM((PAST_LEN, B_W), BF16), pltpu.VMEM((B_W, PAST_LEN), BF16),
                   pltpu.VMEM((TQ, B_W), BF16)]
        body = functools.partial(_sample_even_kernel, lam_init=lam_init, layer=layer)
    else:
        chunk = pltpu.VMEM((DEC_SEQ // LANES, KV_W, LANES), BF16)
        scratch = [slab, seq_wide(KV_W, BF16), chunk, seq_wide(C_W, F32),
                   slab, seq_wide(KV_W, BF16), chunk, seq_wide(D_W, F32),
                   pltpu.VMEM((TQ, C_W + D_W), BF16)]
        body = functools.partial(_sample_odd_kernel, layer=layer)
    return pl.pallas_call(
        body,
        grid=(DEC_BATCH, N_PHASE),
        in_specs=in_specs,
        out_specs=pl.BlockSpec((TQ, D_MODEL), lambda b, ph: (q_tile(b, ph), 0)),
        out_shape=jax.ShapeDtypeStruct((n, D_MODEL), F32),
        scratch_shapes=scratch,
        compiler_params=_params("arbitrary", "arbitrary"),
        name=f"sample_layer{layer}",
    )(*args)


def _rope_tables():
    nf = DH // 4
    t = jnp.arange(DEC_SEQ)
    row = (t // GRID_W).astype(F32)
    col = (t % GRID_W).astype(F32)
    inv = ROPE_THETA ** (-jnp.arange(nf, dtype=F32) / nf)
    d = jnp.arange(DH)
    axis = d // (2 * nf)
    second = (d % (2 * nf)) // nf
    f = d % nf
    pos = jnp.where(axis[None, :] == 0, row[:, None], col[:, None])
    ang = pos * inv[f][None, :]
    cos = jnp.cos(ang)
    sin = jnp.where(second[None, :] == 0, -jnp.sin(ang), jnp.sin(ang))
    reps = B_W // DH
    cos = jnp.tile(cos, (1, reps))
    sin = jnp.tile(sin, (1, reps))
    return cos, sin, cos.T, sin.T


def kernel(x_prompt, x_sample, cache_b_k, cache_b_v, cache_c_k, cache_c_v, cache_d_k, cache_d_v, c, c_ctx, norm_pre, norm_post, w_mod, b_mod, w_in_even, a_conv_w, a_conv_b, a_ln_g, a_ln_b, b_lambda, b_subln_g, w_out_even, w_in_odd, c_q_norm, c_k_norm, d_sink, w_out_odd):
    n_even = (DEPTH + 1) // 2
    n_odd = DEPTH // 2
    cond8 = jnp.zeros((SUBLANES, D_MODEL), F32).at[0].set(c_ctx).at[1:1 + DEC_BATCH].set(c)
    n_chunk = BATCH * SEQ // TM
    cond_chunks = jnp.moveaxis(cond8.reshape(SUBLANES, n_chunk, D_MODEL // n_chunk), 1, 0)
    tables = _rope_tables()

    xp = x_prompt.reshape(BATCH * SEQ, D_MODEL)
    xs = x_sample.reshape(DEC_BATCH * DEC_SEQ, D_MODEL)
    feat = lambda a, w: jnp.moveaxis(a.reshape(a.shape[:3] + (w,)), 2, 3)
    cbk = feat(cache_b_k, B_W)
    cbv = cache_b_v.reshape(DEC_BATCH, n_even, PAST_LEN * H_B, 2 * DH)
    cck = feat(cache_c_k, KV_W)
    ccv = feat(cache_c_v, KV_W)
    cdk = feat(cache_d_k, KV_W)
    cdv = feat(cache_d_v, KV_W)

    g_pre, g_post = norm_pre, norm_post
    conv = (jnp.zeros((n_even, 4 * SUBLANES, A_W), F32).at[:, :CONV_K].set(a_conv_w),
            a_conv_b, a_ln_g, a_ln_b)
    subln = b_subln_g
    qkn = (jnp.tile(c_q_norm, (1, C_W // DH)).reshape(n_odd, C_W, 1), jnp.tile(c_k_norm, (1, KV_W // DH)))
    sink = d_sink

    f32_weights = ((w_in_even, w_out_even), (w_in_odd, w_out_odd))
    *weights, mod = _prep_first_layer(f32_weights[0] + (0, cond_chunks, w_mod, b_mod))
    new_even, new_odd = (), ()
    for l in range(DEPTH):
        w_in, w_out = weights
        mod_l = mod
        prep_next = (f32_weights[(l + 1) % 2] + ((l + 1) // 2, cond_chunks, w_mod, b_mod)
                     if l + 1 < DEPTH else None)
        if l % 2 == 0:
            lam_init = 0.8 - 0.6 * math.exp(-0.3 * l)
            outs = _prompt_layer(xp, l, g_pre, g_post, mod_l, w_in, w_out, 0, conv, (b_lambda, subln),
                                 new_even, lam_init, prep_next)
            xp, new_even, (*weights, mod) = outs[0], outs[1:3], outs[3:]
            xs = _sample_layer(xs, l, g_pre, g_post, mod_l, w_in, w_out, 0, conv, (b_lambda, subln),
                               (cbk, cbv), tables, lam_init)
        else:
            outs = _prompt_layer(xp, l, g_pre, g_post, mod_l, w_in, w_out, 0, qkn, (sink,), new_odd,
                                 None, prep_next)
            xp, new_odd, (*weights, mod) = outs[0], outs[1:5], (outs[5:] if prep_next else (None, None, None))
            xs = _sample_layer(xs, l, g_pre, g_post, mod_l, w_in, w_out, 0, qkn, (sink,),
                               (cck, ccv, cdk, cdv), tables)

    def token_major(a, heads):
        return jnp.moveaxis(a.reshape(a.shape[:2] + heads + (DH, SEQ)), -1, 2)

    kt, v = new_even
    ckt, cvt, dkt, dvt = new_odd
    return (xp.reshape(BATCH, SEQ, D_MODEL), xs.reshape(DEC_BATCH, DEC_SEQ, D_MODEL),
            token_major(kt, (H_B, 2)), v.reshape(BATCH, n_even, SEQ, H_B, 2 * DH),
            token_major(ckt, (2,)), token_major(cvt, (2,)), token_major(dkt, (2,)), token_major(dvt, (2,)))
```

```python
import functools
import math

import jax
import jax.numpy as jnp
from jax import lax
from jax.experimental import pallas as pl
from jax.experimental.pallas import tpu as pltpu

F32 = jnp.float32
BF16 = jnp.bfloat16

D_MODEL = 1024
BATCH = 16
SEQ = 256
DEPTH = 4
DEC_BATCH = 2
DEC_SEQ = 1024
PAST_LEN = 512
GRID_W = 64
ROPE_THETA = 10000.0
NORM_EPS = 1e-6
DH = 64
A_W = 512
CONV_K = 31
H_B = 4
B_W = 512
C_W = 512
KV_W = 128
D_W = 512
WINDOW = 128
LOG2E = math.log2(math.e)
QK_SCALE = DH ** -0.5 * LOG2E

LANES = 128
SUBLANES = 8
VMEM_LIMIT = 56 * 1024 * 1024

TM = 512
TQ = SEQ
HALO = 16
GQA_AHEAD = (8, 3)
DIFF_AHEAD = (3, 2)
ROW_CHUNK = 64
DEN_ROWS = 16


def _params(*sem):
    return pltpu.CompilerParams(dimension_semantics=sem, vmem_limit_bytes=VMEM_LIMIT)


def _silu(x):
    return x * jax.nn.sigmoid(x)


def _dot(a, b):
    return jnp.dot(a, b, preferred_element_type=F32)


def _pick(stacked, idx):
    if stacked.ndim == 2:
        return _whole(stacked)
    return pl.BlockSpec((None,) + stacked.shape[1:], lambda *_: (idx,) + (0,) * (stacked.ndim - 1))


def _whole(a):
    return pl.BlockSpec(a.shape, lambda *_: (0,) * a.ndim)


def _layer_rows(layer, mod_row, g_ref, mod_ref, gp_ref, *half):
    one = lambda ref, i: ref.at[pl.ds(i, 1)]
    return ((one(g_ref, layer), one(mod_ref, mod_row), one(gp_ref, layer))
            + tuple(one(r, layer // 2) for r in half))


def _pre_norm(x_ref, g_ref, mod_ref):
    return _modulate(x_ref[...], g_ref, mod_ref)


def _modulate(x, g_ref, mod_ref):
    ms = jnp.mean(x * x, axis=-1, keepdims=True)
    mod = mod_ref[...]
    sh = mod[:, :D_MODEL]
    sc = mod[:, D_MODEL:2 * D_MODEL]
    h = (x * lax.rsqrt(ms + NORM_EPS)) * (g_ref[...] * (1.0 + sc)) + sh
    return h.astype(BF16)


def _rope(x, cos, sin_signed):
    w = x.shape[-1]
    lane = lax.broadcasted_iota(jnp.int32, (1, w), 1)
    first = (lane % 32) < 16
    partner = jnp.where(first, pltpu.roll(x, w - 16, 1), pltpu.roll(x, 16, 1))
    return x * cos + partner * sin_signed


def _rope_t(x, cos_t, sin_t):
    r = x.shape[0]
    row = lax.broadcasted_iota(jnp.int32, (r, 1), 0)
    first = (row % 32) < 16
    partner = jnp.where(first, pltpu.roll(x, r - 16, 0), pltpu.roll(x, 16, 0))
    return x * cos_t + partner * sin_t


def _store_chunks(ref, xt):
    for c in range(xt.shape[1] // LANES):
        ref[c] = xt[:, c * LANES:(c + 1) * LANES]


def _store_per_seq(ref, xt):
    for s in range(xt.shape[1] // SEQ):
        ref[s] = xt[:, s * SEQ:(s + 1) * SEQ]


def _own_slot(ref, slot, fill=True):
    if len(ref.shape) == 3:
        return ref
    for other in range(ref.shape[1]):
        if fill and other != slot:
            ref[:, other] = jnp.zeros((ref.shape[0],) + tuple(ref.shape[2:]), ref.dtype)
    return ref.at[:, slot]


def _glu(ug):
    return ug[:, :A_W] * jax.nn.sigmoid(ug[:, A_W:])


def _conv_mix(pad, az, cw_ref, cb_ref, lg_ref, lb_ref, sh_ref, acc_ref):
    rows = sh_ref.shape[1]
    for b in range(SUBLANES):
        sh_ref[b] = pad[b:b + rows]
    base = HALO - CONV_K // 2
    for c0 in range(0, A_W, LANES):
        cs = slice(c0, c0 + LANES)
        for r0 in range(0, SEQ, ROW_CHUNK):
            acc = jnp.zeros((ROW_CHUNK, LANES), F32) + cb_ref[:, cs]
            for k in range(CONV_K):
                j = k + base
                s = r0 + (j // SUBLANES) * SUBLANES
                acc = acc + sh_ref[j % SUBLANES, s:s + ROW_CHUNK, cs] * cw_ref[k:k + 1, cs]
            acc_ref[r0:r0 + ROW_CHUNK, cs] = acc
    a = acc_ref[...]
    mu = jnp.mean(a, axis=-1, keepdims=True)
    d = a - mu
    var = jnp.mean(d * d, axis=-1, keepdims=True)
    y = d * lax.rsqrt(var + NORM_EPS) * lg_ref[...] + lb_ref[...]
    return _silu(y) * _silu(az)


def _in_even_kernel(x_ref, g_ref, mod_ref, w_ref, cw_ref, cb_ref, lg_ref, lb_ref, *rest,
                    sample, n_alias=0, slot=0, pos=None):
    if sample:
        (xp_ref, xn_ref, cos_ref, sin_ref, cost_ref, sint_ref,
         ma_ref, qt_ref, k_ref, vt_ref, bz_ref, sh_ref, acc_ref) = rest
    else:
        ma_ref, qt_ref, k_ref, kt_ref, v_ref, vt_ref, bz_ref, sh_ref, acc_ref = rest[n_alias:]
        kt_ref = _own_slot(kt_ref, slot)
        v_ref = _own_slot(v_ref, slot)
    hb = _pre_norm(x_ref, g_ref, mod_ref)
    a = _glu(_dot(hb, w_ref[:, 0:2 * A_W]))
    az = _dot(hb, w_ref[:, 2 * A_W:3 * A_W])
    n_sub = TM // SEQ
    if sample:
        tiles_per_seq = DEC_SEQ // TM
        xh = jnp.concatenate([xp_ref[...], xn_ref[...]], axis=0)
        ah = _glu(_dot(_modulate(xh, g_ref, mod_ref), w_ref[:, 0:2 * A_W]))
        prev = jnp.where(pos != 0, ah[:HALO], 0.0)
        nxt = jnp.where(pos != tiles_per_seq - 1, ah[HALO:], 0.0)
        full = jnp.concatenate([prev, a, nxt], axis=0)
        pads = [full[j * SEQ:(j + 1) * SEQ + 2 * HALO] for j in range(n_sub)]
    else:
        zeros = jnp.zeros((HALO, A_W), F32)
        pads = [jnp.concatenate([zeros, a[j * SEQ:(j + 1) * SEQ], zeros], axis=0) for j in range(n_sub)]
    o = 3 * A_W
    q = _dot(hb, w_ref[:, o:o + B_W])
    k = _dot(hb, w_ref[:, o + B_W:o + 2 * B_W])
    v = _dot(hb, w_ref[:, o + 2 * B_W:o + 3 * B_W])
    bz_ref[...] = _dot(hb, w_ref[:, o + 3 * B_W:o + 4 * B_W])
    qt = q.T
    if sample:
        qt = _rope_t(qt, cost_ref[...], sint_ref[...])
        k = _rope(k, cos_ref[...], sin_ref[...])
    else:
        _store_per_seq(kt_ref, k.T)
        for s in range(TM // SEQ):
            for h in range(H_B):
                v_ref[s, pl.ds(h, SEQ, stride=H_B), :] = v[s * SEQ:(s + 1) * SEQ, h * LANES:(h + 1) * LANES]
    _store_per_seq(qt_ref, (qt * QK_SCALE).astype(BF16))
    k_ref[...] = k.astype(BF16)
    _store_per_seq(vt_ref, v.T.astype(BF16))
    for j, pad in enumerate(pads):
        rs = slice(j * SEQ, (j + 1) * SEQ)
        ma_ref[rs, :] = _conv_mix(pad, az[rs], cw_ref, cb_ref, lg_ref, lb_ref, sh_ref, acc_ref).astype(BF16)


def _group_mean_sq(x):
    width = x.shape[-1]
    xx = x * x
    hi = xx.astype(BF16)
    lo = (xx - hi.astype(F32)).astype(BF16)
    r = lax.broadcasted_iota(jnp.int32, (width, width), 0) // DH
    c = lax.broadcasted_iota(jnp.int32, (width, width), 1) // DH
    g = jnp.where(r == c, 1.0, 0.0).astype(BF16)
    return (_dot(hi, g) + _dot(lo, g)) * (1.0 / DH)


def _head_rms_t(xt, gain_col):
    parts = []
    for j in range(xt.shape[0] // DH):
        blk = xt[j * DH:(j + 1) * DH]
        ms = jnp.mean(blk * blk, axis=0, keepdims=True)
        parts.append(blk * lax.rsqrt(ms + NORM_EPS))
    return jnp.concatenate(parts, axis=0) * gain_col


def _in_odd_kernel(x_ref, g_ref, mod_ref, w_ref, qn_ref, kn_ref, *rest, sample, n_alias=0, slot=0):
    if sample:
        (cos_ref, sin_ref, cost_ref, sint_ref,
         cqt_ref, ck_ref, cvt_ref, cz_ref, dqt_ref, dk_ref, dvt_ref, dz_ref) = rest
    else:
        (cqt_ref, ck_ref, ckt_ref, cvt_ref, cz_ref,
         dqt_ref, dk_ref, dkt_ref, dvt_ref, dz_ref) = rest[n_alias:]
        ckt_ref, cvt_ref, dkt_ref, dvt_ref = [_own_slot(r, slot) for r in (ckt_ref, cvt_ref, dkt_ref, dvt_ref)]
    hb = _pre_norm(x_ref, g_ref, mod_ref)
    y = _dot(hb, w_ref[...])
    o = 0
    cqt = _head_rms_t(y[:, o:o + C_W].T, qn_ref[...])
    o += C_W
    ck = y[:, o:o + KV_W]
    ck = ck * lax.rsqrt(_group_mean_sq(ck) + NORM_EPS) * kn_ref[...]
    o += KV_W
    cvt = y[:, o:o + KV_W].T
    o += KV_W
    cz_ref[...] = y[:, o:o + C_W]
    o += C_W
    dqt = y[:, o:o + D_W].T
    o += D_W
    dk = y[:, o:o + KV_W]
    o += KV_W
    dvt = y[:, o:o + KV_W].T
    o += KV_W
    dz_ref[...] = y[:, o:o + D_W]
    if sample:
        cos_t = cost_ref[...]
        sin_t = sint_ref[...]
        cqt = _rope_t(cqt, cos_t, sin_t)
        dqt = _rope_t(dqt, cos_t, sin_t)
        cos = cos_ref[...][:, :KV_W]
        sin = sin_ref[...][:, :KV_W]
        ck = _rope(ck, cos, sin)
        dk = _rope(dk, cos, sin)
        _store_chunks(cvt_ref, cvt.astype(BF16))
        _store_chunks(dvt_ref, dvt.astype(BF16))
    else:
        _store_per_seq(ckt_ref, ck.T)
        _store_per_seq(dkt_ref, dk.T)
        _store_per_seq(cvt_ref, cvt)
        _store_per_seq(dvt_ref, dvt)
    _store_per_seq(cqt_ref, (cqt * QK_SCALE).astype(BF16))
    _store_per_seq(dqt_ref, (dqt * QK_SCALE).astype(BF16))
    ck_ref[...] = ck.astype(BF16)
    dk_ref[...] = dk.astype(BF16)


def _exp_terms(segs, extra=None):
    m = None
    for s in segs:
        mi = jnp.max(s, axis=0, keepdims=True)
        m = mi if m is None else jnp.maximum(m, mi)
    if extra is not None:
        extra = extra * LOG2E
        m = jnp.maximum(m, extra)
    es = [jnp.exp2(s - m) for s in segs]
    return es, (None if extra is None else jnp.exp2(extra - m))


def _keep_rows(xt, lo, hi):
    zeros = lambda r: jnp.zeros((r, xt.shape[1]), xt.dtype)
    parts = []
    if lo > 0:
        parts.append(zeros(lo))
    parts.append(xt[lo:hi])
    if hi < xt.shape[0]:
        parts.append(zeros(xt.shape[0] - hi))
    return jnp.concatenate(parts, axis=0)


def _pipelined(n, scores, finish, ahead):
    ready = [scores(j) for j in range(min(ahead, n))]
    for j in range(n):
        if j + ahead < n:
            ready.append(scores(j + ahead))
        finish(j, ready.pop(0))


def _diff_attn(qt_ref, kvs, z_ref, lam_ref, g_ref, o_ref, *, lam_init):
    lv = lam_ref[...]
    lam = (jnp.exp(jnp.sum(lv[0:1] * lv[1:2], axis=-1, keepdims=True))
           - jnp.exp(jnp.sum(lv[2:3] * lv[3:4], axis=-1, keepdims=True)) + lam_init)

    def scores(h):
        cs = slice(h * LANES, (h + 1) * LANES)
        qt = qt_ref[cs, :]
        ks = [get_k(cs) for get_k, _ in kvs]
        return [[_dot(kk, _keep_rows(qt, c * DH, (c + 1) * DH)) for kk in ks] for c in range(2)]

    def finish(h, ss):
        cs = slice(h * LANES, (h + 1) * LANES)
        acc = [None, None]
        for c in range(2):
            es, _ = _exp_terms(ss[c])
            for e, (_, get_vt) in zip(es, kvs):
                vt = get_vt(cs)
                vt1 = jnp.concatenate([vt, jnp.ones((DEN_ROWS, vt.shape[1]), BF16)], axis=0)
                oi = _dot(vt1, e.astype(BF16))
                acc[c] = oi if acc[c] is None else acc[c] + oi
        v_rows = 2 * DH
        r0 = 1.0 / acc[0][v_rows:v_rows + 1]
        r1 = lam / acc[1][v_rows:v_rows + 1]
        ot = acc[0][:v_rows] * r0 - acc[1][:v_rows] * r1
        ms = jnp.mean(ot * ot, axis=0, keepdims=True)
        o = (ot * lax.rsqrt(ms + NORM_EPS)).T
        o = (o * g_ref[...]) * (1.0 - lam_init)
        o_ref[:, cs] = (o * _silu(z_ref[:, cs])).astype(o_ref.dtype)

    _pipelined(H_B, scores, finish, DIFF_AHEAD[len(kvs) - 1])


def _post_residual(o, x_ref, g_ref, mod_ref, o_ref):
    ms = jnp.mean(o * o, axis=-1, keepdims=True)
    r = o * lax.rsqrt(ms + NORM_EPS) * g_ref[...]
    gate = mod_ref[...][:, 2 * D_MODEL:]
    o_ref[...] = x_ref[...] + gate * r


def _diff_prompt_kernel(qt_ref, k_ref, vt_ref, z_ref, lam_ref, sg_ref, ma_ref, x_ref, w_ref, g_ref,
                        mod_ref, o_ref, mb_ref, *, lam_init):
    oa = _dot(ma_ref[...], w_ref[0:A_W, :])
    for s in range(qt_ref.shape[0]):
        rs = pl.ds(s * SEQ, SEQ)
        kv = (lambda cs, s=s: k_ref[s * SEQ:(s + 1) * SEQ, cs], lambda cs, s=s: vt_ref[s, cs, :])
        _diff_attn(qt_ref.at[s], [kv], z_ref.at[rs], lam_ref, sg_ref, mb_ref.at[rs], lam_init=lam_init)
    _post_residual(oa + _dot(mb_ref[...], w_ref[A_W:, :]), x_ref, g_ref, mod_ref, o_ref)


def _prep_diff_cache(ck_ref, cv_ref, ckb_ref, cvt_ref):
    ckb_ref[...] = ck_ref[...].T.astype(BF16)
    for h in range(H_B):
        cs = slice(h * LANES, (h + 1) * LANES)
        cvt_ref[cs, :] = cv_ref[pl.ds(h, PAST_LEN, stride=H_B), :].T.astype(BF16)


def _diff_sample_body(qt_ref, k_ref, vt_ref, z_ref, lam_ref, sg_ref, ma_ref, x_ref, w_ref, g_ref,
                      mod_ref, o_ref, ckb_ref, cvt_ref, mb_ref, *, lam_init):
    ctx = (lambda cs: ckb_ref[:, cs], lambda cs: cvt_ref[cs, :])
    loc = (lambda cs: k_ref[:, cs],
           lambda cs: jnp.concatenate([vt_ref[c, cs, :] for c in range(vt_ref.shape[0])], axis=1))
    oa = _dot(ma_ref[...], w_ref[0:A_W, :])
    _diff_attn(qt_ref, [ctx, loc], z_ref, lam_ref, sg_ref, mb_ref, lam_init=lam_init)
    _post_residual(oa + _dot(mb_ref[...], w_ref[A_W:, :]), x_ref, g_ref, mod_ref, o_ref)


def _gqa(qt_ref, segs, z_ref, o_ref, sink_ref=None):
    halves = []

    def scores(j):
        n = j // 4
        qj = qt_ref[j * DH:(j + 1) * DH, :]
        zero = jnp.zeros_like(qj)
        qz = jnp.concatenate([qj, zero] if n == 0 else [zero, qj], axis=0)
        return [_dot(k, qz) for k, _, _ in segs]

    def finish(j, ss):
        n = j // 4
        ss = [s if valid is None else jnp.where(valid, s, -jnp.inf)
              for s, (_, _, valid) in zip(ss, segs)]
        extra = None if sink_ref is None else sink_ref[:, j:j + 1]
        es, den = _exp_terms(ss, extra)
        ot = None
        for e, (_, vt, _) in zip(es, segs):
            ones = jnp.ones((DEN_ROWS, vt.shape[1]), BF16)
            vt1 = jnp.concatenate([vt[n * DH:(n + 1) * DH], ones], axis=0)
            oi = _dot(vt1, e.astype(BF16))
            ot = oi if ot is None else ot + oi
        den = ot[DH:DH + 1] if den is None else den + ot[DH:DH + 1]
        halves.append(ot[:DH] * (1.0 / den))
        if j % 2 == 1:
            cs = slice((j // 2) * LANES, (j // 2 + 1) * LANES)
            o_pair = jnp.concatenate(halves[-2:], axis=0).T
            o_ref[:, cs] = (o_pair * _silu(z_ref[:, cs])).astype(o_ref.dtype)

    _pipelined(2 * 4, scores, finish, GQA_AHEAD[len(segs) - 1])


def _gqa_prompt_kernel(cqt_ref, ck_ref, cvt_ref, cz_ref, dqt_ref, dk_ref, dvt_ref, dz_ref, sink_ref,
                       x_ref, w_ref, g_ref, mod_ref, o_ref, m_ref):
    for s in range(cqt_ref.shape[0]):
        rows = slice(s * SEQ, (s + 1) * SEQ)
        rs = pl.ds(s * SEQ, SEQ)
        seg = lambda k_ref, vt_ref: (k_ref[rows, :], vt_ref[s].astype(BF16), None)
        _gqa(cqt_ref.at[s], [seg(ck_ref, cvt_ref)], cz_ref.at[rs], m_ref.at[rs, pl.ds(0, C_W)])
        _gqa(dqt_ref.at[s], [seg(dk_ref, dvt_ref)], dz_ref.at[rs], m_ref.at[rs, pl.ds(C_W, D_W)], sink_ref)
    _post_residual(_dot(m_ref[...], w_ref[...]), x_ref, g_ref, mod_ref, o_ref)


def _gqa_sample_body(cqt_ref, ck_ref, cvt_ref, cck_ref, ccv_ref, cz_ref,
                     dqt_ref, dk_ref, dvt_ref, cdk_ref, cdv_ref, dz_ref, sink_ref,
                     x_ref, w_ref, g_ref, mod_ref, o_ref, m_ref, *, t):
    oc_ref = m_ref.at[:, pl.ds(0, C_W)]
    od_ref = m_ref.at[:, pl.ds(C_W, D_W)]
    ctx = lambda kt_ref, vt_ref: (kt_ref[...].T.astype(BF16), vt_ref[...].astype(BF16), None)
    n_chunks = DEC_SEQ // LANES
    cvt = jnp.concatenate([cvt_ref[c] for c in range(n_chunks)], axis=1)
    _gqa(cqt_ref, [ctx(cck_ref, ccv_ref), (ck_ref[...], cvt, None)], cz_ref, oc_ref)
    span = 2 * TQ
    t0 = t * TQ
    ws = pl.multiple_of(jnp.clip(t0 - WINDOW, 0, DEC_SEQ - span), WINDOW)
    kpos = ws + lax.broadcasted_iota(jnp.int32, (span, TQ), 0)
    qpos = t0 + lax.broadcasted_iota(jnp.int32, (span, TQ), 1)
    valid = jnp.abs(qpos - kpos) <= WINDOW
    c0 = ws // LANES
    dvt = jnp.concatenate([dvt_ref[c0 + c] for c in range(span // LANES)], axis=1)
    _gqa(dqt_ref, [ctx(cdk_ref, cdv_ref), (dk_ref[pl.ds(ws, span), :], dvt, valid)],
         dz_ref, od_ref, sink_ref)
    _post_residual(_dot(m_ref[...], w_ref[...]), x_ref, g_ref, mod_ref, o_ref)


N_PREP_IN = 5
N_PREP_OUT = 3


def _prep_next_layer(rest, prep, n_alias, n_out, next_layer):
    if not prep:
        return rest[n_alias:]
    w_in_src, w_out_src, cond_ref, wm_ref, bm_ref = rest[:N_PREP_IN]
    rest = rest[N_PREP_IN + n_alias:]
    w_in_dst, w_out_dst, mod_dst = rest[n_out:n_out + N_PREP_OUT]
    w_in_dst[...] = w_in_src[...].astype(BF16)
    w_out_dst[...] = w_out_src[...].astype(BF16)
    part = _dot(_silu(cond_ref[...]).astype(BF16), wm_ref[...].astype(BF16))
    first = pl.program_id(0) == 0

    @pl.when(first)
    def _():
        mod_dst[...] = part + bm_ref[next_layer:next_layer + 1, :]

    @pl.when(jnp.logical_not(first))
    def _():
        mod_dst[...] += part

    return rest[:n_out] + rest[n_out + N_PREP_OUT:]


def _prep_specs(prep, layer, n_steps):
    w_in_f32, w_out_f32, src_idx, cond_chunks, w_mod, b_mod = prep
    chunk = D_MODEL // n_steps
    assert cond_chunks.shape == (n_steps, SUBLANES, chunk)
    in_specs, args, outs = [], [], []
    for src in (w_in_f32, w_out_f32):
        in_specs.append(pl.BlockSpec((None, chunk, src.shape[2]), lambda i: (src_idx, i, 0)))
        args.append(src)
        outs.append((pl.BlockSpec((None, chunk, src.shape[2]), lambda i: (0, i, 0)),
                     jax.ShapeDtypeStruct((1,) + src.shape[1:], BF16)))
    in_specs += [pl.BlockSpec((None, SUBLANES, chunk), lambda i: (i, 0, 0)),
                 pl.BlockSpec((None, chunk, 3 * D_MODEL), lambda i: (layer, i, 0)),
                 _whole(b_mod)]
    args += [cond_chunks, w_mod, b_mod]
    outs.append((pl.BlockSpec((SUBLANES, 3 * D_MODEL), lambda i: (0, 0)),
                 jax.ShapeDtypeStruct((SUBLANES, 3 * D_MODEL), F32)))
    return in_specs, args, outs


def _prep_first_kernel(*refs):
    _prep_next_layer(refs, True, 0, 0, 0)


def _prep_first_layer(prep):
    n_steps = prep[3].shape[0]
    in_specs, args, outs = _prep_specs(prep, 0, n_steps)
    return pl.pallas_call(
        _prep_first_kernel,
        grid=(n_steps,),
        in_specs=in_specs,
        out_specs=[o[0] for o in outs],
        out_shape=[o[1] for o in outs],
        compiler_params=_params("arbitrary"),
        name="prep_layer0",
    )(*args)


def _prompt_even_kernel(x_ref, g_ref, mod_ref, w_ref, cw_ref, cb_ref, lg_ref, lb_ref, lam_ref, sg_ref,
                        wo_ref, gp_ref, *rest, lam_init, layer, n_alias, slot, prep):
    g_ref, mod_ref, gp_ref, cb_ref, lg_ref, lb_ref, sg_ref = _layer_rows(
        layer, 0, g_ref, mod_ref, gp_ref, cb_ref, lg_ref, lb_ref, sg_ref)
    rest = _prep_next_layer(rest, prep, n_alias, 3, layer + 1)
    (o_ref, kt_ref, v_ref,
     ma_ref, qt_ref, k_ref, vt_ref, bz_ref, sh_ref, acc_ref, mb_ref) = rest
    _in_even_kernel(x_ref, g_ref, mod_ref, w_ref, cw_ref, cb_ref, lg_ref, lb_ref,
                    ma_ref, qt_ref, k_ref, kt_ref, v_ref, vt_ref, bz_ref, sh_ref, acc_ref,
                    sample=False, slot=slot)
    _diff_prompt_kernel(qt_ref, k_ref, vt_ref, bz_ref, lam_ref, sg_ref, ma_ref, x_ref, wo_ref, gp_ref,
                        mod_ref, o_ref, mb_ref, lam_init=lam_init)


def _prompt_odd_kernel(x_ref, g_ref, mod_ref, w_ref, qn_ref, kn_ref, sink_ref, wo_ref, gp_ref, *rest,
                       layer, n_alias, slot, prep):
    g_ref, mod_ref, gp_ref, kn_ref, sink_ref = _layer_rows(layer, 0, g_ref, mod_ref, gp_ref, kn_ref, sink_ref)
    rest = _prep_next_layer(rest, prep, n_alias, 5, layer + 1)
    (o_ref, ckt_ref, cvt_ref, dkt_ref, dvt_ref,
     cqt_ref, ck_ref, cz_ref, dqt_ref, dk_ref, dz_ref, m_ref) = rest
    _in_odd_kernel(x_ref, g_ref, mod_ref, w_ref, qn_ref, kn_ref,
                   cqt_ref, ck_ref, ckt_ref, cvt_ref, cz_ref, dqt_ref, dk_ref, dkt_ref, dvt_ref, dz_ref,
                   sample=False, slot=slot)
    _gqa_prompt_kernel(cqt_ref, ck_ref, _own_slot(cvt_ref, slot, fill=False), cz_ref,
                       dqt_ref, dk_ref, _own_slot(dvt_ref, slot, fill=False), dz_ref, sink_ref,
                       x_ref, wo_ref, gp_ref, mod_ref, o_ref, m_ref)


def _prompt_layer(x, layer, g_pre, g_post, mod, w_in, w_out, w_idx, head, tail, carry, lam_init=None,
                  prep_next=None):
    n = x.shape[0]
    even = layer % 2 == 0
    row = lambda i: (i, 0)
    n_seq = TM // SEQ
    params = (g_pre, w_in) + tuple(head) + tuple(tail) + (w_out, g_post)
    idx = (layer, w_idx) + (layer // 2,) * (len(head) + len(tail)) + (w_idx, layer)
    specs = [_pick(p, i) for p, i in zip(params, idx)]
    in_specs = [pl.BlockSpec((TM, D_MODEL), row), specs[0],
                _whole(mod)] + specs[1:]
    args = [x, g_pre, mod, w_in] + list(head) + list(tail) + [w_out, g_post]
    n_layers = (DEPTH + 1 - layer % 2) // 2
    if carry:
        cache_spec = lambda r, c: pl.BlockSpec((n_seq, None, r, c), lambda i: (i, layer // 2, 0, 0))
    else:
        cache_spec = lambda r, c: pl.BlockSpec((n_seq, n_layers, r, c), lambda i: (i, 0, 0, 0))
    cache = lambda r, c: (cache_spec(r, c), jax.ShapeDtypeStruct((n // SEQ, n_layers, r, c), F32))
    outs = [(pl.BlockSpec((TM, D_MODEL), row), jax.ShapeDtypeStruct((n, D_MODEL), F32))]
    wide = pltpu.VMEM((TM, B_W), BF16)
    slab = pltpu.VMEM((n_seq, B_W, SEQ), BF16)
    if even:
        outs += [cache(B_W, SEQ), cache(SEQ * H_B, LANES)]
        scratch = [wide, slab, wide, slab, pltpu.VMEM((TM, B_W), F32),
                   pltpu.VMEM((SUBLANES, SEQ + 3 * SUBLANES, A_W), F32), pltpu.VMEM((SEQ, A_W), F32), wide]
        body = functools.partial(_prompt_even_kernel, lam_init=lam_init)
    else:
        outs += [cache(KV_W, SEQ)] * 4
        narrow = pltpu.VMEM((TM, KV_W), BF16)
        gate = pltpu.VMEM((TM, C_W), F32)
        scratch = [slab, narrow, gate, slab, narrow, gate, pltpu.VMEM((TM, C_W + D_W), BF16)]
        body = _prompt_odd_kernel
    if prep_next is not None:
        prep_in, prep_args, prep_outs = _prep_specs(prep_next, layer + 1, n // TM)
        in_specs += prep_in
        args += prep_args
        outs += prep_outs
    aliases = {}
    for j, a in enumerate(carry):
        aliases[len(args)] = 1 + j
        in_specs.append(pl.BlockSpec(memory_space=pl.ANY))
        args.append(a)
    return pl.pallas_call(
        functools.partial(body, layer=layer, n_alias=len(carry), slot=layer // 2,
                          prep=prep_next is not None),
        grid=(n // TM,),
        in_specs=in_specs,
        out_specs=[o[0] for o in outs],
        out_shape=[o[1] for o in outs],
        scratch_shapes=scratch,
        input_output_aliases=aliases,
        compiler_params=_params("arbitrary"),
        name=f"prompt_layer{layer}",
    )(*args)


N_IN = DEC_SEQ // TM
N_Q = DEC_SEQ // TQ
FIRST_Q = N_IN - 1
N_PHASE = N_IN + N_Q - 1


def _rows(ref, start, size):
    if isinstance(start, int):
        return ref.at[pl.ds(start, size)]
    return ref.at[pl.ds(pl.multiple_of(start, size), size)]


def _run_phases(ph, project, attend):
    if FIRST_Q > 0:
        @pl.when(ph < FIRST_Q)
        def _():
            project(ph)

    @pl.when(ph == FIRST_Q)
    def _():
        project(FIRST_Q)
        attend(0)

    @pl.when(ph > FIRST_Q)
    def _():
        attend(ph - FIRST_Q)


def _sample_even_kernel(x_ref, xp_ref, xn_ref, g_ref, mod_ref, w_ref, cw_ref, cb_ref, lg_ref, lb_ref,
                        cos_ref, sin_ref, cost_ref, sint_ref, ck_ref, cv_ref, lam_ref, sg_ref,
                        xr_ref, wo_ref, gp_ref, o_ref,
                        ma_ref, qt_ref, k_ref, vt_ref, bz_ref, sh_ref, acc_ref, ckb_ref, cvt_ref, mb_ref,
                        *, lam_init, layer):
    g_ref, mod_ref, gp_ref, cb_ref, lg_ref, lb_ref, sg_ref = _layer_rows(
        layer, 1 + pl.program_id(0), g_ref, mod_ref, gp_ref, cb_ref, lg_ref, lb_ref, sg_ref)
    ph = pl.program_id(1)
    per = TM // SEQ

    def project(p):
        _in_even_kernel(x_ref, g_ref, mod_ref, w_ref, cw_ref, cb_ref, lg_ref, lb_ref,
                        xp_ref, xn_ref, cos_ref, sin_ref, cost_ref, sint_ref,
                        _rows(ma_ref, p * TM, TM), qt_ref.at[pl.ds(p * per, per)],
                        _rows(k_ref, p * TM, TM), vt_ref.at[pl.ds(p * per, per)],
                        _rows(bz_ref, p * TM, TM), sh_ref, acc_ref, sample=True, pos=p)

    def attend(t):
        _diff_sample_body(qt_ref.at[t], k_ref, vt_ref, _rows(bz_ref, t * TQ, TQ), lam_ref, sg_ref,
                          _rows(ma_ref, t * TQ, TQ), xr_ref, wo_ref, gp_ref, mod_ref, o_ref,
                          ckb_ref, cvt_ref, mb_ref, lam_init=lam_init)

    @pl.when(ph == 0)
    def _():
        _prep_diff_cache(ck_ref, cv_ref, ckb_ref, cvt_ref)

    _run_phases(ph, project, attend)


def _sample_odd_kernel(x_ref, g_ref, mod_ref, w_ref, qn_ref, kn_ref,
                       cos_ref, sin_ref, cost_ref, sint_ref, cck_ref, ccv_ref, cdk_ref, cdv_ref, sink_ref,
                       xr_ref, wo_ref, gp_ref, o_ref,
                       cqt_ref, ck_ref, cvt_ref, cz_ref, dqt_ref, dk_ref, dvt_ref, dz_ref, m_ref, *, layer):
    g_ref, mod_ref, gp_ref, kn_ref, sink_ref = _layer_rows(
        layer, 1 + pl.program_id(0), g_ref, mod_ref, gp_ref, kn_ref, sink_ref)
    ph = pl.program_id(1)
    per = TM // SEQ
    chunks = TM // LANES

    def project(p):
        _in_odd_kernel(x_ref, g_ref, mod_ref, w_ref, qn_ref, kn_ref,
                       cos_ref, sin_ref, cost_ref, sint_ref,
                       cqt_ref.at[pl.ds(p * per, per)], _rows(ck_ref, p * TM, TM),
                       cvt_ref.at[pl.ds(p * chunks, chunks)], _rows(cz_ref, p * TM, TM),
                       dqt_ref.at[pl.ds(p * per, per)], _rows(dk_ref, p * TM, TM),
                       dvt_ref.at[pl.ds(p * chunks, chunks)], _rows(dz_ref, p * TM, TM), sample=True)

    def attend(t):
        _gqa_sample_body(cqt_ref.at[t], ck_ref, cvt_ref, cck_ref, ccv_ref, _rows(cz_ref, t * TQ, TQ),
                         dqt_ref.at[t], dk_ref, dvt_ref, cdk_ref, cdv_ref, _rows(dz_ref, t * TQ, TQ),
                         sink_ref, xr_ref, wo_ref, gp_ref, mod_ref, o_ref, m_ref, t=t)

    _run_phases(ph, project, attend)


def _sample_layer(x, layer, g_pre, g_post, mod, w_in, w_out, w_idx, head, tail, caches, tables,
                  lam_init=None):
    n = x.shape[0]
    even = layer % 2 == 0
    in_tile = lambda b, ph: b * N_IN + jnp.minimum(ph, N_IN - 1)
    q_tile = lambda b, ph: b * N_Q + jnp.maximum(ph - FIRST_Q, 0)
    tab = lambda b, ph: jnp.minimum(ph, N_IN - 1)
    cos, sin, cos_t, sin_t = tables
    x_spec = pl.BlockSpec((TM, D_MODEL), lambda b, ph: (in_tile(b, ph), 0))
    mod_spec = _whole(mod)
    table_specs = [pl.BlockSpec((TM, B_W), lambda b, ph: (tab(b, ph), 0)),
                   pl.BlockSpec((TM, B_W), lambda b, ph: (tab(b, ph), 0)),
                   pl.BlockSpec((B_W, TM), lambda b, ph: (0, tab(b, ph))),
                   pl.BlockSpec((B_W, TM), lambda b, ph: (0, tab(b, ph)))]
    cache_specs = [pl.BlockSpec((None, None) + c.shape[2:], lambda b, ph: (b, layer // 2, 0, 0))
                   for c in caches]
    res_spec = pl.BlockSpec((TQ, D_MODEL), lambda b, ph: (q_tile(b, ph), 0))
    in_specs = [x_spec]
    args = [x]
    if even:
        hb = TM // HALO
        last = n // HALO - 1
        in_specs += [pl.BlockSpec((HALO, D_MODEL), lambda b, ph: (jnp.maximum(in_tile(b, ph) * hb - 1, 0), 0)),
                     pl.BlockSpec((HALO, D_MODEL),
                                  lambda b, ph: (jnp.minimum((in_tile(b, ph) + 1) * hb, last), 0))]
        args += [x, x]
    in_specs += [_pick(g_pre, layer), mod_spec, _pick(w_in, w_idx)] + [_pick(h, layer // 2) for h in head]
    args += [g_pre, mod, w_in] + list(head)
    in_specs += table_specs + cache_specs + [_pick(t, layer // 2) for t in tail]
    args += [cos, sin, cos_t, sin_t] + list(caches) + list(tail)
    in_specs += [res_spec, _pick(w_out, w_idx), _pick(g_post, layer)]
    args += [x, w_out, g_post]
    seq_wide = lambda w, dt: pltpu.VMEM((DEC_SEQ, w), dt)
    slab = pltpu.VMEM((DEC_SEQ // SEQ, B_W, SEQ), BF16)
    if even:
        scratch = [seq_wide(A_W, BF16), slab, seq_wide(B_W, BF16), slab, seq_wide(B_W, F32),
                   pltpu.VMEM((SUBLANES, SEQ + 3 * SUBLANES, A_W), F32), pltpu.VMEM((SEQ, A_W), F32),
                   pltpu.VMEM((PAST_LEN, B_W), BF16), pltpu.VMEM((B_W, PAST_LEN), BF16),
                   pltpu.VMEM((TQ, B_W), BF16)]
        body = functools.partial(_sample_even_kernel, lam_init=lam_init, layer=layer)
    else:
        chunk = pltpu.VMEM((DEC_SEQ // LANES, KV_W, LANES), BF16)
        scratch = [slab, seq_wide(KV_W, BF16), chunk, seq_wide(C_W, F32),
                   slab, seq_wide(KV_W, BF16), chunk, seq_wide(D_W, F32),
                   pltpu.VMEM((TQ, C_W + D_W), BF16)]
        body = functools.partial(_sample_odd_kernel, layer=layer)
    return pl.pallas_call(
        body,
        grid=(DEC_BATCH, N_PHASE),
        in_specs=in_specs,
        out_specs=pl.BlockSpec((TQ, D_MODEL), lambda b, ph: (q_tile(b, ph), 0)),
        out_shape=jax.ShapeDtypeStruct((n, D_MODEL), F32),
        scratch_shapes=scratch,
        compiler_params=_params("arbitrary", "arbitrary"),
        name=f"sample_layer{layer}",
    )(*args)


def _rope_tables():
    nf = DH // 4
    t = jnp.arange(DEC_SEQ)
    row = (t // GRID_W).astype(F32)
    col = (t % GRID_W).astype(F32)
    inv = ROPE_THETA ** (-jnp.arange(nf, dtype=F32) / nf)
    d = jnp.arange(DH)
    axis = d // (2 * nf)
    second = (d % (2 * nf)) // nf
    f = d % nf
    pos = jnp.where(axis[None, :] == 0, row[:, None], col[:, None])
    ang = pos * inv[f][None, :]
    cos = jnp.cos(ang)
    sin = jnp.where(second[None, :] == 0, -jnp.sin(ang), jnp.sin(ang))
    reps = B_W // DH
    cos = jnp.tile(cos, (1, reps))
    sin = jnp.tile(sin, (1, reps))
    return cos, sin, cos.T, sin.T


def kernel(x_prompt, x_sample, cache_b_k, cache_b_v, cache_c_k, cache_c_v, cache_d_k, cache_d_v, c, c_ctx, norm_pre, norm_post, w_mod, b_mod, w_in_even, a_conv_w, a_conv_b, a_ln_g, a_ln_b, b_lambda, b_subln_g, w_out_even, w_in_odd, c_q_norm, c_k_norm, d_sink, w_out_odd):
    n_even = (DEPTH + 1) // 2
    n_odd = DEPTH // 2
    cond8 = jnp.zeros((SUBLANES, D_MODEL), F32).at[0].set(c_ctx).at[1:1 + DEC_BATCH].set(c)
    n_chunk = BATCH * SEQ // TM
    cond_chunks = jnp.moveaxis(cond8.reshape(SUBLANES, n_chunk, D_MODEL // n_chunk), 1, 0)
    tables = _rope_tables()

    xp = x_prompt.reshape(BATCH * SEQ, D_MODEL)
    xs = x_sample.reshape(DEC_BATCH * DEC_SEQ, D_MODEL)
    feat = lambda a, w: jnp.moveaxis(a.reshape(a.shape[:3] + (w,)), 2, 3)
    cbk = feat(cache_b_k, B_W)
    cbv = cache_b_v.reshape(DEC_BATCH, n_even, PAST_LEN * H_B, 2 * DH)
    cck = feat(cache_c_k, KV_W)
    ccv = feat(cache_c_v, KV_W)
    cdk = feat(cache_d_k, KV_W)
    cdv = feat(cache_d_v, KV_W)

    g_pre, g_post = norm_pre, norm_post
    conv = (jnp.zeros((n_even, 4 * SUBLANES, A_W), F32).at[:, :CONV_K].set(a_conv_w),
            a_conv_b, a_ln_g, a_ln_b)
    subln = b_subln_g
    qkn = (jnp.tile(c_q_norm, (1, C_W // DH)).reshape(n_odd, C_W, 1), jnp.tile(c_k_norm, (1, KV_W // DH)))
    sink = d_sink

    f32_weights = ((w_in_even, w_out_even), (w_in_odd, w_out_odd))
    *weights, mod = _prep_first_layer(f32_weights[0] + (0, cond_chunks, w_mod, b_mod))
    new_even, new_odd = (), ()
    for l in range(DEPTH):
        w_in, w_out = weights
        mod_l = mod
        prep_next = (f32_weights[(l + 1) % 2] + ((l + 1) // 2, cond_chunks, w_mod, b_mod)
                     if l + 1 < DEPTH else None)
        if l % 2 == 0:
            lam_init = 0.8 - 0.6 * math.exp(-0.3 * l)
            outs = _prompt_layer(xp, l, g_pre, g_post, mod_l, w_in, w_out, 0, conv, (b_lambda, subln),
                                 new_even, lam_init, prep_next)
            xp, new_even, (*weights, mod) = outs[0], outs[1:3], outs[3:]
            xs = _sample_layer(xs, l, g_pre, g_post, mod_l, w_in, w_out, 0, conv, (b_lambda, subln),
                               (cbk, cbv), tables, lam_init)
        else:
            outs = _prompt_layer(xp, l, g_pre, g_post, mod_l, w_in, w_out, 0, qkn, (sink,), new_odd,
                                 None, prep_next)
            xp, new_odd, (*weights, mod) = outs[0], outs[1:5], (outs[5:] if prep_next else (None, None, None))
            xs = _sample_layer(xs, l, g_pre, g_post, mod_l, w_in, w_out, 0, qkn, (sink,),
                               (cck, ccv, cdk, cdv), tables)

    def token_major(a, heads):
        return jnp.moveaxis(a.reshape(a.shape[:2] + heads + (DH, SEQ)), -1, 2)

    kt, v = new_even
    ckt, cvt, dkt, dvt = new_odd
    return (xp.reshape(BATCH, SEQ, D_MODEL), xs.reshape(DEC_BATCH, DEC_SEQ, D_MODEL),
            token_major(kt, (H_B, 2)), v.reshape(BATCH, n_even, SEQ, H_B, 2 * DH),
            token_major(ckt, (2,)), token_major(cvt, (2,)), token_major(dkt, (2,)), token_major(dvt, (2,)))
```

```python
import functools
import math

import jax
import jax.numpy as jnp
from jax import lax
from jax.experimental import pallas as pl
from jax.experimental.pallas import tpu as pltpu

F32 = jnp.float32
BF16 = jnp.bfloat16

D_MODEL = 1024
BATCH = 16
SEQ = 256
DEPTH = 4
DEC_BATCH = 2
DEC_SEQ = 1024
PAST_LEN = 512
GRID_W = 64
ROPE_THETA = 10000.0
NORM_EPS = 1e-6
DH = 64
A_W = 512
CONV_K = 31
H_B = 4
B_W = 512
C_W = 512
KV_W = 128
D_W = 512
WINDOW = 128
LOG2E = math.log2(math.e)
QK_SCALE = DH ** -0.5 * LOG2E

LANES = 128
SUBLANES = 8
VMEM_LIMIT = 56 * 1024 * 1024

TM = 512
TQ = SEQ
HALO = 16
GQA_AHEAD = (8, 3)
DIFF_AHEAD = (3, 2)
ROW_CHUNK = 64
DEN_ROWS = 16


def _params(*sem):
    return pltpu.CompilerParams(dimension_semantics=sem, vmem_limit_bytes=VMEM_LIMIT)


def _silu(x):
    return x * jax.nn.sigmoid(x)


def _dot(a, b):
    return jnp.dot(a, b, preferred_element_type=F32)


def _pick(stacked, idx):
    if stacked.ndim == 2:
        return _whole(stacked)
    return pl.BlockSpec((None,) + stacked.shape[1:], lambda *_: (idx,) + (0,) * (stacked.ndim - 1))


def _whole(a):
    return pl.BlockSpec(a.shape, lambda *_: (0,) * a.ndim)


def _layer_rows(layer, mod_row, g_ref, mod_ref, gp_ref, *half):
    one = lambda ref, i: ref.at[pl.ds(i, 1)]
    return ((one(g_ref, layer), one(mod_ref, mod_row), one(gp_ref, layer))
            + tuple(one(r, layer // 2) for r in half))


def _pre_norm(x_ref, g_ref, mod_ref):
    return _modulate(x_ref[...], g_ref, mod_ref)


def _modulate(x, g_ref, mod_ref):
    ms = jnp.mean(x * x, axis=-1, keepdims=True)
    mod = mod_ref[...]
    sh = mod[:, :D_MODEL]
    sc = mod[:, D_MODEL:2 * D_MODEL]
    h = (x * lax.rsqrt(ms + NORM_EPS)) * (g_ref[...] * (1.0 + sc)) + sh
    return h.astype(BF16)


def _rope(x, cos, sin_signed):
    lane = lax.broadcasted_iota(jnp.int32, (1, LANES), 1)
    first = (lane % 32) < 16
    blocks = []
    for c in range(0, x.shape[-1], LANES):
        xb = x[:, c:c + LANES]
        partner = jnp.where(first, pltpu.roll(xb, LANES - 16, 1), pltpu.roll(xb, 16, 1))
        blocks.append(xb * cos + partner * sin_signed)
    return blocks[0] if len(blocks) == 1 else jnp.concatenate(blocks, axis=1)


def _rope_t(x, cos_t, sin_t):
    row = lax.broadcasted_iota(jnp.int32, (LANES, 1), 0)
    first = (row % 32) < 16
    blocks = []
    for r in range(0, x.shape[0], LANES):
        xb = x[r:r + LANES]
        partner = jnp.where(first, pltpu.roll(xb, LANES - 16, 0), pltpu.roll(xb, 16, 0))
        blocks.append(xb * cos_t + partner * sin_t)
    return jnp.concatenate(blocks, axis=0)


def _store_chunks(ref, xt):
    for c in range(xt.shape[1] // LANES):
        ref[c] = xt[:, c * LANES:(c + 1) * LANES]


def _store_per_seq(ref, xt):
    for s in range(xt.shape[1] // SEQ):
        ref[s] = xt[:, s * SEQ:(s + 1) * SEQ]


def _own_slot(ref, slot, fill=True):
    if len(ref.shape) == 3:
        return ref
    for other in range(ref.shape[1]):
        if fill and other != slot:
            ref[:, other] = jnp.zeros((ref.shape[0],) + tuple(ref.shape[2:]), ref.dtype)
    return ref.at[:, slot]


def _glu(ug):
    return ug[:, :A_W] * jax.nn.sigmoid(ug[:, A_W:])


def _conv_mix(pad, az, cw_ref, cb_ref, lg_ref, lb_ref, sh_ref, acc_ref):
    rows = sh_ref.shape[1]
    for b in range(SUBLANES):
        sh_ref[b] = pad[b:b + rows]
    base = HALO - CONV_K // 2
    for c0 in range(0, A_W, LANES):
        cs = slice(c0, c0 + LANES)
        for r0 in range(0, SEQ, ROW_CHUNK):
            acc = jnp.zeros((ROW_CHUNK, LANES), F32) + cb_ref[:, cs]
            for k in range(CONV_K):
                j = k + base
                s = r0 + (j // SUBLANES) * SUBLANES
                acc = acc + sh_ref[j % SUBLANES, s:s + ROW_CHUNK, cs] * cw_ref[k:k + 1, cs]
            acc_ref[r0:r0 + ROW_CHUNK, cs] = acc
    a = acc_ref[...]
    mu = jnp.mean(a, axis=-1, keepdims=True)
    d = a - mu
    var = jnp.mean(d * d, axis=-1, keepdims=True)
    y = d * lax.rsqrt(var + NORM_EPS) * lg_ref[...] + lb_ref[...]
    return _silu(y) * _silu(az)


def _in_even_kernel(x_ref, g_ref, mod_ref, w_ref, cw_ref, cb_ref, lg_ref, lb_ref, *rest,
                    sample, n_alias=0, slot=0, pos=None):
    if sample:
        (xp_ref, xn_ref, cos_ref, sin_ref, cost_ref, sint_ref,
         ma_ref, qt_ref, k_ref, vt_ref, bz_ref, sh_ref, acc_ref) = rest
    else:
        ma_ref, qt_ref, k_ref, kt_ref, v_ref, vt_ref, bz_ref, sh_ref, acc_ref = rest[n_alias:]
        kt_ref = _own_slot(kt_ref, slot)
        v_ref = _own_slot(v_ref, slot)
    hb = _pre_norm(x_ref, g_ref, mod_ref)
    a = _glu(_dot(hb, w_ref[:, 0:2 * A_W]))
    az = _dot(hb, w_ref[:, 2 * A_W:3 * A_W])
    n_sub = TM // SEQ
    if sample:
        tiles_per_seq = DEC_SEQ // TM
        xh = jnp.concatenate([xp_ref[...], xn_ref[...]], axis=0)
        ah = _glu(_dot(_modulate(xh, g_ref, mod_ref), w_ref[:, 0:2 * A_W]))
        prev = jnp.where(pos != 0, ah[:HALO], 0.0)
        nxt = jnp.where(pos != tiles_per_seq - 1, ah[HALO:], 0.0)
        full = jnp.concatenate([prev, a, nxt], axis=0)
        pads = [full[j * SEQ:(j + 1) * SEQ + 2 * HALO] for j in range(n_sub)]
    else:
        zeros = jnp.zeros((HALO, A_W), F32)
        pads = [jnp.concatenate([zeros, a[j * SEQ:(j + 1) * SEQ], zeros], axis=0) for j in range(n_sub)]
    o = 3 * A_W
    q = _dot(hb, w_ref[:, o:o + B_W])
    k = _dot(hb, w_ref[:, o + B_W:o + 2 * B_W])
    v = _dot(hb, w_ref[:, o + 2 * B_W:o + 3 * B_W])
    bz_ref[...] = _dot(hb, w_ref[:, o + 3 * B_W:o + 4 * B_W])
    qt = q.T
    if sample:
        qt = _rope_t(qt, cost_ref[...], sint_ref[...])
        k = _rope(k, cos_ref[...], sin_ref[...])
    else:
        _store_per_seq(kt_ref, k.T)
        for s in range(TM // SEQ):
            for h in range(H_B):
                v_ref[s, pl.ds(h, SEQ, stride=H_B), :] = v[s * SEQ:(s + 1) * SEQ, h * LANES:(h + 1) * LANES]
    _store_per_seq(qt_ref, (qt * QK_SCALE).astype(BF16))
    k_ref[...] = k.astype(BF16)
    _store_per_seq(vt_ref, v.T.astype(BF16))
    for j, pad in enumerate(pads):
        rs = slice(j * SEQ, (j + 1) * SEQ)
        ma_ref[rs, :] = _conv_mix(pad, az[rs], cw_ref, cb_ref, lg_ref, lb_ref, sh_ref, acc_ref).astype(BF16)


def _group_mean_sq(x):
    width = x.shape[-1]
    xx = x * x
    hi = xx.astype(BF16)
    lo = (xx - hi.astype(F32)).astype(BF16)
    r = lax.broadcasted_iota(jnp.int32, (width, width), 0) // DH
    c = lax.broadcasted_iota(jnp.int32, (width, width), 1) // DH
    g = jnp.where(r == c, 1.0, 0.0).astype(BF16)
    return (_dot(hi, g) + _dot(lo, g)) * (1.0 / DH)


def _head_rms_t(xt, gain_col):
    parts = []
    for j in range(xt.shape[0] // DH):
        blk = xt[j * DH:(j + 1) * DH]
        ms = jnp.mean(blk * blk, axis=0, keepdims=True)
        parts.append(blk * lax.rsqrt(ms + NORM_EPS))
    return jnp.concatenate(parts, axis=0) * gain_col


def _in_odd_kernel(x_ref, g_ref, mod_ref, w_ref, qn_ref, kn_ref, *rest, sample, n_alias=0, slot=0):
    if sample:
        (cos_ref, sin_ref, cost_ref, sint_ref,
         cqt_ref, ck_ref, cvt_ref, cz_ref, dqt_ref, dk_ref, dvt_ref, dz_ref) = rest
    else:
        (cqt_ref, ck_ref, ckt_ref, cvt_ref, cz_ref,
         dqt_ref, dk_ref, dkt_ref, dvt_ref, dz_ref) = rest[n_alias:]
        ckt_ref, cvt_ref, dkt_ref, dvt_ref = [_own_slot(r, slot) for r in (ckt_ref, cvt_ref, dkt_ref, dvt_ref)]
    hb = _pre_norm(x_ref, g_ref, mod_ref)
    y = _dot(hb, w_ref[...])
    o = 0
    cqt = _head_rms_t(y[:, o:o + C_W].T, qn_ref[...])
    o += C_W
    ck = y[:, o:o + KV_W]
    ck = ck * lax.rsqrt(_group_mean_sq(ck) + NORM_EPS) * kn_ref[...]
    o += KV_W
    cvt = y[:, o:o + KV_W].T
    o += KV_W
    cz_ref[...] = y[:, o:o + C_W]
    o += C_W
    dqt = y[:, o:o + D_W].T
    o += D_W
    dk = y[:, o:o + KV_W]
    o += KV_W
    dvt = y[:, o:o + KV_W].T
    o += KV_W
    dz_ref[...] = y[:, o:o + D_W]
    if sample:
        cos_t = cost_ref[...]
        sin_t = sint_ref[...]
        cqt = _rope_t(cqt, cos_t, sin_t)
        dqt = _rope_t(dqt, cos_t, sin_t)
        cos = cos_ref[...]
        sin = sin_ref[...]
        ck = _rope(ck, cos, sin)
        dk = _rope(dk, cos, sin)
        _store_chunks(cvt_ref, cvt.astype(BF16))
        _store_chunks(dvt_ref, dvt.astype(BF16))
    else:
        _store_per_seq(ckt_ref, ck.T)
        _store_per_seq(dkt_ref, dk.T)
        _store_per_seq(cvt_ref, cvt)
        _store_per_seq(dvt_ref, dvt)
    _store_per_seq(cqt_ref, (cqt * QK_SCALE).astype(BF16))
    _store_per_seq(dqt_ref, (dqt * QK_SCALE).astype(BF16))
    ck_ref[...] = ck.astype(BF16)
    dk_ref[...] = dk.astype(BF16)


def _exp_terms(segs, extra=None):
    m = None
    for s in segs:
        mi = jnp.max(s, axis=0, keepdims=True)
        m = mi if m is None else jnp.maximum(m, mi)
    if extra is not None:
        extra = extra * LOG2E
        m = jnp.maximum(m, extra)
    es = [jnp.exp2(s - m) for s in segs]
    return es, (None if extra is None else jnp.exp2(extra - m))


def _softmax_t(segs):
    es, _ = _exp_terms(segs)
    den = None
    for e in es:
        di = jnp.sum(e, axis=0, keepdims=True)
        den = di if den is None else den + di
    return es, den


def _keep_rows(xt, lo, hi):
    zeros = lambda r: jnp.zeros((r, xt.shape[1]), xt.dtype)
    parts = []
    if lo > 0:
        parts.append(zeros(lo))
    parts.append(xt[lo:hi])
    if hi < xt.shape[0]:
        parts.append(zeros(xt.shape[0] - hi))
    return jnp.concatenate(parts, axis=0)


def _pipelined(n, scores, finish, ahead):
    ready = [scores(j) for j in range(min(ahead, n))]
    for j in range(n):
        if j + ahead < n:
            ready.append(scores(j + ahead))
        finish(j, ready.pop(0))


def _diff_attn(qt_ref, kvs, z_ref, lam_ref, g_ref, o_ref, *, lam_init):
    lv = lam_ref[...]
    lam = (jnp.exp(jnp.sum(lv[0:1] * lv[1:2], axis=-1, keepdims=True))
           - jnp.exp(jnp.sum(lv[2:3] * lv[3:4], axis=-1, keepdims=True)) + lam_init)

    def scores(h):
        cs = slice(h * LANES, (h + 1) * LANES)
        qt = qt_ref[cs, :]
        ks = [get_k(cs) for get_k, _ in kvs]
        return [[_dot(kk, _keep_rows(qt, c * DH, (c + 1) * DH)) for kk in ks] for c in range(2)]

    def finish(h, ss):
        cs = slice(h * LANES, (h + 1) * LANES)
        es0, den0 = _softmax_t(ss[0])
        es1, den1 = _softmax_t(ss[1])
        r0 = 1.0 / den0
        r1 = lam / den1
        ot = None
        for e0, e1, (_, get_vt) in zip(es0, es1, kvs):
            w = e0 * r0 - e1 * r1
            oi = _dot(get_vt(cs), w.astype(BF16))
            ot = oi if ot is None else ot + oi
        ms = jnp.mean(ot * ot, axis=0, keepdims=True)
        o = (ot * lax.rsqrt(ms + NORM_EPS)).T
        o = (o * g_ref[...]) * (1.0 - lam_init)
        o_ref[:, cs] = (o * _silu(z_ref[:, cs])).astype(o_ref.dtype)

    _pipelined(H_B, scores, finish, DIFF_AHEAD[len(kvs) - 1])


def _post_residual(o, x_ref, g_ref, mod_ref, o_ref):
    ms = jnp.mean(o * o, axis=-1, keepdims=True)
    r = o * lax.rsqrt(ms + NORM_EPS) * g_ref[...]
    gate = mod_ref[...][:, 2 * D_MODEL:]
    o_ref[...] = x_ref[...] + gate * r


def _diff_prompt_kernel(qt_ref, k_ref, vt_ref, z_ref, lam_ref, sg_ref, ma_ref, x_ref, w_ref, g_ref,
                        mod_ref, o_ref, mb_ref, *, lam_init):
    oa = _dot(ma_ref[...], w_ref[0:A_W, :])
    for s in range(qt_ref.shape[0]):
        rs = pl.ds(s * SEQ, SEQ)
        kv = (lambda cs, s=s: k_ref[s * SEQ:(s + 1) * SEQ, cs], lambda cs, s=s: vt_ref[s, cs, :])
        _diff_attn(qt_ref.at[s], [kv], z_ref.at[rs], lam_ref, sg_ref, mb_ref.at[rs], lam_init=lam_init)
    _post_residual(oa + _dot(mb_ref[...], w_ref[A_W:, :]), x_ref, g_ref, mod_ref, o_ref)


def _prep_diff_cache(ck_ref, cv_ref, ckb_ref, cvt_ref):
    ckb_ref[...] = ck_ref[...].T.astype(BF16)
    for h in range(H_B):
        cs = slice(h * LANES, (h + 1) * LANES)
        cvt_ref[cs, :] = cv_ref[pl.ds(h, PAST_LEN, stride=H_B), :].T.astype(BF16)


def _diff_sample_body(qt_ref, k_ref, vt_ref, z_ref, lam_ref, sg_ref, ma_ref, x_ref, w_ref, g_ref,
                      mod_ref, o_ref, ckb_ref, cvt_ref, mb_ref, *, lam_init):
    ctx = (lambda cs: ckb_ref[:, cs], lambda cs: cvt_ref[cs, :])
    loc = (lambda cs: k_ref[:, cs],
           lambda cs: jnp.concatenate([vt_ref[c, cs, :] for c in range(vt_ref.shape[0])], axis=1))
    oa = _dot(ma_ref[...], w_ref[0:A_W, :])
    _diff_attn(qt_ref, [ctx, loc], z_ref, lam_ref, sg_ref, mb_ref, lam_init=lam_init)
    _post_residual(oa + _dot(mb_ref[...], w_ref[A_W:, :]), x_ref, g_ref, mod_ref, o_ref)


def _gqa(qt_ref, segs, z_ref, o_ref, sink_ref=None):
    halves = []

    def scores(j):
        n = j // 4
        qj = qt_ref[j * DH:(j + 1) * DH, :]
        zero = jnp.zeros_like(qj)
        qz = jnp.concatenate([qj, zero] if n == 0 else [zero, qj], axis=0)
        return [_dot(k, qz) for k, _, _ in segs]

    def finish(j, ss):
        n = j // 4
        ss = [s if valid is None else jnp.where(valid, s, -jnp.inf)
              for s, (_, _, valid) in zip(ss, segs)]
        extra = None if sink_ref is None else sink_ref[:, j:j + 1]
        es, den = _exp_terms(ss, extra)
        ot = None
        for e, (_, vt, _) in zip(es, segs):
            ones = jnp.ones((DEN_ROWS, vt.shape[1]), BF16)
            vt1 = jnp.concatenate([vt[n * DH:(n + 1) * DH], ones], axis=0)
            oi = _dot(vt1, e.astype(BF16))
            ot = oi if ot is None else ot + oi
        den = ot[DH:DH + 1] if den is None else den + ot[DH:DH + 1]
        halves.append(ot[:DH] * (1.0 / den))
        if j % 2 == 1:
            cs = slice((j // 2) * LANES, (j // 2 + 1) * LANES)
            o_pair = jnp.concatenate(halves[-2:], axis=0).T
            o_ref[:, cs] = (o_pair * _silu(z_ref[:, cs])).astype(o_ref.dtype)

    _pipelined(2 * 4, scores, finish, GQA_AHEAD[len(segs) - 1])


def _gqa_prompt_kernel(cqt_ref, ck_ref, cvt_ref, cz_ref, dqt_ref, dk_ref, dvt_ref, dz_ref, sink_ref,
                       x_ref, w_ref, g_ref, mod_ref, o_ref, m_ref):
    for s in range(cqt_ref.shape[0]):
        rows = slice(s * SEQ, (s + 1) * SEQ)
        rs = pl.ds(s * SEQ, SEQ)
        seg = lambda k_ref, vt_ref: (k_ref[rows, :], vt_ref[s].astype(BF16), None)
        _gqa(cqt_ref.at[s], [seg(ck_ref, cvt_ref)], cz_ref.at[rs], m_ref.at[rs, pl.ds(0, C_W)])
        _gqa(dqt_ref.at[s], [seg(dk_ref, dvt_ref)], dz_ref.at[rs], m_ref.at[rs, pl.ds(C_W, D_W)], sink_ref)
    _post_residual(_dot(m_ref[...], w_ref[...]), x_ref, g_ref, mod_ref, o_ref)


def _gqa_sample_body(cqt_ref, ck_ref, cvt_ref, cck_ref, ccv_ref, cz_ref,
                     dqt_ref, dk_ref, dvt_ref, cdk_ref, cdv_ref, dz_ref, sink_ref,
                     x_ref, w_ref, g_ref, mod_ref, o_ref, m_ref, *, t):
    oc_ref = m_ref.at[:, pl.ds(0, C_W)]
    od_ref = m_ref.at[:, pl.ds(C_W, D_W)]
    ctx = lambda kt_ref, vt_ref: (kt_ref[...].T.astype(BF16), vt_ref[...].astype(BF16), None)
    n_chunks = DEC_SEQ // LANES
    cvt = jnp.concatenate([cvt_ref[c] for c in range(n_chunks)], axis=1)
    _gqa(cqt_ref, [ctx(cck_ref, ccv_ref), (ck_ref[...], cvt, None)], cz_ref, oc_ref)
    span = 2 * TQ
    t0 = t * TQ
    ws = pl.multiple_of(jnp.clip(t0 - WINDOW, 0, DEC_SEQ - span), WINDOW)
    kpos = ws + lax.broadcasted_iota(jnp.int32, (span, TQ), 0)
    qpos = t0 + lax.broadcasted_iota(jnp.int32, (span, TQ), 1)
    valid = jnp.abs(qpos - kpos) <= WINDOW
    c0 = ws // LANES
    dvt = jnp.concatenate([dvt_ref[c0 + c] for c in range(span // LANES)], axis=1)
    _gqa(dqt_ref, [ctx(cdk_ref, cdv_ref), (dk_ref[pl.ds(ws, span), :], dvt, valid)],
         dz_ref, od_ref, sink_ref)
    _post_residual(_dot(m_ref[...], w_ref[...]), x_ref, g_ref, mod_ref, o_ref)


N_PREP_IN = 5
N_PREP_OUT = 3


def _prep_next_layer(rest, prep, n_alias, n_out, next_layer):
    if not prep:
        return rest[n_alias:]
    w_in_src, w_out_src, cond_ref, wm_ref, bm_ref = rest[:N_PREP_IN]
    rest = rest[N_PREP_IN + n_alias:]
    w_in_dst, w_out_dst, mod_dst = rest[n_out:n_out + N_PREP_OUT]
    w_in_dst[...] = w_in_src[...].astype(BF16)
    w_out_dst[...] = w_out_src[...].astype(BF16)
    part = _dot(_silu(cond_ref[...]).astype(BF16), wm_ref[...].astype(BF16))
    first = pl.program_id(0) == 0

    @pl.when(first)
    def _():
        mod_dst[...] = part + bm_ref[next_layer:next_layer + 1, :]

    @pl.when(jnp.logical_not(first))
    def _():
        mod_dst[...] += part

    return rest[:n_out] + rest[n_out + N_PREP_OUT:]


def _prep_specs(prep, layer, n_steps):
    w_in_f32, w_out_f32, src_idx, cond_chunks, w_mod, b_mod = prep
    chunk = D_MODEL // n_steps
    assert cond_chunks.shape == (n_steps, SUBLANES, chunk)
    in_specs, args, outs = [], [], []
    for src in (w_in_f32, w_out_f32):
        in_specs.append(pl.BlockSpec((None, chunk, src.shape[2]), lambda i: (src_idx, i, 0)))
        args.append(src)
        outs.append((pl.BlockSpec((None, chunk, src.shape[2]), lambda i: (0, i, 0)),
                     jax.ShapeDtypeStruct((1,) + src.shape[1:], BF16)))
    in_specs += [pl.BlockSpec((None, SUBLANES, chunk), lambda i: (i, 0, 0)),
                 pl.BlockSpec((None, chunk, 3 * D_MODEL), lambda i: (layer, i, 0)),
                 _whole(b_mod)]
    args += [cond_chunks, w_mod, b_mod]
    outs.append((pl.BlockSpec((SUBLANES, 3 * D_MODEL), lambda i: (0, 0)),
                 jax.ShapeDtypeStruct((SUBLANES, 3 * D_MODEL), F32)))
    return in_specs, args, outs


def _prep_first_kernel(*refs):
    _prep_next_layer(refs, True, 0, 0, 0)


def _prep_first_layer(prep):
    n_steps = prep[3].shape[0]
    in_specs, args, outs = _prep_specs(prep, 0, n_steps)
    return pl.pallas_call(
        _prep_first_kernel,
        grid=(n_steps,),
        in_specs=in_specs,
        out_specs=[o[0] for o in outs],
        out_shape=[o[1] for o in outs],
        compiler_params=_params("arbitrary"),
        name="prep_layer0",
    )(*args)


def _prompt_even_kernel(x_ref, g_ref, mod_ref, w_ref, cw_ref, cb_ref, lg_ref, lb_ref, lam_ref, sg_ref,
                        wo_ref, gp_ref, *rest, lam_init, layer, n_alias, slot, prep):
    g_ref, mod_ref, gp_ref, cb_ref, lg_ref, lb_ref, sg_ref = _layer_rows(
        layer, 0, g_ref, mod_ref, gp_ref, cb_ref, lg_ref, lb_ref, sg_ref)
    rest = _prep_next_layer(rest, prep, n_alias, 3, layer + 1)
    (o_ref, kt_ref, v_ref,
     ma_ref, qt_ref, k_ref, vt_ref, bz_ref, sh_ref, acc_ref, mb_ref) = rest
    _in_even_kernel(x_ref, g_ref, mod_ref, w_ref, cw_ref, cb_ref, lg_ref, lb_ref,
                    ma_ref, qt_ref, k_ref, kt_ref, v_ref, vt_ref, bz_ref, sh_ref, acc_ref,
                    sample=False, slot=slot)
    _diff_prompt_kernel(qt_ref, k_ref, vt_ref, bz_ref, lam_ref, sg_ref, ma_ref, x_ref, wo_ref, gp_ref,
                        mod_ref, o_ref, mb_ref, lam_init=lam_init)


def _prompt_odd_kernel(x_ref, g_ref, mod_ref, w_ref, qn_ref, kn_ref, sink_ref, wo_ref, gp_ref, *rest,
                       layer, n_alias, slot, prep):
    g_ref, mod_ref, gp_ref, kn_ref, sink_ref = _layer_rows(layer, 0, g_ref, mod_ref, gp_ref, kn_ref, sink_ref)
    rest = _prep_next_layer(rest, prep, n_alias, 5, layer + 1)
    (o_ref, ckt_ref, cvt_ref, dkt_ref, dvt_ref,
     cqt_ref, ck_ref, cz_ref, dqt_ref, dk_ref, dz_ref, m_ref) = rest
    _in_odd_kernel(x_ref, g_ref, mod_ref, w_ref, qn_ref, kn_ref,
                   cqt_ref, ck_ref, ckt_ref, cvt_ref, cz_ref, dqt_ref, dk_ref, dkt_ref, dvt_ref, dz_ref,
                   sample=False, slot=slot)
    _gqa_prompt_kernel(cqt_ref, ck_ref, _own_slot(cvt_ref, slot, fill=False), cz_ref,
                       dqt_ref, dk_ref, _own_slot(dvt_ref, slot, fill=False), dz_ref, sink_ref,
                       x_ref, wo_ref, gp_ref, mod_ref, o_ref, m_ref)


def _prompt_layer(x, layer, g_pre, g_post, mod, w_in, w_out, w_idx, head, tail, carry, lam_init=None,
                  prep_next=None):
    n = x.shape[0]
    even = layer % 2 == 0
    row = lambda i: (i, 0)
    n_seq = TM // SEQ
    params = (g_pre, w_in) + tuple(head) + tuple(tail) + (w_out, g_post)
    idx = (layer, w_idx) + (layer // 2,) * (len(head) + len(tail)) + (w_idx, layer)
    specs = [_pick(p, i) for p, i in zip(params, idx)]
    in_specs = [pl.BlockSpec((TM, D_MODEL), row), specs[0],
                _whole(mod)] + specs[1:]
    args = [x, g_pre, mod, w_in] + list(head) + list(tail) + [w_out, g_post]
    n_layers = (DEPTH + 1 - layer % 2) // 2
    if carry:
        cache_spec = lambda r, c: pl.BlockSpec((n_seq, None, r, c), lambda i: (i, layer // 2, 0, 0))
    else:
        cache_spec = lambda r, c: pl.BlockSpec((n_seq, n_layers, r, c), lambda i: (i, 0, 0, 0))
    cache = lambda r, c: (cache_spec(r, c), jax.ShapeDtypeStruct((n // SEQ, n_layers, r, c), F32))
    outs = [(pl.BlockSpec((TM, D_MODEL), row), jax.ShapeDtypeStruct((n, D_MODEL), F32))]
    wide = pltpu.VMEM((TM, B_W), BF16)
    slab = pltpu.VMEM((n_seq, B_W, SEQ), BF16)
    if even:
        outs += [cache(B_W, SEQ), cache(SEQ * H_B, LANES)]
        scratch = [wide, slab, wide, slab, pltpu.VMEM((TM, B_W), F32),
                   pltpu.VMEM((SUBLANES, SEQ + 3 * SUBLANES, A_W), F32), pltpu.VMEM((SEQ, A_W), F32), wide]
        body = functools.partial(_prompt_even_kernel, lam_init=lam_init)
    else:
        outs += [cache(KV_W, SEQ)] * 4
        narrow = pltpu.VMEM((TM, KV_W), BF16)
        gate = pltpu.VMEM((TM, C_W), F32)
        scratch = [slab, narrow, gate, slab, narrow, gate, pltpu.VMEM((TM, C_W + D_W), BF16)]
        body = _prompt_odd_kernel
    if prep_next is not None:
        prep_in, prep_args, prep_outs = _prep_specs(prep_next, layer + 1, n // TM)
        in_specs += prep_in
        args += prep_args
        outs += prep_outs
    aliases = {}
    for j, a in enumerate(carry):
        aliases[len(args)] = 1 + j
        in_specs.append(pl.BlockSpec(memory_space=pl.ANY))
        args.append(a)
    return pl.pallas_call(
        functools.partial(body, layer=layer, n_alias=len(carry), slot=layer // 2,
                          prep=prep_next is not None),
        grid=(n // TM,),
        in_specs=in_specs,
        out_specs=[o[0] for o in outs],
        out_shape=[o[1] for o in outs],
        scratch_shapes=scratch,
        input_output_aliases=aliases,
        compiler_params=_params("arbitrary"),
        name=f"prompt_layer{layer}",
    )(*args)


N_IN = DEC_SEQ // TM
N_Q = DEC_SEQ // TQ
FIRST_Q = N_IN - 1
N_PHASE = N_IN + N_Q - 1


def _rows(ref, start, size):
    if isinstance(start, int):
        return ref.at[pl.ds(start, size)]
    return ref.at[pl.ds(pl.multiple_of(start, size), size)]


def _run_phases(ph, project, attend):
    if FIRST_Q > 0:
        @pl.when(ph < FIRST_Q)
        def _():
            project(ph)

    @pl.when(ph == FIRST_Q)
    def _():
        project(FIRST_Q)
        attend(0)

    @pl.when(ph > FIRST_Q)
    def _():
        attend(ph - FIRST_Q)


def _sample_even_kernel(x_ref, xp_ref, xn_ref, g_ref, mod_ref, w_ref, cw_ref, cb_ref, lg_ref, lb_ref,
                        cos_ref, sin_ref, cost_ref, sint_ref, ck_ref, cv_ref, lam_ref, sg_ref,
                        xr_ref, wo_ref, gp_ref, o_ref,
                        ma_ref, qt_ref, k_ref, vt_ref, bz_ref, sh_ref, acc_ref, ckb_ref, cvt_ref, mb_ref,
                        *, lam_init, layer):
    g_ref, mod_ref, gp_ref, cb_ref, lg_ref, lb_ref, sg_ref = _layer_rows(
        layer, 1 + pl.program_id(0), g_ref, mod_ref, gp_ref, cb_ref, lg_ref, lb_ref, sg_ref)
    ph = pl.program_id(1)
    per = TM // SEQ

    def project(p):
        _in_even_kernel(x_ref, g_ref, mod_ref, w_ref, cw_ref, cb_ref, lg_ref, lb_ref,
                        xp_ref, xn_ref, cos_ref, sin_ref, cost_ref, sint_ref,
                        _rows(ma_ref, p * TM, TM), qt_ref.at[pl.ds(p * per, per)],
                        _rows(k_ref, p * TM, TM), vt_ref.at[pl.ds(p * per, per)],
                        _rows(bz_ref, p * TM, TM), sh_ref, acc_ref, sample=True, pos=p)

    def attend(t):
        _diff_sample_body(qt_ref.at[t], k_ref, vt_ref, _rows(bz_ref, t * TQ, TQ), lam_ref, sg_ref,
                          _rows(ma_ref, t * TQ, TQ), xr_ref, wo_ref, gp_ref, mod_ref, o_ref,
                          ckb_ref, cvt_ref, mb_ref, lam_init=lam_init)

    @pl.when(ph == 0)
    def _():
        _prep_diff_cache(ck_ref, cv_ref, ckb_ref, cvt_ref)

    _run_phases(ph, project, attend)


def _sample_odd_kernel(x_ref, g_ref, mod_ref, w_ref, qn_ref, kn_ref,
                       cos_ref, sin_ref, cost_ref, sint_ref, cck_ref, ccv_ref, cdk_ref, cdv_ref, sink_ref,
                       xr_ref, wo_ref, gp_ref, o_ref,
                       cqt_ref, ck_ref, cvt_ref, cz_ref, dqt_ref, dk_ref, dvt_ref, dz_ref, m_ref, *, layer):
    g_ref, mod_ref, gp_ref, kn_ref, sink_ref = _layer_rows(
        layer, 1 + pl.program_id(0), g_ref, mod_ref, gp_ref, kn_ref, sink_ref)
    ph = pl.program_id(1)
    per = TM // SEQ
    chunks = TM // LANES

    def project(p):
        _in_odd_kernel(x_ref, g_ref, mod_ref, w_ref, qn_ref, kn_ref,
                       cos_ref, sin_ref, cost_ref, sint_ref,
                       cqt_ref.at[pl.ds(p * per, per)], _rows(ck_ref, p * TM, TM),
                       cvt_ref.at[pl.ds(p * chunks, chunks)], _rows(cz_ref, p * TM, TM),
                       dqt_ref.at[pl.ds(p * per, per)], _rows(dk_ref, p * TM, TM),
                       dvt_ref.at[pl.ds(p * chunks, chunks)], _rows(dz_ref, p * TM, TM), sample=True)

    def attend(t):
        _gqa_sample_body(cqt_ref.at[t], ck_ref, cvt_ref, cck_ref, ccv_ref, _rows(cz_ref, t * TQ, TQ),
                         dqt_ref.at[t], dk_ref, dvt_ref, cdk_ref, cdv_ref, _rows(dz_ref, t * TQ, TQ),
                         sink_ref, xr_ref, wo_ref, gp_ref, mod_ref, o_ref, m_ref, t=t)

    _run_phases(ph, project, attend)


def _sample_layer(x, layer, g_pre, g_post, mod, w_in, w_out, w_idx, head, tail, caches, tables,
                  lam_init=None):
    n = x.shape[0]
    even = layer % 2 == 0
    in_tile = lambda b, ph: b * N_IN + jnp.minimum(ph, N_IN - 1)
    q_tile = lambda b, ph: b * N_Q + jnp.maximum(ph - FIRST_Q, 0)
    tab = lambda b, ph: jnp.minimum(ph, N_IN - 1)
    cos, sin, cos_t, sin_t = tables
    x_spec = pl.BlockSpec((TM, D_MODEL), lambda b, ph: (in_tile(b, ph), 0))
    mod_spec = _whole(mod)
    table_specs = [pl.BlockSpec((TM, LANES), lambda b, ph: (tab(b, ph), 0)),
                   pl.BlockSpec((TM, LANES), lambda b, ph: (tab(b, ph), 0)),
                   pl.BlockSpec((LANES, TM), lambda b, ph: (0, tab(b, ph))),
                   pl.BlockSpec((LANES, TM), lambda b, ph: (0, tab(b, ph)))]
    cache_specs = [pl.BlockSpec((None, None) + c.shape[2:], lambda b, ph: (b, layer // 2, 0, 0))
                   for c in caches]
    res_spec = pl.BlockSpec((TQ, D_MODEL), lambda b, ph: (q_tile(b, ph), 0))
    in_specs = [x_spec]
    args = [x]
    if even:
        hb = TM // HALO
        last = n // HALO - 1
        in_specs += [pl.BlockSpec((HALO, D_MODEL), lambda b, ph: (jnp.maximum(in_tile(b, ph) * hb - 1, 0), 0)),
                     pl.BlockSpec((HALO, D_MODEL),
                                  lambda b, ph: (jnp.minimum((in_tile(b, ph) + 1) * hb, last), 0))]
        args += [x, x]
    in_specs += [_pick(g_pre, layer), mod_spec, _pick(w_in, w_idx)] + [_pick(h, layer // 2) for h in head]
    args += [g_pre, mod, w_in] + list(head)
    in_specs += table_specs + cache_specs + [_pick(t, layer // 2) for t in tail]
    args += [cos, sin, cos_t, sin_t] + list(caches) + list(tail)
    in_specs += [res_spec, _pick(w_out, w_idx), _pick(g_post, layer)]
    args += [x, w_out, g_post]
    seq_wide = lambda w, dt: pltpu.VMEM((DEC_SEQ, w), dt)
    slab = pltpu.VMEM((DEC_SEQ // SEQ, B_W, SEQ), BF16)
    if even:
        scratch = [seq_wide(A_W, BF16), slab, seq_wide(B_W, BF16), slab, seq_wide(B_W, F32),
                   pltpu.VMEM((SUBLANES, SEQ + 3 * SUBLANES, A_W), F32), pltpu.VMEM((SEQ, A_W), F32),
                   pltpu.VMEM((PAST_LEN, B_W), BF16), pltpu.VMEM((B_W, PAST_LEN), BF16),
                   pltpu.VMEM((TQ, B_W), BF16)]
        body = functools.partial(_sample_even_kernel, lam_init=lam_init, layer=layer)
    else:
        chunk = pltpu.VMEM((DEC_SEQ // LANES, KV_W, LANES), BF16)
        scratch = [slab, seq_wide(KV_W, BF16), chunk, seq_wide(C_W, F32),
                   slab, seq_wide(KV_W, BF16), chunk, seq_wide(D_W, F32),
                   pltpu.VMEM((TQ, C_W + D_W), BF16)]
        body = functools.partial(_sample_odd_kernel, layer=layer)
    return pl.pallas_call(
        body,
        grid=(DEC_BATCH, N_PHASE),
        in_specs=in_specs,
        out_specs=pl.BlockSpec((TQ, D_MODEL), lambda b, ph: (q_tile(b, ph), 0)),
        out_shape=jax.ShapeDtypeStruct((n, D_MODEL), F32),
        scratch_shapes=scratch,
        compiler_params=_params("arbitrary", "arbitrary"),
        name=f"sample_layer{layer}",
    )(*args)


def _rope_tables():
    nf = DH // 4
    t = jnp.arange(DEC_SEQ)
    row = (t // GRID_W).astype(F32)
    col = (t % GRID_W).astype(F32)
    inv = ROPE_THETA ** (-jnp.arange(nf, dtype=F32) / nf)
    d = jnp.arange(DH)
    axis = d // (2 * nf)
    second = (d % (2 * nf)) // nf
    f = d % nf
    pos = jnp.where(axis[None, :] == 0, row[:, None], col[:, None])
    ang = pos * inv[f][None, :]
    cos = jnp.cos(ang)
    sin = jnp.where(second[None, :] == 0, -jnp.sin(ang), jnp.sin(ang))
    reps = LANES // DH
    cos = jnp.tile(cos, (1, reps))
    sin = jnp.tile(sin, (1, reps))
    return cos, sin, cos.T, sin.T


def kernel(x_prompt, x_sample, cache_b_k, cache_b_v, cache_c_k, cache_c_v, cache_d_k, cache_d_v, c, c_ctx, norm_pre, norm_post, w_mod, b_mod, w_in_even, a_conv_w, a_conv_b, a_ln_g, a_ln_b, b_lambda, b_subln_g, w_out_even, w_in_odd, c_q_norm, c_k_norm, d_sink, w_out_odd):
    n_even = (DEPTH + 1) // 2
    n_odd = DEPTH // 2
    cond8 = jnp.zeros((SUBLANES, D_MODEL), F32).at[0].set(c_ctx).at[1:1 + DEC_BATCH].set(c)
    n_chunk = BATCH * SEQ // TM
    cond_chunks = jnp.moveaxis(cond8.reshape(SUBLANES, n_chunk, D_MODEL // n_chunk), 1, 0)
    tables = _rope_tables()

    xp = x_prompt.reshape(BATCH * SEQ, D_MODEL)
    xs = x_sample.reshape(DEC_BATCH * DEC_SEQ, D_MODEL)
    feat = lambda a, w: jnp.moveaxis(a.reshape(a.shape[:3] + (w,)), 2, 3)
    cbk = feat(cache_b_k, B_W)
    cbv = cache_b_v.reshape(DEC_BATCH, n_even, PAST_LEN * H_B, 2 * DH)
    cck = feat(cache_c_k, KV_W)
    ccv = feat(cache_c_v, KV_W)
    cdk = feat(cache_d_k, KV_W)
    cdv = feat(cache_d_v, KV_W)

    g_pre, g_post = norm_pre, norm_post
    conv = (jnp.zeros((n_even, 4 * SUBLANES, A_W), F32).at[:, :CONV_K].set(a_conv_w),
            a_conv_b, a_ln_g, a_ln_b)
    subln = b_subln_g
    qkn = (jnp.tile(c_q_norm, (1, C_W // DH)).reshape(n_odd, C_W, 1), jnp.tile(c_k_norm, (1, KV_W // DH)))
    sink = d_sink

    f32_weights = ((w_in_even, w_out_even), (w_in_odd, w_out_odd))
    *weights, mod = _prep_first_layer(f32_weights[0] + (0, cond_chunks, w_mod, b_mod))
    new_even, new_odd = (), ()
    for l in range(DEPTH):
        w_in, w_out = weights
        mod_l = mod
        prep_next = (f32_weights[(l + 1) % 2] + ((l + 1) // 2, cond_chunks, w_mod, b_mod)
                     if l + 1 < DEPTH else None)
        if l % 2 == 0:
            lam_init = 0.8 - 0.6 * math.exp(-0.3 * l)
            outs = _prompt_layer(xp, l, g_pre, g_post, mod_l, w_in, w_out, 0, conv, (b_lambda, subln),
                                 new_even, lam_init, prep_next)
            xp, new_even, (*weights, mod) = outs[0], outs[1:3], outs[3:]
            xs = _sample_layer(xs, l, g_pre, g_post, mod_l, w_in, w_out, 0, conv, (b_lambda, subln),
                               (cbk, cbv), tables, lam_init)
        else:
            outs = _prompt_layer(xp, l, g_pre, g_post, mod_l, w_in, w_out, 0, qkn, (sink,), new_odd,
                                 None, prep_next)
            xp, new_odd, (*weights, mod) = outs[0], outs[1:5], (outs[5:] if prep_next else (None, None, None))
            xs = _sample_layer(xs, l, g_pre, g_post, mod_l, w_in, w_out, 0, qkn, (sink,),
                               (cck, ccv, cdk, cdv), tables)

    def token_major(a, heads):
        return jnp.moveaxis(a.reshape(a.shape[:2] + heads + (DH, SEQ)), -1, 2)

    kt, v = new_even
    ckt, cvt, dkt, dvt = new_odd
    return (xp.reshape(BATCH, SEQ, D_MODEL), xs.reshape(DEC_BATCH, DEC_SEQ, D_MODEL),
            token_major(kt, (H_B, 2)), v.reshape(BATCH, n_even, SEQ, H_B, 2 * DH),
            token_major(ckt, (2,)), token_major(cvt, (2,)), token_major(dkt, (2,)), token_major(dvt, (2,)))
```

```python
import functools
import math

import jax
import jax.numpy as jnp
from jax import lax
from jax.experimental import pallas as pl
from jax.experimental.pallas import tpu as pltpu

F32 = jnp.float32
BF16 = jnp.bfloat16

D_MODEL = 1024
BATCH = 16
SEQ = 256
DEPTH = 4
DEC_BATCH = 2
DEC_SEQ = 1024
PAST_LEN = 512
GRID_W = 64
ROPE_THETA = 10000.0
NORM_EPS = 1e-6
DH = 64
A_W = 512
CONV_K = 31
H_B = 4
B_W = 512
C_W = 512
KV_W = 128
D_W = 512
WINDOW = 128
LOG2E = math.log2(math.e)
QK_SCALE = DH ** -0.5 * LOG2E

LANES = 128
SUBLANES = 8
VMEM_LIMIT = 56 * 1024 * 1024

TM = 512
TQ = SEQ
HALO = 16
GQA_AHEAD = (8, 3)
DIFF_AHEAD = (3, 2)
ROW_CHUNK = 64
DEN_ROWS = 16


def _params(*sem):
    return pltpu.CompilerParams(dimension_semantics=sem, vmem_limit_bytes=VMEM_LIMIT)


def _silu(x):
    return x * jax.nn.sigmoid(x)


def _dot(a, b):
    return jnp.dot(a, b, preferred_element_type=F32)


def _pick(stacked, idx):
    if stacked.ndim == 2:
        return _whole(stacked)
    return pl.BlockSpec((None,) + stacked.shape[1:], lambda *_: (idx,) + (0,) * (stacked.ndim - 1))


def _whole(a):
    return pl.BlockSpec(a.shape, lambda *_: (0,) * a.ndim)


def _layer_rows(layer, mod_row, g_ref, mod_ref, gp_ref, *half):
    one = lambda ref, i: ref.at[pl.ds(i, 1)]
    return ((one(g_ref, layer), one(mod_ref, mod_row), one(gp_ref, layer))
            + tuple(one(r, layer // 2) for r in half))


def _pre_norm(x_ref, g_ref, mod_ref):
    return _modulate(x_ref[...], g_ref, mod_ref)


def _modulate(x, g_ref, mod_ref):
    ms = jnp.mean(x * x, axis=-1, keepdims=True)
    mod = mod_ref[...]
    sh = mod[:, :D_MODEL]
    sc = mod[:, D_MODEL:2 * D_MODEL]
    h = (x * lax.rsqrt(ms + NORM_EPS)) * (g_ref[...] * (1.0 + sc)) + sh
    return h.astype(BF16)


def _rope(x, cos, sin_signed):
    lane = lax.broadcasted_iota(jnp.int32, (1, LANES), 1)
    first = (lane % 32) < 16
    blocks = []
    for c in range(0, x.shape[-1], LANES):
        xb = x[:, c:c + LANES]
        partner = jnp.where(first, pltpu.roll(xb, LANES - 16, 1), pltpu.roll(xb, 16, 1))
        blocks.append(xb * cos + partner * sin_signed)
    return blocks[0] if len(blocks) == 1 else jnp.concatenate(blocks, axis=1)


def _rope_t(x, cos_t, sin_t):
    row = lax.broadcasted_iota(jnp.int32, (LANES, 1), 0)
    first = (row % 32) < 16
    blocks = []
    for r in range(0, x.shape[0], LANES):
        xb = x[r:r + LANES]
        partner = jnp.where(first, pltpu.roll(xb, LANES - 16, 0), pltpu.roll(xb, 16, 0))
        blocks.append(xb * cos_t + partner * sin_t)
    return jnp.concatenate(blocks, axis=0)


def _store_chunks(ref, xt):
    for c in range(xt.shape[1] // LANES):
        ref[c] = xt[:, c * LANES:(c + 1) * LANES]


def _store_per_seq(ref, xt):
    for s in range(xt.shape[1] // SEQ):
        ref[s] = xt[:, s * SEQ:(s + 1) * SEQ]


def _own_slot(ref, slot, fill=True):
    if len(ref.shape) == 3:
        return ref
    for other in range(ref.shape[1]):
        if fill and other != slot:
            ref[:, other] = jnp.zeros((ref.shape[0],) + tuple(ref.shape[2:]), ref.dtype)
    return ref.at[:, slot]


def _glu(ug):
    return ug[:, :A_W] * jax.nn.sigmoid(ug[:, A_W:])


def _conv_mix(pad, az, cw_ref, cb_ref, lg_ref, lb_ref, sh_ref, acc_ref):
    rows = sh_ref.shape[1]
    for b in range(SUBLANES):
        sh_ref[b] = pad[b:b + rows]
    base = HALO - CONV_K // 2
    for c0 in range(0, A_W, LANES):
        cs = slice(c0, c0 + LANES)
        for r0 in range(0, SEQ, ROW_CHUNK):
            acc = jnp.zeros((ROW_CHUNK, LANES), F32) + cb_ref[:, cs]
            for k in range(CONV_K):
                j = k + base
                s = r0 + (j // SUBLANES) * SUBLANES
                acc = acc + sh_ref[j % SUBLANES, s:s + ROW_CHUNK, cs] * cw_ref[k:k + 1, cs]
            acc_ref[r0:r0 + ROW_CHUNK, cs] = acc
    a = acc_ref[...]
    mu = jnp.mean(a, axis=-1, keepdims=True)
    d = a - mu
    var = jnp.mean(d * d, axis=-1, keepdims=True)
    y = d * lax.rsqrt(var + NORM_EPS) * lg_ref[...] + lb_ref[...]
    return _silu(y) * _silu(az)


def _in_even_kernel(x_ref, g_ref, mod_ref, w_ref, cw_ref, cb_ref, lg_ref, lb_ref, *rest,
                    sample, n_alias=0, slot=0, pos=None):
    if sample:
        (xp_ref, xn_ref, cos_ref, sin_ref, cost_ref, sint_ref,
         ma_ref, qt_ref, k_ref, vt_ref, bz_ref, sh_ref, acc_ref) = rest
    else:
        ma_ref, qt_ref, k_ref, kt_ref, v_ref, vt_ref, bz_ref, sh_ref, acc_ref = rest[n_alias:]
        kt_ref = _own_slot(kt_ref, slot)
        v_ref = _own_slot(v_ref, slot)
    hb = _pre_norm(x_ref, g_ref, mod_ref)
    a = _glu(_dot(hb, w_ref[:, 0:2 * A_W]))
    az = _dot(hb, w_ref[:, 2 * A_W:3 * A_W])
    n_sub = TM // SEQ
    if sample:
        tiles_per_seq = DEC_SEQ // TM
        xh = jnp.concatenate([xp_ref[...], xn_ref[...]], axis=0)
        ah = _glu(_dot(_modulate(xh, g_ref, mod_ref), w_ref[:, 0:2 * A_W]))
        prev = jnp.where(pos != 0, ah[:HALO], 0.0)
        nxt = jnp.where(pos != tiles_per_seq - 1, ah[HALO:], 0.0)
        full = jnp.concatenate([prev, a, nxt], axis=0)
        pads = [full[j * SEQ:(j + 1) * SEQ + 2 * HALO] for j in range(n_sub)]
    else:
        zeros = jnp.zeros((HALO, A_W), F32)
        pads = [jnp.concatenate([zeros, a[j * SEQ:(j + 1) * SEQ], zeros], axis=0) for j in range(n_sub)]
    o = 3 * A_W
    q = _dot(hb, w_ref[:, o:o + B_W])
    k = _dot(hb, w_ref[:, o + B_W:o + 2 * B_W])
    v = _dot(hb, w_ref[:, o + 2 * B_W:o + 3 * B_W])
    bz_ref[...] = _dot(hb, w_ref[:, o + 3 * B_W:o + 4 * B_W])
    qt = q.T
    if sample:
        qt = _rope_t(qt, cost_ref[...], sint_ref[...])
        k = _rope(k, cos_ref[...], sin_ref[...])
    else:
        _store_per_seq(kt_ref, k.T)
        for s in range(TM // SEQ):
            for h in range(H_B):
                v_ref[s, pl.ds(h, SEQ, stride=H_B), :] = v[s * SEQ:(s + 1) * SEQ, h * LANES:(h + 1) * LANES]
    _store_per_seq(qt_ref, (qt * QK_SCALE).astype(BF16))
    k_ref[...] = k.astype(BF16)
    _store_per_seq(vt_ref, v.T.astype(BF16))
    for j, pad in enumerate(pads):
        rs = slice(j * SEQ, (j + 1) * SEQ)
        ma_ref[rs, :] = _conv_mix(pad, az[rs], cw_ref, cb_ref, lg_ref, lb_ref, sh_ref, acc_ref).astype(BF16)


def _group_mean_sq(x):
    width = x.shape[-1]
    xx = x * x
    hi = xx.astype(BF16)
    lo = (xx - hi.astype(F32)).astype(BF16)
    r = lax.broadcasted_iota(jnp.int32, (width, width), 0) // DH
    c = lax.broadcasted_iota(jnp.int32, (width, width), 1) // DH
    g = jnp.where(r == c, 1.0, 0.0).astype(BF16)
    return (_dot(hi, g) + _dot(lo, g)) * (1.0 / DH)


def _head_rms_t(xt, gain_col):
    parts = []
    for j in range(xt.shape[0] // DH):
        blk = xt[j * DH:(j + 1) * DH]
        ms = jnp.mean(blk * blk, axis=0, keepdims=True)
        parts.append(blk * lax.rsqrt(ms + NORM_EPS))
    return jnp.concatenate(parts, axis=0) * gain_col


def _in_odd_kernel(x_ref, g_ref, mod_ref, w_ref, qn_ref, kn_ref, *rest, sample, n_alias=0, slot=0):
    if sample:
        (cos_ref, sin_ref, cost_ref, sint_ref,
         cqt_ref, ck_ref, cvt_ref, cz_ref, dqt_ref, dk_ref, dvt_ref, dz_ref) = rest
    else:
        (cqt_ref, ck_ref, ckt_ref, cvt_ref, cz_ref,
         dqt_ref, dk_ref, dkt_ref, dvt_ref, dz_ref) = rest[n_alias:]
        ckt_ref, cvt_ref, dkt_ref, dvt_ref = [_own_slot(r, slot) for r in (ckt_ref, cvt_ref, dkt_ref, dvt_ref)]
    hb = _pre_norm(x_ref, g_ref, mod_ref)
    y = _dot(hb, w_ref[...])
    o = 0
    cqt = _head_rms_t(y[:, o:o + C_W].T, qn_ref[...])
    o += C_W
    ck = y[:, o:o + KV_W]
    ck = ck * lax.rsqrt(_group_mean_sq(ck) + NORM_EPS) * kn_ref[...]
    o += KV_W
    cvt = y[:, o:o + KV_W].T
    o += KV_W
    cz_ref[...] = y[:, o:o + C_W]
    o += C_W
    dqt = y[:, o:o + D_W].T
    o += D_W
    dk = y[:, o:o + KV_W]
    o += KV_W
    dvt = y[:, o:o + KV_W].T
    o += KV_W
    dz_ref[...] = y[:, o:o + D_W]
    if sample:
        cos_t = cost_ref[...]
        sin_t = sint_ref[...]
        cqt = _rope_t(cqt, cos_t, sin_t)
        dqt = _rope_t(dqt, cos_t, sin_t)
        cos = cos_ref[...]
        sin = sin_ref[...]
        ck = _rope(ck, cos, sin)
        dk = _rope(dk, cos, sin)
        _store_chunks(cvt_ref, cvt.astype(BF16))
        _store_chunks(dvt_ref, dvt.astype(BF16))
    else:
        _store_per_seq(ckt_ref, ck.T)
        _store_per_seq(dkt_ref, dk.T)
        _store_per_seq(cvt_ref, cvt)
        _store_per_seq(dvt_ref, dvt)
    _store_per_seq(cqt_ref, (cqt * QK_SCALE).astype(BF16))
    _store_per_seq(dqt_ref, (dqt * QK_SCALE).astype(BF16))
    ck_ref[...] = ck.astype(BF16)
    dk_ref[...] = dk.astype(BF16)


def _exp_terms(segs, extra=None):
    m = None
    for s in segs:
        mi = jnp.max(s, axis=0, keepdims=True)
        m = mi if m is None else jnp.maximum(m, mi)
    if extra is not None:
        extra = extra * LOG2E
        m = jnp.maximum(m, extra)
    es = [jnp.exp2(s - m) for s in segs]
    return es, (None if extra is None else jnp.exp2(extra - m))


def _softmax_t(segs):
    es, _ = _exp_terms(segs)
    den = None
    for e in es:
        di = jnp.sum(e, axis=0, keepdims=True)
        den = di if den is None else den + di
    return es, den


def _keep_rows(xt, lo, hi):
    zeros = lambda r: jnp.zeros((r, xt.shape[1]), xt.dtype)
    parts = []
    if lo > 0:
        parts.append(zeros(lo))
    parts.append(xt[lo:hi])
    if hi < xt.shape[0]:
        parts.append(zeros(xt.shape[0] - hi))
    return jnp.concatenate(parts, axis=0)


def _pipelined(n, scores, finish, ahead):
    ready = [scores(j) for j in range(min(ahead, n))]
    for j in range(n):
        if j + ahead < n:
            ready.append(scores(j + ahead))
        finish(j, ready.pop(0))


def _diff_attn(qt_ref, kvs, z_ref, lam_ref, g_ref, o_ref, *, lam_init):
    lv = lam_ref[...]
    lam = (jnp.exp(jnp.sum(lv[0:1] * lv[1:2], axis=-1, keepdims=True))
           - jnp.exp(jnp.sum(lv[2:3] * lv[3:4], axis=-1, keepdims=True)) + lam_init)

    def scores(h):
        cs = slice(h * LANES, (h + 1) * LANES)
        qt = qt_ref[cs, :]
        ks = [get_k(cs) for get_k, _ in kvs]
        return [[_dot(kk, _keep_rows(qt, c * DH, (c + 1) * DH)) for kk in ks] for c in range(2)]

    def finish(h, ss):
        cs = slice(h * LANES, (h + 1) * LANES)
        es0, den0 = _softmax_t(ss[0])
        es1, den1 = _softmax_t(ss[1])
        r0 = 1.0 / den0
        r1 = lam / den1
        ot = None
        for e0, e1, (_, get_vt) in zip(es0, es1, kvs):
            w = e0 * r0 - e1 * r1
            oi = _dot(get_vt(cs), w.astype(BF16))
            ot = oi if ot is None else ot + oi
        ms = jnp.mean(ot * ot, axis=0, keepdims=True)
        o = (ot * lax.rsqrt(ms + NORM_EPS)).T
        o = (o * g_ref[...]) * (1.0 - lam_init)
        o_ref[:, cs] = (o * _silu(z_ref[:, cs])).astype(o_ref.dtype)

    _pipelined(H_B, scores, finish, DIFF_AHEAD[len(kvs) - 1])


def _post_residual(o, x_ref, g_ref, mod_ref, o_ref):
    ms = jnp.mean(o * o, axis=-1, keepdims=True)
    r = o * lax.rsqrt(ms + NORM_EPS) * g_ref[...]
    gate = mod_ref[...][:, 2 * D_MODEL:]
    o_ref[...] = x_ref[...] + gate * r


def _diff_prompt_kernel(qt_ref, k_ref, vt_ref, z_ref, lam_ref, sg_ref, ma_ref, x_ref, w_ref, g_ref,
                        mod_ref, o_ref, mb_ref, *, lam_init):
    oa = _dot(ma_ref[...], w_ref[0:A_W, :])
    for s in range(qt_ref.shape[0]):
        rs = pl.ds(s * SEQ, SEQ)
        kv = (lambda cs, s=s: k_ref[s * SEQ:(s + 1) * SEQ, cs], lambda cs, s=s: vt_ref[s, cs, :])
        _diff_attn(qt_ref.at[s], [kv], z_ref.at[rs], lam_ref, sg_ref, mb_ref.at[rs], lam_init=lam_init)
    _post_residual(oa + _dot(mb_ref[...], w_ref[A_W:, :]), x_ref, g_ref, mod_ref, o_ref)


def _prep_diff_cache(ck_ref, cv_ref, ckb_ref, cvt_ref):
    ckb_ref[...] = ck_ref[...].T.astype(BF16)
    for h in range(H_B):
        cs = slice(h * LANES, (h + 1) * LANES)
        cvt_ref[cs, :] = cv_ref[pl.ds(h, PAST_LEN, stride=H_B), :].T.astype(BF16)


def _diff_sample_body(qt_ref, k_ref, vt_ref, z_ref, lam_ref, sg_ref, ma_ref, x_ref, w_ref, g_ref,
                      mod_ref, o_ref, ckb_ref, cvt_ref, mb_ref, *, lam_init):
    ctx = (lambda cs: ckb_ref[:, cs], lambda cs: cvt_ref[cs, :])
    loc = (lambda cs: k_ref[:, cs],
           lambda cs: jnp.concatenate([vt_ref[c, cs, :] for c in range(vt_ref.shape[0])], axis=1))
    oa = _dot(ma_ref[...], w_ref[0:A_W, :])
    _diff_attn(qt_ref, [ctx, loc], z_ref, lam_ref, sg_ref, mb_ref, lam_init=lam_init)
    _post_residual(oa + _dot(mb_ref[...], w_ref[A_W:, :]), x_ref, g_ref, mod_ref, o_ref)


def _gqa(qt_ref, segs, z_ref, o_ref, sink_ref=None):
    halves = []

    def scores(j):
        n = j // 4
        qj = qt_ref[j * DH:(j + 1) * DH, :]
        zero = jnp.zeros_like(qj)
        qz = jnp.concatenate([qj, zero] if n == 0 else [zero, qj], axis=0)
        return [_dot(k, qz) for k, _, _ in segs]

    def finish(j, ss):
        n = j // 4
        ss = [s if valid is None else jnp.where(valid, s, -jnp.inf)
              for s, (_, _, valid) in zip(ss, segs)]
        extra = None if sink_ref is None else sink_ref[:, j:j + 1]
        es, den = _exp_terms(ss, extra)
        ot = None
        for e, (_, vt, _) in zip(es, segs):
            ones = jnp.ones((DEN_ROWS, vt.shape[1]), BF16)
            vt1 = jnp.concatenate([vt[n * DH:(n + 1) * DH], ones], axis=0)
            oi = _dot(vt1, e.astype(BF16))
            ot = oi if ot is None else ot + oi
        den = ot[DH:DH + 1] if den is None else den + ot[DH:DH + 1]
        halves.append(ot[:DH] * (1.0 / den))
        if j % 2 == 1:
            cs = slice((j // 2) * LANES, (j // 2 + 1) * LANES)
            o_pair = jnp.concatenate(halves[-2:], axis=0).T
            o_ref[:, cs] = (o_pair * _silu(z_ref[:, cs])).astype(o_ref.dtype)

    _pipelined(2 * 4, scores, finish, GQA_AHEAD[len(segs) - 1])


def _gqa_prompt_kernel(cqt_ref, ck_ref, cvt_ref, cz_ref, dqt_ref, dk_ref, dvt_ref, dz_ref, sink_ref,
                       x_ref, w_ref, g_ref, mod_ref, o_ref, m_ref):
    for s in range(cqt_ref.shape[0]):
        rows = slice(s * SEQ, (s + 1) * SEQ)
        rs = pl.ds(s * SEQ, SEQ)
        seg = lambda k_ref, vt_ref: (k_ref[rows, :], vt_ref[s].astype(BF16), None)
        _gqa(cqt_ref.at[s], [seg(ck_ref, cvt_ref)], cz_ref.at[rs], m_ref.at[rs, pl.ds(0, C_W)])
        _gqa(dqt_ref.at[s], [seg(dk_ref, dvt_ref)], dz_ref.at[rs], m_ref.at[rs, pl.ds(C_W, D_W)], sink_ref)
    _post_residual(_dot(m_ref[...], w_ref[...]), x_ref, g_ref, mod_ref, o_ref)


def _gqa_sample_body(cqt_ref, ck_ref, cvt_ref, cck_ref, ccv_ref, cz_ref,
                     dqt_ref, dk_ref, dvt_ref, cdk_ref, cdv_ref, dz_ref, sink_ref,
                     x_ref, w_ref, g_ref, mod_ref, o_ref, m_ref, *, t):
    oc_ref = m_ref.at[:, pl.ds(0, C_W)]
    od_ref = m_ref.at[:, pl.ds(C_W, D_W)]
    ctx = lambda kt_ref, vt_ref: (kt_ref[...].T.astype(BF16), vt_ref[...].astype(BF16), None)
    n_chunks = DEC_SEQ // LANES
    cvt = jnp.concatenate([cvt_ref[c] for c in range(n_chunks)], axis=1)
    _gqa(cqt_ref, [ctx(cck_ref, ccv_ref), (ck_ref[...], cvt, None)], cz_ref, oc_ref)
    span = 2 * TQ
    t0 = t * TQ
    ws = pl.multiple_of(jnp.clip(t0 - WINDOW, 0, DEC_SEQ - span), WINDOW)
    kpos = ws + lax.broadcasted_iota(jnp.int32, (span, TQ), 0)
    qpos = t0 + lax.broadcasted_iota(jnp.int32, (span, TQ), 1)
    valid = jnp.abs(qpos - kpos) <= WINDOW
    c0 = ws // LANES
    dvt = jnp.concatenate([dvt_ref[c0 + c] for c in range(span // LANES)], axis=1)
    _gqa(dqt_ref, [ctx(cdk_ref, cdv_ref), (dk_ref[pl.ds(ws, span), :], dvt, valid)],
         dz_ref, od_ref, sink_ref)
    _post_residual(_dot(m_ref[...], w_ref[...]), x_ref, g_ref, mod_ref, o_ref)


N_PREP_IN = 5
N_PREP_OUT = 3


def _prep_next_layer(rest, prep, n_alias, n_out, next_layer):
    if not prep:
        return rest[n_alias:]
    w_in_src, w_out_src, cond_ref, wm_ref, bm_ref = rest[:N_PREP_IN]
    rest = rest[N_PREP_IN + n_alias:]
    w_in_dst, w_out_dst, mod_dst = rest[n_out:n_out + N_PREP_OUT]
    w_in_dst[...] = w_in_src[...].astype(BF16)
    w_out_dst[...] = w_out_src[...].astype(BF16)
    part = _dot(_silu(cond_ref[...]).astype(BF16), wm_ref[...].astype(BF16))
    first = pl.program_id(0) == 0

    @pl.when(first)
    def _():
        mod_dst[...] = part + bm_ref[next_layer:next_layer + 1, :]

    @pl.when(jnp.logical_not(first))
    def _():
        mod_dst[...] += part

    return rest[:n_out] + rest[n_out + N_PREP_OUT:]


def _prep_specs(prep, layer, n_steps):
    w_in_f32, w_out_f32, src_idx, cond_chunks, w_mod, b_mod = prep
    chunk = D_MODEL // n_steps
    assert cond_chunks.shape == (n_steps, SUBLANES, chunk)
    in_specs, args, outs = [], [], []
    for src in (w_in_f32, w_out_f32):
        in_specs.append(pl.BlockSpec((None, chunk, src.shape[2]), lambda i: (src_idx, i, 0)))
        args.append(src)
        outs.append((pl.BlockSpec((None, chunk, src.shape[2]), lambda i: (0, i, 0)),
                     jax.ShapeDtypeStruct((1,) + src.shape[1:], BF16)))
    in_specs += [pl.BlockSpec((None, SUBLANES, chunk), lambda i: (i, 0, 0)),
                 pl.BlockSpec((None, chunk, 3 * D_MODEL), lambda i: (layer, i, 0)),
                 _whole(b_mod)]
    args += [cond_chunks, w_mod, b_mod]
    outs.append((pl.BlockSpec((SUBLANES, 3 * D_MODEL), lambda i: (0, 0)),
                 jax.ShapeDtypeStruct((SUBLANES, 3 * D_MODEL), F32)))
    return in_specs, args, outs


def _prep_first_kernel(*refs):
    _prep_next_layer(refs, True, 0, 0, 0)


def _prep_first_layer(prep):
    n_steps = prep[3].shape[0]
    in_specs, args, outs = _prep_specs(prep, 0, n_steps)
    return pl.pallas_call(
        _prep_first_kernel,
        grid=(n_steps,),
        in_specs=in_specs,
        out_specs=[o[0] for o in outs],
        out_shape=[o[1] for o in outs],
        compiler_params=_params("arbitrary"),
        name="prep_layer0",
    )(*args)


def _prompt_even_kernel(x_ref, g_ref, mod_ref, w_ref, cw_ref, cb_ref, lg_ref, lb_ref, lam_ref, sg_ref,
                        wo_ref, gp_ref, *rest, lam_init, layer, n_alias, slot, prep):
    g_ref, mod_ref, gp_ref, cb_ref, lg_ref, lb_ref, sg_ref = _layer_rows(
        layer, 0, g_ref, mod_ref, gp_ref, cb_ref, lg_ref, lb_ref, sg_ref)
    rest = _prep_next_layer(rest, prep, n_alias, 3, layer + 1)
    (o_ref, kt_ref, v_ref,
     ma_ref, qt_ref, k_ref, vt_ref, bz_ref, sh_ref, acc_ref, mb_ref) = rest
    _in_even_kernel(x_ref, g_ref, mod_ref, w_ref, cw_ref, cb_ref, lg_ref, lb_ref,
                    ma_ref, qt_ref, k_ref, kt_ref, v_ref, vt_ref, bz_ref, sh_ref, acc_ref,
                    sample=False, slot=slot)
    _diff_prompt_kernel(qt_ref, k_ref, vt_ref, bz_ref, lam_ref, sg_ref, ma_ref, x_ref, wo_ref, gp_ref,
                        mod_ref, o_ref, mb_ref, lam_init=lam_init)


def _prompt_odd_kernel(x_ref, g_ref, mod_ref, w_ref, qn_ref, kn_ref, sink_ref, wo_ref, gp_ref, *rest,
                       layer, n_alias, slot, prep):
    g_ref, mod_ref, gp_ref, kn_ref, sink_ref = _layer_rows(layer, 0, g_ref, mod_ref, gp_ref, kn_ref, sink_ref)
    rest = _prep_next_layer(rest, prep, n_alias, 5, layer + 1)
    (o_ref, ckt_ref, cvt_ref, dkt_ref, dvt_ref,
     cqt_ref, ck_ref, cz_ref, dqt_ref, dk_ref, dz_ref, m_ref) = rest
    _in_odd_kernel(x_ref, g_ref, mod_ref, w_ref, qn_ref, kn_ref,
                   cqt_ref, ck_ref, ckt_ref, cvt_ref, cz_ref, dqt_ref, dk_ref, dkt_ref, dvt_ref, dz_ref,
                   sample=False, slot=slot)
    _gqa_prompt_kernel(cqt_ref, ck_ref, _own_slot(cvt_ref, slot, fill=False), cz_ref,
                       dqt_ref, dk_ref, _own_slot(dvt_ref, slot, fill=False), dz_ref, sink_ref,
                       x_ref, wo_ref, gp_ref, mod_ref, o_ref, m_ref)


def _prompt_layer(x, layer, g_pre, g_post, mod, w_in, w_out, w_idx, head, tail, carry, lam_init=None,
                  prep_next=None):
    n = x.shape[0]
    even = layer % 2 == 0
    row = lambda i: (i, 0)
    n_seq = TM // SEQ
    params = (g_pre, w_in) + tuple(head) + tuple(tail) + (w_out, g_post)
    idx = (layer, w_idx) + (layer // 2,) * (len(head) + len(tail)) + (w_idx, layer)
    specs = [_pick(p, i) for p, i in zip(params, idx)]
    in_specs = [pl.BlockSpec((TM, D_MODEL), row), specs[0],
                _whole(mod)] + specs[1:]
    args = [x, g_pre, mod, w_in] + list(head) + list(tail) + [w_out, g_post]
    n_layers = (DEPTH + 1 - layer % 2) // 2
    if carry:
        cache_spec = lambda r, c: pl.BlockSpec((n_seq, None, r, c), lambda i: (i, layer // 2, 0, 0))
    else:
        cache_spec = lambda r, c: pl.BlockSpec((n_seq, n_layers, r, c), lambda i: (i, 0, 0, 0))
    cache = lambda r, c: (cache_spec(r, c), jax.ShapeDtypeStruct((n // SEQ, n_layers, r, c), F32))
    outs = [(pl.BlockSpec((TM, D_MODEL), row), jax.ShapeDtypeStruct((n, D_MODEL), F32))]
    wide = pltpu.VMEM((TM, B_W), BF16)
    slab = pltpu.VMEM((n_seq, B_W, SEQ), BF16)
    if even:
        outs += [cache(B_W, SEQ), cache(SEQ * H_B, LANES)]
        scratch = [wide, slab, wide, slab, pltpu.VMEM((TM, B_W), F32),
                   pltpu.VMEM((SUBLANES, SEQ + 3 * SUBLANES, A_W), F32), pltpu.VMEM((SEQ, A_W), F32), wide]
        body = functools.partial(_prompt_even_kernel, lam_init=lam_init)
    else:
        outs += [cache(KV_W, SEQ)] * 4
        narrow = pltpu.VMEM((TM, KV_W), BF16)
        gate = pltpu.VMEM((TM, C_W), F32)
        scratch = [slab, narrow, gate, slab, narrow, gate, pltpu.VMEM((TM, C_W + D_W), BF16)]
        body = _prompt_odd_kernel
    if prep_next is not None:
        prep_in, prep_args, prep_outs = _prep_specs(prep_next, layer + 1, n // TM)
        in_specs += prep_in
        args += prep_args
        outs += prep_outs
    aliases = {}
    for j, a in enumerate(carry):
        aliases[len(args)] = 1 + j
        in_specs.append(pl.BlockSpec(memory_space=pl.ANY))
        args.append(a)
    return pl.pallas_call(
        functools.partial(body, layer=layer, n_alias=len(carry), slot=layer // 2,
                          prep=prep_next is not None),
        grid=(n // TM,),
        in_specs=in_specs,
        out_specs=[o[0] for o in outs],
        out_shape=[o[1] for o in outs],
        scratch_shapes=scratch,
        input_output_aliases=aliases,
        compiler_params=_params("arbitrary"),
        name=f"prompt_layer{layer}",
    )(*args)


N_IN = DEC_SEQ // TM
N_Q = DEC_SEQ // TQ
FIRST_Q = N_IN - 1
N_PHASE = N_IN + N_Q - 1


def _rows(ref, start, size):
    if isinstance(start, int):
        return ref.at[pl.ds(start, size)]
    return ref.at[pl.ds(pl.multiple_of(start, size), size)]


def _run_phases(ph, project, attend):
    if FIRST_Q > 0:
        @pl.when(ph < FIRST_Q)
        def _():
            project(ph)

    @pl.when(ph == FIRST_Q)
    def _():
        project(FIRST_Q)
        attend(0)

    @pl.when(ph > FIRST_Q)
    def _():
        attend(ph - FIRST_Q)


def _sample_even_kernel(x_ref, xp_ref, xn_ref, g_ref, mod_ref, w_ref, cw_ref, cb_ref, lg_ref, lb_ref,
                        cos_ref, sin_ref, cost_ref, sint_ref, ck_ref, cv_ref, lam_ref, sg_ref,
                        wo_ref, gp_ref, o_ref,
                        ma_ref, qt_ref, k_ref, vt_ref, bz_ref, sh_ref, acc_ref, ckb_ref, cvt_ref, mb_ref,
                        xs_ref, *, lam_init, layer):
    g_ref, mod_ref, gp_ref, cb_ref, lg_ref, lb_ref, sg_ref = _layer_rows(
        layer, 1 + pl.program_id(0), g_ref, mod_ref, gp_ref, cb_ref, lg_ref, lb_ref, sg_ref)
    ph = pl.program_id(1)
    per = TM // SEQ

    def project(p):
        _rows(xs_ref, p * TM, TM)[...] = x_ref[...]
        _in_even_kernel(x_ref, g_ref, mod_ref, w_ref, cw_ref, cb_ref, lg_ref, lb_ref,
                        xp_ref, xn_ref, cos_ref, sin_ref, cost_ref, sint_ref,
                        _rows(ma_ref, p * TM, TM), qt_ref.at[pl.ds(p * per, per)],
                        _rows(k_ref, p * TM, TM), vt_ref.at[pl.ds(p * per, per)],
                        _rows(bz_ref, p * TM, TM), sh_ref, acc_ref, sample=True, pos=p)

    def attend(t):
        _diff_sample_body(qt_ref.at[t], k_ref, vt_ref, _rows(bz_ref, t * TQ, TQ), lam_ref, sg_ref,
                          _rows(ma_ref, t * TQ, TQ), _rows(xs_ref, t * TQ, TQ), wo_ref, gp_ref, mod_ref, o_ref,
                          ckb_ref, cvt_ref, mb_ref, lam_init=lam_init)

    @pl.when(ph == 0)
    def _():
        _prep_diff_cache(ck_ref, cv_ref, ckb_ref, cvt_ref)

    _run_phases(ph, project, attend)


def _sample_odd_kernel(x_ref, g_ref, mod_ref, w_ref, qn_ref, kn_ref,
                       cos_ref, sin_ref, cost_ref, sint_ref, cck_ref, ccv_ref, cdk_ref, cdv_ref, sink_ref,
                       wo_ref, gp_ref, o_ref,
                       cqt_ref, ck_ref, cvt_ref, cz_ref, dqt_ref, dk_ref, dvt_ref, dz_ref, m_ref, xs_ref,
                       *, layer):
    g_ref, mod_ref, gp_ref, kn_ref, sink_ref = _layer_rows(
        layer, 1 + pl.program_id(0), g_ref, mod_ref, gp_ref, kn_ref, sink_ref)
    ph = pl.program_id(1)
    per = TM // SEQ
    chunks = TM // LANES

    def project(p):
        _rows(xs_ref, p * TM, TM)[...] = x_ref[...]
        _in_odd_kernel(x_ref, g_ref, mod_ref, w_ref, qn_ref, kn_ref,
                       cos_ref, sin_ref, cost_ref, sint_ref,
                       cqt_ref.at[pl.ds(p * per, per)], _rows(ck_ref, p * TM, TM),
                       cvt_ref.at[pl.ds(p * chunks, chunks)], _rows(cz_ref, p * TM, TM),
                       dqt_ref.at[pl.ds(p * per, per)], _rows(dk_ref, p * TM, TM),
                       dvt_ref.at[pl.ds(p * chunks, chunks)], _rows(dz_ref, p * TM, TM), sample=True)

    def attend(t):
        _gqa_sample_body(cqt_ref.at[t], ck_ref, cvt_ref, cck_ref, ccv_ref, _rows(cz_ref, t * TQ, TQ),
                         dqt_ref.at[t], dk_ref, dvt_ref, cdk_ref, cdv_ref, _rows(dz_ref, t * TQ, TQ),
                         sink_ref, _rows(xs_ref, t * TQ, TQ), wo_ref, gp_ref, mod_ref, o_ref, m_ref, t=t)

    _run_phases(ph, project, attend)


def _sample_layer(x, layer, g_pre, g_post, mod, w_in, w_out, w_idx, head, tail, caches, tables,
                  lam_init=None):
    n = x.shape[0]
    even = layer % 2 == 0
    in_tile = lambda b, ph: b * N_IN + jnp.minimum(ph, N_IN - 1)
    q_tile = lambda b, ph: b * N_Q + jnp.maximum(ph - FIRST_Q, 0)
    tab = lambda b, ph: jnp.minimum(ph, N_IN - 1)
    cos, sin, cos_t, sin_t = tables
    x_spec = pl.BlockSpec((TM, D_MODEL), lambda b, ph: (in_tile(b, ph), 0))
    mod_spec = _whole(mod)
    table_specs = [pl.BlockSpec((TM, LANES), lambda b, ph: (tab(b, ph), 0)),
                   pl.BlockSpec((TM, LANES), lambda b, ph: (tab(b, ph), 0)),
                   pl.BlockSpec((LANES, TM), lambda b, ph: (0, tab(b, ph))),
                   pl.BlockSpec((LANES, TM), lambda b, ph: (0, tab(b, ph)))]
    cache_specs = [pl.BlockSpec((None, None) + c.shape[2:], lambda b, ph: (b, layer // 2, 0, 0))
                   for c in caches]
    in_specs = [x_spec]
    args = [x]
    if even:
        hb = TM // HALO
        last = n // HALO - 1
        in_specs += [pl.BlockSpec((HALO, D_MODEL), lambda b, ph: (jnp.maximum(in_tile(b, ph) * hb - 1, 0), 0)),
                     pl.BlockSpec((HALO, D_MODEL),
                                  lambda b, ph: (jnp.minimum((in_tile(b, ph) + 1) * hb, last), 0))]
        args += [x, x]
    in_specs += [_pick(g_pre, layer), mod_spec, _pick(w_in, w_idx)] + [_pick(h, layer // 2) for h in head]
    args += [g_pre, mod, w_in] + list(head)
    in_specs += table_specs + cache_specs + [_pick(t, layer // 2) for t in tail]
    args += [cos, sin, cos_t, sin_t] + list(caches) + list(tail)
    in_specs += [_pick(w_out, w_idx), _pick(g_post, layer)]
    args += [w_out, g_post]
    seq_wide = lambda w, dt: pltpu.VMEM((DEC_SEQ, w), dt)
    slab = pltpu.VMEM((DEC_SEQ // SEQ, B_W, SEQ), BF16)
    if even:
        scratch = [seq_wide(A_W, BF16), slab, seq_wide(B_W, BF16), slab, seq_wide(B_W, F32),
                   pltpu.VMEM((SUBLANES, SEQ + 3 * SUBLANES, A_W), F32), pltpu.VMEM((SEQ, A_W), F32),
                   pltpu.VMEM((PAST_LEN, B_W), BF16), pltpu.VMEM((B_W, PAST_LEN), BF16),
                   pltpu.VMEM((TQ, B_W), BF16), seq_wide(D_MODEL, F32)]
        body = functools.partial(_sample_even_kernel, lam_init=lam_init, layer=layer)
    else:
        chunk = pltpu.VMEM((DEC_SEQ // LANES, KV_W, LANES), BF16)
        scratch = [slab, seq_wide(KV_W, BF16), chunk, seq_wide(C_W, F32),
                   slab, seq_wide(KV_W, BF16), chunk, seq_wide(D_W, F32),
                   pltpu.VMEM((TQ, C_W + D_W), BF16), seq_wide(D_MODEL, F32)]
        body = functools.partial(_sample_odd_kernel, layer=layer)
    return pl.pallas_call(
        body,
        grid=(DEC_BATCH, N_PHASE),
        in_specs=in_specs,
        out_specs=pl.BlockSpec((TQ, D_MODEL), lambda b, ph: (q_tile(b, ph), 0)),
        out_shape=jax.ShapeDtypeStruct((n, D_MODEL), F32),
        scratch_shapes=scratch,
        compiler_params=_params("arbitrary", "arbitrary"),
        name=f"sample_layer{layer}",
    )(*args)


def _rope_tables():
    nf = DH // 4
    t = jnp.arange(DEC_SEQ)
    row = (t // GRID_W).astype(F32)
    col = (t % GRID_W).astype(F32)
    inv = ROPE_THETA ** (-jnp.arange(nf, dtype=F32) / nf)
    d = jnp.arange(DH)
    axis = d // (2 * nf)
    second = (d % (2 * nf)) // nf
    f = d % nf
    pos = jnp.where(axis[None, :] == 0, row[:, None], col[:, None])
    ang = pos * inv[f][None, :]
    cos = jnp.cos(ang)
    sin = jnp.where(second[None, :] == 0, -jnp.sin(ang), jnp.sin(ang))
    reps = LANES // DH
    cos = jnp.tile(cos, (1, reps))
    sin = jnp.tile(sin, (1, reps))
    return cos, sin, cos.T, sin.T


def kernel(x_prompt, x_sample, cache_b_k, cache_b_v, cache_c_k, cache_c_v, cache_d_k, cache_d_v, c, c_ctx, norm_pre, norm_post, w_mod, b_mod, w_in_even, a_conv_w, a_conv_b, a_ln_g, a_ln_b, b_lambda, b_subln_g, w_out_even, w_in_odd, c_q_norm, c_k_norm, d_sink, w_out_odd):
    n_even = (DEPTH + 1) // 2
    n_odd = DEPTH // 2
    cond8 = jnp.zeros((SUBLANES, D_MODEL), F32).at[0].set(c_ctx).at[1:1 + DEC_BATCH].set(c)
    n_chunk = BATCH * SEQ // TM
    cond_chunks = jnp.moveaxis(cond8.reshape(SUBLANES, n_chunk, D_MODEL // n_chunk), 1, 0)
    tables = _rope_tables()

    xp = x_prompt.reshape(BATCH * SEQ, D_MODEL)
    xs = x_sample.reshape(DEC_BATCH * DEC_SEQ, D_MODEL)
    feat = lambda a, w: jnp.moveaxis(a.reshape(a.shape[:3] + (w,)), 2, 3)
    cbk = feat(cache_b_k, B_W)
    cbv = cache_b_v.reshape(DEC_BATCH, n_even, PAST_LEN * H_B, 2 * DH)
    cck = feat(cache_c_k, KV_W)
    ccv = feat(cache_c_v, KV_W)
    cdk = feat(cache_d_k, KV_W)
    cdv = feat(cache_d_v, KV_W)

    g_pre, g_post = norm_pre, norm_post
    conv = (jnp.zeros((n_even, 4 * SUBLANES, A_W), F32).at[:, :CONV_K].set(a_conv_w),
            a_conv_b, a_ln_g, a_ln_b)
    subln = b_subln_g
    qkn = (jnp.tile(c_q_norm, (1, C_W // DH)).reshape(n_odd, C_W, 1), jnp.tile(c_k_norm, (1, KV_W // DH)))
    sink = d_sink

    f32_weights = ((w_in_even, w_out_even), (w_in_odd, w_out_odd))
    *weights, mod = _prep_first_layer(f32_weights[0] + (0, cond_chunks, w_mod, b_mod))
    new_even, new_odd = (), ()
    for l in range(DEPTH):
        w_in, w_out = weights
        mod_l = mod
        prep_next = (f32_weights[(l + 1) % 2] + ((l + 1) // 2, cond_chunks, w_mod, b_mod)
                     if l + 1 < DEPTH else None)
        if l % 2 == 0:
            lam_init = 0.8 - 0.6 * math.exp(-0.3 * l)
            outs = _prompt_layer(xp, l, g_pre, g_post, mod_l, w_in, w_out, 0, conv, (b_lambda, subln),
                                 new_even, lam_init, prep_next)
            xp, new_even, (*weights, mod) = outs[0], outs[1:3], outs[3:]
            xs = _sample_layer(xs, l, g_pre, g_post, mod_l, w_in, w_out, 0, conv, (b_lambda, subln),
                               (cbk, cbv), tables, lam_init)
        else:
            outs = _prompt_layer(xp, l, g_pre, g_post, mod_l, w_in, w_out, 0, qkn, (sink,), new_odd,
                                 None, prep_next)
            xp, new_odd, (*weights, mod) = outs[0], outs[1:5], (outs[5:] if prep_next else (None, None, None))
            xs = _sample_layer(xs, l, g_pre, g_post, mod_l, w_in, w_out, 0, qkn, (sink,),
                               (cck, ccv, cdk, cdv), tables)

    def token_major(a, heads):
        return jnp.moveaxis(a.reshape(a.shape[:2] + heads + (DH, SEQ)), -1, 2)

    kt, v = new_even
    ckt, cvt, dkt, dvt = new_odd
    return (xp.reshape(BATCH, SEQ, D_MODEL), xs.reshape(DEC_BATCH, DEC_SEQ, D_MODEL),
            token_major(kt, (H_B, 2)), v.reshape(BATCH, n_even, SEQ, H_B, 2 * DH),
            token_major(ckt, (2,)), token_major(cvt, (2,)), token_major(dkt, (2,)), token_major(dvt, (2,)))
```

```python
import functools
import math

import jax
import jax.numpy as jnp
from jax import lax
from jax.experimental import pallas as pl
from jax.experimental.pallas import tpu as pltpu

F32 = jnp.float32
BF16 = jnp.bfloat16

D_MODEL = 1024
BATCH = 16
SEQ = 256
DEPTH = 4
DEC_BATCH = 2
DEC_SEQ = 1024
PAST_LEN = 512
GRID_W = 64
ROPE_THETA = 10000.0
NORM_EPS = 1e-6
DH = 64
A_W = 512
CONV_K = 31
H_B = 4
B_W = 512
C_W = 512
KV_W = 128
D_W = 512
WINDOW = 128
LOG2E = math.log2(math.e)
QK_SCALE = DH ** -0.5 * LOG2E

LANES = 128
SUBLANES = 8
VMEM_LIMIT = 56 * 1024 * 1024

TM = 512
TQ = SEQ
HALO = 16
GQA_AHEAD = (8, 3)
DIFF_AHEAD = (3, 2)
ROW_CHUNK = 64
DEN_ROWS = 16


def _params(*sem):
    return pltpu.CompilerParams(dimension_semantics=sem, vmem_limit_bytes=VMEM_LIMIT)


def _silu(x):
    return x * jax.nn.sigmoid(x)


def _dot(a, b):
    return jnp.dot(a, b, preferred_element_type=F32)


def _pick(stacked, idx, mode=None):
    if stacked.ndim == 2:
        return _whole(stacked)
    return pl.BlockSpec((None,) + stacked.shape[1:], lambda *_: (idx,) + (0,) * (stacked.ndim - 1),
                        pipeline_mode=mode)


ONE_BUFFER = pl.Buffered(1)


def _whole(a):
    return pl.BlockSpec(a.shape, lambda *_: (0,) * a.ndim)


def _layer_rows(layer, mod_row, g_ref, mod_ref, gp_ref, *half):
    one = lambda ref, i: ref.at[pl.ds(i, 1)]
    return ((one(g_ref, layer), one(mod_ref, mod_row), one(gp_ref, layer))
            + tuple(one(r, layer // 2) for r in half))


def _pre_norm(x_ref, g_ref, mod_ref):
    return _modulate(x_ref[...], g_ref, mod_ref)


def _modulate(x, g_ref, mod_ref):
    ms = jnp.mean(x * x, axis=-1, keepdims=True)
    mod = mod_ref[...]
    sh = mod[:, :D_MODEL]
    sc = mod[:, D_MODEL:2 * D_MODEL]
    h = (x * lax.rsqrt(ms + NORM_EPS)) * (g_ref[...] * (1.0 + sc)) + sh
    return h.astype(BF16)


def _rope(x, cos, sin_signed):
    lane = lax.broadcasted_iota(jnp.int32, (1, LANES), 1)
    first = (lane % 32) < 16
    blocks = []
    for c in range(0, x.shape[-1], LANES):
        xb = x[:, c:c + LANES]
        partner = jnp.where(first, pltpu.roll(xb, LANES - 16, 1), pltpu.roll(xb, 16, 1))
        blocks.append(xb * cos + partner * sin_signed)
    return blocks[0] if len(blocks) == 1 else jnp.concatenate(blocks, axis=1)


def _rope_t(x, cos_t, sin_t):
    row = lax.broadcasted_iota(jnp.int32, (LANES, 1), 0)
    first = (row % 32) < 16
    blocks = []
    for r in range(0, x.shape[0], LANES):
        xb = x[r:r + LANES]
        partner = jnp.where(first, pltpu.roll(xb, LANES - 16, 0), pltpu.roll(xb, 16, 0))
        blocks.append(xb * cos_t + partner * sin_t)
    return jnp.concatenate(blocks, axis=0)


def _store_chunks(ref, xt):
    for c in range(xt.shape[1] // LANES):
        ref[c] = xt[:, c * LANES:(c + 1) * LANES]


def _store_per_seq(ref, xt):
    for s in range(xt.shape[1] // SEQ):
        ref[s] = xt[:, s * SEQ:(s + 1) * SEQ]


def _own_slot(ref, slot, fill=True):
    if len(ref.shape) == 3:
        return ref
    for other in range(ref.shape[1]):
        if fill and other != slot:
            ref[:, other] = jnp.zeros((ref.shape[0],) + tuple(ref.shape[2:]), ref.dtype)
    return ref.at[:, slot]


def _glu(ug):
    return ug[:, :A_W] * jax.nn.sigmoid(ug[:, A_W:])


def _conv_mix(pad, az, cw_ref, cb_ref, lg_ref, lb_ref, sh_ref, acc_ref):
    rows = sh_ref.shape[1]
    for b in range(SUBLANES):
        sh_ref[b] = pad[b:b + rows]
    base = HALO - CONV_K // 2
    for c0 in range(0, A_W, LANES):
        cs = slice(c0, c0 + LANES)
        for r0 in range(0, SEQ, ROW_CHUNK):
            acc = jnp.zeros((ROW_CHUNK, LANES), F32) + cb_ref[:, cs]
            for k in range(CONV_K):
                j = k + base
                s = r0 + (j // SUBLANES) * SUBLANES
                acc = acc + sh_ref[j % SUBLANES, s:s + ROW_CHUNK, cs] * cw_ref[k:k + 1, cs]
            acc_ref[r0:r0 + ROW_CHUNK, cs] = acc
    a = acc_ref[...]
    mu = jnp.mean(a, axis=-1, keepdims=True)
    d = a - mu
    var = jnp.mean(d * d, axis=-1, keepdims=True)
    y = d * lax.rsqrt(var + NORM_EPS) * lg_ref[...] + lb_ref[...]
    return _silu(y) * _silu(az)


def _in_even_kernel(x_ref, g_ref, mod_ref, w_ref, cw_ref, cb_ref, lg_ref, lb_ref, *rest,
                    sample, n_alias=0, slot=0, pos=None):
    if sample:
        (xp_ref, xn_ref, cos_ref, sin_ref, cost_ref, sint_ref,
         ma_ref, qt_ref, k_ref, vt_ref, bz_ref, sh_ref, acc_ref) = rest
    else:
        ma_ref, qt_ref, k_ref, kt_ref, v_ref, vt_ref, bz_ref, sh_ref, acc_ref = rest[n_alias:]
        kt_ref = _own_slot(kt_ref, slot)
        v_ref = _own_slot(v_ref, slot)
    hb = _pre_norm(x_ref, g_ref, mod_ref)
    a = _glu(_dot(hb, w_ref[:, 0:2 * A_W]))
    az = _dot(hb, w_ref[:, 2 * A_W:3 * A_W])
    n_sub = TM // SEQ
    if sample:
        tiles_per_seq = DEC_SEQ // TM
        xh = jnp.concatenate([xp_ref[...], xn_ref[...]], axis=0)
        ah = _glu(_dot(_modulate(xh, g_ref, mod_ref), w_ref[:, 0:2 * A_W]))
        prev = jnp.where(pos != 0, ah[:HALO], 0.0)
        nxt = jnp.where(pos != tiles_per_seq - 1, ah[HALO:], 0.0)
        full = jnp.concatenate([prev, a, nxt], axis=0)
        pads = [full[j * SEQ:(j + 1) * SEQ + 2 * HALO] for j in range(n_sub)]
    else:
        zeros = jnp.zeros((HALO, A_W), F32)
        pads = [jnp.concatenate([zeros, a[j * SEQ:(j + 1) * SEQ], zeros], axis=0) for j in range(n_sub)]
    o = 3 * A_W
    q = _dot(hb, w_ref[:, o:o + B_W])
    k = _dot(hb, w_ref[:, o + B_W:o + 2 * B_W])
    v = _dot(hb, w_ref[:, o + 2 * B_W:o + 3 * B_W])
    bz_ref[...] = _dot(hb, w_ref[:, o + 3 * B_W:o + 4 * B_W])
    qt = q.T
    if sample:
        qt = _rope_t(qt, cost_ref[...], sint_ref[...])
        k = _rope(k, cos_ref[...], sin_ref[...])
    else:
        _store_per_seq(kt_ref, k.T)
        for s in range(TM // SEQ):
            for h in range(H_B):
                v_ref[s, pl.ds(h, SEQ, stride=H_B), :] = v[s * SEQ:(s + 1) * SEQ, h * LANES:(h + 1) * LANES]
    _store_per_seq(qt_ref, (qt * QK_SCALE).astype(BF16))
    k_ref[...] = k.astype(BF16)
    _store_per_seq(vt_ref, v.T.astype(BF16))
    for j, pad in enumerate(pads):
        rs = slice(j * SEQ, (j + 1) * SEQ)
        ma_ref[rs, :] = _conv_mix(pad, az[rs], cw_ref, cb_ref, lg_ref, lb_ref, sh_ref, acc_ref).astype(BF16)


def _group_mean_sq(x):
    width = x.shape[-1]
    xx = x * x
    hi = xx.astype(BF16)
    lo = (xx - hi.astype(F32)).astype(BF16)
    r = lax.broadcasted_iota(jnp.int32, (width, width), 0) // DH
    c = lax.broadcasted_iota(jnp.int32, (width, width), 1) // DH
    g = jnp.where(r == c, 1.0, 0.0).astype(BF16)
    return (_dot(hi, g) + _dot(lo, g)) * (1.0 / DH)


def _head_rms_t(xt, gain_col):
    parts = []
    for j in range(xt.shape[0] // DH):
        blk = xt[j * DH:(j + 1) * DH]
        ms = jnp.mean(blk * blk, axis=0, keepdims=True)
        parts.append(blk * lax.rsqrt(ms + NORM_EPS))
    return jnp.concatenate(parts, axis=0) * gain_col


def _in_odd_kernel(x_ref, g_ref, mod_ref, w_ref, qn_ref, kn_ref, *rest, sample, n_alias=0, slot=0):
    if sample:
        (cos_ref, sin_ref, cost_ref, sint_ref,
         cqt_ref, ck_ref, cvt_ref, cz_ref, dqt_ref, dk_ref, dvt_ref, dz_ref) = rest
    else:
        (cqt_ref, ck_ref, ckt_ref, cvt_ref, cz_ref,
         dqt_ref, dk_ref, dkt_ref, dvt_ref, dz_ref) = rest[n_alias:]
        ckt_ref, cvt_ref, dkt_ref, dvt_ref = [_own_slot(r, slot) for r in (ckt_ref, cvt_ref, dkt_ref, dvt_ref)]
    hb = _pre_norm(x_ref, g_ref, mod_ref)
    y = _dot(hb, w_ref[...])
    o = 0
    cqt = _head_rms_t(y[:, o:o + C_W].T, qn_ref[...])
    o += C_W
    ck = y[:, o:o + KV_W]
    ck = ck * lax.rsqrt(_group_mean_sq(ck) + NORM_EPS) * kn_ref[...]
    o += KV_W
    cvt = y[:, o:o + KV_W].T
    o += KV_W
    cz_ref[...] = y[:, o:o + C_W]
    o += C_W
    dqt = y[:, o:o + D_W].T
    o += D_W
    dk = y[:, o:o + KV_W]
    o += KV_W
    dvt = y[:, o:o + KV_W].T
    o += KV_W
    dz_ref[...] = y[:, o:o + D_W]
    if sample:
        cos_t = cost_ref[...]
        sin_t = sint_ref[...]
        cqt = _rope_t(cqt, cos_t, sin_t)
        dqt = _rope_t(dqt, cos_t, sin_t)
        cos = cos_ref[...]
        sin = sin_ref[...]
        ck = _rope(ck, cos, sin)
        dk = _rope(dk, cos, sin)
        _store_chunks(cvt_ref, cvt.astype(BF16))
        _store_chunks(dvt_ref, dvt.astype(BF16))
    else:
        _store_per_seq(ckt_ref, ck.T)
        _store_per_seq(dkt_ref, dk.T)
        _store_per_seq(cvt_ref, cvt)
        _store_per_seq(dvt_ref, dvt)
    _store_per_seq(cqt_ref, (cqt * QK_SCALE).astype(BF16))
    _store_per_seq(dqt_ref, (dqt * QK_SCALE).astype(BF16))
    ck_ref[...] = ck.astype(BF16)
    dk_ref[...] = dk.astype(BF16)


def _exp_terms(segs, extra=None):
    m = None
    for s in segs:
        mi = jnp.max(s, axis=0, keepdims=True)
        m = mi if m is None else jnp.maximum(m, mi)
    if extra is not None:
        extra = extra * LOG2E
        m = jnp.maximum(m, extra)
    es = [jnp.exp2(s - m) for s in segs]
    return es, (None if extra is None else jnp.exp2(extra - m))


def _softmax_t(segs):
    es, _ = _exp_terms(segs)
    den = None
    for e in es:
        di = jnp.sum(e, axis=0, keepdims=True)
        den = di if den is None else den + di
    return es, den


def _keep_rows(xt, lo, hi):
    zeros = lambda r: jnp.zeros((r, xt.shape[1]), xt.dtype)
    parts = []
    if lo > 0:
        parts.append(zeros(lo))
    parts.append(xt[lo:hi])
    if hi < xt.shape[0]:
        parts.append(zeros(xt.shape[0] - hi))
    return jnp.concatenate(parts, axis=0)


def _pipelined(n, scores, finish, ahead):
    ready = [scores(j) for j in range(min(ahead, n))]
    for j in range(n):
        if j + ahead < n:
            ready.append(scores(j + ahead))
        finish(j, ready.pop(0))


def _diff_attn(qt_ref, kvs, z_ref, lam_ref, g_ref, o_ref, *, lam_init):
    lv = lam_ref[...]
    lam = (jnp.exp(jnp.sum(lv[0:1] * lv[1:2], axis=-1, keepdims=True))
           - jnp.exp(jnp.sum(lv[2:3] * lv[3:4], axis=-1, keepdims=True)) + lam_init)

    def scores(h):
        cs = slice(h * LANES, (h + 1) * LANES)
        qt = qt_ref[cs, :]
        ks = [get_k(cs) for get_k, _ in kvs]
        return [[_dot(kk, _keep_rows(qt, c * DH, (c + 1) * DH)) for kk in ks] for c in range(2)]

    def finish(h, ss):
        cs = slice(h * LANES, (h + 1) * LANES)
        es0, den0 = _softmax_t(ss[0])
        es1, den1 = _softmax_t(ss[1])
        r0 = 1.0 / den0
        r1 = lam / den1
        ot = None
        for e0, e1, (_, get_vt) in zip(es0, es1, kvs):
            w = e0 * r0 - e1 * r1
            oi = _dot(get_vt(cs), w.astype(BF16))
            ot = oi if ot is None else ot + oi
        ms = jnp.mean(ot * ot, axis=0, keepdims=True)
        o = (ot * lax.rsqrt(ms + NORM_EPS)).T
        o = (o * g_ref[...]) * (1.0 - lam_init)
        o_ref[:, cs] = (o * _silu(z_ref[:, cs])).astype(o_ref.dtype)

    _pipelined(H_B, scores, finish, DIFF_AHEAD[len(kvs) - 1])


def _post_residual(o, x_ref, g_ref, mod_ref, o_ref):
    ms = jnp.mean(o * o, axis=-1, keepdims=True)
    r = o * lax.rsqrt(ms + NORM_EPS) * g_ref[...]
    gate = mod_ref[...][:, 2 * D_MODEL:]
    o_ref[...] = x_ref[...] + gate * r


def _diff_prompt_kernel(qt_ref, k_ref, vt_ref, z_ref, lam_ref, sg_ref, ma_ref, x_ref, w_ref, g_ref,
                        mod_ref, o_ref, mb_ref, *, lam_init):
    oa = _dot(ma_ref[...], w_ref[0:A_W, :])
    for s in range(qt_ref.shape[0]):
        rs = pl.ds(s * SEQ, SEQ)
        kv = (lambda cs, s=s: k_ref[s * SEQ:(s + 1) * SEQ, cs], lambda cs, s=s: vt_ref[s, cs, :])
        _diff_attn(qt_ref.at[s], [kv], z_ref.at[rs], lam_ref, sg_ref, mb_ref.at[rs], lam_init=lam_init)
    _post_residual(oa + _dot(mb_ref[...], w_ref[A_W:, :]), x_ref, g_ref, mod_ref, o_ref)


def _prep_diff_cache(ck_ref, cv_ref, ckb_ref, cvt_ref):
    ckb_ref[...] = ck_ref[...].T.astype(BF16)
    for h in range(H_B):
        cs = slice(h * LANES, (h + 1) * LANES)
        cvt_ref[cs, :] = cv_ref[pl.ds(h, PAST_LEN, stride=H_B), :].T.astype(BF16)


def _diff_sample_body(qt_ref, k_ref, vt_ref, z_ref, lam_ref, sg_ref, ma_ref, x_ref, w_ref, g_ref,
                      mod_ref, o_ref, ckb_ref, cvt_ref, mb_ref, *, lam_init):
    ctx = (lambda cs: ckb_ref[:, cs], lambda cs: cvt_ref[cs, :])
    loc = (lambda cs: k_ref[:, cs],
           lambda cs: jnp.concatenate([vt_ref[c, cs, :] for c in range(vt_ref.shape[0])], axis=1))
    oa = _dot(ma_ref[...], w_ref[0:A_W, :])
    _diff_attn(qt_ref, [ctx, loc], z_ref, lam_ref, sg_ref, mb_ref, lam_init=lam_init)
    _post_residual(oa + _dot(mb_ref[...], w_ref[A_W:, :]), x_ref, g_ref, mod_ref, o_ref)


def _gqa(qt_ref, segs, z_ref, o_ref, sink_ref=None):
    halves = []

    def scores(j):
        n = j // 4
        qj = qt_ref[j * DH:(j + 1) * DH, :]
        zero = jnp.zeros_like(qj)
        qz = jnp.concatenate([qj, zero] if n == 0 else [zero, qj], axis=0)
        return [_dot(k, qz) for k, _, _ in segs]

    def finish(j, ss):
        n = j // 4
        ss = [s if valid is None else jnp.where(valid, s, -jnp.inf)
              for s, (_, _, valid) in zip(ss, segs)]
        extra = None if sink_ref is None else sink_ref[:, j:j + 1]
        es, den = _exp_terms(ss, extra)
        ot = None
        for e, (_, vt, _) in zip(es, segs):
            ones = jnp.ones((DEN_ROWS, vt.shape[1]), BF16)
            vt1 = jnp.concatenate([vt[n * DH:(n + 1) * DH], ones], axis=0)
            oi = _dot(vt1, e.astype(BF16))
            ot = oi if ot is None else ot + oi
        den = ot[DH:DH + 1] if den is None else den + ot[DH:DH + 1]
        halves.append(ot[:DH] * (1.0 / den))
        if j % 2 == 1:
            cs = slice((j // 2) * LANES, (j // 2 + 1) * LANES)
            o_pair = jnp.concatenate(halves[-2:], axis=0).T
            o_ref[:, cs] = (o_pair * _silu(z_ref[:, cs])).astype(o_ref.dtype)

    _pipelined(2 * 4, scores, finish, GQA_AHEAD[len(segs) - 1])


def _gqa_prompt_kernel(cqt_ref, ck_ref, cvt_ref, cz_ref, dqt_ref, dk_ref, dvt_ref, dz_ref, sink_ref,
                       x_ref, w_ref, g_ref, mod_ref, o_ref, m_ref):
    for s in range(cqt_ref.shape[0]):
        rows = slice(s * SEQ, (s + 1) * SEQ)
        rs = pl.ds(s * SEQ, SEQ)
        seg = lambda k_ref, vt_ref: (k_ref[rows, :], vt_ref[s].astype(BF16), None)
        _gqa(cqt_ref.at[s], [seg(ck_ref, cvt_ref)], cz_ref.at[rs], m_ref.at[rs, pl.ds(0, C_W)])
        _gqa(dqt_ref.at[s], [seg(dk_ref, dvt_ref)], dz_ref.at[rs], m_ref.at[rs, pl.ds(C_W, D_W)], sink_ref)
    _post_residual(_dot(m_ref[...], w_ref[...]), x_ref, g_ref, mod_ref, o_ref)


def _gqa_sample_body(cqt_ref, ck_ref, cvt_ref, cck_ref, ccv_ref, cz_ref,
                     dqt_ref, dk_ref, dvt_ref, cdk_ref, cdv_ref, dz_ref, sink_ref,
                     x_ref, w_ref, g_ref, mod_ref, o_ref, m_ref, *, t):
    oc_ref = m_ref.at[:, pl.ds(0, C_W)]
    od_ref = m_ref.at[:, pl.ds(C_W, D_W)]
    ctx = lambda kt_ref, vt_ref: (kt_ref[...].T.astype(BF16), vt_ref[...].astype(BF16), None)
    n_chunks = DEC_SEQ // LANES
    cvt = jnp.concatenate([cvt_ref[c] for c in range(n_chunks)], axis=1)
    _gqa(cqt_ref, [ctx(cck_ref, ccv_ref), (ck_ref[...], cvt, None)], cz_ref, oc_ref)
    span = 2 * TQ
    t0 = t * TQ
    ws = pl.multiple_of(jnp.clip(t0 - WINDOW, 0, DEC_SEQ - span), WINDOW)
    kpos = ws + lax.broadcasted_iota(jnp.int32, (span, TQ), 0)
    qpos = t0 + lax.broadcasted_iota(jnp.int32, (span, TQ), 1)
    valid = jnp.abs(qpos - kpos) <= WINDOW
    c0 = ws // LANES
    dvt = jnp.concatenate([dvt_ref[c0 + c] for c in range(span // LANES)], axis=1)
    _gqa(dqt_ref, [ctx(cdk_ref, cdv_ref), (dk_ref[pl.ds(ws, span), :], dvt, valid)],
         dz_ref, od_ref, sink_ref)
    _post_residual(_dot(m_ref[...], w_ref[...]), x_ref, g_ref, mod_ref, o_ref)


N_PREP_IN = 5
N_PREP_OUT = 3


def _prep_next_layer(rest, prep, n_alias, n_out, next_layer):
    if not prep:
        return rest[n_alias:]
    w_in_src, w_out_src, cond_ref, wm_ref, bm_ref = rest[:N_PREP_IN]
    rest = rest[N_PREP_IN + n_alias:]
    w_in_dst, w_out_dst, mod_dst = rest[n_out:n_out + N_PREP_OUT]
    w_in_dst[...] = w_in_src[...].astype(BF16)
    w_out_dst[...] = w_out_src[...].astype(BF16)
    part = _dot(_silu(cond_ref[...]).astype(BF16), wm_ref[...].astype(BF16))
    first = pl.program_id(0) == 0

    @pl.when(first)
    def _():
        mod_dst[...] = part + bm_ref[next_layer:next_layer + 1, :]

    @pl.when(jnp.logical_not(first))
    def _():
        mod_dst[...] += part

    return rest[:n_out] + rest[n_out + N_PREP_OUT:]


def _prep_specs(prep, layer, n_steps):
    w_in_f32, w_out_f32, src_idx, cond_chunks, w_mod, b_mod = prep
    chunk = D_MODEL // n_steps
    assert cond_chunks.shape == (n_steps, SUBLANES, chunk)
    in_specs, args, outs = [], [], []
    for src in (w_in_f32, w_out_f32):
        in_specs.append(pl.BlockSpec((None, chunk, src.shape[2]), lambda i: (src_idx, i, 0)))
        args.append(src)
        outs.append((pl.BlockSpec((None, chunk, src.shape[2]), lambda i: (0, i, 0)),
                     jax.ShapeDtypeStruct((1,) + src.shape[1:], BF16)))
    in_specs += [pl.BlockSpec((None, SUBLANES, chunk), lambda i: (i, 0, 0)),
                 pl.BlockSpec((None, chunk, 3 * D_MODEL), lambda i: (layer, i, 0)),
                 _whole(b_mod)]
    args += [cond_chunks, w_mod, b_mod]
    outs.append((pl.BlockSpec((SUBLANES, 3 * D_MODEL), lambda i: (0, 0)),
                 jax.ShapeDtypeStruct((SUBLANES, 3 * D_MODEL), F32)))
    return in_specs, args, outs


def _prep_first_kernel(*refs):
    _prep_next_layer(refs, True, 0, 0, 0)


def _prep_first_layer(prep):
    n_steps = prep[3].shape[0]
    in_specs, args, outs = _prep_specs(prep, 0, n_steps)
    return pl.pallas_call(
        _prep_first_kernel,
        grid=(n_steps,),
        in_specs=in_specs,
        out_specs=[o[0] for o in outs],
        out_shape=[o[1] for o in outs],
        compiler_params=_params("arbitrary"),
        name="prep_layer0",
    )(*args)


def _prompt_even_kernel(x_ref, g_ref, mod_ref, w_ref, cw_ref, cb_ref, lg_ref, lb_ref, lam_ref, sg_ref,
                        wo_ref, gp_ref, *rest, lam_init, layer, n_alias, slot, prep):
    g_ref, mod_ref, gp_ref, cb_ref, lg_ref, lb_ref, sg_ref = _layer_rows(
        layer, 0, g_ref, mod_ref, gp_ref, cb_ref, lg_ref, lb_ref, sg_ref)
    rest = _prep_next_layer(rest, prep, n_alias, 3, layer + 1)
    (o_ref, kt_ref, v_ref,
     ma_ref, qt_ref, k_ref, vt_ref, bz_ref, sh_ref, acc_ref, mb_ref) = rest
    _in_even_kernel(x_ref, g_ref, mod_ref, w_ref, cw_ref, cb_ref, lg_ref, lb_ref,
                    ma_ref, qt_ref, k_ref, kt_ref, v_ref, vt_ref, bz_ref, sh_ref, acc_ref,
                    sample=False, slot=slot)
    _diff_prompt_kernel(qt_ref, k_ref, vt_ref, bz_ref, lam_ref, sg_ref, ma_ref, x_ref, wo_ref, gp_ref,
                        mod_ref, o_ref, mb_ref, lam_init=lam_init)


def _prompt_odd_kernel(x_ref, g_ref, mod_ref, w_ref, qn_ref, kn_ref, sink_ref, wo_ref, gp_ref, *rest,
                       layer, n_alias, slot, prep):
    g_ref, mod_ref, gp_ref, kn_ref, sink_ref = _layer_rows(layer, 0, g_ref, mod_ref, gp_ref, kn_ref, sink_ref)
    rest = _prep_next_layer(rest, prep, n_alias, 5, layer + 1)
    (o_ref, ckt_ref, cvt_ref, dkt_ref, dvt_ref,
     cqt_ref, ck_ref, cz_ref, dqt_ref, dk_ref, dz_ref, m_ref) = rest
    _in_odd_kernel(x_ref, g_ref, mod_ref, w_ref, qn_ref, kn_ref,
                   cqt_ref, ck_ref, ckt_ref, cvt_ref, cz_ref, dqt_ref, dk_ref, dkt_ref, dvt_ref, dz_ref,
                   sample=False, slot=slot)
    _gqa_prompt_kernel(cqt_ref, ck_ref, _own_slot(cvt_ref, slot, fill=False), cz_ref,
                       dqt_ref, dk_ref, _own_slot(dvt_ref, slot, fill=False), dz_ref, sink_ref,
                       x_ref, wo_ref, gp_ref, mod_ref, o_ref, m_ref)


def _prompt_layer(x, layer, g_pre, g_post, mod, w_in, w_out, w_idx, head, tail, carry, lam_init=None,
                  prep_next=None):
    n = x.shape[0]
    even = layer % 2 == 0
    row = lambda i: (i, 0)
    n_seq = TM // SEQ
    params = (g_pre, w_in) + tuple(head) + tuple(tail) + (w_out, g_post)
    idx = (layer, w_idx) + (layer // 2,) * (len(head) + len(tail)) + (w_idx, layer)
    specs = [_pick(p, i, ONE_BUFFER if p is w_in or p is w_out else None) for p, i in zip(params, idx)]
    in_specs = [pl.BlockSpec((TM, D_MODEL), row), specs[0],
                _whole(mod)] + specs[1:]
    args = [x, g_pre, mod, w_in] + list(head) + list(tail) + [w_out, g_post]
    n_layers = (DEPTH + 1 - layer % 2) // 2
    if carry:
        cache_spec = lambda r, c: pl.BlockSpec((n_seq, None, r, c), lambda i: (i, layer // 2, 0, 0))
    else:
        cache_spec = lambda r, c: pl.BlockSpec((n_seq, n_layers, r, c), lambda i: (i, 0, 0, 0))
    cache = lambda r, c: (cache_spec(r, c), jax.ShapeDtypeStruct((n // SEQ, n_layers, r, c), F32))
    outs = [(pl.BlockSpec((TM, D_MODEL), row), jax.ShapeDtypeStruct((n, D_MODEL), F32))]
    wide = pltpu.VMEM((TM, B_W), BF16)
    slab = pltpu.VMEM((n_seq, B_W, SEQ), BF16)
    if even:
        outs += [cache(B_W, SEQ), cache(SEQ * H_B, LANES)]
        scratch = [wide, slab, wide, slab, pltpu.VMEM((TM, B_W), F32),
                   pltpu.VMEM((SUBLANES, SEQ + 3 * SUBLANES, A_W), F32), pltpu.VMEM((SEQ, A_W), F32), wide]
        body = functools.partial(_prompt_even_kernel, lam_init=lam_init)
    else:
        outs += [cache(KV_W, SEQ)] * 4
        narrow = pltpu.VMEM((TM, KV_W), BF16)
        gate = pltpu.VMEM((TM, C_W), F32)
        scratch = [slab, narrow, gate, slab, narrow, gate, pltpu.VMEM((TM, C_W + D_W), BF16)]
        body = _prompt_odd_kernel
    if prep_next is not None:
        prep_in, prep_args, prep_outs = _prep_specs(prep_next, layer + 1, n // TM)
        in_specs += prep_in
        args += prep_args
        outs += prep_outs
    aliases = {}
    for j, a in enumerate(carry):
        aliases[len(args)] = 1 + j
        in_specs.append(pl.BlockSpec(memory_space=pl.ANY))
        args.append(a)
    return pl.pallas_call(
        functools.partial(body, layer=layer, n_alias=len(carry), slot=layer // 2,
                          prep=prep_next is not None),
        grid=(n // TM,),
        in_specs=in_specs,
        out_specs=[o[0] for o in outs],
        out_shape=[o[1] for o in outs],
        scratch_shapes=scratch,
        input_output_aliases=aliases,
        compiler_params=_params("arbitrary"),
        name=f"prompt_layer{layer}",
    )(*args)


N_IN = DEC_SEQ // TM
N_Q = DEC_SEQ // TQ
FIRST_Q = N_IN - 1
N_PHASE = N_IN + N_Q - 1


def _rows(ref, start, size):
    if isinstance(start, int):
        return ref.at[pl.ds(start, size)]
    return ref.at[pl.ds(pl.multiple_of(start, size), size)]


def _run_phases(ph, project, attend):
    if FIRST_Q > 0:
        @pl.when(ph < FIRST_Q)
        def _():
            project(ph)

    @pl.when(ph == FIRST_Q)
    def _():
        project(FIRST_Q)
        attend(0)

    @pl.when(ph > FIRST_Q)
    def _():
        attend(ph - FIRST_Q)


def _sample_even_kernel(x_ref, xp_ref, xn_ref, g_ref, mod_ref, w_ref, cw_ref, cb_ref, lg_ref, lb_ref,
                        cos_ref, sin_ref, cost_ref, sint_ref, ck_ref, cv_ref, lam_ref, sg_ref,
                        xr_ref, wo_ref, gp_ref, o_ref,
                        ma_ref, qt_ref, k_ref, vt_ref, bz_ref, sh_ref, acc_ref, ckb_ref, cvt_ref, mb_ref,
                        *, lam_init, layer):
    g_ref, mod_ref, gp_ref, cb_ref, lg_ref, lb_ref, sg_ref = _layer_rows(
        layer, 1 + pl.program_id(0), g_ref, mod_ref, gp_ref, cb_ref, lg_ref, lb_ref, sg_ref)
    ph = pl.program_id(1)
    per = TM // SEQ

    def project(p):
        _in_even_kernel(x_ref, g_ref, mod_ref, w_ref, cw_ref, cb_ref, lg_ref, lb_ref,
                        xp_ref, xn_ref, cos_ref, sin_ref, cost_ref, sint_ref,
                        _rows(ma_ref, p * TM, TM), qt_ref.at[pl.ds(p * per, per)],
                        _rows(k_ref, p * TM, TM), vt_ref.at[pl.ds(p * per, per)],
                        _rows(bz_ref, p * TM, TM), sh_ref, acc_ref, sample=True, pos=p)

    def attend(t):
        _diff_sample_body(qt_ref.at[t], k_ref, vt_ref, _rows(bz_ref, t * TQ, TQ), lam_ref, sg_ref,
                          _rows(ma_ref, t * TQ, TQ), xr_ref, wo_ref, gp_ref, mod_ref, o_ref,
                          ckb_ref, cvt_ref, mb_ref, lam_init=lam_init)

    @pl.when(ph == 0)
    def _():
        _prep_diff_cache(ck_ref, cv_ref, ckb_ref, cvt_ref)

    _run_phases(ph, project, attend)


def _sample_odd_kernel(x_ref, g_ref, mod_ref, w_ref, qn_ref, kn_ref,
                       cos_ref, sin_ref, cost_ref, sint_ref, cck_ref, ccv_ref, cdk_ref, cdv_ref, sink_ref,
                       xr_ref, wo_ref, gp_ref, o_ref,
                       cqt_ref, ck_ref, cvt_ref, cz_ref, dqt_ref, dk_ref, dvt_ref, dz_ref, m_ref, *, layer):
    g_ref, mod_ref, gp_ref, kn_ref, sink_ref = _layer_rows(
        layer, 1 + pl.program_id(0), g_ref, mod_ref, gp_ref, kn_ref, sink_ref)
    ph = pl.program_id(1)
    per = TM // SEQ
    chunks = TM // LANES

    def project(p):
        _in_odd_kernel(x_ref, g_ref, mod_ref, w_ref, qn_ref, kn_ref,
                       cos_ref, sin_ref, cost_ref, sint_ref,
                       cqt_ref.at[pl.ds(p * per, per)], _rows(ck_ref, p * TM, TM),
                       cvt_ref.at[pl.ds(p * chunks, chunks)], _rows(cz_ref, p * TM, TM),
                       dqt_ref.at[pl.ds(p * per, per)], _rows(dk_ref, p * TM, TM),
                       dvt_ref.at[pl.ds(p * chunks, chunks)], _rows(dz_ref, p * TM, TM), sample=True)

    def attend(t):
        _gqa_sample_body(cqt_ref.at[t], ck_ref, cvt_ref, cck_ref, ccv_ref, _rows(cz_ref, t * TQ, TQ),
                         dqt_ref.at[t], dk_ref, dvt_ref, cdk_ref, cdv_ref, _rows(dz_ref, t * TQ, TQ),
                         sink_ref, xr_ref, wo_ref, gp_ref, mod_ref, o_ref, m_ref, t=t)

    _run_phases(ph, project, attend)


def _sample_layer(x, layer, g_pre, g_post, mod, w_in, w_out, w_idx, head, tail, caches, tables,
                  lam_init=None):
    n = x.shape[0]
    even = layer % 2 == 0
    in_tile = lambda b, ph: b * N_IN + jnp.minimum(ph, N_IN - 1)
    q_tile = lambda b, ph: b * N_Q + jnp.maximum(ph - FIRST_Q, 0)
    tab = lambda b, ph: jnp.minimum(ph, N_IN - 1)
    cos, sin, cos_t, sin_t = tables
    x_spec = pl.BlockSpec((TM, D_MODEL), lambda b, ph: (in_tile(b, ph), 0))
    mod_spec = _whole(mod)
    table_specs = [pl.BlockSpec((TM, LANES), lambda b, ph: (tab(b, ph), 0)),
                   pl.BlockSpec((TM, LANES), lambda b, ph: (tab(b, ph), 0)),
                   pl.BlockSpec((LANES, TM), lambda b, ph: (0, tab(b, ph))),
                   pl.BlockSpec((LANES, TM), lambda b, ph: (0, tab(b, ph)))]
    cache_specs = [pl.BlockSpec((None, None) + c.shape[2:], lambda b, ph: (b, layer // 2, 0, 0))
                   for c in caches]
    res_spec = pl.BlockSpec((TQ, D_MODEL), lambda b, ph: (q_tile(b, ph), 0))
    in_specs = [x_spec]
    args = [x]
    if even:
        hb = TM // HALO
        last = n // HALO - 1
        in_specs += [pl.BlockSpec((HALO, D_MODEL), lambda b, ph: (jnp.maximum(in_tile(b, ph) * hb - 1, 0), 0)),
                     pl.BlockSpec((HALO, D_MODEL),
                                  lambda b, ph: (jnp.minimum((in_tile(b, ph) + 1) * hb, last), 0))]
        args += [x, x]
    in_specs += [_pick(g_pre, layer), mod_spec, _pick(w_in, w_idx, ONE_BUFFER)] + [_pick(h, layer // 2) for h in head]
    args += [g_pre, mod, w_in] + list(head)
    in_specs += table_specs + cache_specs + [_pick(t, layer // 2) for t in tail]
    args += [cos, sin, cos_t, sin_t] + list(caches) + list(tail)
    in_specs += [res_spec, _pick(w_out, w_idx, ONE_BUFFER), _pick(g_post, layer)]
    args += [x, w_out, g_post]
    seq_wide = lambda w, dt: pltpu.VMEM((DEC_SEQ, w), dt)
    slab = pltpu.VMEM((DEC_SEQ // SEQ, B_W, SEQ), BF16)
    if even:
        scratch = [seq_wide(A_W, BF16), slab, seq_wide(B_W, BF16), slab, seq_wide(B_W, F32),
                   pltpu.VMEM((SUBLANES, SEQ + 3 * SUBLANES, A_W), F32), pltpu.VMEM((SEQ, A_W), F32),
                   pltpu.VMEM((PAST_LEN, B_W), BF16), pltpu.VMEM((B_W, PAST_LEN), BF16),
                   pltpu.VMEM((TQ, B_W), BF16)]
        body = functools.partial(_sample_even_kernel, lam_init=lam_init, layer=layer)
    else:
        chunk = pltpu.VMEM((DEC_SEQ // LANES, KV_W, LANES), BF16)
        scratch = [slab, seq_wide(KV_W, BF16), chunk, seq_wide(C_W, F32),
                   slab, seq_wide(KV_W, BF16), chunk, seq_wide(D_W, F32),
                   pltpu.VMEM((TQ, C_W + D_W), BF16)]
        body = functools.partial(_sample_odd_kernel, layer=layer)
    return pl.pallas_call(
        body,
        grid=(DEC_BATCH, N_PHASE),
        in_specs=in_specs,
        out_specs=pl.BlockSpec((TQ, D_MODEL), lambda b, ph: (q_tile(b, ph), 0)),
        out_shape=jax.ShapeDtypeStruct((n, D_MODEL), F32),
        scratch_shapes=scratch,
        compiler_params=_params("arbitrary", "arbitrary"),
        name=f"sample_layer{layer}",
    )(*args)


def _rope_tables():
    nf = DH // 4
    t = jnp.arange(DEC_SEQ)
    row = (t // GRID_W).astype(F32)
    col = (t % GRID_W).astype(F32)
    inv = ROPE_THETA ** (-jnp.arange(nf, dtype=F32) / nf)
    d = jnp.arange(DH)
    axis = d // (2 * nf)
    second = (d % (2 * nf)) // nf
    f = d % nf
    pos = jnp.where(axis[None, :] == 0, row[:, None], col[:, None])
    ang = pos * inv[f][None, :]
    cos = jnp.cos(ang)
    sin = jnp.where(second[None, :] == 0, -jnp.sin(ang), jnp.sin(ang))
    reps = LANES // DH
    cos = jnp.tile(cos, (1, reps))
    sin = jnp.tile(sin, (1, reps))
    return cos, sin, cos.T, sin.T


def kernel(x_prompt, x_sample, cache_b_k, cache_b_v, cache_c_k, cache_c_v, cache_d_k, cache_d_v, c, c_ctx, norm_pre, norm_post, w_mod, b_mod, w_in_even, a_conv_w, a_conv_b, a_ln_g, a_ln_b, b_lambda, b_subln_g, w_out_even, w_in_odd, c_q_norm, c_k_norm, d_sink, w_out_odd):
    n_even = (DEPTH + 1) // 2
    n_odd = DEPTH // 2
    cond8 = jnp.zeros((SUBLANES, D_MODEL), F32).at[0].set(c_ctx).at[1:1 + DEC_BATCH].set(c)
    n_chunk = BATCH * SEQ // TM
    cond_chunks = jnp.moveaxis(cond8.reshape(SUBLANES, n_chunk, D_MODEL // n_chunk), 1, 0)
    tables = _rope_tables()

    xp = x_prompt.reshape(BATCH * SEQ, D_MODEL)
    xs = x_sample.reshape(DEC_BATCH * DEC_SEQ, D_MODEL)
    feat = lambda a, w: jnp.moveaxis(a.reshape(a.shape[:3] + (w,)), 2, 3)
    cbk = feat(cache_b_k, B_W)
    cbv = cache_b_v.reshape(DEC_BATCH, n_even, PAST_LEN * H_B, 2 * DH)
    cck = feat(cache_c_k, KV_W)
    ccv = feat(cache_c_v, KV_W)
    cdk = feat(cache_d_k, KV_W)
    cdv = feat(cache_d_v, KV_W)

    g_pre, g_post = norm_pre, norm_post
    conv = (jnp.zeros((n_even, 4 * SUBLANES, A_W), F32).at[:, :CONV_K].set(a_conv_w),
            a_conv_b, a_ln_g, a_ln_b)
    subln = b_subln_g
    qkn = (jnp.tile(c_q_norm, (1, C_W // DH)).reshape(n_odd, C_W, 1), jnp.tile(c_k_norm, (1, KV_W // DH)))
    sink = d_sink

    f32_weights = ((w_in_even, w_out_even), (w_in_odd, w_out_odd))
    *weights, mod = _prep_first_layer(f32_weights[0] + (0, cond_chunks, w_mod, b_mod))
    new_even, new_odd = (), ()
    for l in range(DEPTH):
        w_in, w_out = weights
        mod_l = mod
        prep_next = (f32_weights[(l + 1) % 2] + ((l + 1) // 2, cond_chunks, w_mod, b_mod)
                     if l + 1 < DEPTH else None)
        if l % 2 == 0:
            lam_init = 0.8 - 0.6 * math.exp(-0.3 * l)
            outs = _prompt_layer(xp, l, g_pre, g_post, mod_l, w_in, w_out, 0, conv, (b_lambda, subln),
                                 new_even, lam_init, prep_next)
            xp, new_even, (*weights, mod) = outs[0], outs[1:3], outs[3:]
            xs = _sample_layer(xs, l, g_pre, g_post, mod_l, w_in, w_out, 0, conv, (b_lambda, subln),
                               (cbk, cbv), tables, lam_init)
        else:
            outs = _prompt_layer(xp, l, g_pre, g_post, mod_l, w_in, w_out, 0, qkn, (sink,), new_odd,
                                 None, prep_next)
            xp, new_odd, (*weights, mod) = outs[0], outs[1:5], (outs[5:] if prep_next else (None, None, None))
            xs = _sample_layer(xs, l, g_pre, g_post, mod_l, w_in, w_out, 0, qkn, (sink,),
                               (cck, ccv, cdk, cdv), tables)

    def token_major(a, heads):
        return jnp.moveaxis(a.reshape(a.shape[:2] + heads + (DH, SEQ)), -1, 2)

    kt, v = new_even
    ckt, cvt, dkt, dvt = new_odd
    return (xp.reshape(BATCH, SEQ, D_MODEL), xs.reshape(DEC_BATCH, DEC_SEQ, D_MODEL),
            token_major(kt, (H_B, 2)), v.reshape(BATCH, n_even, SEQ, H_B, 2 * DH),
            token_major(ckt, (2,)), token_major(cvt, (2,)), token_major(dkt, (2,)), token_major(dvt, (2,)))
```
